```python
import jax
import jax.numpy as jnp
from jax import lax
import numpy as np

D_MODEL = 1024
BATCH = 8
SEQ = 2048
DEPTH = 4
DEC_BATCH = 32
DEC_SEQ = 1
PAST_LEN = 16384
PAGE_SIZE = 128

N_MIXERS = 3
MIX_NSA, MIX_MLA, MIX_MOBA = 0, 1, 2
N_NSA = len(range(MIX_NSA, DEPTH, N_MIXERS))
N_MLA = len(range(MIX_MLA, DEPTH, N_MIXERS))
N_MOBA = len(range(MIX_MOBA, DEPTH, N_MIXERS))

ALPHA = (2 * DEPTH) ** 0.25
BETA = (8 * DEPTH) ** -0.25
LN_EPS = 1e-5
RMS_EPS = 1e-6
ROPE_THETA = 10000.0
NEG = -1e30
FORCE = 1e9

NSA_HEADS = 16
NSA_KV_HEADS = 2
NSA_HEAD_DIM = 64
CMP_BLOCK = 32
CMP_STRIDE = 16
CMP_HIDDEN = 128
SEL_BLOCK = 64
SEL_RATIO = SEL_BLOCK // CMP_STRIDE
SEL_TOP = 16
WINDOW = 512
NSA_IN = NSA_HEADS * NSA_HEAD_DIM + 6 * NSA_KV_HEADS * NSA_HEAD_DIM + 3 * NSA_HEADS

MLA_HEADS = 16
Q_LORA = 384
KV_LORA = 256
QK_NOPE = 64
QK_ROPE = 32
V_DIM = 64

MOBA_HEADS = 16
MOBA_KV_HEADS = 4
MOBA_HEAD_DIM = 64
MOBA_BLOCK = 256
MOBA_TOP = 3
MOBA_IN = MOBA_HEADS * MOBA_HEAD_DIM + 2 * MOBA_KV_HEADS * MOBA_HEAD_DIM

N_EXPERTS = 64
TOP_K = 8
D_EXPERT = 128
D_SHARED = 128
ROUTED_SCALE = 2.5

Q_BLOCK = 128
NSA_Q_BLOCK = 64
MOBA_Q_BLOCK = 16
TOKEN_BLOCK = 1024

kernel_name = 'hybrid_nsa_mla_moba_moe_step'


def layer_norm(x, g, b):
    xf = x.astype(jnp.float32)
    mu = jnp.mean(xf, axis=-1, keepdims=True)
    var = jnp.mean(jnp.square(xf - mu), axis=-1, keepdims=True)
    return ((xf - mu) * lax.rsqrt(var + LN_EPS) * g + b).astype(x.dtype)


def rms_norm(x, g):
    xf = x.astype(jnp.float32)
    return (xf * lax.rsqrt(jnp.mean(jnp.square(xf), axis=-1, keepdims=True) + RMS_EPS) * g).astype(x.dtype)


def rope(x, pos):
    half = x.shape[-1] // 2
    inv = jnp.power(ROPE_THETA, -jnp.arange(half, dtype=jnp.float32) / half)
    ang = pos.astype(jnp.float32)[:, None] * inv
    cos, sin = jnp.cos(ang)[:, None, :], jnp.sin(ang)[:, None, :]
    x1 = x[..., :half].astype(jnp.float32)
    x2 = x[..., half:].astype(jnp.float32)
    return jnp.concatenate([x1 * cos - x2 * sin, x2 * cos + x1 * sin], axis=-1).astype(x.dtype)


def rope_kv(kv, pos):
    return jnp.stack([rope(kv[..., 0, :], pos), kv[..., 1, :]], axis=-2)


def masked_softmax(s, mask):
    p = jax.nn.softmax(jnp.where(mask, s.astype(jnp.float32), NEG), axis=-1)
    return p * mask


def sweep_queries(fn, qs, pos, block):
    T = pos.shape[0]
    if T <= block or T % block:
        return fn(qs, pos)
    n = T // block
    split = lambda a: jnp.moveaxis(a.reshape(a.shape[0], n, block, *a.shape[2:]), 1, 0)
    out = lax.map(lambda a: fn(a[0], a[1]), (tuple(split(a) for a in qs), pos.reshape(n, block)))
    out = jnp.moveaxis(out, 0, 1)
    return out.reshape(out.shape[0], T, *out.shape[3:])


def gather_past(cache, slot, page_table):
    g = cache[slot, page_table]
    return g.reshape(g.shape[0], g.shape[1] * g.shape[2], *g.shape[3:])


def fetch_kv(pos, heads, new_kv, past_len, cache, slot, page_table):
    b = jnp.arange(pos.shape[0])[:, None, None, None]
    h = heads[None, None, :, None]
    rows = new_kv[b, jnp.clip(pos - past_len, 0, new_kv.shape[1] - 1), h]
    if cache is not None:
        p = jnp.clip(pos, 0, past_len - 1)
        old = cache[slot, page_table[b, p // PAGE_SIZE], p % PAGE_SIZE, h]
        rows = jnp.where((pos < past_len)[..., None, None], old, rows)
    return rows


def dense_attention(q_parts, k_parts, v, q_pos, k_pos, scale):
    B, _, H, _ = q_parts[0].shape
    S = k_pos.shape[0]

    def head_scores(qb, kb):
        kh = kb.shape[2]
        qg = qb.reshape(B, qb.shape[1], kh, H // kh, qb.shape[-1])
        s = jnp.einsum('btgrd,bsgd->bgrts', qg, kb, preferred_element_type=jnp.float32)
        return s.reshape(B, H, qb.shape[1], S)

    def block(qs, pq):
        s = sum(head_scores(qb, kb) for qb, kb in zip(qs, k_parts)) * scale
        p = masked_softmax(s, k_pos[None, :] <= pq[:, None])
        khv = v.shape[2]
        pg = p.astype(v.dtype).reshape(B, khv, H // khv, pq.shape[0], S)
        o = jnp.einsum('bgrts,bsgd->btgrd', pg, v)
        return o.reshape(B, pq.shape[0], H, v.shape[-1])

    return sweep_queries(block, tuple(q_parts), q_pos, Q_BLOCK)


def nsa_mixer(x, pos, w_in, cmp_pe, cmp_w1, cmp_b1, cmp_w2, cmp_b2, w_o,
              past_len=0, cache_cmp=None, cache_sel=None, win_buf=None, slot=0, page_table=None):
    B, T, _ = x.shape
    H, G, dh = NSA_HEADS, NSA_KV_HEADS, NSA_HEAD_DIM
    R = H // G
    scale = dh ** -0.5
    nq, nkv = H * dh, 6 * G * dh
    proj = x @ w_in
    q = proj[..., :nq].reshape(B, T, H, dh)
    kv = proj[..., nq:nq + nkv].reshape(B, T, 3, G, 2, dh)
    gates = jax.nn.sigmoid(proj[..., nq + nkv:].astype(jnp.float32)).reshape(B, T, 3, H)
    kv_cmp = kv[:, :, 0]
    kv_sel = rope_kv(kv[:, :, 1], pos)
    kv_win = rope_kv(kv[:, :, 2], pos)
    q_rot = rope(q, pos)

    rows = kv_cmp if cache_cmp is None else jnp.concatenate([gather_past(cache_cmp, slot, page_table), kv_cmp], axis=1)
    L = rows.shape[1]
    n_chunk = L // CMP_STRIDE
    n_cmp = n_chunk - 1
    chunks = rows[:, :n_chunk * CMP_STRIDE].reshape(B, n_chunk, CMP_STRIDE, G, 2, dh)
    lo = jnp.einsum('bcrgkd,rkdh->bcgkh', chunks, cmp_w1[:CMP_STRIDE])
    hi = jnp.einsum('bcrgkd,rkdh->bcgkh', chunks, cmp_w1[CMP_STRIDE:])
    pe_term = jnp.einsum('rkd,rkdh->kh', cmp_pe, cmp_w1)
    hid = jax.nn.gelu(lo[:, :-1] + hi[:, 1:] + pe_term + cmp_b1)
    cmp = jnp.einsum('bcgkh,khd->bcgkd', hid, cmp_w2) + cmp_b2
    k_c, v_c = cmp[..., 0, :], cmp[..., 1, :]
    c_last = jnp.arange(n_cmp, dtype=jnp.int32) * CMP_STRIDE + CMP_BLOCK - 1
    n_sel = -(-L // SEL_BLOCK)
    n_top = min(SEL_TOP, n_sel)
    blk = jnp.arange(n_sel, dtype=jnp.int32)

    if win_buf is None:
        win_cat = kv_win
        new_win = kv_win[:, -min(WINDOW, T):]
    else:
        win_cat = jnp.concatenate([win_buf, kv_win], axis=1)
        new_win = win_cat[:, -win_buf.shape[1]:]
    win_src = jnp.pad(win_cat, ((0, 0), (WINDOW + T - win_cat.shape[1], 0), (0, 0), (0, 0), (0, 0)))

    def attend(qs, pq):
        qb, qr, gb = qs
        Tb = pq.shape[0]
        qg = qb.reshape(B, Tb, G, R, dh)
        qrg = qr.reshape(B, Tb, G, R, dh)
        s = jnp.einsum('btgrd,bcgd->bgrtc', qg, k_c, preferred_element_type=jnp.float32) * scale
        p = masked_softmax(s, c_last[None, :] <= pq[:, None])
        o_cmp = jnp.einsum('bgrtc,bcgd->btgrd', p.astype(v_c.dtype), v_c)
        imp = jnp.pad(p.sum(axis=2), ((0, 0), (0, 0), (0, 0), (1, SEL_RATIO * (n_sel + 1) - n_cmp - 1)))
        imp = imp.reshape(B, G, Tb, n_sel + 1, SEL_RATIO)
        imp = imp[..., :n_sel, :].sum(-1) + imp[..., 1:, 0]
        cur = pq // SEL_BLOCK
        forced = (blk[None] == 0) | (blk[None] == cur[:, None]) | (blk[None] == cur[:, None] - 1)
        imp = jnp.where(blk[None] > cur[:, None], NEG, jnp.where(forced, FORCE, imp))
        _, idx = lax.top_k(imp, n_top)
        kpos = (idx[..., None] * SEL_BLOCK + jnp.arange(SEL_BLOCK)).reshape(B, G, Tb, n_top * SEL_BLOCK)
        sel = fetch_kv(kpos.transpose(0, 2, 1, 3), jnp.arange(G), kv_sel, past_len, cache_sel, slot, page_table)
        s = jnp.einsum('btgrd,btgnd->bgrtn', qrg, sel[..., 0, :], preferred_element_type=jnp.float32) * scale
        p = masked_softmax(s, (kpos <= pq[None, None, :, None])[:, :, None])
        o_sel = jnp.einsum('bgrtn,btgnd->btgrd', p.astype(sel.dtype), sel[..., 1, :])
        src = lax.dynamic_slice_in_dim(win_src, pq[0] - past_len, Tb + WINDOW, axis=1)
        wpos = pq[0] - WINDOW + jnp.arange(Tb + WINDOW, dtype=jnp.int32)
        dist = pq[:, None] - wpos[None, :]
        s = jnp.einsum('btgrd,bsgd->bgrts', qrg, src[..., 0, :], preferred_element_type=jnp.float32) * scale
        p = masked_softmax(s, (dist >= 0) & (dist <= WINDOW) & (wpos[None, :] >= 0))
        o_win = jnp.einsum('bgrts,bsgd->btgrd', p.astype(src.dtype), src[..., 1, :])
        o = jnp.stack([o_cmp, o_sel, o_win], axis=2).reshape(B, Tb, 3, H, dh)
        return jnp.einsum('btch,btchd->bthd', gb.astype(o.dtype), o)

    o = sweep_queries(attend, (q, q_rot, gates), pos, NSA_Q_BLOCK)
    y = o.reshape(B, T, H * dh) @ w_o
    return y, (kv_cmp, kv_sel, new_win)


def mla_mixer(x, pos, w_dn, g_q, w_uq, g_kv, w_uk, w_uv, w_o,
              past_len=0, cache_ckv=None, cache_krope=None, slot=0, page_table=None):
    B, T, _ = x.shape
    H = MLA_HEADS
    scale = (QK_NOPE + QK_ROPE) ** -0.5
    dn = x @ w_dn
    c_q = rms_norm(dn[..., :Q_LORA], g_q)
    c_kv = rms_norm(dn[..., Q_LORA:Q_LORA + KV_LORA], g_kv)
    k_r = rope(dn[..., Q_LORA + KV_LORA:][:, :, None, :], pos)
    q = (c_q @ w_uq).reshape(B, T, H, QK_NOPE + QK_ROPE)
    q_nope, q_rope = q[..., :QK_NOPE], rope(q[..., QK_NOPE:], pos)
    if cache_ckv is None:
        k_nope = jnp.einsum('btc,chd->bthd', c_kv, w_uk)
        v = jnp.einsum('btc,chd->bthd', c_kv, w_uv)
        o = dense_attention((q_nope, q_rope), (k_nope, k_r), v, pos, pos, scale)
    else:
        c_all = jnp.concatenate([gather_past(cache_ckv, slot, page_table), c_kv], axis=1)[:, :, None, :]
        r_all = jnp.concatenate([gather_past(cache_krope, slot, page_table)[:, :, None, :], k_r], axis=1)
        k_pos = jnp.arange(past_len + T, dtype=jnp.int32)
        q_lat = jnp.einsum('bthd,chd->bthc', q_nope, w_uk)
        o_lat = dense_attention((q_lat, q_rope), (c_all, r_all), c_all, pos, k_pos, scale)
        o = jnp.einsum('bthc,chd->bthd', o_lat, w_uv)
    y = o.reshape(B, T, H * V_DIM) @ w_o
    return y, (c_kv, k_r[:, :, 0])


def moba_mixer(x, pos, w_in, w_o, past_len=0, cache=None, slot=0, page_table=None):
    B, T, _ = x.shape
    H, KH, dh = MOBA_HEADS, MOBA_KV_HEADS, MOBA_HEAD_DIM
    R = H // KH
    scale = dh ** -0.5
    proj = x @ w_in
    q = rope(proj[..., :H * dh].reshape(B, T, H, dh), pos)
    kv = rope_kv(proj[..., H * dh:].reshape(B, T, KH, 2, dh), pos)
    heads = jnp.arange(H, dtype=jnp.int32) // R
    nb = (past_len + T - 1) // MOBA_BLOCK
    n_top = min(MOBA_TOP, nb)
    if n_top > 0:
        n_rows = nb * MOBA_BLOCK
        if cache is None:
            k_rows = kv[:, :n_rows, :, 0]
        else:
            n_pg = -(-min(n_rows, past_len) // PAGE_SIZE)
            k_rows = cache[slot, page_table[:, :n_pg], :, :, :1].reshape(B, n_pg * PAGE_SIZE, KH, dh)
            if n_rows > past_len:
                k_rows = jnp.concatenate([k_rows, kv[..., 0, :]], axis=1)
        means = k_rows[:, :n_rows].astype(jnp.float32).reshape(B, nb, MOBA_BLOCK, KH, dh).mean(axis=2)

    def attend(qs, pq):
        (qb,) = qs
        Tb = pq.shape[0]
        cur = pq[None, :, None, None] // MOBA_BLOCK
        blocks = jnp.broadcast_to(cur, (B, Tb, H, 1))
        bvalid = jnp.ones((B, Tb, H, 1), dtype=bool)
        if n_top > 0:
            g = jnp.einsum('btgrd,bjgd->btgrj', qb.reshape(B, Tb, KH, R, dh).astype(jnp.float32), means)
            g = g.reshape(B, Tb, H, nb)
            _, idx = lax.top_k(jnp.where(jnp.arange(nb) < cur, g, NEG), n_top)
            blocks = jnp.concatenate([idx, blocks], axis=-1)
            bvalid = jnp.concatenate([idx < cur, bvalid], axis=-1)
        kpos = (blocks[..., None] * MOBA_BLOCK + jnp.arange(MOBA_BLOCK)).reshape(B, Tb, H, -1)
        mask = jnp.repeat(bvalid, MOBA_BLOCK, axis=-1) & (kpos <= pq[None, :, None, None])
        rows = fetch_kv(kpos, heads, kv, past_len, cache, slot, page_table)
        s = jnp.einsum('bthd,bthnd->bthn', qb, rows[..., 0, :], preferred_element_type=jnp.float32) * scale
        p = masked_softmax(s, mask)
        return jnp.einsum('bthn,bthnd->bthd', p.astype(rows.dtype), rows[..., 1, :])

    o = sweep_queries(attend, (q,), pos, MOBA_Q_BLOCK)
    y = o.reshape(B, T, H * dh) @ w_o
    return y, kv


def moe_ffn(x, w_router, b_router, w_gate, w_up, w_down, ws_gate, ws_up, ws_down):
    B, T, D = x.shape
    N = B * T

    def block(xb):
        s = jax.nn.sigmoid((xb @ w_router).astype(jnp.float32))
        _, idx = lax.top_k(s + b_router, TOP_K)
        w = jnp.take_along_axis(s, idx, axis=-1)
        w = w / jnp.sum(w, axis=-1, keepdims=True) * ROUTED_SCALE
        gate = jnp.einsum('nk,nke->ne', w, jax.nn.one_hot(idx, N_EXPERTS, dtype=jnp.float32))
        h = jax.nn.silu(jnp.einsum('nd,edf->nef', xb, w_gate)) * jnp.einsum('nd,edf->nef', xb, w_up)
        y = jnp.einsum('nef,efd->nd', h * gate[..., None].astype(h.dtype), w_down)
        return y + (jax.nn.silu(xb @ ws_gate) * (xb @ ws_up)) @ ws_down

    xt = x.reshape(N, D)
    if N > TOKEN_BLOCK and N % TOKEN_BLOCK == 0:
        y = lax.map(block, xt.reshape(N // TOKEN_BLOCK, TOKEN_BLOCK, D)).reshape(N, D)
    else:
        y = block(xt)
    return y.reshape(B, T, D)


def setup_inputs(seed: int = 0) -> dict:
    key = jax.random.key(seed)
    keys = iter(jax.random.split(key, 48))

    def nrm(shape, scale=1.0):
        return jax.random.normal(next(keys), shape, jnp.float32) * scale

    def gain(shape):
        return 1.0 + nrm(shape, 0.02)

    D = D_MODEL
    n_pages = PAST_LEN // PAGE_SIZE
    n_pool = (DEC_BATCH * n_pages * 5) // 4
    page_table = jax.random.permutation(next(keys), n_pool)[:DEC_BATCH * n_pages]
    page_table = page_table.reshape(DEC_BATCH, n_pages).astype(jnp.int32)
    nsa_kv = (NSA_KV_HEADS, 2, NSA_HEAD_DIM)
    moba_kv = (MOBA_KV_HEADS, 2, MOBA_HEAD_DIM)
    x_prompt = nrm((BATCH, SEQ, D))
    x_sample = nrm((DEC_BATCH, DEC_SEQ, D))
    cache_nsa_cmp = nrm((N_NSA, n_pool, PAGE_SIZE) + nsa_kv)
    cache_nsa_sel = nrm((N_NSA, n_pool, PAGE_SIZE) + nsa_kv)
    state_nsa_win = nrm((N_NSA, DEC_BATCH, min(WINDOW, PAST_LEN)) + nsa_kv)
    cache_mla_ckv = nrm((N_MLA, n_pool, PAGE_SIZE, KV_LORA))
    cache_mla_krope = nrm((N_MLA, n_pool, PAGE_SIZE, QK_ROPE))
    cache_moba_kv = nrm((N_MOBA, n_pool, PAGE_SIZE) + moba_kv)
    return {
        'x_prompt': x_prompt,
        'x_sample': x_sample,
        'cache_nsa_cmp': cache_nsa_cmp,
        'cache_nsa_sel': cache_nsa_sel,
        'state_nsa_win': state_nsa_win,
        'cache_mla_ckv': cache_mla_ckv,
        'cache_mla_krope': cache_mla_krope,
        'cache_moba_kv': cache_moba_kv,
        'page_table': page_table,
        'nsa_w_in': nrm((N_NSA, D, NSA_IN), D ** -0.5),
        'nsa_cmp_pe': nrm((N_NSA, CMP_BLOCK, 2, NSA_HEAD_DIM), 0.5),
        'nsa_cmp_w1': nrm((N_NSA, CMP_BLOCK, 2, NSA_HEAD_DIM, CMP_HIDDEN), (CMP_BLOCK * NSA_HEAD_DIM) ** -0.5),
        'nsa_cmp_b1': nrm((N_NSA, 2, CMP_HIDDEN), 0.02),
        'nsa_cmp_w2': nrm((N_NSA, 2, CMP_HIDDEN, NSA_HEAD_DIM), CMP_HIDDEN ** -0.5),
        'nsa_cmp_b2': nrm((N_NSA, 2, NSA_HEAD_DIM), 0.02),
        'nsa_w_o': nrm((N_NSA, NSA_HEADS * NSA_HEAD_DIM, D), BETA * (NSA_HEADS * NSA_HEAD_DIM) ** -0.5),
        'mla_w_dn': nrm((N_MLA, D, Q_LORA + KV_LORA + QK_ROPE), D ** -0.5),
        'mla_g_q': gain((N_MLA, Q_LORA)),
        'mla_w_uq': nrm((N_MLA, Q_LORA, MLA_HEADS * (QK_NOPE + QK_ROPE)), Q_LORA ** -0.5),
        'mla_g_kv': gain((N_MLA, KV_LORA)),
        'mla_w_uk': nrm((N_MLA, KV_LORA, MLA_HEADS, QK_NOPE), KV_LORA ** -0.5),
        'mla_w_uv': nrm((N_MLA, KV_LORA, MLA_HEADS, V_DIM), KV_LORA ** -0.5),
        'mla_w_o': nrm((N_MLA, MLA_HEADS * V_DIM, D), BETA * (MLA_HEADS * V_DIM) ** -0.5),
        'moba_w_in': nrm((N_MOBA, D, MOBA_IN), D ** -0.5),
        'moba_w_o': nrm((N_MOBA, MOBA_HEADS * MOBA_HEAD_DIM, D), BETA * (MOBA_HEADS * MOBA_HEAD_DIM) ** -0.5),
        'ln1_g': gain((DEPTH, D)),
        'ln1_b': nrm((DEPTH, D), 0.02),
        'ln2_g': gain((DEPTH, D)),
        'ln2_b': nrm((DEPTH, D), 0.02),
        'moe_w_router': nrm((DEPTH, D, N_EXPERTS), D ** -0.5),
        'moe_b_router': nrm((DEPTH, N_EXPERTS), 0.01),
        'moe_w_gate': nrm((DEPTH, N_EXPERTS, D, D_EXPERT), D ** -0.5),
        'moe_w_up': nrm((DEPTH, N_EXPERTS, D, D_EXPERT), D ** -0.5),
        'moe_w_down': nrm((DEPTH, N_EXPERTS, D_EXPERT, D), BETA * D_EXPERT ** -0.5),
        'moe_ws_gate': nrm((DEPTH, D, D_SHARED), D ** -0.5),
        'moe_ws_up': nrm((DEPTH, D, D_SHARED), D ** -0.5),
        'moe_ws_down': nrm((DEPTH, D_SHARED, D), BETA * D_SHARED ** -0.5),
    }


def reference(x_prompt, x_sample, cache_nsa_cmp, cache_nsa_sel, state_nsa_win, cache_mla_ckv,
              cache_mla_krope, cache_moba_kv, page_table,
              nsa_w_in, nsa_cmp_pe, nsa_cmp_w1, nsa_cmp_b1, nsa_cmp_w2, nsa_cmp_b2, nsa_w_o,
              mla_w_dn, mla_g_q, mla_w_uq, mla_g_kv, mla_w_uk, mla_w_uv, mla_w_o,
              moba_w_in, moba_w_o,
              ln1_g, ln1_b, ln2_g, ln2_b,
              moe_w_router, moe_b_router, moe_w_gate, moe_w_up, moe_w_down,
              moe_ws_gate, moe_ws_up, moe_ws_down):
    past_len = page_table.shape[1] * PAGE_SIZE
    pos_p = jnp.arange(x_prompt.shape[1], dtype=jnp.int32)
    pos_s = past_len + jnp.arange(x_sample.shape[1], dtype=jnp.int32)
    hp, hs = x_prompt, x_sample
    cmp_p, cmp_s, sel_p, sel_s, win_p, win_s = [], [], [], [], [], []
    ckv_p, ckv_s, kr_p, kr_s, mb_p, mb_s = [], [], [], [], [], []
    for i in range(DEPTH):
        kind, slot = i % N_MIXERS, i // N_MIXERS
        if kind == MIX_NSA:
            w = (nsa_w_in[slot], nsa_cmp_pe[slot], nsa_cmp_w1[slot], nsa_cmp_b1[slot],
                 nsa_cmp_w2[slot], nsa_cmp_b2[slot], nsa_w_o[slot])
            yp, (a_p, b_p, c_p) = nsa_mixer(hp, pos_p, *w)
            ys, (a_s, b_s, c_s) = nsa_mixer(hs, pos_s, *w, past_len=past_len, cache_cmp=cache_nsa_cmp,
                                            cache_sel=cache_nsa_sel, win_buf=state_nsa_win[slot],
                                            slot=slot, page_table=page_table)
            cmp_p.append(a_p); cmp_s.append(a_s)
            sel_p.append(b_p); sel_s.append(b_s)
            win_p.append(c_p); win_s.append(c_s)
        elif kind == MIX_MLA:
            w = (mla_w_dn[slot], mla_g_q[slot], mla_w_uq[slot], mla_g_kv[slot],
                 mla_w_uk[slot], mla_w_uv[slot], mla_w_o[slot])
            yp, (a_p, b_p) = mla_mixer(hp, pos_p, *w)
            ys, (a_s, b_s) = mla_mixer(hs, pos_s, *w, past_len=past_len, cache_ckv=cache_mla_ckv,
                                       cache_krope=cache_mla_krope, slot=slot, page_table=page_table)
            ckv_p.append(a_p); ckv_s.append(a_s)
            kr_p.append(b_p); kr_s.append(b_s)
        else:
            yp, a_p = moba_mixer(hp, pos_p, moba_w_in[slot], moba_w_o[slot])
            ys, a_s = moba_mixer(hs, pos_s, moba_w_in[slot], moba_w_o[slot], past_len=past_len,
                                 cache=cache_moba_kv, slot=slot, page_table=page_table)
            mb_p.append(a_p); mb_s.append(a_s)
        hp = layer_norm(ALPHA * hp + yp, ln1_g[i], ln1_b[i])
        hs = layer_norm(ALPHA * hs + ys, ln1_g[i], ln1_b[i])
        mw = (moe_w_router[i], moe_b_router[i], moe_w_gate[i], moe_w_up[i], moe_w_down[i],
              moe_ws_gate[i], moe_ws_up[i], moe_ws_down[i])
        hp = layer_norm(ALPHA * hp + moe_ffn(hp, *mw), ln2_g[i], ln2_b[i])
        hs = layer_norm(ALPHA * hs + moe_ffn(hs, *mw), ln2_g[i], ln2_b[i])
    return (hp, hs,
            jnp.stack(cmp_p), jnp.stack(cmp_s), jnp.stack(sel_p), jnp.stack(sel_s),
            jnp.stack(win_p), jnp.stack(win_s),
            jnp.stack(ckv_p), jnp.stack(ckv_s), jnp.stack(kr_p), jnp.stack(kr_s),
            jnp.stack(mb_p), jnp.stack(mb_s))
```

```python
import functools

import jax
import jax.numpy as jnp
from jax import lax
from jax.experimental import pallas as pl
from jax.experimental.pallas import tpu as pltpu

D_MODEL = 1024
DEPTH = 4
PAGE_SIZE = 128
N_MIXERS = 3
MIX_NSA, MIX_MLA, MIX_MOBA = 0, 1, 2

ALPHA = (2 * DEPTH) ** 0.25
LN_EPS = 1e-5
RMS_EPS = 1e-6
ROPE_THETA = 10000.0
NEG = -1e30
FORCE = 1e9

NSA_HEADS = 16
NSA_KV_HEADS = 2
NSA_HEAD_DIM = 64
CMP_BLOCK = 32
CMP_STRIDE = 16
CMP_HIDDEN = 128
SEL_BLOCK = 64
SEL_RATIO = SEL_BLOCK // CMP_STRIDE
SEL_TOP = 16
WINDOW = 512

MLA_HEADS = 16
Q_LORA = 384
KV_LORA = 256
QK_NOPE = 64
QK_ROPE = 32
V_DIM = 64
MLA_SLOT = 128

MOBA_HEADS = 16
MOBA_KV_HEADS = 4
MOBA_HEAD_DIM = 64
MOBA_BLOCK = 256
MOBA_TOP = 3

N_EXPERTS = 64
TOP_K = 8
D_EXPERT = 128
D_SHARED = 128
ROUTED_SCALE = 2.5

LANES = 128
VMEM_LIMIT = 48 * 1024 * 1024

F32 = jnp.float32
BF16 = jnp.bfloat16
I32 = jnp.int32


def _cp(*sem):
    return pltpu.CompilerParams(dimension_semantics=sem, vmem_limit_bytes=VMEM_LIMIT)


def _dot(a, b):
    return jnp.dot(a, b, preferred_element_type=F32)


def _dot_nt(a, b):
    return lax.dot_general(a, b, (((1,), (1,)), ((), ())), preferred_element_type=F32)


def _dot_exact(a, b):
    return jnp.dot(a, b, preferred_element_type=F32, precision=lax.Precision.HIGHEST)


def _iota(shape, axis):
    return lax.broadcasted_iota(I32, shape, axis)


def _layer_norm(z, g, b):
    mu = jnp.mean(z, axis=-1, keepdims=True)
    d = z - mu
    var = jnp.mean(d * d, axis=-1, keepdims=True)
    return d * lax.rsqrt(var + LN_EPS) * g + b


def _silu(x):
    return x * jax.nn.sigmoid(x)


def _softmax_rows(s, valid):
    s = jnp.where(valid, s, NEG)
    m = jnp.max(s, axis=-1, keepdims=True)
    e = jnp.where(valid, jnp.exp(s - m), 0.0)
    l = jnp.sum(e, axis=-1, keepdims=True)
    return e / jnp.where(l > 0.0, l, 1.0)


def _rank_before(x, n_cols, lane):
    rank = jnp.zeros(x.shape, F32)
    for c in range(n_cols):
        col = x[:, c:c + 1]
        rank = rank + jnp.where(col > x, 1.0, jnp.where(col == x, jnp.where(lane > c, 1.0, 0.0), 0.0))
    return rank


def _proj_kernel(x_ref, w_ref, o_ref):
    o_ref[...] = _dot(x_ref[...].astype(BF16), w_ref[...])


def _proj_rope_kernel(x_ref, w_ref, a_ref, b_ref, o_ref, *, half, tn):
    acc = _dot(x_ref[...].astype(BF16), w_ref[...])
    lane = _iota(acc.shape, 1)
    first = (lane % (2 * half)) < half
    partner = jnp.where(first, pltpu.roll(acc, tn - half, 1), pltpu.roll(acc, half, 1))
    o_ref[...] = acc * a_ref[...] + partner * b_ref[...]


def _col_tile(n):
    return next(t for t in (512, 384, 256, 128) if n % t == 0)


def _proj(x, w, tabs=None, half=0):
    M, K = x.shape
    N = w.shape[1]
    tm = min(M, 512)
    tn = _col_tile(N)
    grid = (M // tm, N // tn)
    x_spec = pl.BlockSpec((tm, K), lambda i, j: (i, 0))
    w_spec = pl.BlockSpec((K, tn), lambda i, j: (0, j))
    o_spec = pl.BlockSpec((tm, tn), lambda i, j: (i, j))
    out_shape = jax.ShapeDtypeStruct((M, N), F32)
    if tabs is None:
        return pl.pallas_call(_proj_kernel, out_shape=out_shape, grid=grid, in_specs=[x_spec, w_spec],
                              out_specs=o_spec, compiler_params=_cp("parallel", "parallel"), name="proj")(x, w)
    a, bm = tabs
    nrb = a.shape[0] // tm
    t_spec = pl.BlockSpec((tm, tn), lambda i, j: (i % nrb, j))
    return pl.pallas_call(functools.partial(_proj_rope_kernel, half=half, tn=tn), out_shape=out_shape, grid=grid,
                          in_specs=[x_spec, w_spec, t_spec, t_spec], out_specs=o_spec,
                          compiler_params=_cp("parallel", "parallel"), name="proj_rope")(x, w, a, bm)


def _proj_ln_kernel(x_ref, w_ref, r_ref, g_ref, b_ref, o_ref):
    y = _dot(x_ref[...].astype(BF16), w_ref[...])
    o_ref[...] = _layer_norm(ALPHA * r_ref[...] + y, g_ref[...], b_ref[...])


def _proj_ln(x, w, res, g, b):
    M, K = x.shape
    N = w.shape[1]
    tm = min(M, 256)
    row = lambda i: (i, 0)
    fix = lambda i: (0, 0)
    return pl.pallas_call(
        _proj_ln_kernel, out_shape=jax.ShapeDtypeStruct((M, N), F32), grid=(M // tm,),
        in_specs=[pl.BlockSpec((tm, K), row), pl.BlockSpec((K, N), fix), pl.BlockSpec((tm, N), row),
                  pl.BlockSpec((1, N), fix), pl.BlockSpec((1, N), fix)],
        out_specs=pl.BlockSpec((tm, N), row), compiler_params=_cp("parallel"), name="proj_ln",
    )(x, w, res, g.reshape(1, N), b.reshape(1, N))


def _rope_unit(pos, half):
    inv = jnp.power(ROPE_THETA, -jnp.arange(half, dtype=F32) / half)
    ang = pos.astype(F32)[:, None] * inv
    cos, sin = jnp.cos(ang), jnp.sin(ang)
    return jnp.concatenate([cos, cos], axis=1), jnp.concatenate([-sin, sin], axis=1)


def _tables(pos, half, layout):
    ua, ub = _rope_unit(pos, half)
    T = pos.shape[0]
    a_parts, b_parts = [], []
    for kind, n in layout:
        if kind == 'r':
            a_parts.append(jnp.tile(ua, (1, n)))
            b_parts.append(jnp.tile(ub, (1, n)))
        else:
            a_parts.append(jnp.ones((T, n), F32))
            b_parts.append(jnp.zeros((T, n), F32))
    return jnp.concatenate(a_parts, axis=1), jnp.concatenate(b_parts, axis=1)


def _sweep(qh, k_ref, v_ref, k_cols, lo, hi, tk, valid_fn):
    tq = qh.shape[0]
    dv = v_ref.shape[1]

    def body(kt, carry):
        m, l, acc = carry
        k0 = pl.multiple_of(kt * tk, tk)
        kk = k_ref[pl.ds(k0, tk), k_cols]
        vv = v_ref[pl.ds(k0, tk), :]
        kpos = k0 + _iota((1, tk), 1)
        valid = valid_fn(kt, kpos)
        s = jnp.where(valid, _dot_nt(qh, kk), NEG)
        m_new = jnp.maximum(m, jnp.max(s, axis=-1, keepdims=True))
        alpha = jnp.exp(m - m_new)
        e = jnp.where(valid, jnp.exp(s - m_new), 0.0)
        l = alpha * l + jnp.sum(e, axis=-1, keepdims=True)
        acc = alpha * acc + _dot(e.astype(BF16), vv)
        return m_new, l, acc

    init = (jnp.full((tq, 1), NEG, F32), jnp.zeros((tq, 1), F32), jnp.zeros((tq, dv), F32))
    _, l, acc = lax.fori_loop(lo, hi, body, init)
    return acc / jnp.where(l > 0.0, l, 1.0)


def _cmp_lohi_kernel(*refs, n_pages, rows_per_page):
    G = NSA_KV_HEADS
    refs = refs[len(refs) - G * n_pages - 2:]
    row_refs, w_ref, o_ref = refs[:G * n_pages], refs[G * n_pages], refs[G * n_pages + 1]
    cpp = rows_per_page // CMP_STRIDE
    nh = w_ref.shape[2] // 2
    for g in range(G):
        acc = jnp.zeros((n_pages * cpp, 2 * nh), F32)
        for r in range(CMP_STRIDE):
            xr = [row_refs[g * n_pages + p][0, pl.ds(r, cpp, stride=CMP_STRIDE), :] for p in range(n_pages)]
            xr = xr[0] if n_pages == 1 else jnp.concatenate(xr, axis=0)
            acc = acc + _dot(xr.astype(BF16), w_ref[r])
        o_ref[0, :, g * nh:(g + 1) * nh] = acc[:, :nh]
        o_ref[0, :, (G + g) * nh:(G + g + 1) * nh] = acc[:, nh:]


def _cmp_lohi(rows, w_lohi, col_block=0):
    B, L = rows.shape[0], rows.shape[1]
    G = NSA_KV_HEADS
    nout = G * w_lohi.shape[2]
    row_specs = [pl.BlockSpec((1, L, LANES), functools.partial(lambda b, g: (b, 0, col_block + g), g=g))
                 for g in range(G)]
    return pl.pallas_call(
        functools.partial(_cmp_lohi_kernel, n_pages=1, rows_per_page=L),
        out_shape=jax.ShapeDtypeStruct((B, L // CMP_STRIDE, nout), F32), grid=(B,),
        in_specs=row_specs + [pl.BlockSpec(w_lohi.shape, lambda b: (0, 0, 0))],
        out_specs=pl.BlockSpec((1, L // CMP_STRIDE, nout), lambda b: (b, 0, 0)),
        compiler_params=_cp("parallel"), name="cmp_lohi",
    )(*([rows] * G), w_lohi)


def _cmp_lohi_paged(cache, pages, w_lohi, per_step):
    B, n_pages = pages.shape
    G = NSA_KV_HEADS
    nout = G * w_lohi.shape[2]
    cpp = PAGE_SIZE // CMP_STRIDE
    page_specs = [pl.BlockSpec((1, PAGE_SIZE, LANES), functools.partial(
        lambda b, s, pt, p, g: (pt[b, s * per_step + p], 0, g), p=p, g=g))
        for g in range(G) for p in range(per_step)]
    grid_spec = pltpu.PrefetchScalarGridSpec(
        num_scalar_prefetch=1, grid=(B, n_pages // per_step),
        in_specs=page_specs + [pl.BlockSpec(w_lohi.shape, lambda b, s, pt: (0, 0, 0))],
        out_specs=pl.BlockSpec((1, per_step * cpp, nout), lambda b, s, pt: (b, s, 0)))
    return pl.pallas_call(
        functools.partial(_cmp_lohi_kernel, n_pages=per_step, rows_per_page=PAGE_SIZE),
        out_shape=jax.ShapeDtypeStruct((B, n_pages * cpp, nout), F32), grid_spec=grid_spec,
        compiler_params=_cp("parallel", "parallel"), name="cmp_lohi_paged",
    )(pages, *([cache] * (G * per_step)), w_lohi)


def _cmp_combine_kernel(lohi_ref, pe_ref, b1_ref, w2_ref, b2_ref, o_ref):
    nch = lohi_ref.shape[1]
    nh = lohi_ref.shape[2] // 2
    lo = lohi_ref[0, :, :nh]
    hi_next = pltpu.roll(lohi_ref[0, :, nh:], nch - 1, 0)
    pe = pe_ref[0, 0:1, :nh] + pe_ref[0, 1:2, nh:]
    hid = jax.nn.gelu(lo + hi_next + pe + b1_ref[...])
    o_ref[0] = _dot(hid.astype(BF16), w2_ref[...]) + b2_ref[...]


def _cmp_combine(lohi, pe_lohi, b1, w2, b2):
    B, nch, n2 = lohi.shape
    nh = n2 // 2
    nout = w2.shape[1]
    fix2 = lambda b: (0, 0)
    return pl.pallas_call(
        _cmp_combine_kernel, out_shape=jax.ShapeDtypeStruct((B, nch, nout), F32), grid=(B,),
        in_specs=[pl.BlockSpec((1, nch, n2), lambda b: (b, 0, 0)), pl.BlockSpec((1, 2, n2), lambda b: (0, 0, 0)),
                  pl.BlockSpec((1, nh), fix2), pl.BlockSpec((nh, nout), fix2), pl.BlockSpec((1, nout), fix2)],
        out_specs=pl.BlockSpec((1, nch, nout), lambda b: (b, 0, 0)),
        compiler_params=_cp("parallel"), name="cmp_combine",
    )(lohi, pe_lohi, b1, w2, b2)


def _sel_importance(p_sum, n_lanes):
    nc = p_sum.shape[1]
    c = _iota((nc, n_lanes), 0)
    j = _iota((nc, n_lanes), 1)
    a = jnp.where(c >= SEL_RATIO * j - 1, jnp.where(c <= SEL_RATIO * j + SEL_RATIO - 1, 1.0, 0.0), 0.0)
    return _dot_exact(p_sum, a)


def _attn_nsa_kernel(q_ref, qr_ref, cmp_ref, sel_ref, win_ref, gl_ref, o_ref,
                     ks, vs, kw, vw, mask_ref, *, T, tq, tk, n_sel, n_top):
    i = pl.program_id(2)
    t0 = i * tq
    R = NSA_HEADS // NSA_KV_HEADS
    dh = NSA_HEAD_DIM
    scale = dh ** -0.5

    @pl.when(i == 0)
    def _():
        ks[...] = sel_ref[:, :dh].astype(BF16)
        vs[...] = sel_ref[:, dh:].astype(BF16)
        kw[...] = win_ref[:, :dh].astype(BF16)
        vw[...] = win_ref[:, dh:].astype(BF16)

    tpos = t0 + _iota((tq, 1), 0)
    kc = cmp_ref[0, :, :dh].astype(BF16)
    vc = cmp_ref[0, :, dh:].astype(BF16)
    nc = kc.shape[0]
    cvalid = (_iota((1, nc), 1) * CMP_STRIDE + CMP_BLOCK - 1) <= tpos
    p_sum = jnp.zeros((tq, nc), F32)
    o_cmp = []
    for r in range(R):
        qh = (q_ref[:, r * dh:(r + 1) * dh] * scale).astype(BF16)
        p = _softmax_rows(_dot_nt(qh, kc), cvalid)
        p_sum = p_sum + p
        o_cmp.append(_dot(p.astype(BF16), vc))

    blk = _iota((tq, LANES), 1)
    cur = tpos // SEL_BLOCK
    imp = _sel_importance(p_sum, LANES)
    forced = jnp.where(blk == 0, 1.0, jnp.where(blk == cur, 1.0, jnp.where(blk == cur - 1, 1.0, 0.0)))
    imp = jnp.where(blk > cur, NEG, jnp.where(forced > 0.5, FORCE, imp))
    rank = _rank_before(imp, n_sel, blk)
    chosen = jnp.where(blk < n_sel, jnp.where(rank < n_top, 1.0, 0.0), 0.0)
    expand = jnp.where(_iota((LANES, T), 1) // SEL_BLOCK == _iota((LANES, T), 0), 1.0, 0.0).astype(BF16)
    key_mask = _dot(chosen.astype(BF16), expand)
    for kt in range(T // tk):
        mask_ref[kt] = key_mask[:, kt * tk:(kt + 1) * tk]

    def sel_valid(kt, kpos):
        return jnp.where(kpos <= tpos, mask_ref[kt], 0.0) > 0.5

    def win_valid(kt, kpos):
        d = tpos - kpos
        return jnp.where(d >= 0, jnp.where(d <= WINDOW, 1.0, 0.0), 0.0) > 0.5

    hi = (t0 + tq + tk - 1) // tk
    win_lo = jnp.maximum(t0 - WINDOW, 0) // tk
    gate = jax.nn.sigmoid(gl_ref[...])
    all_cols = slice(None)
    for r in range(R):
        qh = (qr_ref[:, r * dh:(r + 1) * dh] * scale).astype(BF16)
        o_sel = _sweep(qh, ks, vs, all_cols, 0, hi, tk, sel_valid)
        o_win = _sweep(qh, kw, vw, all_cols, win_lo, hi, tk, win_valid)
        o_ref[:, r * dh:(r + 1) * dh] = (gate[:, r:r + 1] * o_cmp[r] + gate[:, R + r:R + r + 1] * o_sel
                                         + gate[:, 2 * R + r:2 * R + r + 1] * o_win)


def _attn_nsa(proj, cmp_tok, B, T):
    G = NSA_KV_HEADS
    tq = min(T, 128)
    tk = min(T, 256)
    nq = T // tq
    n_sel = -(-T // SEL_BLOCK)
    n_top = min(SEL_TOP, n_sel)
    qw = NSA_HEADS * NSA_HEAD_DIM // G
    kv0 = 2 * NSA_HEADS * NSA_HEAD_DIM // LANES
    gl0 = kv0 + 3 * G
    nch = cmp_tok.shape[1]
    dh = NSA_HEAD_DIM
    kern = functools.partial(_attn_nsa_kernel, T=T, tq=tq, tk=tk, n_sel=n_sel, n_top=n_top)
    return pl.pallas_call(
        kern, out_shape=jax.ShapeDtypeStruct((B * T, NSA_HEADS * dh), F32), grid=(B, G, nq),
        in_specs=[pl.BlockSpec((tq, qw), lambda b, g, i: (b * nq + i, g)),
                  pl.BlockSpec((tq, qw), lambda b, g, i: (b * nq + i, G + g)),
                  pl.BlockSpec((1, nch, 2 * dh), lambda b, g, i: (b, 0, g)),
                  pl.BlockSpec((T, 2 * dh), lambda b, g, i: (b, kv0 + G + g)),
                  pl.BlockSpec((T, 2 * dh), lambda b, g, i: (b, kv0 + 2 * G + g)),
                  pl.BlockSpec((tq, LANES), lambda b, g, i: (b * nq + i, gl0 + g))],
        out_specs=pl.BlockSpec((tq, qw), lambda b, g, i: (b * nq + i, g)),
        scratch_shapes=[pltpu.VMEM((T, dh), BF16)] * 4 + [pltpu.VMEM((T // tk, tq, tk), F32)],
        compiler_params=_cp("parallel", "parallel", "arbitrary"), name="attn_nsa",
    )(proj, proj, cmp_tok, proj, proj, proj)


def _nsa_dec_cmp_kernel(q_ref, cmp_ref, o_ref, imp_ref, *, t, n_sel):
    G = NSA_KV_HEADS
    R = NSA_HEADS // G
    dh = NSA_HEAD_DIM
    nc = cmp_ref.shape[1]
    nl = imp_ref.shape[2]
    cvalid = (_iota((1, nc), 1) * CMP_STRIDE + CMP_BLOCK - 1) <= t
    blk = _iota((1, nl), 1)
    cur = t // SEL_BLOCK
    for g in range(G):
        kc = cmp_ref[0, :, 2 * g * dh:(2 * g + 1) * dh].astype(BF16)
        vc = cmp_ref[0, :, (2 * g + 1) * dh:(2 * g + 2) * dh].astype(BF16)
        qg = (q_ref[0, g * R:(g + 1) * R, :] * dh ** -0.5).astype(BF16)
        p = _softmax_rows(_dot_nt(qg, kc), cvalid)
        o_ref[0, g * R:(g + 1) * R, :] = _dot(p.astype(BF16), vc)
        imp = _sel_importance(jnp.sum(p, axis=0, keepdims=True), nl)
        forced = jnp.where(blk == 0, 1.0, jnp.where(blk == cur, 1.0, jnp.where(blk == cur - 1, 1.0, 0.0)))
        imp = jnp.where(blk > cur, NEG, jnp.where(forced > 0.5, FORCE, imp))
        imp_ref[0, g:g + 1, :] = jnp.where(blk < n_sel, imp, -jnp.inf)


def _nsa_dec_cmp(q3, cmp_tok, t, n_sel):
    B = q3.shape[0]
    nl = -(-n_sel // LANES) * LANES
    nc = cmp_tok.shape[1]
    blk3 = lambda b: (b, 0, 0)
    return pl.pallas_call(
        functools.partial(_nsa_dec_cmp_kernel, t=t, n_sel=n_sel),
        out_shape=(jax.ShapeDtypeStruct(q3.shape, F32), jax.ShapeDtypeStruct((B, NSA_KV_HEADS, nl), F32)),
        grid=(B,),
        in_specs=[pl.BlockSpec((1,) + q3.shape[1:], blk3), pl.BlockSpec((1, nc, cmp_tok.shape[2]), blk3)],
        out_specs=(pl.BlockSpec((1,) + q3.shape[1:], blk3), pl.BlockSpec((1, NSA_KV_HEADS, nl), blk3)),
        compiler_params=_cp("parallel"), name="nsa_dec_cmp",
    )(q3, cmp_tok)


def _topk_idx_kernel(x_ref, o_ref, *, k):
    x = x_ref[...]
    lane = _iota(x.shape, 1)
    out_lane = _iota(o_ref.shape, 1)
    out = jnp.zeros(o_ref.shape, I32)
    big = x.shape[1]
    for n in range(k):
        m = jnp.max(x, axis=-1, keepdims=True)
        idx = jnp.min(jnp.where(x == m, lane, big), axis=-1, keepdims=True)
        out = jnp.where(out_lane == n, idx, out)
        x = jnp.where(lane == idx, -jnp.inf, x)
    o_ref[...] = out


def _topk_idx(x, k):
    rows = x.shape[0]
    return pl.pallas_call(functools.partial(_topk_idx_kernel, k=k),
                          out_shape=jax.ShapeDtypeStruct((rows, LANES), I32), name="topk_idx")(x)


def _nsa_dec_attn_kernel(*refs, n_top, n_past_blocks):
    idx_ref, rb_ref = refs[0], refs[1]
    G = NSA_KV_HEADS
    R = NSA_HEADS // G
    dh = NSA_HEAD_DIM
    blocks = refs[2:2 + G * n_top]
    qr_ref, new_ref, win_ref, ocmp_ref, gl_ref, o_ref = refs[2 + G * n_top:]
    b = pl.program_id(0)
    scale = dh ** -0.5
    nk = n_top * SEL_BLOCK
    kblk = _iota((1, nk), 1) // SEL_BLOCK
    for g in range(G):
        qg = qr_ref[0, g * R:(g + 1) * R, :] * scale
        qb = qg.astype(BF16)
        kcat = jnp.concatenate([blocks[g * n_top + n][:, :dh] for n in range(n_top)], axis=0).astype(BF16)
        vcat = jnp.concatenate([blocks[g * n_top + n][:, dh:] for n in range(n_top)], axis=0).astype(BF16)
        valid = jnp.zeros((1, nk), F32)
        for n in range(n_top):
            flag = jnp.where(idx_ref[b, g * n_top + n] < n_past_blocks, 1.0, 0.0)
            valid = jnp.where(kblk == n, flag, valid)
        valid = valid > 0.5
        k_new = new_ref[0, :, (2 * G + 2 * g) * dh:(2 * G + 2 * g + 1) * dh]
        v_new = new_ref[0, :, (2 * G + 2 * g + 1) * dh:(2 * G + 2 * g + 2) * dh]
        s = jnp.where(valid, _dot_nt(qb, kcat), NEG)
        s_new = jnp.sum(qg * k_new, axis=-1, keepdims=True)
        m = jnp.maximum(jnp.max(s, axis=-1, keepdims=True), s_new)
        e = jnp.where(valid, jnp.exp(s - m), 0.0)
        e_new = jnp.exp(s_new - m)
        o_sel = (_dot(e.astype(BF16), vcat) + e_new * v_new) / (jnp.sum(e, axis=-1, keepdims=True) + e_new)
        kwin = win_ref[0, :, 2 * g * dh:(2 * g + 1) * dh].astype(BF16)
        vwin = win_ref[0, :, (2 * g + 1) * dh:(2 * g + 2) * dh].astype(BF16)
        k_new = new_ref[0, :, (4 * G + 2 * g) * dh:(4 * G + 2 * g + 1) * dh]
        v_new = new_ref[0, :, (4 * G + 2 * g + 1) * dh:(4 * G + 2 * g + 2) * dh]
        s = _dot_nt(qb, kwin)
        s_new = jnp.sum(qg * k_new, axis=-1, keepdims=True)
        m = jnp.maximum(jnp.max(s, axis=-1, keepdims=True), s_new)
        e = jnp.exp(s - m)
        e_new = jnp.exp(s_new - m)
        o_win = (_dot(e.astype(BF16), vwin) + e_new * v_new) / (jnp.sum(e, axis=-1, keepdims=True) + e_new)
        gate = jax.nn.sigmoid(gl_ref[0, g * R:(g + 1) * R, :])
        o_ref[0, g * R:(g + 1) * R, :] = (gate[:, 0:1] * ocmp_ref[0, g * R:(g + 1) * R, :]
                                          + gate[:, 1:2] * o_sel + gate[:, 2:3] * o_win)


def _nsa_dec_attn(idx, row_blocks, cache_rows, qr3, new_kv, win_buf, win_index0, o_cmp, gl3, n_past_blocks):
    B = qr3.shape[0]
    G = NSA_KV_HEADS
    n_top = idx.shape[1] // G
    dh = NSA_HEAD_DIM
    blk3 = lambda b, idx, rb: (b, 0, 0)
    block_specs = [pl.BlockSpec((SEL_BLOCK, 2 * dh), functools.partial(
        lambda b, idx, rb, n, g: (rb[b, n], g), n=g * n_top + n, g=g)) for g in range(G) for n in range(n_top)]
    grid_spec = pltpu.PrefetchScalarGridSpec(
        num_scalar_prefetch=2, grid=(B,),
        in_specs=block_specs + [
            pl.BlockSpec((1,) + qr3.shape[1:], blk3), pl.BlockSpec((1,) + new_kv.shape[1:], blk3),
            pl.BlockSpec((1,) + win_buf.shape[1:], lambda b, idx, rb: (win_index0 + b, 0, 0)),
            pl.BlockSpec((1,) + o_cmp.shape[1:], blk3), pl.BlockSpec((1,) + gl3.shape[1:], blk3)],
        out_specs=pl.BlockSpec((1,) + qr3.shape[1:], blk3))
    return pl.pallas_call(
        functools.partial(_nsa_dec_attn_kernel, n_top=n_top, n_past_blocks=n_past_blocks),
        out_shape=jax.ShapeDtypeStruct(qr3.shape, F32), grid_spec=grid_spec,
        compiler_params=_cp("parallel"), name="nsa_dec_attn",
    )(idx, row_blocks, *([cache_rows] * (G * n_top)), qr3, new_kv, win_buf, o_cmp, gl3)


def _mla_post_kernel(dn_ref, gq_ref, gkv_ref, a_ref, b_ref, cq_ref, ckr_ref):
    x = dn_ref[...]
    cq = x[:, :Q_LORA]
    cq_ref[...] = cq * lax.rsqrt(jnp.mean(cq * cq, axis=-1, keepdims=True) + RMS_EPS) * gq_ref[...]
    ckv = x[:, Q_LORA:Q_LORA + KV_LORA]
    ckv = ckv * lax.rsqrt(jnp.mean(ckv * ckv, axis=-1, keepdims=True) + RMS_EPS) * gkv_ref[...]
    kr = x[:, Q_LORA + KV_LORA:]
    half = QK_ROPE // 2
    first = (_iota(kr.shape, 1) % QK_ROPE) < half
    partner = jnp.where(first, pltpu.roll(kr, LANES - half, 1), pltpu.roll(kr, half, 1))
    ckr_ref[...] = jnp.concatenate([ckv, kr * a_ref[...] + partner * b_ref[...]], axis=1)


def _mla_post(dn, g_q, g_kv, tabs):
    M, N = dn.shape
    tm = min(M, 512)
    nrb = tabs[0].shape[0] // tm
    row = lambda i: (i, 0)
    fix = lambda i: (0, 0)
    tab = pl.BlockSpec((tm, LANES), lambda i: (i % nrb, 0))
    return pl.pallas_call(
        _mla_post_kernel,
        out_shape=(jax.ShapeDtypeStruct((M, Q_LORA), F32), jax.ShapeDtypeStruct((M, KV_LORA + LANES), F32)),
        grid=(M // tm,),
        in_specs=[pl.BlockSpec((tm, N), row), pl.BlockSpec((1, Q_LORA), fix), pl.BlockSpec((1, KV_LORA), fix),
                  tab, tab],
        out_specs=(pl.BlockSpec((tm, Q_LORA), row), pl.BlockSpec((tm, KV_LORA + LANES), row)),
        compiler_params=_cp("parallel"), name="mla_post",
    )(dn, g_q.reshape(1, -1), g_kv.reshape(1, -1), *tabs)


def _attn_mla_kernel(q_ref, k_ref, v_ref, o_ref, k_sc, v_sc, *, tq, tk):
    i = pl.program_id(2)
    t0 = i * tq
    scale = (QK_NOPE + QK_ROPE) ** -0.5

    @pl.when(i == 0)
    def _():
        k_sc[...] = k_ref[...].astype(BF16)
        v_sc[...] = v_ref[...].astype(BF16)

    tpos = t0 + _iota((tq, 1), 0)
    hi = (t0 + tq + tk - 1) // tk
    causal = lambda kt, kpos: kpos <= tpos
    for hh in range(2):
        qh = (q_ref[:, hh * MLA_SLOT:(hh + 1) * MLA_SLOT] * scale).astype(BF16)
        o = _sweep(qh, k_sc, v_sc, slice(hh * MLA_SLOT, (hh + 1) * MLA_SLOT), 0, hi, tk, causal)
        o_ref[:, hh * V_DIM:(hh + 1) * V_DIM] = o[:, hh * V_DIM:(hh + 1) * V_DIM]


def _attn_mla(q_ext, kv_ext, B, T):
    H = MLA_HEADS
    tq = min(T, 256)
    tk = min(T, 256)
    nq = T // tq
    v0 = H * MLA_SLOT // LANES
    return pl.pallas_call(
        functools.partial(_attn_mla_kernel, tq=tq, tk=tk),
        out_shape=jax.ShapeDtypeStruct((B * T, H * V_DIM), F32), grid=(B, H // 2, nq),
        in_specs=[pl.BlockSpec((tq, 2 * MLA_SLOT), lambda b, h, i: (b * nq + i, h)),
                  pl.BlockSpec((T, 2 * MLA_SLOT), lambda b, h, i: (b, h)),
                  pl.BlockSpec((T, 2 * V_DIM), lambda b, h, i: (b, v0 + h))],
        out_specs=pl.BlockSpec((tq, 2 * V_DIM), lambda b, h, i: (b * nq + i, h)),
        scratch_shapes=[pltpu.VMEM((T, 2 * MLA_SLOT), BF16), pltpu.VMEM((T, 2 * V_DIM), BF16)],
        compiler_params=_cp("parallel", "parallel", "arbitrary"), name="attn_mla",
    )(q_ext, kv_ext, kv_ext)


def _mla_decode_kernel(*refs, per_step):
    pt_ref = refs[0]
    ckv_pages = refs[1:1 + per_step]
    kr_pages = refs[1 + per_step:1 + 2 * per_step]
    ql_ref, qr_ref, cnew_ref, rnew_ref, o_ref, m_sc, l_sc, acc_sc = refs[1 + 2 * per_step:]
    s_idx = pl.program_id(1)
    scale = (QK_NOPE + QK_ROPE) ** -0.5
    ql = ql_ref[0] * scale
    qr = qr_ref[0] * scale

    @pl.when(s_idx == 0)
    def _():
        c_new = cnew_ref[0]
        s_new = (jnp.sum(ql * c_new, axis=-1, keepdims=True) + jnp.sum(qr * rnew_ref[0], axis=-1, keepdims=True))
        m_sc[...] = s_new
        l_sc[...] = jnp.ones(l_sc.shape, F32)
        acc_sc[...] = jnp.broadcast_to(c_new, acc_sc.shape)

    qlb = ql.astype(BF16)
    qrb = qr.astype(BF16)
    cs = [ref[0].astype(BF16) for ref in ckv_pages]
    s = jnp.concatenate([_dot_nt(qlb, c) + _dot_nt(qrb, r[0].astype(BF16)) for c, r in zip(cs, kr_pages)], axis=1)
    m_old = m_sc[...]
    m_new = jnp.maximum(m_old, jnp.max(s, axis=-1, keepdims=True))
    alpha = jnp.exp(m_old - m_new)
    e = jnp.exp(s - m_new).astype(BF16)
    pv = _dot(e[:, :PAGE_SIZE], cs[0])
    for p in range(1, per_step):
        pv = pv + _dot(e[:, p * PAGE_SIZE:(p + 1) * PAGE_SIZE], cs[p])
    l_sc[...] = alpha * l_sc[...] + jnp.sum(e.astype(F32), axis=-1, keepdims=True)
    acc_sc[...] = alpha * acc_sc[...] + pv
    m_sc[...] = m_new

    @pl.when(s_idx == pl.num_programs(1) - 1)
    def _():
        o_ref[0] = acc_sc[...] / l_sc[...]


def _mla_decode(pages, cache_ckv, cache_kr, q_lat, q_rope, c_new, r_new, per_step):
    B, n_pages = pages.shape
    H = MLA_HEADS
    blk3 = lambda b, s, pt: (b, 0, 0)
    page_map = lambda p: functools.partial(lambda b, s, pt, p: (pt[b, s * per_step + p], 0, 0), p=p)
    grid_spec = pltpu.PrefetchScalarGridSpec(
        num_scalar_prefetch=1, grid=(B, n_pages // per_step),
        in_specs=([pl.BlockSpec((1, PAGE_SIZE, KV_LORA), page_map(p)) for p in range(per_step)]
                  + [pl.BlockSpec((1, PAGE_SIZE, QK_ROPE), page_map(p)) for p in range(per_step)]
                  + [pl.BlockSpec((1, H, KV_LORA), blk3), pl.BlockSpec((1, H, QK_ROPE), blk3),
                     pl.BlockSpec((1, 1, KV_LORA), blk3), pl.BlockSpec((1, 1, QK_ROPE), blk3)]),
        out_specs=pl.BlockSpec((1, H, KV_LORA), blk3),
        scratch_shapes=[pltpu.VMEM((H, 1), F32), pltpu.VMEM((H, 1), F32), pltpu.VMEM((H, KV_LORA), F32)])
    return pl.pallas_call(
        functools.partial(_mla_decode_kernel, per_step=per_step),
        out_shape=jax.ShapeDtypeStruct((B, H, KV_LORA), F32), grid_spec=grid_spec,
        compiler_params=_cp("parallel", "arbitrary"), name="mla_decode",
    )(pages, *([cache_ckv] * per_step), *([cache_kr] * per_step), q_lat, q_rope, c_new, r_new)


def _attn_moba_kernel(q_ref, kv_ref, o_ref, k_sc, v_sc, mean_sc, *, T, tq, nb, n_top):
    i = pl.program_id(2)
    t0 = i * tq
    R = MOBA_HEADS // MOBA_KV_HEADS
    dh = MOBA_HEAD_DIM
    tk = MOBA_BLOCK
    nbp = mean_sc.shape[0]

    @pl.when(i == 0)
    def _():
        k_sc[...] = kv_ref[:, :dh].astype(BF16)
        v_sc[...] = kv_ref[:, dh:].astype(BF16)
        mean_sc[...] = jnp.zeros(mean_sc.shape, F32)
        for j in range(T // tk):
            mean_sc[j:j + 1, :] = jnp.sum(kv_ref[j * tk:(j + 1) * tk, :dh], axis=0, keepdims=True) / tk

    tpos = t0 + _iota((tq, 1), 0)
    cur = tpos // tk
    jb = _iota((tq, nbp), 1)
    means = mean_sc[...].astype(BF16)
    hi = (t0 + tq + tk - 1) // tk
    for r in range(R):
        q = q_ref[:, r * dh:(r + 1) * dh]
        block_mask = jnp.where(jb == cur, 1.0, 0.0)
        if n_top > 0:
            gm = jnp.where(jb < cur, _dot_nt(q.astype(BF16), means), NEG)
            rank = _rank_before(gm, nb, jb)
            block_mask = jnp.where(jb < cur, jnp.where(rank < n_top, 1.0, 0.0), block_mask)

        def valid(kt, kpos, block_mask=block_mask):
            picked = jnp.sum(jnp.where(jb == kt, block_mask, 0.0), axis=-1, keepdims=True)
            return jnp.where(kpos <= tpos, picked, 0.0) > 0.5

        qh = (q * dh ** -0.5).astype(BF16)
        o_ref[:, r * dh:(r + 1) * dh] = _sweep(qh, k_sc, v_sc, slice(None), 0, hi, tk, valid)


def _attn_moba(proj, B, T):
    KH = MOBA_KV_HEADS
    dh = MOBA_HEAD_DIM
    tq = min(T, 128)
    nq = T // tq
    nb = (T - 1) // MOBA_BLOCK
    n_top = min(MOBA_TOP, nb)
    qw = MOBA_HEADS * dh // KH
    kv0 = MOBA_HEADS * dh // LANES
    nbp = max(T // MOBA_BLOCK, 8)
    return pl.pallas_call(
        functools.partial(_attn_moba_kernel, T=T, tq=tq, nb=nb, n_top=n_top),
        out_shape=jax.ShapeDtypeStruct((B * T, MOBA_HEADS * dh), F32), grid=(B, KH, nq),
        in_specs=[pl.BlockSpec((tq, qw), lambda b, h, i: (b * nq + i, h)),
                  pl.BlockSpec((T, 2 * dh), lambda b, h, i: (b, kv0 + h))],
        out_specs=pl.BlockSpec((tq, qw), lambda b, h, i: (b * nq + i, h)),
        scratch_shapes=[pltpu.VMEM((T, dh), BF16), pltpu.VMEM((T, dh), BF16), pltpu.VMEM((nbp, dh), F32)],
        compiler_params=_cp("parallel", "parallel", "arbitrary"), name="attn_moba",
    )(proj, proj)


def _moba_means_kernel(*refs, per_step):
    pages, o_ref = refs[1:1 + per_step], refs[1 + per_step]
    ppb = MOBA_BLOCK // PAGE_SIZE
    for j in range(per_step // ppb):
        tot = jnp.sum(pages[j * ppb][0], axis=0, keepdims=True)
        for p in range(1, ppb):
            tot = tot + jnp.sum(pages[j * ppb + p][0], axis=0, keepdims=True)
        o_ref[0, j:j + 1, :] = tot / MOBA_BLOCK


def _moba_means(pages, cache, per_step):
    B, n_pages = pages.shape
    width = cache.shape[2]
    ppb = MOBA_BLOCK // PAGE_SIZE
    grid_spec = pltpu.PrefetchScalarGridSpec(
        num_scalar_prefetch=1, grid=(B, n_pages // per_step),
        in_specs=[pl.BlockSpec((1, PAGE_SIZE, width), functools.partial(
            lambda b, s, pt, p: (pt[b, s * per_step + p], 0, 0), p=p)) for p in range(per_step)],
        out_specs=pl.BlockSpec((1, per_step // ppb, width), lambda b, s, pt: (b, s, 0)))
    return pl.pallas_call(
        functools.partial(_moba_means_kernel, per_step=per_step),
        out_shape=jax.ShapeDtypeStruct((B, n_pages // ppb, width), F32), grid_spec=grid_spec,
        compiler_params=_cp("parallel", "parallel"), name="moba_means",
    )(pages, *([cache] * per_step))


def _moba_dec_score_kernel(q_ref, mean_ref, o_ref, *, nb):
    KH = MOBA_KV_HEADS
    R = MOBA_HEADS // KH
    dh = MOBA_HEAD_DIM
    pad = o_ref.shape[2] - nb
    for kh in range(KH):
        mk = mean_ref[0, :, 2 * kh * dh:(2 * kh + 1) * dh].astype(BF16)
        o_ref[0, kh * R:(kh + 1) * R, :nb] = _dot_nt(q_ref[0, kh * R:(kh + 1) * R, :].astype(BF16), mk)
        if pad:
            o_ref[0, kh * R:(kh + 1) * R, nb:] = jnp.full((R, pad), -jnp.inf, F32)


def _moba_dec_score(q3, means):
    B, H, dh = q3.shape
    nb = means.shape[1]
    nl = -(-nb // LANES) * LANES
    blk3 = lambda b: (b, 0, 0)
    return pl.pallas_call(
        functools.partial(_moba_dec_score_kernel, nb=nb),
        out_shape=jax.ShapeDtypeStruct((B, H, nl), F32), grid=(B,),
        in_specs=[pl.BlockSpec((1, H, dh), blk3), pl.BlockSpec((1,) + means.shape[1:], blk3)],
        out_specs=pl.BlockSpec((1, H, nl), blk3), compiler_params=_cp("parallel"), name="moba_dec_score",
    )(q3, means)


def _moba_dec_attn_kernel(*refs, n_blk):
    R = MOBA_HEADS // MOBA_KV_HEADS
    dh = MOBA_HEAD_DIM
    pages = refs[1:1 + R * n_blk]
    q_ref, new_ref, o_ref = refs[1 + R * n_blk:]
    k_new = new_ref[0, :, :dh]
    v_new = new_ref[0, :, dh:]
    for r in range(R):
        q = q_ref[0, 0, r:r + 1, :] * dh ** -0.5
        kcat = jnp.concatenate([pages[r * n_blk + n][:, :dh] for n in range(n_blk)], axis=0).astype(BF16)
        vcat = jnp.concatenate([pages[r * n_blk + n][:, dh:] for n in range(n_blk)], axis=0).astype(BF16)
        s = _dot_nt(q.astype(BF16), kcat)
        s_new = jnp.sum(q * k_new, axis=-1, keepdims=True)
        m = jnp.maximum(jnp.max(s, axis=-1, keepdims=True), s_new)
        e = jnp.exp(s - m)
        e_new = jnp.exp(s_new - m)
        o_ref[0, 0, r:r + 1, :] = ((_dot(e.astype(BF16), vcat) + e_new * v_new)
                                   / (jnp.sum(e, axis=-1, keepdims=True) + e_new))


def _moba_dec_attn(page_ids, cache_rows, q4, new_kv):
    B, KH, R, dh = q4.shape
    n_blk = page_ids.shape[1] // (KH * R)
    page_specs = [pl.BlockSpec((PAGE_SIZE, 2 * dh), functools.partial(
        lambda b, h, pg, n: (pg[b, h * R * n_blk + n], h), n=n)) for n in range(R * n_blk)]
    grid_spec = pltpu.PrefetchScalarGridSpec(
        num_scalar_prefetch=1, grid=(B, KH),
        in_specs=page_specs + [pl.BlockSpec((1, 1, R, dh), lambda b, h, pg: (b, h, 0, 0)),
                               pl.BlockSpec((1, 1, 2 * dh), lambda b, h, pg: (b, 0, h))],
        out_specs=pl.BlockSpec((1, 1, R, dh), lambda b, h, pg: (b, h, 0, 0)))
    return pl.pallas_call(
        functools.partial(_moba_dec_attn_kernel, n_blk=n_blk),
        out_shape=jax.ShapeDtypeStruct(q4.shape, F32), grid_spec=grid_spec,
        compiler_params=_cp("parallel", "parallel"), name="moba_dec_attn",
    )(page_ids, *([cache_rows] * (R * n_blk)), q4, new_kv)


def _router_kernel(x_ref, w_ref, b_ref, o_ref):
    s = jax.nn.sigmoid(_dot_exact(x_ref[...], w_ref[...]))
    lane = _iota(s.shape, 1)
    x = jnp.where(lane < N_EXPERTS, s + b_ref[...], -jnp.inf)
    chosen = jnp.zeros(s.shape, F32)
    for _ in range(TOP_K):
        m = jnp.max(x, axis=-1, keepdims=True)
        idx = jnp.min(jnp.where(x == m, lane, LANES), axis=-1, keepdims=True)
        hit = lane == idx
        chosen = jnp.where(hit, 1.0, chosen)
        x = jnp.where(hit, -jnp.inf, x)
    w = chosen * s
    o_ref[...] = w / jnp.sum(w, axis=-1, keepdims=True) * ROUTED_SCALE


def _moe_router(x, w_router, b_router):
    M, K = x.shape
    tm = min(M, 256)
    row = lambda i: (i, 0)
    fix = lambda i: (0, 0)
    return pl.pallas_call(
        _router_kernel, out_shape=jax.ShapeDtypeStruct((M, LANES), F32), grid=(M // tm,),
        in_specs=[pl.BlockSpec((tm, K), row), pl.BlockSpec((K, LANES), fix), pl.BlockSpec((1, LANES), fix)],
        out_specs=pl.BlockSpec((tm, LANES), row), compiler_params=_cp("parallel"), name="moe_router",
    )(x, w_router, b_router)


def _moe_kernel(x_ref, gate_ref, w1_ref, w2_ref, wsgu_ref, wsd_ref, g_ref, b_ref, o_ref, xb_sc, acc_sc):
    e = pl.program_id(1)
    pair = 2 * D_EXPERT

    @pl.when(e == 0)
    def _():
        xb = x_ref[...].astype(BF16)
        xb_sc[...] = xb
        gu = _dot(xb, wsgu_ref[...])
        acc_sc[...] = _dot((_silu(gu[:, :D_SHARED]) * gu[:, D_SHARED:]).astype(BF16), wsd_ref[...])

    gu = _dot(xb_sc[...], w1_ref[0])
    pick = jnp.where(_iota((LANES, pair), 0) == 2 * e + _iota((LANES, pair), 1) // D_EXPERT, 1.0, 0.0).astype(BF16)
    gate = gate_ref[...]
    gate_hi = gate.astype(BF16)
    gate_lo = (gate - gate_hi.astype(F32)).astype(BF16)
    gw = _dot(gate_hi, pick) + _dot(gate_lo, pick)
    h = _silu(gu[:, :pair]) * gu[:, pair:] * gw
    acc_sc[...] += _dot(h.astype(BF16), w2_ref[0])

    @pl.when(e == pl.num_programs(1) - 1)
    def _():
        o_ref[...] = _layer_norm(ALPHA * x_ref[...] + acc_sc[...], g_ref[...], b_ref[...])


def _moe(x, gate, w1, w2, wsgu, wsd, g, b):
    M, D = x.shape
    tm = min(M, 1024)
    npair = w1.shape[0]
    row = lambda i, e: (i, 0)
    fix = lambda i, e: (0, 0)
    return pl.pallas_call(
        _moe_kernel, out_shape=jax.ShapeDtypeStruct((M, D), F32), grid=(M // tm, npair),
        in_specs=[pl.BlockSpec((tm, D), row), pl.BlockSpec((tm, LANES), row),
                  pl.BlockSpec((1,) + w1.shape[1:], lambda i, e: (e, 0, 0)),
                  pl.BlockSpec((1,) + w2.shape[1:], lambda i, e: (e, 0, 0)),
                  pl.BlockSpec(wsgu.shape, fix), pl.BlockSpec(wsd.shape, fix),
                  pl.BlockSpec((1, D), fix), pl.BlockSpec((1, D), fix)],
        out_specs=pl.BlockSpec((tm, D), row),
        scratch_shapes=[pltpu.VMEM((tm, D), BF16), pltpu.VMEM((tm, D), F32)],
        compiler_params=_cp("parallel", "arbitrary"), name="moe",
    )(x, gate, w1, w2, wsgu, wsd, g.reshape(1, D), b.reshape(1, D))


def _pad_cols(w, n):
    return jnp.pad(w, ((0, 0), (0, n - w.shape[1])))


def _block_diag(blocks):
    n, a, b = blocks.shape
    eye = jnp.eye(n, dtype=blocks.dtype)
    return (eye[:, None, :, None] * blocks[:, :, None, :]).reshape(n * a, n * b)


def _nsa_weights(w_in, cmp_pe, cmp_w1, cmp_b1, cmp_w2, cmp_b2, w_o):
    H, G, dh = NSA_HEADS, NSA_KV_HEADS, NSA_HEAD_DIM
    R = H // G
    nq, nkv = H * dh, 6 * G * dh
    wq, wkv = w_in[:, :nq], w_in[:, nq:nq + nkv]
    wg = w_in[:, nq + nkv:].reshape(-1, 3, G, R).transpose(0, 2, 1, 3).reshape(-1, G, 3 * R)
    wg = jnp.pad(wg, ((0, 0), (0, 0), (0, LANES - 3 * R))).reshape(-1, G * LANES)
    w_ext = jnp.concatenate([wq, wq, wkv, wg], axis=1).astype(BF16)
    w1 = cmp_w1.reshape(2, CMP_STRIDE, 2, dh, CMP_HIDDEN)
    eye_k = jnp.eye(2, dtype=F32)
    w_lohi = (w1.transpose(1, 2, 3, 0, 4)[:, :, :, :, None, :] * eye_k[None, :, None, None, :, None])
    w_lohi = w_lohi.reshape(CMP_STRIDE, 2 * dh, 2 * 2 * CMP_HIDDEN).astype(BF16)
    pe_rows = jnp.broadcast_to(cmp_pe[:, None], (CMP_BLOCK, G, 2, dh)).reshape(1, CMP_BLOCK, G * 2 * dh)
    b1 = jnp.broadcast_to(cmp_b1[None], (G, 2, CMP_HIDDEN)).reshape(1, -1)
    b2 = jnp.broadcast_to(cmp_b2[None], (G, 2, dh)).reshape(1, -1)
    w2 = _block_diag(jnp.tile(cmp_w2, (G, 1, 1))).astype(BF16)
    return dict(w_ext=w_ext, w_lohi=w_lohi, pe_rows=pe_rows, b1=b1, b2=b2, w2=w2, w_o=w_o.astype(BF16))


def _nsa_tables(pos):
    H, G = NSA_HEADS, NSA_KV_HEADS
    dh = NSA_HEAD_DIM
    kv_rot = [('r', 1), ('n', dh)] * G
    layout = ([('n', H * dh), ('r', H), ('n', 2 * G * dh)] + kv_rot + kv_rot + [('n', G * LANES)])
    return _tables(pos, dh // 2, layout)


def _mla_weights(w_dn, g_q, w_uq, g_kv, w_uk, w_uv, w_o):
    H = MLA_HEADS
    pad = MLA_SLOT - QK_NOPE - QK_ROPE
    w_dn_p = _pad_cols(w_dn, Q_LORA + KV_LORA + LANES).astype(BF16)
    wq = jnp.pad(w_uq.reshape(Q_LORA, H, QK_NOPE + QK_ROPE), ((0, 0), (0, 0), (0, pad)))
    w_q = wq.reshape(Q_LORA, H * MLA_SLOT).astype(BF16)
    wk_c = jnp.pad(w_uk, ((0, 0), (0, 0), (0, MLA_SLOT - QK_NOPE))).reshape(KV_LORA, H * MLA_SLOT)
    eye = jnp.pad(jnp.eye(QK_ROPE, dtype=F32), ((0, 0), (QK_NOPE, pad)))
    wk_r = jnp.tile(eye, (1, H))
    wk = jnp.concatenate([wk_c, wk_r, jnp.zeros((LANES - QK_ROPE, H * MLA_SLOT), F32)], axis=0)
    wv = jnp.pad(w_uv.reshape(KV_LORA, H * V_DIM), ((0, LANES), (0, 0)))
    w_kv = jnp.concatenate([wk, wv], axis=1).astype(BF16)
    absorb = jnp.pad(w_uk.transpose(1, 2, 0), ((0, 0), (0, MLA_SLOT - QK_NOPE), (0, 0)))
    w_absorb = _block_diag(absorb).astype(BF16)
    w_unabsorb = _block_diag(w_uv.transpose(1, 0, 2)).astype(BF16)
    return dict(w_dn=w_dn_p, g_q=g_q, g_kv=g_kv, w_q=w_q, w_kv=w_kv, w_absorb=w_absorb,
                w_unabsorb=w_unabsorb, w_o=w_o.astype(BF16))


def _mla_q_tables(pos):
    half = QK_ROPE // 2
    ua, ub = _rope_unit(pos, half)
    T = pos.shape[0]
    pad = MLA_SLOT - QK_NOPE - QK_ROPE
    a = jnp.concatenate([jnp.ones((T, QK_NOPE), F32), ua, jnp.ones((T, pad), F32)], axis=1)
    b = jnp.concatenate([jnp.zeros((T, QK_NOPE), F32), ub, jnp.zeros((T, pad), F32)], axis=1)
    return jnp.tile(a, (1, MLA_HEADS)), jnp.tile(b, (1, MLA_HEADS))


def _mla_kr_tables(pos):
    return _tables(pos, QK_ROPE // 2, [('r', 1), ('n', LANES - QK_ROPE)])


def _moba_weights(w_in, w_o):
    return dict(w_in=w_in.astype(BF16), w_o=w_o.astype(BF16))


def _moba_tables(pos):
    dh = MOBA_HEAD_DIM
    return _tables(pos, dh // 2, [('r', MOBA_HEADS)] + [('r', 1), ('n', dh)] * MOBA_KV_HEADS)


def _moe_weights(w_router, b_router, w_gate, w_up, w_down, ws_gate, ws_up, ws_down):
    E, D, F = w_gate.shape
    wg = w_gate.reshape(E // 2, 2, D, F).transpose(0, 2, 1, 3).reshape(E // 2, D, 2 * F)
    wu = w_up.reshape(E // 2, 2, D, F).transpose(0, 2, 1, 3).reshape(E // 2, D, 2 * F)
    w1 = jnp.concatenate([wg, wu], axis=2).astype(BF16)
    w2 = w_down.reshape(E // 2, 2 * F, D).astype(BF16)
    return dict(w_router=_pad_cols(w_router, LANES), b_router=_pad_cols(b_router.reshape(1, E), LANES),
                w1=w1, w2=w2, wsgu=jnp.concatenate([ws_gate, ws_up], axis=1).astype(BF16),
                wsd=ws_down.astype(BF16))


def _nsa_cmp_tokens(rows_lohi, w):
    pe_lohi = _cmp_lohi(w['pe_rows'], w['w_lohi'])
    return _cmp_combine(rows_lohi, pe_lohi, w['b1'], w['w2'], w['b2'])


def _nsa_prompt(h, B, T, w, tabs):
    G, dh = NSA_KV_HEADS, NSA_HEAD_DIM
    proj = _proj(h, w['w_ext'], tabs, dh // 2)
    kv0 = 2 * NSA_HEADS * dh
    width = 2 * G * dh
    lohi = _cmp_lohi(proj.reshape(B, T, -1), w['w_lohi'], col_block=kv0 // LANES)
    cmp_tok = _nsa_cmp_tokens(lohi, w)
    y = _attn_nsa(proj, cmp_tok, B, T)
    kv = proj[:, kv0:kv0 + 3 * width].reshape(B, T, 3, G, 2, dh)
    return y, (kv[:, :, 0], kv[:, :, 1], kv[:, -min(WINDOW, T):, 2])


def _nsa_sample(h, past_len, w, tabs, cache_cmp, cache_sel, win_state, slot, page_table):
    B = h.shape[0]
    G, H, dh = NSA_KV_HEADS, NSA_HEADS, NSA_HEAD_DIM
    R = H // G
    n_pool = cache_cmp.shape[1]
    width = 2 * G * dh
    proj = _proj(h, w['w_ext'], tabs, dh // 2)
    pages = page_table + slot * n_pool
    lohi = _cmp_lohi_paged(cache_cmp.reshape(-1, PAGE_SIZE, width), pages, w['w_lohi'],
                           per_step=min(32, page_table.shape[1]))
    cmp_tok = _nsa_cmp_tokens(lohi, w)
    n_sel = -(-(past_len + 1) // SEL_BLOCK)
    n_top = min(SEL_TOP, n_sel)
    q3 = proj[:, :H * dh].reshape(B, H, dh)
    qr3 = proj[:, H * dh:2 * H * dh].reshape(B, H, dh)
    o_cmp, imp = _nsa_dec_cmp(q3, cmp_tok, past_len, n_sel)
    idx = _topk_idx(imp.reshape(B * G, -1), n_top)[:, :n_top].reshape(B, G * n_top)
    n_past_blocks = past_len // SEL_BLOCK
    per_page = PAGE_SIZE // SEL_BLOCK
    past_idx = jnp.minimum(idx, n_past_blocks - 1)
    page = jnp.take_along_axis(pages, past_idx // per_page, axis=1)
    row_blocks = page * per_page + past_idx % per_page
    kv0 = 2 * H * dh
    new_kv = proj[:, kv0:kv0 + 3 * width].reshape(B, 1, 3 * width)
    gl3 = proj[:, kv0 + 3 * width:].reshape(B, G, LANES)[:, :, :3 * R].reshape(B, G, 3, R)
    gl3 = gl3.transpose(0, 1, 3, 2).reshape(B, H, 3)
    Wn = win_state.shape[2]
    win_flat = win_state.reshape(-1, Wn, width)
    o = _nsa_dec_attn(idx, row_blocks, cache_sel.reshape(-1, width), qr3, new_kv, win_flat, slot * B,
                      o_cmp, gl3, n_past_blocks)
    kv = new_kv.reshape(B, 1, 3, G, 2, dh)
    new_win = jnp.concatenate([win_state[slot], kv[:, :, 2]], axis=1)[:, -Wn:]
    return o.reshape(B, H * dh), (kv[:, :, 0], kv[:, :, 1], new_win)


def _mla_front(h, w, q_tabs, kr_tabs):
    dn = _proj(h, w['w_dn'])
    c_q, ckr = _mla_post(dn, w['g_q'], w['g_kv'], kr_tabs)
    q_ext = _proj(c_q, w['w_q'], q_tabs, QK_ROPE // 2)
    return q_ext, ckr


def _mla_prompt(h, B, T, w, q_tabs, kr_tabs):
    q_ext, ckr = _mla_front(h, w, q_tabs, kr_tabs)
    kv_ext = _proj(ckr, w['w_kv'])
    y = _attn_mla(q_ext, kv_ext, B, T)
    return y, (ckr[:, :KV_LORA].reshape(B, T, KV_LORA), ckr[:, KV_LORA:KV_LORA + QK_ROPE].reshape(B, T, QK_ROPE))


def _mla_sample(h, w, q_tabs, kr_tabs, cache_ckv, cache_kr, slot, page_table):
    B = h.shape[0]
    H = MLA_HEADS
    n_pool = cache_ckv.shape[1]
    q_ext, ckr = _mla_front(h, w, q_tabs, kr_tabs)
    q_lat = _proj(q_ext, w['w_absorb']).reshape(B, H, KV_LORA)
    q_rope = q_ext.reshape(B, H, MLA_SLOT)[:, :, QK_NOPE:QK_NOPE + QK_ROPE]
    c_new = ckr[:, :KV_LORA].reshape(B, 1, KV_LORA)
    r_new = ckr[:, KV_LORA:KV_LORA + QK_ROPE].reshape(B, 1, QK_ROPE)
    pages = page_table + slot * n_pool
    o_lat = _mla_decode(pages, cache_ckv.reshape(-1, PAGE_SIZE, KV_LORA), cache_kr.reshape(-1, PAGE_SIZE, QK_ROPE),
                        q_lat, q_rope, c_new, r_new, per_step=min(8, page_table.shape[1]))
    y = _proj(o_lat.reshape(B, H * KV_LORA), w['w_unabsorb'])
    return y, (c_new, r_new)


def _moba_prompt(h, B, T, w, tabs):
    KH, dh = MOBA_KV_HEADS, MOBA_HEAD_DIM
    proj = _proj(h, w['w_in'], tabs, dh // 2)
    y = _attn_moba(proj, B, T)
    return y, proj[:, MOBA_HEADS * dh:].reshape(B, T, KH, 2, dh)


def _moba_sample(h, past_len, w, tabs, cache, slot, page_table):
    B = h.shape[0]
    H, KH, dh = MOBA_HEADS, MOBA_KV_HEADS, MOBA_HEAD_DIM
    R = H // KH
    assert past_len % MOBA_BLOCK == 0 and past_len // MOBA_BLOCK >= MOBA_TOP
    n_pool = cache.shape[1]
    width = KH * 2 * dh
    ppb = MOBA_BLOCK // PAGE_SIZE
    proj = _proj(h, w['w_in'], tabs, dh // 2)
    pages = page_table + slot * n_pool
    means = _moba_means(pages, cache.reshape(-1, PAGE_SIZE, width), per_step=min(16, page_table.shape[1]))
    q3 = proj[:, :H * dh].reshape(B, H, dh)
    scores = _moba_dec_score(q3, means)
    idx = _topk_idx(scores.reshape(B * H, -1), MOBA_TOP)[:, :MOBA_TOP].reshape(B, H * MOBA_TOP)
    page_ids = jnp.take_along_axis(pages, (idx[:, :, None] * ppb + jnp.arange(ppb)).reshape(B, -1), axis=1)
    new_kv = proj[:, H * dh:].reshape(B, 1, width)
    o = _moba_dec_attn(page_ids, cache.reshape(-1, width), q3.reshape(B, KH, R, dh), new_kv)
    return o.reshape(B, H * dh), new_kv.reshape(B, 1, KH, 2, dh)


def _moe_layer(h, w, g, b):
    gate = _moe_router(h, w['w_router'], w['b_router'])
    return _moe(h, gate, w['w1'], w['w2'], w['wsgu'], w['wsd'], g, b)


def kernel(x_prompt, x_sample, cache_nsa_cmp, cache_nsa_sel, state_nsa_win, cache_mla_ckv, cache_mla_krope,
           cache_moba_kv, page_table, nsa_w_in, nsa_cmp_pe, nsa_cmp_w1, nsa_cmp_b1, nsa_cmp_w2, nsa_cmp_b2,
           nsa_w_o, mla_w_dn, mla_g_q, mla_w_uq, mla_g_kv, mla_w_uk, mla_w_uv, mla_w_o, moba_w_in, moba_w_o,
           ln1_g, ln1_b, ln2_g, ln2_b, moe_w_router, moe_b_router, moe_w_gate, moe_w_up, moe_w_down,
           moe_ws_gate, moe_ws_up, moe_ws_down):
    B, T, D = x_prompt.shape
    Bs, Ts, _ = x_sample.shape
    assert Ts == 1
    past_len = page_table.shape[1] * PAGE_SIZE
    assert state_nsa_win.shape[2] == WINDOW and past_len >= WINDOW
    pos_p = jnp.arange(T, dtype=I32)
    pos_s = jnp.full((Bs,), past_len, dtype=I32)
    hp = x_prompt.reshape(B * T, D)
    hs = x_sample.reshape(Bs, D)
    outs = {k: [] for k in ('cmp_p', 'cmp_s', 'sel_p', 'sel_s', 'win_p', 'win_s',
                            'ckv_p', 'ckv_s', 'kr_p', 'kr_s', 'mb_p', 'mb_s')}
    for i in range(DEPTH):
        kind, slot = i % N_MIXERS, i // N_MIXERS
        if kind == MIX_NSA:
            w = _nsa_weights(nsa_w_in[slot], nsa_cmp_pe[slot], nsa_cmp_w1[slot], nsa_cmp_b1[slot],
                             nsa_cmp_w2[slot], nsa_cmp_b2[slot], nsa_w_o[slot])
            yp, (a_p, b_p, c_p) = _nsa_prompt(hp, B, T, w, _nsa_tables(pos_p))
            ys, (a_s, b_s, c_s) = _nsa_sample(hs, past_len, w, _nsa_tables(pos_s), cache_nsa_cmp, cache_nsa_sel,
                                              state_nsa_win, slot, page_table)
            outs['cmp_p'].append(a_p); outs['cmp_s'].append(a_s)
            outs['sel_p'].append(b_p); outs['sel_s'].append(b_s)
            outs['win_p'].append(c_p); outs['win_s'].append(c_s)
        elif kind == MIX_MLA:
            w = _mla_weights(mla_w_dn[slot], mla_g_q[slot], mla_w_uq[slot], mla_g_kv[slot], mla_w_uk[slot],
                             mla_w_uv[slot], mla_w_o[slot])
            yp, (a_p, b_p) = _mla_prompt(hp, B, T, w, _mla_q_tables(pos_p), _mla_kr_tables(pos_p))
            ys, (a_s, b_s) = _mla_sample(hs, w, _mla_q_tables(pos_s), _mla_kr_tables(pos_s), cache_mla_ckv,
                                         cache_mla_krope, slot, page_table)
            outs['ckv_p'].append(a_p); outs['ckv_s'].append(a_s)
            outs['kr_p'].append(b_p); outs['kr_s'].append(b_s)
        else:
            w = _moba_weights(moba_w_in[slot], moba_w_o[slot])
            yp, a_p = _moba_prompt(hp, B, T, w, _moba_tables(pos_p))
            ys, a_s = _moba_sample(hs, past_len, w, _moba_tables(pos_s), cache_moba_kv, slot, page_table)
            outs['mb_p'].append(a_p); outs['mb_s'].append(a_s)
        hp = _proj_ln(yp, w['w_o'], hp, ln1_g[i], ln1_b[i])
        hs = _proj_ln(ys, w['w_o'], hs, ln1_g[i], ln1_b[i])
        mw = _moe_weights(moe_w_router[i], moe_b_router[i], moe_w_gate[i], moe_w_up[i], moe_w_down[i],
                          moe_ws_gate[i], moe_ws_up[i], moe_ws_down[i])
        hp = _moe_layer(hp, mw, ln2_g[i], ln2_b[i])
        hs = _moe_layer(hs, mw, ln2_g[i], ln2_b[i])
    st = lambda k: jnp.stack(outs[k])
    return (hp.reshape(B, T, D), hs.reshape(Bs, Ts, D),
            st('cmp_p'), st('cmp_s'), st('sel_p'), st('sel_s'), st('win_p'), st('win_s'),
            st('ckv_p'), st('ckv_s'), st('kr_p'), st('kr_s'), st('mb_p'), st('mb_s'))
```

```python
import functools

import jax
import jax.numpy as jnp
from jax import lax
from jax.experimental import pallas as pl
from jax.experimental.pallas import tpu as pltpu

D_MODEL = 1024
DEPTH = 4
PAGE_SIZE = 128
N_MIXERS = 3
MIX_NSA, MIX_MLA, MIX_MOBA = 0, 1, 2

ALPHA = (2 * DEPTH) ** 0.25
LN_EPS = 1e-5
RMS_EPS = 1e-6
ROPE_THETA = 10000.0
NEG = -1e30
FORCE = 1e9

NSA_HEADS = 16
NSA_KV_HEADS = 2
NSA_HEAD_DIM = 64
CMP_BLOCK = 32
CMP_STRIDE = 16
CMP_HIDDEN = 128
SEL_BLOCK = 64
SEL_RATIO = SEL_BLOCK // CMP_STRIDE
SEL_TOP = 16
WINDOW = 512

MLA_HEADS = 16
Q_LORA = 384
KV_LORA = 256
QK_NOPE = 64
QK_ROPE = 32
V_DIM = 64
MLA_SLOT = 128

MOBA_HEADS = 16
MOBA_KV_HEADS = 4
MOBA_HEAD_DIM = 64
MOBA_BLOCK = 256
MOBA_TOP = 3

N_EXPERTS = 64
TOP_K = 8
D_EXPERT = 128
D_SHARED = 128
ROUTED_SCALE = 2.5

LANES = 128
FLASH_CHAINS = 1
MOE_GROUP = 4
VMEM_LIMIT = 48 * 1024 * 1024

F32 = jnp.float32
BF16 = jnp.bfloat16
I32 = jnp.int32


def _cp(*sem):
    return pltpu.CompilerParams(dimension_semantics=sem, vmem_limit_bytes=VMEM_LIMIT)


def _dot(a, b):
    return jnp.dot(a, b, preferred_element_type=F32)


def _dot_nt(a, b):
    return lax.dot_general(a, b, (((1,), (1,)), ((), ())), preferred_element_type=F32)


def _dot_exact(a, b):
    return jnp.dot(a, b, preferred_element_type=F32, precision=lax.Precision.HIGHEST)


def _iota(shape, axis):
    return lax.broadcasted_iota(I32, shape, axis)


def _layer_norm(z, g, b):
    mu = jnp.mean(z, axis=-1, keepdims=True)
    d = z - mu
    var = jnp.mean(d * d, axis=-1, keepdims=True)
    return d * lax.rsqrt(var + LN_EPS) * g + b


def _silu(x):
    return x * jax.nn.sigmoid(x)


def _softmax_rows(s, valid):
    s = jnp.where(valid, s, NEG)
    m = jnp.max(s, axis=-1, keepdims=True)
    e = jnp.where(valid, jnp.exp(s - m), 0.0)
    l = jnp.sum(e, axis=-1, keepdims=True)
    return e / jnp.where(l > 0.0, l, 1.0)


def _rank_before(x, n_cols, lane):
    rank = jnp.zeros(x.shape, F32)
    for c in range(n_cols):
        col = x[:, c:c + 1]
        rank = rank + jnp.where(col > x, 1.0, jnp.where(col == x, jnp.where(lane > c, 1.0, 0.0), 0.0))
    return rank


def _proj_kernel(x_ref, w_ref, o_ref):
    o_ref[...] = _dot(x_ref[...].astype(BF16), w_ref[...])


def _proj_rope_kernel(x_ref, w_ref, a_ref, b_ref, o_ref, *, half, tn):
    acc = _dot(x_ref[...].astype(BF16), w_ref[...])
    lane = _iota(acc.shape, 1)
    first = (lane % (2 * half)) < half
    partner = jnp.where(first, pltpu.roll(acc, tn - half, 1), pltpu.roll(acc, half, 1))
    o_ref[...] = acc * a_ref[...] + partner * b_ref[...]


def _col_tile(n):
    return next(t for t in (512, 384, 256, 128) if n % t == 0)


def _proj(x, w, tabs=None, half=0):
    M, K = x.shape
    N = w.shape[1]
    tm = min(M, 512)
    tn = _col_tile(N)
    grid = (M // tm, N // tn)
    x_spec = pl.BlockSpec((tm, K), lambda i, j: (i, 0))
    w_spec = pl.BlockSpec((K, tn), lambda i, j: (0, j))
    o_spec = pl.BlockSpec((tm, tn), lambda i, j: (i, j))
    out_shape = jax.ShapeDtypeStruct((M, N), F32)
    if tabs is None:
        return pl.pallas_call(_proj_kernel, out_shape=out_shape, grid=grid, in_specs=[x_spec, w_spec],
                              out_specs=o_spec, compiler_params=_cp("parallel", "parallel"), name="proj")(x, w)
    a, bm = tabs
    nrb = a.shape[0] // tm
    t_spec = pl.BlockSpec((tm, tn), lambda i, j: (i % nrb, j))
    return pl.pallas_call(functools.partial(_proj_rope_kernel, half=half, tn=tn), out_shape=out_shape, grid=grid,
                          in_specs=[x_spec, w_spec, t_spec, t_spec], out_specs=o_spec,
                          compiler_params=_cp("parallel", "parallel"), name="proj_rope")(x, w, a, bm)


def _proj_ln_kernel(x_ref, w_ref, r_ref, g_ref, b_ref, o_ref):
    y = _dot(x_ref[...].astype(BF16), w_ref[...])
    o_ref[...] = _layer_norm(ALPHA * r_ref[...] + y, g_ref[...], b_ref[...])


def _proj_ln(x, w, res, g, b):
    M, K = x.shape
    N = w.shape[1]
    tm = min(M, 256)
    row = lambda i: (i, 0)
    fix = lambda i: (0, 0)
    return pl.pallas_call(
        _proj_ln_kernel, out_shape=jax.ShapeDtypeStruct((M, N), F32), grid=(M // tm,),
        in_specs=[pl.BlockSpec((tm, K), row), pl.BlockSpec((K, N), fix), pl.BlockSpec((tm, N), row),
                  pl.BlockSpec((1, N), fix), pl.BlockSpec((1, N), fix)],
        out_specs=pl.BlockSpec((tm, N), row), compiler_params=_cp("parallel"), name="proj_ln",
    )(x, w, res, g.reshape(1, N), b.reshape(1, N))


def _rope_unit(pos, half):
    inv = jnp.power(ROPE_THETA, -jnp.arange(half, dtype=F32) / half)
    ang = pos.astype(F32)[:, None] * inv
    cos, sin = jnp.cos(ang), jnp.sin(ang)
    return jnp.concatenate([cos, cos], axis=1), jnp.concatenate([-sin, sin], axis=1)


def _tables(pos, half, layout):
    ua, ub = _rope_unit(pos, half)
    T = pos.shape[0]
    a_parts, b_parts = [], []
    for kind, n in layout:
        if kind == 'r':
            a_parts.append(jnp.tile(ua, (1, n)))
            b_parts.append(jnp.tile(ub, (1, n)))
        else:
            a_parts.append(jnp.ones((T, n), F32))
            b_parts.append(jnp.zeros((T, n), F32))
    return jnp.concatenate(a_parts, axis=1), jnp.concatenate(b_parts, axis=1)


def _flash(qs, k_ref, k_cols, v_ref, m_sc, acc_sc, lo, hi, tk, bias_fn, heads):
    rows = qs.shape[0]
    tq = rows // heads
    m_sc[...] = jnp.full(m_sc.shape, NEG, F32)
    acc_sc[...] = jnp.zeros(acc_sc.shape, F32)
    rc = rows // FLASH_CHAINS

    def body(kt, carry):
        k0 = pl.multiple_of(kt * tk, tk)
        kk = k_ref[pl.ds(k0, tk), k_cols]
        vv = v_ref[pl.ds(k0, tk), :]
        bias = bias_fn(kt, k0)
        for c in range(FLASH_CHAINS):
            rs = slice(c * rc, (c + 1) * rc)
            s = _dot_nt(qs[rs], kk)
            if heads > 1:
                s = (s.reshape(rc // tq, tq, tk) + bias[None]).reshape(rc, tk)
            else:
                s = s + bias[rs]
            m_old = m_sc[rs]
            m_new = jnp.maximum(m_old, jnp.max(s, axis=-1, keepdims=True))
            e = jnp.exp(s - m_new).astype(BF16)
            acc_sc[rs] = jnp.exp(m_old - m_new) * acc_sc[rs] + _dot(e, vv)
            m_sc[rs] = m_new
        return carry

    lax.fori_loop(lo, hi, body, 0)
    return acc_sc[...]


def _stack_heads(ref, n_heads, dh, scale):
    parts = []
    for r in range(n_heads):
        pair = ref[:, (r // 2) * 2 * dh:(r // 2 + 1) * 2 * dh] * scale
        lane = _iota(pair.shape, 1)
        own = (lane < dh) if r % 2 == 0 else (lane >= dh)
        parts.append(jnp.where(own, pair, 0.0))
    return jnp.concatenate(parts, axis=0).astype(BF16)


def _unstack_heads(o_ref, heads, dh):
    for p in range(len(heads) // 2):
        even, odd = heads[2 * p], heads[2 * p + 1]
        lane = _iota(even.shape, 1)
        o_ref[:, 2 * p * dh:(2 * p + 2) * dh] = jnp.where(lane < dh, pltpu.roll(even, dh, 1), odd)


def _dup_keys(kv, dh):
    return jnp.where(_iota(kv.shape, 1) < dh, kv, pltpu.roll(kv, dh, 1))


def _ones_values(kv, dh):
    return jnp.where(_iota(kv.shape, 1) < dh, 1.0, kv)


def _cmp_lohi_kernel(*refs, n_pages, rows_per_page):
    G = NSA_KV_HEADS
    refs = refs[len(refs) - G * n_pages - 2:]
    row_refs, w_ref, o_ref = refs[:G * n_pages], refs[G * n_pages], refs[G * n_pages + 1]
    cpp = rows_per_page // CMP_STRIDE
    nh = w_ref.shape[2] // 2
    for g in range(G):
        acc = jnp.zeros((n_pages * cpp, 2 * nh), F32)
        for r in range(CMP_STRIDE):
            xr = [row_refs[g * n_pages + p][0, pl.ds(r, cpp, stride=CMP_STRIDE), :] for p in range(n_pages)]
            xr = xr[0] if n_pages == 1 else jnp.concatenate(xr, axis=0)
            acc = acc + _dot(xr.astype(BF16), w_ref[r])
        o_ref[0, :, g * nh:(g + 1) * nh] = acc[:, :nh]
        o_ref[0, :, (G + g) * nh:(G + g + 1) * nh] = acc[:, nh:]


def _cmp_lohi(rows, w_lohi, col_block=0):
    B, L = rows.shape[0], rows.shape[1]
    G = NSA_KV_HEADS
    nout = G * w_lohi.shape[2]
    row_specs = [pl.BlockSpec((1, L, LANES), functools.partial(lambda b, g: (b, 0, col_block + g), g=g))
                 for g in range(G)]
    return pl.pallas_call(
        functools.partial(_cmp_lohi_kernel, n_pages=1, rows_per_page=L),
        out_shape=jax.ShapeDtypeStruct((B, L // CMP_STRIDE, nout), F32), grid=(B,),
        in_specs=row_specs + [pl.BlockSpec(w_lohi.shape, lambda b: (0, 0, 0))],
        out_specs=pl.BlockSpec((1, L // CMP_STRIDE, nout), lambda b: (b, 0, 0)),
        compiler_params=_cp("parallel"), name="cmp_lohi",
    )(*([rows] * G), w_lohi)


def _cmp_lohi_paged(cache, pages, w_lohi, per_step):
    B, n_pages = pages.shape
    G = NSA_KV_HEADS
    nout = G * w_lohi.shape[2]
    cpp = PAGE_SIZE // CMP_STRIDE
    page_specs = [pl.BlockSpec((1, PAGE_SIZE, LANES), functools.partial(
        lambda b, s, pt, p, g: (pt[b, s * per_step + p], 0, g), p=p, g=g))
        for g in range(G) for p in range(per_step)]
    grid_spec = pltpu.PrefetchScalarGridSpec(
        num_scalar_prefetch=1, grid=(B, n_pages // per_step),
        in_specs=page_specs + [pl.BlockSpec(w_lohi.shape, lambda b, s, pt: (0, 0, 0))],
        out_specs=pl.BlockSpec((1, per_step * cpp, nout), lambda b, s, pt: (b, s, 0)))
    return pl.pallas_call(
        functools.partial(_cmp_lohi_kernel, n_pages=per_step, rows_per_page=PAGE_SIZE),
        out_shape=jax.ShapeDtypeStruct((B, n_pages * cpp, nout), F32), grid_spec=grid_spec,
        compiler_params=_cp("parallel", "parallel"), name="cmp_lohi_paged",
    )(pages, *([cache] * (G * per_step)), w_lohi)


def _cmp_combine_kernel(lohi_ref, pe_ref, b1_ref, w2_ref, b2_ref, o_ref):
    nch = lohi_ref.shape[1]
    nh = lohi_ref.shape[2] // 2
    lo = lohi_ref[0, :, :nh]
    hi_next = pltpu.roll(lohi_ref[0, :, nh:], nch - 1, 0)
    pe = pe_ref[0, 0:1, :nh] + pe_ref[0, 1:2, nh:]
    hid = jax.nn.gelu(lo + hi_next + pe + b1_ref[...])
    o_ref[0] = _dot(hid.astype(BF16), w2_ref[...]) + b2_ref[...]


def _cmp_combine(lohi, pe_lohi, b1, w2, b2):
    B, nch, n2 = lohi.shape
    nh = n2 // 2
    nout = w2.shape[1]
    fix2 = lambda b: (0, 0)
    return pl.pallas_call(
        _cmp_combine_kernel, out_shape=jax.ShapeDtypeStruct((B, nch, nout), F32), grid=(B,),
        in_specs=[pl.BlockSpec((1, nch, n2), lambda b: (b, 0, 0)), pl.BlockSpec((1, 2, n2), lambda b: (0, 0, 0)),
                  pl.BlockSpec((1, nh), fix2), pl.BlockSpec((nh, nout), fix2), pl.BlockSpec((1, nout), fix2)],
        out_specs=pl.BlockSpec((1, nch, nout), lambda b: (b, 0, 0)),
        compiler_params=_cp("parallel"), name="cmp_combine",
    )(lohi, pe_lohi, b1, w2, b2)


def _sel_importance(p_sum, n_lanes):
    nc = p_sum.shape[1]
    c = _iota((nc, n_lanes), 0)
    j = _iota((nc, n_lanes), 1)
    a = jnp.where(c >= SEL_RATIO * j - 1, jnp.where(c <= SEL_RATIO * j + SEL_RATIO - 1, 1.0, 0.0), 0.0)
    return _dot_exact(p_sum, a)


def _attn_nsa_kernel(q_ref, qr_ref, cmp_ref, sel_ref, win_ref, gl_ref, o_ref,
                     k2s, v1s, k2w, v1w, bias_ref, m_sc, acc_sc, *, T, tq, tk, n_sel, n_top):
    i = pl.program_id(2)
    t0 = i * tq
    R = NSA_HEADS // NSA_KV_HEADS
    dh = NSA_HEAD_DIM
    scale = dh ** -0.5
    rows = R * tq
    wk = min(T, WINDOW + 2 * tq)

    @pl.when(i == 0)
    def _():
        k2s[...] = _dup_keys(sel_ref[...], dh).astype(BF16)
        v1s[...] = _ones_values(sel_ref[...], dh).astype(BF16)
        k2w[...] = _dup_keys(win_ref[...], dh).astype(BF16)
        v1w[...] = _ones_values(win_ref[...], dh).astype(BF16)

    tpos = t0 + _iota((tq, 1), 0)
    cmp_tok = cmp_ref[0]
    nc = cmp_tok.shape[0]
    upper = _iota(cmp_tok.shape, 1) >= dh
    s = _dot_nt(_stack_heads(q_ref, R, dh, scale), _dup_keys(cmp_tok, dh).astype(BF16)).reshape(R, tq, nc)
    cvalid = ((_iota((1, nc), 1) * CMP_STRIDE + CMP_BLOCK - 1) <= tpos)[None]
    s = jnp.where(cvalid, s, NEG)
    e = jnp.where(cvalid, jnp.exp(s - jnp.max(s, axis=-1, keepdims=True)), 0.0)
    l = jnp.sum(e, axis=-1, keepdims=True)
    p = e / jnp.where(l > 0.0, l, 1.0)
    o_cmp = _dot(p.reshape(rows, nc).astype(BF16), jnp.where(upper, cmp_tok, 0.0).astype(BF16))
    p_sum = jnp.sum(p, axis=0)

    blk = _iota((tq, LANES), 1)
    cur = tpos // SEL_BLOCK
    imp = _sel_importance(p_sum, LANES)
    forced = jnp.where(blk == 0, 1.0, jnp.where(blk == cur, 1.0, jnp.where(blk == cur - 1, 1.0, 0.0)))
    imp = jnp.where(blk > cur, NEG, jnp.where(forced > 0.5, FORCE, imp))
    rank = _rank_before(imp, n_sel, blk)
    chosen = jnp.where(blk < n_sel, jnp.where(rank < n_top, 1.0, 0.0), 0.0)
    expand = jnp.where(_iota((LANES, T), 1) // SEL_BLOCK == _iota((LANES, T), 0), 1.0, 0.0).astype(BF16)
    key_mask = _dot(chosen.astype(BF16), expand)
    sel_bias = jnp.where(key_mask > 0.5, jnp.where(_iota((1, T), 1) <= tpos, 0.0, NEG), NEG)
    for kt in range(T // tk):
        bias_ref[kt] = sel_bias[:, kt * tk:(kt + 1) * tk]

    hi = (t0 + tq + tk - 1) // tk
    qrs = _stack_heads(qr_ref, R, dh, scale)
    acc = _flash(qrs, k2s, slice(None), v1s, m_sc, acc_sc, 0, hi, tk, lambda kt, k0: bias_ref[kt], R)
    o_sel = acc / acc[:, 0:1]
    kw0 = pl.multiple_of(jnp.clip(t0 - WINDOW, 0, T - wk), tq)
    d = tpos - (kw0 + _iota((1, wk), 1))
    win_bias = jnp.where(d >= 0, jnp.where(d <= WINDOW, 0.0, NEG), NEG)
    s = _dot_nt(qrs, k2w[pl.ds(kw0, wk), :]).reshape(R, tq, wk) + win_bias[None]
    e = jnp.exp(s - jnp.max(s, axis=-1, keepdims=True)).reshape(rows, wk).astype(BF16)
    acc = _dot(e, v1w[pl.ds(kw0, wk), :])
    o_win = acc / acc[:, 0:1]
    gate = jax.nn.sigmoid(gl_ref[...])
    heads = []
    for r in range(R):
        rs = slice(r * tq, (r + 1) * tq)
        heads.append(gate[:, r:r + 1] * o_cmp[rs] + gate[:, R + r:R + r + 1] * o_sel[rs]
                     + gate[:, 2 * R + r:2 * R + r + 1] * o_win[rs])
    _unstack_heads(o_ref, heads, dh)


def _attn_nsa(proj, cmp_tok, B, T, tk=None):
    G = NSA_KV_HEADS
    R = NSA_HEADS // G
    tq = min(T, 64)
    tk = tk or min(T, 512)
    nq = T // tq
    n_sel = -(-T // SEL_BLOCK)
    n_top = min(SEL_TOP, n_sel)
    qw = NSA_HEADS * NSA_HEAD_DIM // G
    kv0 = 2 * NSA_HEADS * NSA_HEAD_DIM // LANES
    gl0 = kv0 + 3 * G
    nch = cmp_tok.shape[1]
    dh = NSA_HEAD_DIM
    kern = functools.partial(_attn_nsa_kernel, T=T, tq=tq, tk=tk, n_sel=n_sel, n_top=n_top)
    return pl.pallas_call(
        kern, out_shape=jax.ShapeDtypeStruct((B * T, NSA_HEADS * dh), F32), grid=(B, G, nq),
        in_specs=[pl.BlockSpec((tq, qw), lambda b, g, i: (b * nq + i, g)),
                  pl.BlockSpec((tq, qw), lambda b, g, i: (b * nq + i, G + g)),
                  pl.BlockSpec((1, nch, 2 * dh), lambda b, g, i: (b, 0, g)),
                  pl.BlockSpec((T, 2 * dh), lambda b, g, i: (b, kv0 + G + g)),
                  pl.BlockSpec((T, 2 * dh), lambda b, g, i: (b, kv0 + 2 * G + g)),
                  pl.BlockSpec((tq, LANES), lambda b, g, i: (b * nq + i, gl0 + g))],
        out_specs=pl.BlockSpec((tq, qw), lambda b, g, i: (b * nq + i, g)),
        scratch_shapes=[pltpu.VMEM((T, 2 * dh), BF16)] * 4 + [
            pltpu.VMEM((T // tk, tq, tk), F32), pltpu.VMEM((R * tq, 1), F32), pltpu.VMEM((R * tq, 2 * dh), F32)],
        compiler_params=_cp("parallel", "parallel", "arbitrary"), name="attn_nsa",
    )(proj, proj, cmp_tok, proj, proj, proj)


def _nsa_dec_cmp_kernel(q_ref, cmp_ref, o_ref, imp_ref, *, t, n_sel):
    G = NSA_KV_HEADS
    R = NSA_HEADS // G
    dh = NSA_HEAD_DIM
    nc = cmp_ref.shape[1]
    nl = imp_ref.shape[2]
    cvalid = (_iota((1, nc), 1) * CMP_STRIDE + CMP_BLOCK - 1) <= t
    blk = _iota((1, nl), 1)
    cur = t // SEL_BLOCK
    for g in range(G):
        kc = cmp_ref[0, :, 2 * g * dh:(2 * g + 1) * dh].astype(BF16)
        vc = cmp_ref[0, :, (2 * g + 1) * dh:(2 * g + 2) * dh].astype(BF16)
        qg = (q_ref[0, g * R:(g + 1) * R, :] * dh ** -0.5).astype(BF16)
        p = _softmax_rows(_dot_nt(qg, kc), cvalid)
        o_ref[0, g * R:(g + 1) * R, :] = _dot(p.astype(BF16), vc)
        imp = _sel_importance(jnp.sum(p, axis=0, keepdims=True), nl)
        forced = jnp.where(blk == 0, 1.0, jnp.where(blk == cur, 1.0, jnp.where(blk == cur - 1, 1.0, 0.0)))
        imp = jnp.where(blk > cur, NEG, jnp.where(forced > 0.5, FORCE, imp))
        imp_ref[0, g:g + 1, :] = jnp.where(blk < n_sel, imp, -jnp.inf)


def _nsa_dec_cmp(q3, cmp_tok, t, n_sel):
    B = q3.shape[0]
    nl = -(-n_sel // LANES) * LANES
    nc = cmp_tok.shape[1]
    blk3 = lambda b: (b, 0, 0)
    return pl.pallas_call(
        functools.partial(_nsa_dec_cmp_kernel, t=t, n_sel=n_sel),
        out_shape=(jax.ShapeDtypeStruct(q3.shape, F32), jax.ShapeDtypeStruct((B, NSA_KV_HEADS, nl), F32)),
        grid=(B,),
        in_specs=[pl.BlockSpec((1,) + q3.shape[1:], blk3), pl.BlockSpec((1, nc, cmp_tok.shape[2]), blk3)],
        out_specs=(pl.BlockSpec((1,) + q3.shape[1:], blk3), pl.BlockSpec((1, NSA_KV_HEADS, nl), blk3)),
        compiler_params=_cp("parallel"), name="nsa_dec_cmp",
    )(q3, cmp_tok)


def _topk_idx_kernel(x_ref, o_ref, *, k):
    x = x_ref[...]
    lane = _iota(x.shape, 1)
    out_lane = _iota(o_ref.shape, 1)
    out = jnp.zeros(o_ref.shape, I32)
    big = x.shape[1]
    for n in range(k):
        m = jnp.max(x, axis=-1, keepdims=True)
        idx = jnp.min(jnp.where(x == m, lane, big), axis=-1, keepdims=True)
        out = jnp.where(out_lane == n, idx, out)
        x = jnp.where(lane == idx, -jnp.inf, x)
    o_ref[...] = out


def _topk_idx(x, k):
    rows = x.shape[0]
    return pl.pallas_call(functools.partial(_topk_idx_kernel, k=k),
                          out_shape=jax.ShapeDtypeStruct((rows, LANES), I32), name="topk_idx")(x)


def _nsa_dec_attn_kernel(*refs, n_top, n_past_blocks):
    idx_ref, rb_ref = refs[0], refs[1]
    G = NSA_KV_HEADS
    R = NSA_HEADS // G
    dh = NSA_HEAD_DIM
    blocks = refs[2:2 + G * n_top]
    qr_ref, new_ref, win_ref, ocmp_ref, gl_ref, o_ref = refs[2 + G * n_top:]
    b = pl.program_id(0)
    scale = dh ** -0.5
    per_page = PAGE_SIZE // SEL_BLOCK
    nk = n_top * PAGE_SIZE
    lane = _iota((1, nk), 1)
    for g in range(G):
        qg = qr_ref[0, g * R:(g + 1) * R, :] * scale
        qb = qg.astype(BF16)
        kt = jnp.concatenate([blocks[g * n_top + n][0, :dh, :] for n in range(n_top)], axis=1).astype(BF16)
        vt = jnp.concatenate([blocks[g * n_top + n][0, dh:, :] for n in range(n_top)], axis=1).astype(BF16)
        valid = jnp.zeros((1, nk), F32)
        for n in range(n_top):
            blk = idx_ref[b, g * n_top + n]
            flag = jnp.where(blk < n_past_blocks, 1.0, 0.0)
            in_block = jnp.where((lane % PAGE_SIZE) // SEL_BLOCK == blk % per_page, flag, 0.0)
            valid = jnp.where(lane // PAGE_SIZE == n, in_block, valid)
        valid = valid > 0.5
        k_new = new_ref[0, :, (2 * G + 2 * g) * dh:(2 * G + 2 * g + 1) * dh]
        v_new = new_ref[0, :, (2 * G + 2 * g + 1) * dh:(2 * G + 2 * g + 2) * dh]
        s = jnp.where(valid, _dot(qb, kt), NEG)
        s_new = jnp.sum(qg * k_new, axis=-1, keepdims=True)
        m = jnp.maximum(jnp.max(s, axis=-1, keepdims=True), s_new)
        e = jnp.where(valid, jnp.exp(s - m), 0.0)
        e_new = jnp.exp(s_new - m)
        o_sel = (_dot_nt(e.astype(BF16), vt) + e_new * v_new) / (jnp.sum(e, axis=-1, keepdims=True) + e_new)
        kwin = win_ref[0, 2 * g * dh:(2 * g + 1) * dh, :].astype(BF16)
        vwin = win_ref[0, (2 * g + 1) * dh:(2 * g + 2) * dh, :].astype(BF16)
        k_new = new_ref[0, :, (4 * G + 2 * g) * dh:(4 * G + 2 * g + 1) * dh]
        v_new = new_ref[0, :, (4 * G + 2 * g + 1) * dh:(4 * G + 2 * g + 2) * dh]
        s = _dot(qb, kwin)
        s_new = jnp.sum(qg * k_new, axis=-1, keepdims=True)
        m = jnp.maximum(jnp.max(s, axis=-1, keepdims=True), s_new)
        e = jnp.exp(s - m)
        e_new = jnp.exp(s_new - m)
        o_win = (_dot_nt(e.astype(BF16), vwin) + e_new * v_new) / (jnp.sum(e, axis=-1, keepdims=True) + e_new)
        gate = jax.nn.sigmoid(gl_ref[0, g * R:(g + 1) * R, :])
        o_ref[0, g * R:(g + 1) * R, :] = (gate[:, 0:1] * ocmp_ref[0, g * R:(g + 1) * R, :]
                                          + gate[:, 1:2] * o_sel + gate[:, 2:3] * o_win)


def _nsa_dec_attn(idx, sel_pages, cache_t, qr3, new_kv, win_buf, win_index0, o_cmp, gl3, n_past_blocks):
    B = qr3.shape[0]
    G = NSA_KV_HEADS
    n_top = idx.shape[1] // G
    dh = NSA_HEAD_DIM
    blk3 = lambda b, idx, rb: (b, 0, 0)
    block_specs = [pl.BlockSpec((1, 2 * dh, PAGE_SIZE), functools.partial(
        lambda b, idx, rb, n, g: (rb[b, n], g, 0), n=g * n_top + n, g=g)) for g in range(G) for n in range(n_top)]
    grid_spec = pltpu.PrefetchScalarGridSpec(
        num_scalar_prefetch=2, grid=(B,),
        in_specs=block_specs + [
            pl.BlockSpec((1,) + qr3.shape[1:], blk3), pl.BlockSpec((1,) + new_kv.shape[1:], blk3),
            pl.BlockSpec((1,) + win_buf.shape[1:], lambda b, idx, rb: (win_index0 + b, 0, 0)),
            pl.BlockSpec((1,) + o_cmp.shape[1:], blk3), pl.BlockSpec((1,) + gl3.shape[1:], blk3)],
        out_specs=pl.BlockSpec((1,) + qr3.shape[1:], blk3))
    return pl.pallas_call(
        functools.partial(_nsa_dec_attn_kernel, n_top=n_top, n_past_blocks=n_past_blocks),
        out_shape=jax.ShapeDtypeStruct(qr3.shape, F32), grid_spec=grid_spec,
        compiler_params=_cp("parallel"), name="nsa_dec_attn",
    )(idx, sel_pages, *([cache_t] * (G * n_top)), qr3, new_kv, win_buf, o_cmp, gl3)


def _mla_post_kernel(dn_ref, gq_ref, gkv_ref, a_ref, b_ref, cq_ref, ckr_ref):
    x = dn_ref[...]
    cq = x[:, :Q_LORA]
    cq_ref[...] = cq * lax.rsqrt(jnp.mean(cq * cq, axis=-1, keepdims=True) + RMS_EPS) * gq_ref[...]
    ckv = x[:, Q_LORA:Q_LORA + KV_LORA]
    ckv = ckv * lax.rsqrt(jnp.mean(ckv * ckv, axis=-1, keepdims=True) + RMS_EPS) * gkv_ref[...]
    kr = x[:, Q_LORA + KV_LORA:]
    half = QK_ROPE // 2
    first = (_iota(kr.shape, 1) % QK_ROPE) < half
    partner = jnp.where(first, pltpu.roll(kr, LANES - half, 1), pltpu.roll(kr, half, 1))
    ckr_ref[...] = jnp.concatenate([ckv, kr * a_ref[...] + partner * b_ref[...]], axis=1)


def _mla_post(dn, g_q, g_kv, tabs):
    M, N = dn.shape
    tm = min(M, 512)
    nrb = tabs[0].shape[0] // tm
    row = lambda i: (i, 0)
    fix = lambda i: (0, 0)
    tab = pl.BlockSpec((tm, LANES), lambda i: (i % nrb, 0))
    return pl.pallas_call(
        _mla_post_kernel,
        out_shape=(jax.ShapeDtypeStruct((M, Q_LORA), F32), jax.ShapeDtypeStruct((M, KV_LORA + LANES), F32)),
        grid=(M // tm,),
        in_specs=[pl.BlockSpec((tm, N), row), pl.BlockSpec((1, Q_LORA), fix), pl.BlockSpec((1, KV_LORA), fix),
                  tab, tab],
        out_specs=(pl.BlockSpec((tm, Q_LORA), row), pl.BlockSpec((tm, KV_LORA + LANES), row)),
        compiler_params=_cp("parallel"), name="mla_post",
    )(dn, g_q.reshape(1, -1), g_kv.reshape(1, -1), *tabs)


def _attn_mla_kernel(q_ref, k_ref, v_ref, o_ref, k_sc, v_sc, m_sc, acc_sc, *, tq, tk):
    i = pl.program_id(2)
    t0 = i * tq
    scale = (QK_NOPE + QK_ROPE) ** -0.5

    @pl.when(i == 0)
    def _():
        k_sc[...] = k_ref[...].astype(BF16)
        v = v_ref[...]
        lane = _iota(v.shape, 1)
        v_sc[0] = jnp.where(lane < V_DIM, v, 1.0).astype(BF16)
        v_sc[1] = jnp.where(lane < V_DIM, 1.0, v).astype(BF16)

    tpos = t0 + _iota((tq, 1), 0)
    hi = (t0 + tq + tk - 1) // tk
    causal = lambda kt, k0: jnp.where(k0 + _iota((1, tk), 1) <= tpos, 0.0, NEG)
    for hh in range(2):
        qh = (q_ref[:, hh * MLA_SLOT:(hh + 1) * MLA_SLOT] * scale).astype(BF16)
        acc = _flash(qh, k_sc, slice(hh * MLA_SLOT, (hh + 1) * MLA_SLOT), v_sc.at[hh], m_sc, acc_sc, 0, hi, tk,
                     causal, 1)
        denom = acc[:, V_DIM:V_DIM + 1] if hh == 0 else acc[:, 0:1]
        o_ref[:, hh * V_DIM:(hh + 1) * V_DIM] = acc[:, hh * V_DIM:(hh + 1) * V_DIM] / denom


def _attn_mla(q_ext, kv_ext, B, T, tk=None):
    H = MLA_HEADS
    tq = min(T, 512)
    tk = tk or min(T, 512)
    nq = T // tq
    v0 = H * MLA_SLOT // LANES
    return pl.pallas_call(
        functools.partial(_attn_mla_kernel, tq=tq, tk=tk),
        out_shape=jax.ShapeDtypeStruct((B * T, H * V_DIM), F32), grid=(B, H // 2, nq),
        in_specs=[pl.BlockSpec((tq, 2 * MLA_SLOT), lambda b, h, i: (b * nq + i, h)),
                  pl.BlockSpec((T, 2 * MLA_SLOT), lambda b, h, i: (b, h)),
                  pl.BlockSpec((T, 2 * V_DIM), lambda b, h, i: (b, v0 + h))],
        out_specs=pl.BlockSpec((tq, 2 * V_DIM), lambda b, h, i: (b * nq + i, h)),
        scratch_shapes=[pltpu.VMEM((T, 2 * MLA_SLOT), BF16), pltpu.VMEM((2, T, 2 * V_DIM), BF16),
                        pltpu.VMEM((tq, 1), F32), pltpu.VMEM((tq, 2 * V_DIM), F32)],
        compiler_params=_cp("parallel", "parallel", "arbitrary"), name="attn_mla",
    )(q_ext, kv_ext, kv_ext)


def _mla_decode_kernel(*refs, per_step):
    pt_ref = refs[0]
    ckv_pages = refs[1:1 + per_step]
    kr_pages = refs[1 + per_step:1 + 2 * per_step]
    ql_ref, qr_ref, cnew_ref, rnew_ref, o_ref, m_sc, l_sc, acc_sc = refs[1 + 2 * per_step:]
    s_idx = pl.program_id(1)
    scale = (QK_NOPE + QK_ROPE) ** -0.5
    ql = ql_ref[0] * scale
    qr = qr_ref[0] * scale

    @pl.when(s_idx == 0)
    def _():
        c_new = cnew_ref[0]
        s_new = (jnp.sum(ql * c_new, axis=-1, keepdims=True) + jnp.sum(qr * rnew_ref[0], axis=-1, keepdims=True))
        m_sc[...] = s_new
        l_sc[...] = jnp.ones(l_sc.shape, F32)
        acc_sc[...] = jnp.broadcast_to(c_new, acc_sc.shape)

    qlb = ql.astype(BF16)
    qrb = qr.astype(BF16)
    cs = [ref[0].astype(BF16) for ref in ckv_pages]
    s = jnp.concatenate([_dot_nt(qlb, c) + _dot(qrb, r[0].astype(BF16)) for c, r in zip(cs, kr_pages)], axis=1)
    m_old = m_sc[...]
    m_new = jnp.maximum(m_old, jnp.max(s, axis=-1, keepdims=True))
    alpha = jnp.exp(m_old - m_new)
    e = jnp.exp(s - m_new).astype(BF16)
    pv = _dot(e[:, :PAGE_SIZE], cs[0])
    for p in range(1, per_step):
        pv = pv + _dot(e[:, p * PAGE_SIZE:(p + 1) * PAGE_SIZE], cs[p])
    l_sc[...] = alpha * l_sc[...] + jnp.sum(e.astype(F32), axis=-1, keepdims=True)
    acc_sc[...] = alpha * acc_sc[...] + pv
    m_sc[...] = m_new

    @pl.when(s_idx == pl.num_programs(1) - 1)
    def _():
        o_ref[0] = acc_sc[...] / l_sc[...]


def _mla_decode(pages, cache_ckv, cache_kr, q_lat, q_rope, c_new, r_new, per_step):
    B, n_pages = pages.shape
    H = MLA_HEADS
    blk3 = lambda b, s, pt: (b, 0, 0)
    page_map = lambda p: functools.partial(lambda b, s, pt, p: (pt[b, s * per_step + p], 0, 0), p=p)
    grid_spec = pltpu.PrefetchScalarGridSpec(
        num_scalar_prefetch=1, grid=(B, n_pages // per_step),
        in_specs=([pl.BlockSpec((1, PAGE_SIZE, KV_LORA), page_map(p)) for p in range(per_step)]
                  + [pl.BlockSpec((1, QK_ROPE, PAGE_SIZE), page_map(p)) for p in range(per_step)]
                  + [pl.BlockSpec((1, H, KV_LORA), blk3), pl.BlockSpec((1, H, QK_ROPE), blk3),
                     pl.BlockSpec((1, 1, KV_LORA), blk3), pl.BlockSpec((1, 1, QK_ROPE), blk3)]),
        out_specs=pl.BlockSpec((1, H, KV_LORA), blk3),
        scratch_shapes=[pltpu.VMEM((H, 1), F32), pltpu.VMEM((H, 1), F32), pltpu.VMEM((H, KV_LORA), F32)])
    return pl.pallas_call(
        functools.partial(_mla_decode_kernel, per_step=per_step),
        out_shape=jax.ShapeDtypeStruct((B, H, KV_LORA), F32), grid_spec=grid_spec,
        compiler_params=_cp("parallel", "arbitrary"), name="mla_decode",
    )(pages, *([cache_ckv] * per_step), *([cache_kr] * per_step), q_lat, q_rope, c_new, r_new)


def _attn_moba_kernel(q_ref, kv_ref, o_ref, k2, v1, mean_sc, m_sc, acc_sc, *, T, tq, tk, nb, n_top):
    i = pl.program_id(2)
    t0 = i * tq
    R = MOBA_HEADS // MOBA_KV_HEADS
    dh = MOBA_HEAD_DIM
    rows = R * tq
    nbp = mean_sc.shape[0]
    bpt = tk // MOBA_BLOCK

    @pl.when(i == 0)
    def _():
        k2[...] = _dup_keys(kv_ref[...], dh).astype(BF16)
        v1[...] = _ones_values(kv_ref[...], dh).astype(BF16)
        mean_sc[...] = jnp.zeros(mean_sc.shape, F32)
        for j in range(T // MOBA_BLOCK):
            blk_rows = _dup_keys(kv_ref[j * MOBA_BLOCK:(j + 1) * MOBA_BLOCK, :], dh)
            mean_sc[j:j + 1, :] = jnp.sum(blk_rows, axis=0, keepdims=True) / MOBA_BLOCK

    tpos = t0 + _iota((rows, 1), 0) % tq
    cur = tpos // MOBA_BLOCK
    jb = _iota((rows, nbp), 1)
    qs = _stack_heads(q_ref, R, dh, dh ** -0.5)
    block_mask = jnp.where(jb == cur, 1.0, 0.0)
    if n_top > 0:
        gm = jnp.where(jb < cur, _dot_nt(qs, mean_sc[...].astype(BF16)), NEG)
        rank = _rank_before(gm, nb, jb)
        block_mask = jnp.where(jb < cur, jnp.where(rank < n_top, 1.0, 0.0), block_mask)

    def bias(kt, k0):
        lane = _iota((1, tk), 1)
        picked = jnp.zeros((rows, tk), F32)
        for j in range(bpt):
            col = jnp.sum(jnp.where(jb == kt * bpt + j, block_mask, 0.0), axis=-1, keepdims=True)
            picked = jnp.where(lane // MOBA_BLOCK == j, col, picked)
        return jnp.where(picked > 0.5, jnp.where(k0 + lane <= tpos, 0.0, NEG), NEG)

    hi = (t0 + tq + tk - 1) // tk
    acc = _flash(qs, k2, slice(None), v1, m_sc, acc_sc, 0, hi, tk, bias, 1)
    o = acc / acc[:, 0:1]
    _unstack_heads(o_ref, [o[r * tq:(r + 1) * tq] for r in range(R)], dh)


def _attn_moba(proj, B, T, tk=None):
    KH = MOBA_KV_HEADS
    R = MOBA_HEADS // KH
    dh = MOBA_HEAD_DIM
    tq = min(T, 128)
    tk = tk or min(T, 512)
    nq = T // tq
    nb = (T - 1) // MOBA_BLOCK
    n_top = min(MOBA_TOP, nb)
    qw = MOBA_HEADS * dh // KH
    kv0 = MOBA_HEADS * dh // LANES
    nbp = max(T // MOBA_BLOCK, 8)
    return pl.pallas_call(
        functools.partial(_attn_moba_kernel, T=T, tq=tq, tk=tk, nb=nb, n_top=n_top),
        out_shape=jax.ShapeDtypeStruct((B * T, MOBA_HEADS * dh), F32), grid=(B, KH, nq),
        in_specs=[pl.BlockSpec((tq, qw), lambda b, h, i: (b * nq + i, h)),
                  pl.BlockSpec((T, 2 * dh), lambda b, h, i: (b, kv0 + h))],
        out_specs=pl.BlockSpec((tq, qw), lambda b, h, i: (b * nq + i, h)),
        scratch_shapes=[pltpu.VMEM((T, 2 * dh), BF16), pltpu.VMEM((T, 2 * dh), BF16), pltpu.VMEM((nbp, 2 * dh), F32),
                        pltpu.VMEM((R * tq, 1), F32), pltpu.VMEM((R * tq, 2 * dh), F32)],
        compiler_params=_cp("parallel", "parallel", "arbitrary"), name="attn_moba",
    )(proj, proj)


def _moba_means_kernel(*refs, per_step):
    pages, o_ref = refs[1:1 + per_step], refs[1 + per_step]
    KH, dh = MOBA_KV_HEADS, MOBA_HEAD_DIM
    ppb = MOBA_BLOCK // PAGE_SIZE
    s = pl.program_id(1)

    @pl.when(s == 0)
    def _():
        o_ref[...] = jnp.zeros(o_ref.shape, F32)

    lane = _iota(o_ref.shape[1:], 1)
    out = o_ref[0]
    for j in range(per_step // ppb):
        cols = []
        for kh in range(KH):
            k_rows = slice(2 * kh * dh, (2 * kh + 1) * dh)
            tot = pages[j * ppb][0, k_rows, :]
            for p in range(1, ppb):
                tot = tot + pages[j * ppb + p][0, k_rows, :]
            cols.append(jnp.sum(tot, axis=1, keepdims=True))
        col = jnp.concatenate(cols, axis=0) / MOBA_BLOCK
        out = jnp.where(lane == s * (per_step // ppb) + j, col, out)
    o_ref[0] = out


def _moba_means(pages, cache_t, per_step):
    B, n_pages = pages.shape
    width = cache_t.shape[1]
    nk = MOBA_KV_HEADS * MOBA_HEAD_DIM
    assert n_pages * PAGE_SIZE // MOBA_BLOCK <= LANES
    grid_spec = pltpu.PrefetchScalarGridSpec(
        num_scalar_prefetch=1, grid=(B, n_pages // per_step),
        in_specs=[pl.BlockSpec((1, width, PAGE_SIZE), functools.partial(
            lambda b, s, pt, p: (pt[b, s * per_step + p], 0, 0), p=p)) for p in range(per_step)],
        out_specs=pl.BlockSpec((1, nk, LANES), lambda b, s, pt: (b, 0, 0)))
    return pl.pallas_call(
        functools.partial(_moba_means_kernel, per_step=per_step),
        out_shape=jax.ShapeDtypeStruct((B, nk, LANES), F32), grid_spec=grid_spec,
        compiler_params=_cp("parallel", "arbitrary"), name="moba_means",
    )(pages, *([cache_t] * per_step))


def _moba_dec_score_kernel(q_ref, mean_ref, o_ref, *, nb):
    KH = MOBA_KV_HEADS
    R = MOBA_HEADS // KH
    dh = MOBA_HEAD_DIM
    lane = _iota((R, LANES), 1)
    for kh in range(KH):
        mk = mean_ref[0, kh * dh:(kh + 1) * dh, :].astype(BF16)
        g = _dot(q_ref[0, kh * R:(kh + 1) * R, :].astype(BF16), mk)
        o_ref[0, kh * R:(kh + 1) * R, :] = jnp.where(lane < nb, g, -jnp.inf)


def _moba_dec_score(q3, means, nb):
    B, H, dh = q3.shape
    blk3 = lambda b: (b, 0, 0)
    return pl.pallas_call(
        functools.partial(_moba_dec_score_kernel, nb=nb),
        out_shape=jax.ShapeDtypeStruct((B, H, LANES), F32), grid=(B,),
        in_specs=[pl.BlockSpec((1, H, dh), blk3), pl.BlockSpec((1,) + means.shape[1:], blk3)],
        out_specs=pl.BlockSpec((1, H, LANES), blk3), compiler_params=_cp("parallel"), name="moba_dec_score",
    )(q3, means)


def _moba_dec_attn_kernel(*refs, n_blk):
    R = MOBA_HEADS // MOBA_KV_HEADS
    dh = MOBA_HEAD_DIM
    pages = refs[1:1 + R * n_blk]
    q_ref, new_ref, o_ref = refs[1 + R * n_blk:]
    k_new = new_ref[0, :, :dh]
    v_new = new_ref[0, :, dh:]
    for r in range(R):
        q = q_ref[0, 0, r:r + 1, :] * dh ** -0.5
        kt = jnp.concatenate([pages[r * n_blk + n][0, :dh, :] for n in range(n_blk)], axis=1).astype(BF16)
        vt = jnp.concatenate([pages[r * n_blk + n][0, dh:, :] for n in range(n_blk)], axis=1).astype(BF16)
        s = _dot(q.astype(BF16), kt)
        s_new = jnp.sum(q * k_new, axis=-1, keepdims=True)
        m = jnp.maximum(jnp.max(s, axis=-1, keepdims=True), s_new)
        e = jnp.exp(s - m)
        e_new = jnp.exp(s_new - m)
        o_ref[0, 0, r:r + 1, :] = ((_dot_nt(e.astype(BF16), vt) + e_new * v_new)
                                   / (jnp.sum(e, axis=-1, keepdims=True) + e_new))


def _moba_dec_attn(page_ids, cache_t, q4, new_kv):
    B, KH, R, dh = q4.shape
    n_blk = page_ids.shape[1] // (KH * R)
    page_specs = [pl.BlockSpec((1, 2 * dh, PAGE_SIZE), functools.partial(
        lambda b, h, pg, n: (pg[b, h * R * n_blk + n], h, 0), n=n)) for n in range(R * n_blk)]
    grid_spec = pltpu.PrefetchScalarGridSpec(
        num_scalar_prefetch=1, grid=(B, KH),
        in_specs=page_specs + [pl.BlockSpec((1, 1, R, dh), lambda b, h, pg: (b, h, 0, 0)),
                               pl.BlockSpec((1, 1, 2 * dh), lambda b, h, pg: (b, 0, h))],
        out_specs=pl.BlockSpec((1, 1, R, dh), lambda b, h, pg: (b, h, 0, 0)))
    return pl.pallas_call(
        functools.partial(_moba_dec_attn_kernel, n_blk=n_blk),
        out_shape=jax.ShapeDtypeStruct(q4.shape, F32), grid_spec=grid_spec,
        compiler_params=_cp("parallel", "parallel"), name="moba_dec_attn",
    )(page_ids, *([cache_t] * (R * n_blk)), q4, new_kv)


def _router_kernel(x_ref, w_ref, b_ref, o_ref):
    s = jax.nn.sigmoid(_dot_exact(x_ref[...], w_ref[...]))
    lane = _iota(s.shape, 1)
    x = jnp.where(lane < N_EXPERTS, s + b_ref[...], -jnp.inf)
    chosen = jnp.zeros(s.shape, F32)
    for _ in range(TOP_K):
        m = jnp.max(x, axis=-1, keepdims=True)
        idx = jnp.min(jnp.where(x == m, lane, LANES), axis=-1, keepdims=True)
        hit = lane == idx
        chosen = jnp.where(hit, 1.0, chosen)
        x = jnp.where(hit, -jnp.inf, x)
    w = chosen * s
    o_ref[...] = w / jnp.sum(w, axis=-1, keepdims=True) * ROUTED_SCALE


def _moe_router(x, w_router, b_router):
    M, K = x.shape
    tm = min(M, 256)
    row = lambda i: (i, 0)
    fix = lambda i: (0, 0)
    return pl.pallas_call(
        _router_kernel, out_shape=jax.ShapeDtypeStruct((M, LANES), F32), grid=(M // tm,),
        in_specs=[pl.BlockSpec((tm, K), row), pl.BlockSpec((K, LANES), fix), pl.BlockSpec((1, LANES), fix)],
        out_specs=pl.BlockSpec((tm, LANES), row), compiler_params=_cp("parallel"), name="moe_router",
    )(x, w_router, b_router)


def _moe_kernel(x_ref, gate_ref, w1_ref, w2_ref, wsgu_ref, wsd_ref, g_ref, b_ref, o_ref, xb_sc, acc_sc):
    e = pl.program_id(1)
    nh = MOE_GROUP * D_EXPERT

    @pl.when(e == 0)
    def _():
        xb = x_ref[...].astype(BF16)
        xb_sc[...] = xb
        gu = _dot(xb, wsgu_ref[...])
        acc_sc[...] = _dot((_silu(gu[:, :D_SHARED]) * gu[:, D_SHARED:]).astype(BF16), wsd_ref[...])

    gu = _dot(xb_sc[...], w1_ref[0])
    src = _iota((2 * LANES, nh), 0) % LANES
    pick = jnp.where(src == MOE_GROUP * e + _iota((2 * LANES, nh), 1) // D_EXPERT, 1.0, 0.0).astype(BF16)
    gate = gate_ref[...]
    gate_hi = gate.astype(BF16)
    gate_lo = (gate - gate_hi.astype(F32)).astype(BF16)
    gw = _dot(jnp.concatenate([gate_hi, gate_lo], axis=1), pick)
    h = _silu(gu[:, :nh]) * gu[:, nh:] * gw
    acc_sc[...] += _dot(h.astype(BF16), w2_ref[0])

    @pl.when(e == pl.num_programs(1) - 1)
    def _():
        o_ref[...] = _layer_norm(ALPHA * x_ref[...] + acc_sc[...], g_ref[...], b_ref[...])


def _moe(x, gate, w1, w2, wsgu, wsd, g, b):
    M, D = x.shape
    tm = min(M, 1024)
    npair = w1.shape[0]
    row = lambda i, e: (i, 0)
    fix = lambda i, e: (0, 0)
    return pl.pallas_call(
        _moe_kernel, out_shape=jax.ShapeDtypeStruct((M, D), F32), grid=(M // tm, npair),
        in_specs=[pl.BlockSpec((tm, D), row), pl.BlockSpec((tm, LANES), row),
                  pl.BlockSpec((1,) + w1.shape[1:], lambda i, e: (e, 0, 0)),
                  pl.BlockSpec((1,) + w2.shape[1:], lambda i, e: (e, 0, 0)),
                  pl.BlockSpec(wsgu.shape, fix), pl.BlockSpec(wsd.shape, fix),
                  pl.BlockSpec((1, D), fix), pl.BlockSpec((1, D), fix)],
        out_specs=pl.BlockSpec((tm, D), row),
        scratch_shapes=[pltpu.VMEM((tm, D), BF16), pltpu.VMEM((tm, D), F32)],
        compiler_params=_cp("parallel", "arbitrary"), name="moe",
    )(x, gate, w1, w2, wsgu, wsd, g.reshape(1, D), b.reshape(1, D))


def _pad_cols(w, n):
    return jnp.pad(w, ((0, 0), (0, n - w.shape[1])))


def _block_diag(blocks):
    n, a, b = blocks.shape
    eye = jnp.eye(n, dtype=blocks.dtype)
    return (eye[:, None, :, None] * blocks[:, :, None, :]).reshape(n * a, n * b)


def _nsa_weights(w_in, cmp_pe, cmp_w1, cmp_b1, cmp_w2, cmp_b2, w_o):
    H, G, dh = NSA_HEADS, NSA_KV_HEADS, NSA_HEAD_DIM
    R = H // G
    nq, nkv = H * dh, 6 * G * dh
    wq, wkv = w_in[:, :nq], w_in[:, nq:nq + nkv]
    wg = w_in[:, nq + nkv:].reshape(-1, 3, G, R).transpose(0, 2, 1, 3).reshape(-1, G, 3 * R)
    wg = jnp.pad(wg, ((0, 0), (0, 0), (0, LANES - 3 * R))).reshape(-1, G * LANES)
    w_ext = jnp.concatenate([wq, wq, wkv, wg], axis=1).astype(BF16)
    w1 = cmp_w1.reshape(2, CMP_STRIDE, 2, dh, CMP_HIDDEN)
    eye_k = jnp.eye(2, dtype=F32)
    w_lohi = (w1.transpose(1, 2, 3, 0, 4)[:, :, :, :, None, :] * eye_k[None, :, None, None, :, None])
    w_lohi = w_lohi.reshape(CMP_STRIDE, 2 * dh, 2 * 2 * CMP_HIDDEN).astype(BF16)
    pe_rows = jnp.broadcast_to(cmp_pe[:, None], (CMP_BLOCK, G, 2, dh)).reshape(1, CMP_BLOCK, G * 2 * dh)
    b1 = jnp.broadcast_to(cmp_b1[None], (G, 2, CMP_HIDDEN)).reshape(1, -1)
    b2 = jnp.broadcast_to(cmp_b2[None], (G, 2, dh)).reshape(1, -1)
    w2 = _block_diag(jnp.tile(cmp_w2, (G, 1, 1))).astype(BF16)
    return dict(w_ext=w_ext, w_lohi=w_lohi, pe_rows=pe_rows, b1=b1, b2=b2, w2=w2, w_o=w_o.astype(BF16))


def _nsa_tables(pos):
    H, G = NSA_HEADS, NSA_KV_HEADS
    dh = NSA_HEAD_DIM
    kv_rot = [('r', 1), ('n', dh)] * G
    layout = ([('n', H * dh), ('r', H), ('n', 2 * G * dh)] + kv_rot + kv_rot + [('n', G * LANES)])
    return _tables(pos, dh // 2, layout)


def _mla_weights(w_dn, g_q, w_uq, g_kv, w_uk, w_uv, w_o):
    H = MLA_HEADS
    pad = MLA_SLOT - QK_NOPE - QK_ROPE
    w_dn_p = _pad_cols(w_dn, Q_LORA + KV_LORA + LANES).astype(BF16)
    wq = jnp.pad(w_uq.reshape(Q_LORA, H, QK_NOPE + QK_ROPE), ((0, 0), (0, 0), (0, pad)))
    w_q = wq.reshape(Q_LORA, H * MLA_SLOT).astype(BF16)
    wk_c = jnp.pad(w_uk, ((0, 0), (0, 0), (0, MLA_SLOT - QK_NOPE))).reshape(KV_LORA, H * MLA_SLOT)
    eye = jnp.pad(jnp.eye(QK_ROPE, dtype=F32), ((0, 0), (QK_NOPE, pad)))
    wk_r = jnp.tile(eye, (1, H))
    wk = jnp.concatenate([wk_c, wk_r, jnp.zeros((LANES - QK_ROPE, H * MLA_SLOT), F32)], axis=0)
    wv = jnp.pad(w_uv.reshape(KV_LORA, H * V_DIM), ((0, LANES), (0, 0)))
    w_kv = jnp.concatenate([wk, wv], axis=1).astype(BF16)
    absorb = jnp.pad(w_uk.transpose(1, 2, 0), ((0, 0), (0, MLA_SLOT - QK_NOPE), (0, 0)))
    w_absorb = _block_diag(absorb).astype(BF16)
    w_unabsorb = _block_diag(w_uv.transpose(1, 0, 2)).astype(BF16)
    return dict(w_dn=w_dn_p, g_q=g_q, g_kv=g_kv, w_q=w_q, w_kv=w_kv, w_absorb=w_absorb,
                w_unabsorb=w_unabsorb, w_o=w_o.astype(BF16))


def _mla_q_tables(pos):
    half = QK_ROPE // 2
    ua, ub = _rope_unit(pos, half)
    T = pos.shape[0]
    pad = MLA_SLOT - QK_NOPE - QK_ROPE
    a = jnp.concatenate([jnp.ones((T, QK_NOPE), F32), ua, jnp.ones((T, pad), F32)], axis=1)
    b = jnp.concatenate([jnp.zeros((T, QK_NOPE), F32), ub, jnp.zeros((T, pad), F32)], axis=1)
    return jnp.tile(a, (1, MLA_HEADS)), jnp.tile(b, (1, MLA_HEADS))


def _mla_kr_tables(pos):
    return _tables(pos, QK_ROPE // 2, [('r', 1), ('n', LANES - QK_ROPE)])


def _moba_weights(w_in, w_o):
    return dict(w_in=w_in.astype(BF16), w_o=w_o.astype(BF16))


def _moba_tables(pos):
    dh = MOBA_HEAD_DIM
    return _tables(pos, dh // 2, [('r', MOBA_HEADS)] + [('r', 1), ('n', dh)] * MOBA_KV_HEADS)


def _moe_weights(w_router, b_router, w_gate, w_up, w_down, ws_gate, ws_up, ws_down):
    E, D, F = w_gate.shape
    n = MOE_GROUP
    wg = w_gate.astype(BF16).reshape(E // n, n, D, F).transpose(0, 2, 1, 3).reshape(E // n, D, n * F)
    wu = w_up.astype(BF16).reshape(E // n, n, D, F).transpose(0, 2, 1, 3).reshape(E // n, D, n * F)
    w1 = jnp.concatenate([wg, wu], axis=2)
    w2 = w_down.reshape(E // n, n * F, D).astype(BF16)
    return dict(w_router=_pad_cols(w_router, LANES), b_router=_pad_cols(b_router.reshape(1, E), LANES),
                w1=w1, w2=w2, wsgu=jnp.concatenate([ws_gate, ws_up], axis=1).astype(BF16),
                wsd=ws_down.astype(BF16))


def _feature_major(cache):
    n, pool, rows = cache.shape[:3]
    nd = cache.ndim
    return cache.transpose(0, 1, *range(3, nd), 2).reshape(n * pool, -1, rows)


def _nsa_cmp_tokens(rows_lohi, w):
    pe_lohi = _cmp_lohi(w['pe_rows'], w['w_lohi'])
    return _cmp_combine(rows_lohi, pe_lohi, w['b1'], w['w2'], w['b2'])


def _nsa_prompt(h, B, T, w, tabs):
    G, dh = NSA_KV_HEADS, NSA_HEAD_DIM
    proj = _proj(h, w['w_ext'], tabs, dh // 2)
    kv0 = 2 * NSA_HEADS * dh
    width = 2 * G * dh
    lohi = _cmp_lohi(proj.reshape(B, T, -1), w['w_lohi'], col_block=kv0 // LANES)
    cmp_tok = _nsa_cmp_tokens(lohi, w)
    y = _attn_nsa(proj, cmp_tok, B, T)
    kv = proj[:, kv0:kv0 + 3 * width].reshape(B, T, 3, G, 2, dh)
    return y, (kv[:, :, 0], kv[:, :, 1], kv[:, -min(WINDOW, T):, 2])


def _nsa_sample(h, past_len, w, tabs, cache_cmp, cache_sel, win_state, slot, page_table):
    B = h.shape[0]
    G, H, dh = NSA_KV_HEADS, NSA_HEADS, NSA_HEAD_DIM
    R = H // G
    n_pool = cache_cmp.shape[1]
    width = 2 * G * dh
    proj = _proj(h, w['w_ext'], tabs, dh // 2)
    pages = page_table + slot * n_pool
    lohi = _cmp_lohi_paged(cache_cmp.reshape(-1, PAGE_SIZE, width), pages, w['w_lohi'],
                           per_step=min(32, page_table.shape[1]))
    cmp_tok = _nsa_cmp_tokens(lohi, w)
    n_sel = -(-(past_len + 1) // SEL_BLOCK)
    n_top = min(SEL_TOP, n_sel)
    q3 = proj[:, :H * dh].reshape(B, H, dh)
    qr3 = proj[:, H * dh:2 * H * dh].reshape(B, H, dh)
    o_cmp, imp = _nsa_dec_cmp(q3, cmp_tok, past_len, n_sel)
    idx = _topk_idx(imp.reshape(B * G, -1), n_top)[:, :n_top].reshape(B, G * n_top)
    n_past_blocks = past_len // SEL_BLOCK
    per_page = PAGE_SIZE // SEL_BLOCK
    sel_pages = jnp.take_along_axis(pages, jnp.minimum(idx, n_past_blocks - 1) // per_page, axis=1)
    kv0 = 2 * H * dh
    new_kv = proj[:, kv0:kv0 + 3 * width].reshape(B, 1, 3 * width)
    gl3 = proj[:, kv0 + 3 * width:].reshape(B, G, LANES)[:, :, :3 * R].reshape(B, G, 3, R)
    gl3 = gl3.transpose(0, 1, 3, 2).reshape(B, H, 3)
    Wn = win_state.shape[2]
    o = _nsa_dec_attn(idx, sel_pages, _feature_major(cache_sel), qr3, new_kv, _feature_major(win_state), slot * B,
                      o_cmp, gl3, n_past_blocks)
    kv = new_kv.reshape(B, 1, 3, G, 2, dh)
    new_win = jnp.concatenate([win_state[slot], kv[:, :, 2]], axis=1)[:, -Wn:]
    return o.reshape(B, H * dh), (kv[:, :, 0], kv[:, :, 1], new_win)


def _mla_front(h, w, q_tabs, kr_tabs):
    dn = _proj(h, w['w_dn'])
    c_q, ckr = _mla_post(dn, w['g_q'], w['g_kv'], kr_tabs)
    q_ext = _proj(c_q, w['w_q'], q_tabs, QK_ROPE // 2)
    return q_ext, ckr


def _mla_prompt(h, B, T, w, q_tabs, kr_tabs):
    q_ext, ckr = _mla_front(h, w, q_tabs, kr_tabs)
    kv_ext = _proj(ckr, w['w_kv'])
    y = _attn_mla(q_ext, kv_ext, B, T)
    return y, (ckr[:, :KV_LORA].reshape(B, T, KV_LORA), ckr[:, KV_LORA:KV_LORA + QK_ROPE].reshape(B, T, QK_ROPE))


def _mla_sample(h, w, q_tabs, kr_tabs, cache_ckv, cache_kr, slot, page_table):
    B = h.shape[0]
    H = MLA_HEADS
    n_pool = cache_ckv.shape[1]
    q_ext, ckr = _mla_front(h, w, q_tabs, kr_tabs)
    q_lat = _proj(q_ext, w['w_absorb']).reshape(B, H, KV_LORA)
    q_rope = q_ext.reshape(B, H, MLA_SLOT)[:, :, QK_NOPE:QK_NOPE + QK_ROPE]
    c_new = ckr[:, :KV_LORA].reshape(B, 1, KV_LORA)
    r_new = ckr[:, KV_LORA:KV_LORA + QK_ROPE].reshape(B, 1, QK_ROPE)
    pages = page_table + slot * n_pool
    o_lat = _mla_decode(pages, cache_ckv.reshape(-1, PAGE_SIZE, KV_LORA), _feature_major(cache_kr),
                        q_lat, q_rope, c_new, r_new, per_step=min(8, page_table.shape[1]))
    y = _proj(o_lat.reshape(B, H * KV_LORA), w['w_unabsorb'])
    return y, (c_new, r_new)


def _moba_prompt(h, B, T, w, tabs):
    KH, dh = MOBA_KV_HEADS, MOBA_HEAD_DIM
    proj = _proj(h, w['w_in'], tabs, dh // 2)
    y = _attn_moba(proj, B, T)
    return y, proj[:, MOBA_HEADS * dh:].reshape(B, T, KH, 2, dh)


def _moba_sample(h, past_len, w, tabs, cache, slot, page_table):
    B = h.shape[0]
    H, KH, dh = MOBA_HEADS, MOBA_KV_HEADS, MOBA_HEAD_DIM
    R = H // KH
    assert past_len % MOBA_BLOCK == 0 and past_len // MOBA_BLOCK >= MOBA_TOP
    n_pool = cache.shape[1]
    width = KH * 2 * dh
    ppb = MOBA_BLOCK // PAGE_SIZE
    proj = _proj(h, w['w_in'], tabs, dh // 2)
    pages = page_table + slot * n_pool
    cache_t = _feature_major(cache)
    means = _moba_means(pages, cache_t, per_step=min(16, page_table.shape[1]))
    q3 = proj[:, :H * dh].reshape(B, H, dh)
    scores = _moba_dec_score(q3, means, past_len // MOBA_BLOCK)
    idx = _topk_idx(scores.reshape(B * H, -1), MOBA_TOP)[:, :MOBA_TOP].reshape(B, H * MOBA_TOP)
    page_ids = jnp.take_along_axis(pages, (idx[:, :, None] * ppb + jnp.arange(ppb)).reshape(B, -1), axis=1)
    new_kv = proj[:, H * dh:].reshape(B, 1, width)
    o = _moba_dec_attn(page_ids, cache_t, q3.reshape(B, KH, R, dh), new_kv)
    return o.reshape(B, H * dh), new_kv.reshape(B, 1, KH, 2, dh)


def _moe_layer(h, w, g, b):
    gate = _moe_router(h, w['w_router'], w['b_router'])
    return _moe(h, gate, w['w1'], w['w2'], w['wsgu'], w['wsd'], g, b)


def kernel(x_prompt, x_sample, cache_nsa_cmp, cache_nsa_sel, state_nsa_win, cache_mla_ckv, cache_mla_krope,
           cache_moba_kv, page_table, nsa_w_in, nsa_cmp_pe, nsa_cmp_w1, nsa_cmp_b1, nsa_cmp_w2, nsa_cmp_b2,
           nsa_w_o, mla_w_dn, mla_g_q, mla_w_uq, mla_g_kv, mla_w_uk, mla_w_uv, mla_w_o, moba_w_in, moba_w_o,
           ln1_g, ln1_b, ln2_g, ln2_b, moe_w_router, moe_b_router, moe_w_gate, moe_w_up, moe_w_down,
           moe_ws_gate, moe_ws_up, moe_ws_down):
    B, T, D = x_prompt.shape
    Bs, Ts, _ = x_sample.shape
    assert Ts == 1
    past_len = page_table.shape[1] * PAGE_SIZE
    assert state_nsa_win.shape[2] == WINDOW and past_len >= WINDOW
    pos_p = jnp.arange(T, dtype=I32)
    pos_s = jnp.full((Bs,), past_len, dtype=I32)
    hp = x_prompt.reshape(B * T, D)
    hs = x_sample.reshape(Bs, D)
    outs = {k: [] for k in ('cmp_p', 'cmp_s', 'sel_p', 'sel_s', 'win_p', 'win_s',
                            'ckv_p', 'ckv_s', 'kr_p', 'kr_s', 'mb_p', 'mb_s')}
    for i in range(DEPTH):
        kind, slot = i % N_MIXERS, i // N_MIXERS
        if kind == MIX_NSA:
            w = _nsa_weights(nsa_w_in[slot], nsa_cmp_pe[slot], nsa_cmp_w1[slot], nsa_cmp_b1[slot],
                             nsa_cmp_w2[slot], nsa_cmp_b2[slot], nsa_w_o[slot])
            yp, (a_p, b_p, c_p) = _nsa_prompt(hp, B, T, w, _nsa_tables(pos_p))
            ys, (a_s, b_s, c_s) = _nsa_sample(hs, past_len, w, _nsa_tables(pos_s), cache_nsa_cmp, cache_nsa_sel,
                                              state_nsa_win, slot, page_table)
            outs['cmp_p'].append(a_p); outs['cmp_s'].append(a_s)
            outs['sel_p'].append(b_p); outs['sel_s'].append(b_s)
            outs['win_p'].append(c_p); outs['win_s'].append(c_s)
        elif kind == MIX_MLA:
            w = _mla_weights(mla_w_dn[slot], mla_g_q[slot], mla_w_uq[slot], mla_g_kv[slot], mla_w_uk[slot],
                             mla_w_uv[slot], mla_w_o[slot])
            yp, (a_p, b_p) = _mla_prompt(hp, B, T, w, _mla_q_tables(pos_p), _mla_kr_tables(pos_p))
            ys, (a_s, b_s) = _mla_sample(hs, w, _mla_q_tables(pos_s), _mla_kr_tables(pos_s), cache_mla_ckv,
                                         cache_mla_krope, slot, page_table)
            outs['ckv_p'].append(a_p); outs['ckv_s'].append(a_s)
            outs['kr_p'].append(b_p); outs['kr_s'].append(b_s)
        else:
            w = _moba_weights(moba_w_in[slot], moba_w_o[slot])
            yp, a_p = _moba_prompt(hp, B, T, w, _moba_tables(pos_p))
            ys, a_s = _moba_sample(hs, past_len, w, _moba_tables(pos_s), cache_moba_kv, slot, page_table)
            outs['mb_p'].append(a_p); outs['mb_s'].append(a_s)
        hp = _proj_ln(yp, w['w_o'], hp, ln1_g[i], ln1_b[i])
        hs = _proj_ln(ys, w['w_o'], hs, ln1_g[i], ln1_b[i])
        mw = _moe_weights(moe_w_router[i], moe_b_router[i], moe_w_gate[i], moe_w_up[i], moe_w_down[i],
                          moe_ws_gate[i], moe_ws_up[i], moe_ws_down[i])
        hp = _moe_layer(hp, mw, ln2_g[i], ln2_b[i])
        hs = _moe_layer(hs, mw, ln2_g[i], ln2_b[i])
    st = lambda k: jnp.stack(outs[k])
    return (hp.reshape(B, T, D), hs.reshape(Bs, Ts, D),
            st('cmp_p'), st('cmp_s'), st('sel_p'), st('sel_s'), st('win_p'), st('win_s'),
            st('ckv_p'), st('ckv_s'), st('kr_p'), st('kr_s'), st('mb_p'), st('mb_s'))
```

```python
import functools

import jax
import jax.numpy as jnp
from jax import lax
from jax.experimental import pallas as pl
from jax.experimental.pallas import tpu as pltpu

D_MODEL = 1024
DEPTH = 4
PAGE_SIZE = 128
N_MIXERS = 3
MIX_NSA, MIX_MLA, MIX_MOBA = 0, 1, 2

ALPHA = (2 * DEPTH) ** 0.25
LN_EPS = 1e-5
RMS_EPS = 1e-6
ROPE_THETA = 10000.0
NEG = -1e30
FORCE = 1e9

NSA_HEADS = 16
NSA_KV_HEADS = 2
NSA_HEAD_DIM = 64
CMP_BLOCK = 32
CMP_STRIDE = 16
CMP_HIDDEN = 128
SEL_BLOCK = 64
SEL_RATIO = SEL_BLOCK // CMP_STRIDE
SEL_TOP = 16
WINDOW = 512

MLA_HEADS = 16
Q_LORA = 384
KV_LORA = 256
QK_NOPE = 64
QK_ROPE = 32
V_DIM = 64
MLA_SLOT = 128

MOBA_HEADS = 16
MOBA_KV_HEADS = 4
MOBA_HEAD_DIM = 64
MOBA_BLOCK = 256
MOBA_TOP = 3

N_EXPERTS = 64
TOP_K = 8
D_EXPERT = 128
D_SHARED = 128
ROUTED_SCALE = 2.5

LANES = 128
FLASH_CHAINS = 1
MOE_GROUP = 4
ONES_ROWS = 16
VMEM_LIMIT = 48 * 1024 * 1024

F32 = jnp.float32
BF16 = jnp.bfloat16
I32 = jnp.int32


def _cp(*sem):
    return pltpu.CompilerParams(dimension_semantics=sem, vmem_limit_bytes=VMEM_LIMIT)


def _dot(a, b):
    return jnp.dot(a, b, preferred_element_type=F32)


def _dot_nt(a, b):
    return lax.dot_general(a, b, (((1,), (1,)), ((), ())), preferred_element_type=F32)


def _dot_exact(a, b):
    return jnp.dot(a, b, preferred_element_type=F32, precision=lax.Precision.HIGHEST)


def _iota(shape, axis):
    return lax.broadcasted_iota(I32, shape, axis)


def _layer_norm(z, g, b):
    mu = jnp.mean(z, axis=-1, keepdims=True)
    d = z - mu
    var = jnp.mean(d * d, axis=-1, keepdims=True)
    return d * lax.rsqrt(var + LN_EPS) * g + b


def _silu(x):
    return x * jax.nn.sigmoid(x)


def _softmax_rows(s, valid):
    s = jnp.where(valid, s, NEG)
    m = jnp.max(s, axis=-1, keepdims=True)
    e = jnp.where(valid, jnp.exp(s - m), 0.0)
    l = jnp.sum(e, axis=-1, keepdims=True)
    return e / jnp.where(l > 0.0, l, 1.0)


def _rank_before(x, n_cols, lane):
    rank = jnp.zeros(x.shape, F32)
    for c in range(n_cols):
        col = x[:, c:c + 1]
        rank = rank + jnp.where(col > x, 1.0, jnp.where(col == x, jnp.where(lane > c, 1.0, 0.0), 0.0))
    return rank


def _proj_kernel(x_ref, w_ref, o_ref):
    o_ref[...] = _dot(x_ref[...].astype(BF16), w_ref[...])


def _proj_rope_kernel(x_ref, w_ref, a_ref, b_ref, o_ref, *, half, tn):
    acc = _dot(x_ref[...].astype(BF16), w_ref[...])
    lane = _iota(acc.shape, 1)
    first = (lane % (2 * half)) < half
    partner = jnp.where(first, pltpu.roll(acc, tn - half, 1), pltpu.roll(acc, half, 1))
    o_ref[...] = acc * a_ref[...] + partner * b_ref[...]


def _col_tile(n):
    return next(t for t in (512, 384, 256, 128) if n % t == 0)


def _proj(x, w, tabs=None, half=0):
    M, K = x.shape
    N = w.shape[1]
    tm = min(M, 512)
    tn = _col_tile(N)
    grid = (M // tm, N // tn)
    x_spec = pl.BlockSpec((tm, K), lambda i, j: (i, 0))
    w_spec = pl.BlockSpec((K, tn), lambda i, j: (0, j))
    o_spec = pl.BlockSpec((tm, tn), lambda i, j: (i, j))
    out_shape = jax.ShapeDtypeStruct((M, N), F32)
    if tabs is None:
        return pl.pallas_call(_proj_kernel, out_shape=out_shape, grid=grid, in_specs=[x_spec, w_spec],
                              out_specs=o_spec, compiler_params=_cp("parallel", "parallel"), name="proj")(x, w)
    a, bm = tabs
    nrb = a.shape[0] // tm
    t_spec = pl.BlockSpec((tm, tn), lambda i, j: (i % nrb, j))
    return pl.pallas_call(functools.partial(_proj_rope_kernel, half=half, tn=tn), out_shape=out_shape, grid=grid,
                          in_specs=[x_spec, w_spec, t_spec, t_spec], out_specs=o_spec,
                          compiler_params=_cp("parallel", "parallel"), name="proj_rope")(x, w, a, bm)


def _proj_ln_kernel(x_ref, w_ref, r_ref, g_ref, b_ref, o_ref):
    y = _dot(x_ref[...].astype(BF16), w_ref[...])
    o_ref[...] = _layer_norm(ALPHA * r_ref[...] + y, g_ref[...], b_ref[...])


def _proj_ln(x, w, res, g, b):
    M, K = x.shape
    N = w.shape[1]
    tm = min(M, 256)
    row = lambda i: (i, 0)
    fix = lambda i: (0, 0)
    return pl.pallas_call(
        _proj_ln_kernel, out_shape=jax.ShapeDtypeStruct((M, N), F32), grid=(M // tm,),
        in_specs=[pl.BlockSpec((tm, K), row), pl.BlockSpec((K, N), fix), pl.BlockSpec((tm, N), row),
                  pl.BlockSpec((1, N), fix), pl.BlockSpec((1, N), fix)],
        out_specs=pl.BlockSpec((tm, N), row), compiler_params=_cp("parallel"), name="proj_ln",
    )(x, w, res, g.reshape(1, N), b.reshape(1, N))


def _rope_unit(pos, half):
    inv = jnp.power(ROPE_THETA, -jnp.arange(half, dtype=F32) / half)
    ang = pos.astype(F32)[:, None] * inv
    cos, sin = jnp.cos(ang), jnp.sin(ang)
    return jnp.concatenate([cos, cos], axis=1), jnp.concatenate([-sin, sin], axis=1)


def _tables(pos, half, layout):
    ua, ub = _rope_unit(pos, half)
    T = pos.shape[0]
    a_parts, b_parts = [], []
    for kind, n in layout:
        if kind == 'r':
            a_parts.append(jnp.tile(ua, (1, n)))
            b_parts.append(jnp.tile(ub, (1, n)))
        else:
            a_parts.append(jnp.ones((T, n), F32))
            b_parts.append(jnp.zeros((T, n), F32))
    return jnp.concatenate(a_parts, axis=1), jnp.concatenate(b_parts, axis=1)


def _add_bias(s, bias, heads):
    if heads == 1:
        return s + bias
    tq = s.shape[1] // heads
    return jnp.concatenate([s[:, h * tq:(h + 1) * tq] + bias for h in range(heads)], axis=1)


def _weighted_values(vt_ref, first, e):
    acc = _dot(vt_ref[first], e[:LANES])
    for j in range(1, e.shape[0] // LANES):
        acc = acc + _dot(vt_ref[first + j], e[j * LANES:(j + 1) * LANES])
    return acc


def _flash(qs, k_ref, k_cols, vt_ref, vt_base, m_sc, acc_sc, lo, hi, tk, bias_fn, heads):
    m_sc[...] = jnp.full(m_sc.shape, NEG, F32)
    acc_sc[...] = jnp.zeros(acc_sc.shape, F32)
    sub = tk // LANES

    def body(kt, carry):
        k0 = pl.multiple_of(kt * tk, tk)
        s = _add_bias(_dot_nt(k_ref[pl.ds(k0, tk), k_cols], qs), bias_fn(kt, k0), heads)
        m_old = m_sc[...]
        m_new = jnp.maximum(m_old, jnp.max(s, axis=0, keepdims=True))
        e = jnp.exp(s - m_new).astype(BF16)
        acc_sc[...] = jnp.exp(m_old - m_new) * acc_sc[...] + _weighted_values(vt_ref, vt_base + kt * sub, e)
        m_sc[...] = m_new
        return carry

    lax.fori_loop(lo, hi, body, 0)
    return acc_sc[...]


def _transpose_rows(x):
    eye = jnp.where(_iota((LANES, LANES), 0) == _iota((LANES, LANES), 1), 1.0, 0.0).astype(BF16)
    return _dot_nt(eye, x)


def _store_values_t(vt_ref, base, vt):
    n = vt.shape[1]
    full = jnp.concatenate([jnp.ones((ONES_ROWS, n), F32), vt], axis=0).astype(BF16)
    for j in range(n // LANES):
        vt_ref[base + j] = full[:, j * LANES:(j + 1) * LANES]


def _write_heads_t(o_ref, heads):
    tq = heads[0].shape[1]
    for p in range(len(heads) // 2):
        pair = jnp.concatenate([heads[2 * p], heads[2 * p + 1]], axis=0)
        for c in range(tq // LANES):
            o_ref[c * LANES:(c + 1) * LANES, p * LANES:(p + 1) * LANES] = pair[:, c * LANES:(c + 1) * LANES].T


def _stack_heads(ref, n_heads, dh, scale):
    parts = []
    for r in range(n_heads):
        pair = ref[:, (r // 2) * 2 * dh:(r // 2 + 1) * 2 * dh] * scale
        lane = _iota(pair.shape, 1)
        own = (lane < dh) if r % 2 == 0 else (lane >= dh)
        parts.append(jnp.where(own, pair, 0.0))
    return jnp.concatenate(parts, axis=0).astype(BF16)


def _dup_keys(kv, dh):
    return jnp.where(_iota(kv.shape, 1) < dh, kv, pltpu.roll(kv, dh, 1))


def _cmp_lohi_kernel(*refs, n_pages, rows_per_page):
    G = NSA_KV_HEADS
    refs = refs[len(refs) - G * n_pages - 2:]
    row_refs, w_ref, o_ref = refs[:G * n_pages], refs[G * n_pages], refs[G * n_pages + 1]
    cpp = rows_per_page // CMP_STRIDE
    nh = w_ref.shape[2] // 2

    def chunk_rows(g, r):
        xr = [row_refs[g * n_pages + p][0, pl.ds(r, cpp, stride=CMP_STRIDE), :] for p in range(n_pages)]
        return xr[0] if n_pages == 1 else jnp.concatenate(xr, axis=0)

    for g in range(G):
        acc = jnp.zeros((n_pages * cpp, 2 * nh), F32)
        for r2 in range(CMP_STRIDE // 2):
            xr = jnp.concatenate([chunk_rows(g, 2 * r2), chunk_rows(g, 2 * r2 + 1)], axis=1)
            acc = acc + _dot(xr.astype(BF16), w_ref[r2])
        o_ref[0, :, g * nh:(g + 1) * nh] = acc[:, :nh]
        o_ref[0, :, (G + g) * nh:(G + g + 1) * nh] = acc[:, nh:]


def _cmp_lohi(rows, w_lohi, col_block=0):
    B, L = rows.shape[0], rows.shape[1]
    G = NSA_KV_HEADS
    nout = G * w_lohi.shape[2]
    row_specs = [pl.BlockSpec((1, L, LANES), functools.partial(lambda b, g: (b, 0, col_block + g), g=g))
                 for g in range(G)]
    return pl.pallas_call(
        functools.partial(_cmp_lohi_kernel, n_pages=1, rows_per_page=L),
        out_shape=jax.ShapeDtypeStruct((B, L // CMP_STRIDE, nout), F32), grid=(B,),
        in_specs=row_specs + [pl.BlockSpec(w_lohi.shape, lambda b: (0, 0, 0))],
        out_specs=pl.BlockSpec((1, L // CMP_STRIDE, nout), lambda b: (b, 0, 0)),
        compiler_params=_cp("parallel"), name="cmp_lohi",
    )(*([rows] * G), w_lohi)


def _cmp_lohi_paged(cache, pages, w_lohi, per_step):
    B, n_pages = pages.shape
    G = NSA_KV_HEADS
    nout = G * w_lohi.shape[2]
    cpp = PAGE_SIZE // CMP_STRIDE
    page_specs = [pl.BlockSpec((1, PAGE_SIZE, LANES), functools.partial(
        lambda b, s, pt, p, g: (pt[b, s * per_step + p], 0, g), p=p, g=g))
        for g in range(G) for p in range(per_step)]
    grid_spec = pltpu.PrefetchScalarGridSpec(
        num_scalar_prefetch=1, grid=(B, n_pages // per_step),
        in_specs=page_specs + [pl.BlockSpec(w_lohi.shape, lambda b, s, pt: (0, 0, 0))],
        out_specs=pl.BlockSpec((1, per_step * cpp, nout), lambda b, s, pt: (b, s, 0)))
    return pl.pallas_call(
        functools.partial(_cmp_lohi_kernel, n_pages=per_step, rows_per_page=PAGE_SIZE),
        out_shape=jax.ShapeDtypeStruct((B, n_pages * cpp, nout), F32), grid_spec=grid_spec,
        compiler_params=_cp("parallel", "parallel"), name="cmp_lohi_paged",
    )(pages, *([cache] * (G * per_step)), w_lohi)


def _cmp_combine_kernel(lohi_ref, pe_ref, b1_ref, w2_ref, b2_ref, o_ref):
    nch = lohi_ref.shape[1]
    nh = lohi_ref.shape[2] // 2
    lo = lohi_ref[0, :, :nh]
    hi_next = pltpu.roll(lohi_ref[0, :, nh:], nch - 1, 0)
    pe = pe_ref[0, 0:1, :nh] + pe_ref[0, 1:2, nh:]
    hid = jax.nn.gelu(lo + hi_next + pe + b1_ref[...])
    o_ref[0] = _dot(hid.astype(BF16), w2_ref[...]) + b2_ref[...]


def _cmp_combine(lohi, pe_lohi, b1, w2, b2):
    B, nch, n2 = lohi.shape
    nh = n2 // 2
    nout = w2.shape[1]
    fix2 = lambda b: (0, 0)
    return pl.pallas_call(
        _cmp_combine_kernel, out_shape=jax.ShapeDtypeStruct((B, nch, nout), F32), grid=(B,),
        in_specs=[pl.BlockSpec((1, nch, n2), lambda b: (b, 0, 0)), pl.BlockSpec((1, 2, n2), lambda b: (0, 0, 0)),
                  pl.BlockSpec((1, nh), fix2), pl.BlockSpec((nh, nout), fix2), pl.BlockSpec((1, nout), fix2)],
        out_specs=pl.BlockSpec((1, nch, nout), lambda b: (b, 0, 0)),
        compiler_params=_cp("parallel"), name="cmp_combine",
    )(lohi, pe_lohi, b1, w2, b2)


def _sel_importance(p_sum, n_lanes):
    nc = p_sum.shape[1]
    c = _iota((nc, n_lanes), 0)
    j = _iota((nc, n_lanes), 1)
    a = jnp.where(c >= SEL_RATIO * j - 1, jnp.where(c <= SEL_RATIO * j + SEL_RATIO - 1, 1.0, 0.0), 0.0)
    return _dot_exact(p_sum, a)


def _attn_nsa_kernel(q_ref, qr_ref, cmp_ref, sel_ref, win_ref, gl_ref, o_ref,
                     k2s, vts, k2w, vtw, bias_ref, m_sc, acc_sc, *, T, tq, tk, n_sel, n_top):
    i = pl.program_id(2)
    t0 = i * tq
    R = NSA_HEADS // NSA_KV_HEADS
    dh = NSA_HEAD_DIM
    scale = dh ** -0.5
    wk = min(T, WINDOW + tq)

    @pl.when(i == 0)
    def _():
        for src, k_sc, vt_sc in ((sel_ref, k2s, vts), (win_ref, k2w, vtw)):
            kv = src[...]
            k_sc[...] = _dup_keys(kv, dh).astype(BF16)
            _store_values_t(vt_sc, 0, _transpose_rows(kv.astype(BF16))[dh:])

    tpos = t0 + _iota((1, tq), 1)
    cmp_tok = cmp_ref[0]
    nc = cmp_tok.shape[0]
    s = _dot_nt(_dup_keys(cmp_tok, dh).astype(BF16), _stack_heads(q_ref, R, dh, scale))
    cvalid = (_iota((nc, 1), 0) * CMP_STRIDE + CMP_BLOCK - 1) <= tpos
    probs = []
    for h in range(R):
        sh = jnp.where(cvalid, s[:, h * tq:(h + 1) * tq], NEG)
        e = jnp.where(cvalid, jnp.exp(sh - jnp.max(sh, axis=0, keepdims=True)), 0.0)
        l = jnp.sum(e, axis=0, keepdims=True)
        probs.append(e / jnp.where(l > 0.0, l, 1.0))
    p_sum = probs[0]
    for h in range(1, R):
        p_sum = p_sum + probs[h]
    vct = _transpose_rows(cmp_tok.astype(BF16))[dh:].astype(BF16)
    o_cmp = _dot(vct, jnp.concatenate(probs, axis=1).astype(BF16))

    ns = bias_ref.shape[0] // SEL_BLOCK
    blk = _iota((ns, tq), 0)
    cur = tpos // SEL_BLOCK
    c_idx = _iota((ns, nc), 1)
    lo_c = SEL_RATIO * _iota((ns, nc), 0) - 1
    spread = jnp.where(c_idx >= lo_c, jnp.where(c_idx <= lo_c + SEL_RATIO, 1.0, 0.0), 0.0)
    imp = _dot_exact(spread, p_sum)
    forced = jnp.where(blk == 0, 1.0, jnp.where(blk == cur, 1.0, jnp.where(blk == cur - 1, 1.0, 0.0)))
    imp = jnp.where(blk > cur, NEG, jnp.where(forced > 0.5, FORCE, imp))
    rank = jnp.zeros(imp.shape, F32)
    for c in range(n_sel):
        row = imp[c:c + 1, :]
        rank = rank + jnp.where(row > imp, 1.0, jnp.where(row == imp, jnp.where(blk > c, 1.0, 0.0), 0.0))
    chosen = jnp.where(blk < n_sel, jnp.where(rank < n_top, 1.0, 0.0), 0.0)
    for j in range(n_sel):
        picked = jnp.broadcast_to(chosen[j:j + 1, :], (SEL_BLOCK, tq))
        kpos = j * SEL_BLOCK + _iota((SEL_BLOCK, 1), 0)
        bias_ref[j * SEL_BLOCK:(j + 1) * SEL_BLOCK, :] = jnp.where(
            picked > 0.5, jnp.where(kpos <= tpos, 0.0, NEG), NEG)

    hi = (t0 + tq + tk - 1) // tk
    qrs = _stack_heads(qr_ref, R, dh, scale)
    acc = _flash(qrs, k2s, slice(None), vts, 0, m_sc, acc_sc, 0, hi, tk,
                 lambda kt, k0: bias_ref[pl.ds(k0, tk), :], R)
    o_sel = acc[ONES_ROWS:] / acc[0:1]
    kw0 = pl.multiple_of(jnp.clip(t0 - WINDOW, 0, T - wk), LANES)
    d = tpos - (kw0 + _iota((wk, 1), 0))
    win_bias = jnp.where(d >= 0, jnp.where(d <= WINDOW, 0.0, NEG), NEG)
    s = _add_bias(_dot_nt(k2w[pl.ds(kw0, wk), :], qrs), win_bias, R)
    e = jnp.exp(s - jnp.max(s, axis=0, keepdims=True)).astype(BF16)
    acc = _weighted_values(vtw, kw0 // LANES, e)
    o_win = acc[ONES_ROWS:] / acc[0:1]
    gate = jax.nn.sigmoid(gl_ref[...].T)
    heads = []
    for h in range(R):
        hs = slice(h * tq, (h + 1) * tq)
        heads.append(gate[h:h + 1] * o_cmp[:, hs] + gate[R + h:R + h + 1] * o_sel[:, hs]
                     + gate[2 * R + h:2 * R + h + 1] * o_win[:, hs])
    _write_heads_t(o_ref, heads)


def _attn_nsa(proj, cmp_tok, B, T, tk=None):
    G = NSA_KV_HEADS
    R = NSA_HEADS // G
    tq = LANES
    tk = tk or min(T, 512)
    nq = T // tq
    n_sel = -(-T // SEL_BLOCK)
    n_top = min(SEL_TOP, n_sel)
    qw = NSA_HEADS * NSA_HEAD_DIM // G
    kv0 = 2 * NSA_HEADS * NSA_HEAD_DIM // LANES
    gl0 = kv0 + 3 * G
    nch = cmp_tok.shape[1]
    dh = NSA_HEAD_DIM
    kern = functools.partial(_attn_nsa_kernel, T=T, tq=tq, tk=tk, n_sel=n_sel, n_top=n_top)
    return pl.pallas_call(
        kern, out_shape=jax.ShapeDtypeStruct((B * T, NSA_HEADS * dh), F32), grid=(B, G, nq),
        in_specs=[pl.BlockSpec((tq, qw), lambda b, g, i: (b * nq + i, g)),
                  pl.BlockSpec((tq, qw), lambda b, g, i: (b * nq + i, G + g)),
                  pl.BlockSpec((1, nch, 2 * dh), lambda b, g, i: (b, 0, g)),
                  pl.BlockSpec((T, 2 * dh), lambda b, g, i: (b, kv0 + G + g)),
                  pl.BlockSpec((T, 2 * dh), lambda b, g, i: (b, kv0 + 2 * G + g)),
                  pl.BlockSpec((tq, LANES), lambda b, g, i: (b * nq + i, gl0 + g))],
        out_specs=pl.BlockSpec((tq, qw), lambda b, g, i: (b * nq + i, g)),
        scratch_shapes=[pltpu.VMEM((T, 2 * dh), BF16), pltpu.VMEM((T // LANES, ONES_ROWS + dh, LANES), BF16)] * 2 + [
            pltpu.VMEM((T, tq), F32), pltpu.VMEM((1, R * tq), F32), pltpu.VMEM((ONES_ROWS + dh, R * tq), F32)],
        compiler_params=_cp("parallel", "parallel", "arbitrary"), name="attn_nsa",
    )(proj, proj, cmp_tok, proj, proj, proj)


def _nsa_dec_cmp_kernel(q_ref, cmp_ref, o_ref, imp_ref, *, t, n_sel):
    G = NSA_KV_HEADS
    R = NSA_HEADS // G
    dh = NSA_HEAD_DIM
    nc = cmp_ref.shape[1]
    nl = imp_ref.shape[2]
    cvalid = (_iota((1, nc), 1) * CMP_STRIDE + CMP_BLOCK - 1) <= t
    blk = _iota((1, nl), 1)
    cur = t // SEL_BLOCK
    for g in range(G):
        kc = cmp_ref[0, :, 2 * g * dh:(2 * g + 1) * dh].astype(BF16)
        vc = cmp_ref[0, :, (2 * g + 1) * dh:(2 * g + 2) * dh].astype(BF16)
        qg = (q_ref[0, g * R:(g + 1) * R, :] * dh ** -0.5).astype(BF16)
        p = _softmax_rows(_dot_nt(qg, kc), cvalid)
        o_ref[0, g * R:(g + 1) * R, :] = _dot(p.astype(BF16), vc)
        imp = _sel_importance(jnp.sum(p, axis=0, keepdims=True), nl)
        forced = jnp.where(blk == 0, 1.0, jnp.where(blk == cur, 1.0, jnp.where(blk == cur - 1, 1.0, 0.0)))
        imp = jnp.where(blk > cur, NEG, jnp.where(forced > 0.5, FORCE, imp))
        imp_ref[0, g:g + 1, :] = jnp.where(blk < n_sel, imp, -jnp.inf)


def _nsa_dec_cmp(q3, cmp_tok, t, n_sel):
    B = q3.shape[0]
    nl = -(-n_sel // LANES) * LANES
    nc = cmp_tok.shape[1]
    blk3 = lambda b: (b, 0, 0)
    return pl.pallas_call(
        functools.partial(_nsa_dec_cmp_kernel, t=t, n_sel=n_sel),
        out_shape=(jax.ShapeDtypeStruct(q3.shape, F32), jax.ShapeDtypeStruct((B, NSA_KV_HEADS, nl), F32)),
        grid=(B,),
        in_specs=[pl.BlockSpec((1,) + q3.shape[1:], blk3), pl.BlockSpec((1, nc, cmp_tok.shape[2]), blk3)],
        out_specs=(pl.BlockSpec((1,) + q3.shape[1:], blk3), pl.BlockSpec((1, NSA_KV_HEADS, nl), blk3)),
        compiler_params=_cp("parallel"), name="nsa_dec_cmp",
    )(q3, cmp_tok)


def _topk_idx_kernel(x_ref, o_ref, *, k):
    x = x_ref[...]
    lane = _iota(x.shape, 1)
    out_lane = _iota(o_ref.shape, 1)
    out = jnp.zeros(o_ref.shape, I32)
    big = x.shape[1]
    for n in range(k):
        m = jnp.max(x, axis=-1, keepdims=True)
        idx = jnp.min(jnp.where(x == m, lane, big), axis=-1, keepdims=True)
        out = jnp.where(out_lane == n, idx, out)
        x = jnp.where(lane == idx, -jnp.inf, x)
    o_ref[...] = out


def _topk_idx(x, k):
    rows = x.shape[0]
    return pl.pallas_call(functools.partial(_topk_idx_kernel, k=k),
                          out_shape=jax.ShapeDtypeStruct((rows, LANES), I32), name="topk_idx")(x)


def _nsa_dec_attn_kernel(*refs, n_top, n_past_blocks):
    idx_ref, rb_ref = refs[0], refs[1]
    G = NSA_KV_HEADS
    R = NSA_HEADS // G
    dh = NSA_HEAD_DIM
    blocks = refs[2:2 + G * n_top]
    qr_ref, new_ref, win_ref, ocmp_ref, gl_ref, o_ref = refs[2 + G * n_top:]
    b = pl.program_id(0)
    scale = dh ** -0.5
    per_page = PAGE_SIZE // SEL_BLOCK
    nk = n_top * PAGE_SIZE
    lane = _iota((1, nk), 1)
    for g in range(G):
        qg = qr_ref[0, g * R:(g + 1) * R, :] * scale
        qb = qg.astype(BF16)
        kt = jnp.concatenate([blocks[g * n_top + n][0, :dh, :] for n in range(n_top)], axis=1).astype(BF16)
        vt = jnp.concatenate([blocks[g * n_top + n][0, dh:, :] for n in range(n_top)], axis=1).astype(BF16)
        valid = jnp.zeros((1, nk), F32)
        for n in range(n_top):
            blk = idx_ref[b, g * n_top + n]
            flag = jnp.where(blk < n_past_blocks, 1.0, 0.0)
            in_block = jnp.where((lane % PAGE_SIZE) // SEL_BLOCK == blk % per_page, flag, 0.0)
            valid = jnp.where(lane // PAGE_SIZE == n, in_block, valid)
        valid = valid > 0.5
        k_new = new_ref[0, :, (2 * G + 2 * g) * dh:(2 * G + 2 * g + 1) * dh]
        v_new = new_ref[0, :, (2 * G + 2 * g + 1) * dh:(2 * G + 2 * g + 2) * dh]
        s = jnp.where(valid, _dot(qb, kt), NEG)
        s_new = jnp.sum(qg * k_new, axis=-1, keepdims=True)
        m = jnp.maximum(jnp.max(s, axis=-1, keepdims=True), s_new)
        e = jnp.where(valid, jnp.exp(s - m), 0.0)
        e_new = jnp.exp(s_new - m)
        o_sel = (_dot_nt(e.astype(BF16), vt) + e_new * v_new) / (jnp.sum(e, axis=-1, keepdims=True) + e_new)
        kwin = win_ref[0, 2 * g * dh:(2 * g + 1) * dh, :].astype(BF16)
        vwin = win_ref[0, (2 * g + 1) * dh:(2 * g + 2) * dh, :].astype(BF16)
        k_new = new_ref[0, :, (4 * G + 2 * g) * dh:(4 * G + 2 * g + 1) * dh]
        v_new = new_ref[0, :, (4 * G + 2 * g + 1) * dh:(4 * G + 2 * g + 2) * dh]
        s = _dot(qb, kwin)
        s_new = jnp.sum(qg * k_new, axis=-1, keepdims=True)
        m = jnp.maximum(jnp.max(s, axis=-1, keepdims=True), s_new)
        e = jnp.exp(s - m)
        e_new = jnp.exp(s_new - m)
        o_win = (_dot_nt(e.astype(BF16), vwin) + e_new * v_new) / (jnp.sum(e, axis=-1, keepdims=True) + e_new)
        gate = jax.nn.sigmoid(gl_ref[0, g * R:(g + 1) * R, :])
        o_ref[0, g * R:(g + 1) * R, :] = (gate[:, 0:1] * ocmp_ref[0, g * R:(g + 1) * R, :]
                                          + gate[:, 1:2] * o_sel + gate[:, 2:3] * o_win)


def _nsa_dec_attn(idx, sel_pages, cache_t, qr3, new_kv, win_buf, win_index0, o_cmp, gl3, n_past_blocks):
    B = qr3.shape[0]
    G = NSA_KV_HEADS
    n_top = idx.shape[1] // G
    dh = NSA_HEAD_DIM
    blk3 = lambda b, idx, rb: (b, 0, 0)
    block_specs = [pl.BlockSpec((1, 2 * dh, PAGE_SIZE), functools.partial(
        lambda b, idx, rb, n, g: (rb[b, n], g, 0), n=g * n_top + n, g=g)) for g in range(G) for n in range(n_top)]
    grid_spec = pltpu.PrefetchScalarGridSpec(
        num_scalar_prefetch=2, grid=(B,),
        in_specs=block_specs + [
            pl.BlockSpec((1,) + qr3.shape[1:], blk3), pl.BlockSpec((1,) + new_kv.shape[1:], blk3),
            pl.BlockSpec((1,) + win_buf.shape[1:], lambda b, idx, rb: (win_index0 + b, 0, 0)),
            pl.BlockSpec((1,) + o_cmp.shape[1:], blk3), pl.BlockSpec((1,) + gl3.shape[1:], blk3)],
        out_specs=pl.BlockSpec((1,) + qr3.shape[1:], blk3))
    return pl.pallas_call(
        functools.partial(_nsa_dec_attn_kernel, n_top=n_top, n_past_blocks=n_past_blocks),
        out_shape=jax.ShapeDtypeStruct(qr3.shape, F32), grid_spec=grid_spec,
        compiler_params=_cp("parallel"), name="nsa_dec_attn",
    )(idx, sel_pages, *([cache_t] * (G * n_top)), qr3, new_kv, win_buf, o_cmp, gl3)


def _mla_post_kernel(dn_ref, gq_ref, gkv_ref, a_ref, b_ref, cq_ref, ckr_ref):
    x = dn_ref[...]
    cq = x[:, :Q_LORA]
    cq_ref[...] = cq * lax.rsqrt(jnp.mean(cq * cq, axis=-1, keepdims=True) + RMS_EPS) * gq_ref[...]
    ckv = x[:, Q_LORA:Q_LORA + KV_LORA]
    ckv = ckv * lax.rsqrt(jnp.mean(ckv * ckv, axis=-1, keepdims=True) + RMS_EPS) * gkv_ref[...]
    kr = x[:, Q_LORA + KV_LORA:]
    half = QK_ROPE // 2
    first = (_iota(kr.shape, 1) % QK_ROPE) < half
    partner = jnp.where(first, pltpu.roll(kr, LANES - half, 1), pltpu.roll(kr, half, 1))
    ckr_ref[...] = jnp.concatenate([ckv, kr * a_ref[...] + partner * b_ref[...]], axis=1)


def _mla_post(dn, g_q, g_kv, tabs):
    M, N = dn.shape
    tm = min(M, 512)
    nrb = tabs[0].shape[0] // tm
    row = lambda i: (i, 0)
    fix = lambda i: (0, 0)
    tab = pl.BlockSpec((tm, LANES), lambda i: (i % nrb, 0))
    return pl.pallas_call(
        _mla_post_kernel,
        out_shape=(jax.ShapeDtypeStruct((M, Q_LORA), F32), jax.ShapeDtypeStruct((M, KV_LORA + LANES), F32)),
        grid=(M // tm,),
        in_specs=[pl.BlockSpec((tm, N), row), pl.BlockSpec((1, Q_LORA), fix), pl.BlockSpec((1, KV_LORA), fix),
                  tab, tab],
        out_specs=(pl.BlockSpec((tm, Q_LORA), row), pl.BlockSpec((tm, KV_LORA + LANES), row)),
        compiler_params=_cp("parallel"), name="mla_post",
    )(dn, g_q.reshape(1, -1), g_kv.reshape(1, -1), *tabs)


def _attn_mla_kernel(q_ref, k_ref, v_ref, o_ref, k_sc, v_sc, m_sc, acc_sc, *, tq, tk):
    i = pl.program_id(2)
    t0 = i * tq
    scale = (QK_NOPE + QK_ROPE) ** -0.5

    n_vt = k_ref.shape[0] // LANES

    @pl.when(i == 0)
    def _():
        k_sc[...] = k_ref[...].astype(BF16)
        vt = _transpose_rows(v_ref[...].astype(BF16))
        for hh in range(2):
            _store_values_t(v_sc, hh * n_vt, vt[hh * V_DIM:(hh + 1) * V_DIM])

    tpos = t0 + _iota((1, tq), 1)
    hi = (t0 + tq + tk - 1) // tk
    causal = lambda kt, k0: jnp.where(k0 + _iota((tk, 1), 0) <= tpos, 0.0, NEG)
    heads = []
    for hh in range(2):
        qh = (q_ref[:, hh * MLA_SLOT:(hh + 1) * MLA_SLOT] * scale).astype(BF16)
        acc = _flash(qh, k_sc, slice(hh * MLA_SLOT, (hh + 1) * MLA_SLOT), v_sc, hh * n_vt, m_sc, acc_sc, 0, hi, tk,
                     causal, 1)
        heads.append(acc[ONES_ROWS:] / acc[0:1])
    _write_heads_t(o_ref, heads)


def _attn_mla(q_ext, kv_ext, B, T, tk=None):
    H = MLA_HEADS
    tq = min(T, 512)
    tk = tk or min(T, 512)
    nq = T // tq
    v0 = H * MLA_SLOT // LANES
    return pl.pallas_call(
        functools.partial(_attn_mla_kernel, tq=tq, tk=tk),
        out_shape=jax.ShapeDtypeStruct((B * T, H * V_DIM), F32), grid=(B, H // 2, nq),
        in_specs=[pl.BlockSpec((tq, 2 * MLA_SLOT), lambda b, h, i: (b * nq + i, h)),
                  pl.BlockSpec((T, 2 * MLA_SLOT), lambda b, h, i: (b, h)),
                  pl.BlockSpec((T, 2 * V_DIM), lambda b, h, i: (b, v0 + h))],
        out_specs=pl.BlockSpec((tq, 2 * V_DIM), lambda b, h, i: (b * nq + i, h)),
        scratch_shapes=[pltpu.VMEM((T, 2 * MLA_SLOT), BF16),
                        pltpu.VMEM((2 * (T // LANES), ONES_ROWS + V_DIM, LANES), BF16),
                        pltpu.VMEM((1, tq), F32), pltpu.VMEM((ONES_ROWS + V_DIM, tq), F32)],
        compiler_params=_cp("parallel", "parallel", "arbitrary"), name="attn_mla",
    )(q_ext, kv_ext, kv_ext)


def _mla_decode_kernel(*refs, per_step):
    pt_ref = refs[0]
    ckv_pages = refs[1:1 + per_step]
    kr_pages = refs[1 + per_step:1 + 2 * per_step]
    ql_ref, qr_ref, cnew_ref, rnew_ref, o_ref, m_sc, l_sc, acc_sc = refs[1 + 2 * per_step:]
    s_idx = pl.program_id(1)
    scale = (QK_NOPE + QK_ROPE) ** -0.5
    ql = ql_ref[0] * scale
    qr = qr_ref[0] * scale

    @pl.when(s_idx == 0)
    def _():
        c_new = cnew_ref[0]
        s_new = (jnp.sum(ql * c_new, axis=-1, keepdims=True) + jnp.sum(qr * rnew_ref[0], axis=-1, keepdims=True))
        m_sc[...] = s_new
        l_sc[...] = jnp.ones(l_sc.shape, F32)
        acc_sc[...] = jnp.broadcast_to(c_new, acc_sc.shape)

    qlb = ql.astype(BF16)
    qrb = qr.astype(BF16)
    cs = [ref[0].astype(BF16) for ref in ckv_pages]
    s = jnp.concatenate([_dot_nt(qlb, c) + _dot(qrb, r[0].astype(BF16)) for c, r in zip(cs, kr_pages)], axis=1)
    m_old = m_sc[...]
    m_new = jnp.maximum(m_old, jnp.max(s, axis=-1, keepdims=True))
    alpha = jnp.exp(m_old - m_new)
    e = jnp.exp(s - m_new).astype(BF16)
    pv = _dot(e[:, :PAGE_SIZE], cs[0])
    for p in range(1, per_step):
        pv = pv + _dot(e[:, p * PAGE_SIZE:(p + 1) * PAGE_SIZE], cs[p])
    l_sc[...] = alpha * l_sc[...] + jnp.sum(e.astype(F32), axis=-1, keepdims=True)
    acc_sc[...] = alpha * acc_sc[...] + pv
    m_sc[...] = m_new

    @pl.when(s_idx == pl.num_programs(1) - 1)
    def _():
        o_ref[0] = acc_sc[...] / l_sc[...]


def _mla_decode(pages, cache_ckv, cache_kr, q_lat, q_rope, c_new, r_new, per_step):
    B, n_pages = pages.shape
    H = MLA_HEADS
    blk3 = lambda b, s, pt: (b, 0, 0)
    page_map = lambda p: functools.partial(lambda b, s, pt, p: (pt[b, s * per_step + p], 0, 0), p=p)
    grid_spec = pltpu.PrefetchScalarGridSpec(
        num_scalar_prefetch=1, grid=(B, n_pages // per_step),
        in_specs=([pl.BlockSpec((1, PAGE_SIZE, KV_LORA), page_map(p)) for p in range(per_step)]
                  + [pl.BlockSpec((1, QK_ROPE, PAGE_SIZE), page_map(p)) for p in range(per_step)]
                  + [pl.BlockSpec((1, H, KV_LORA), blk3), pl.BlockSpec((1, H, QK_ROPE), blk3),
                     pl.BlockSpec((1, 1, KV_LORA), blk3), pl.BlockSpec((1, 1, QK_ROPE), blk3)]),
        out_specs=pl.BlockSpec((1, H, KV_LORA), blk3),
        scratch_shapes=[pltpu.VMEM((H, 1), F32), pltpu.VMEM((H, 1), F32), pltpu.VMEM((H, KV_LORA), F32)])
    return pl.pallas_call(
        functools.partial(_mla_decode_kernel, per_step=per_step),
        out_shape=jax.ShapeDtypeStruct((B, H, KV_LORA), F32), grid_spec=grid_spec,
        compiler_params=_cp("parallel", "arbitrary"), name="mla_decode",
    )(pages, *([cache_ckv] * per_step), *([cache_kr] * per_step), q_lat, q_rope, c_new, r_new)


def _attn_moba_kernel(q_ref, kv_ref, o_ref, k2, vt_sc, mean_sc, mask_sc, m_sc, acc_sc, *, T, tq, tk, nb, n_top):
    i = pl.program_id(2)
    t0 = i * tq
    R = MOBA_HEADS // MOBA_KV_HEADS
    dh = MOBA_HEAD_DIM
    L = R * tq
    nbp = mean_sc.shape[0]
    bpt = tk // MOBA_BLOCK

    @pl.when(i == 0)
    def _():
        kv = kv_ref[...]
        k2[...] = _dup_keys(kv, dh).astype(BF16)
        _store_values_t(vt_sc, 0, _transpose_rows(kv.astype(BF16))[dh:])
        mean_sc[...] = jnp.zeros(mean_sc.shape, F32)
        for j in range(T // MOBA_BLOCK):
            blk_rows = _dup_keys(kv_ref[j * MOBA_BLOCK:(j + 1) * MOBA_BLOCK, :], dh)
            mean_sc[j:j + 1, :] = jnp.sum(blk_rows, axis=0, keepdims=True) / MOBA_BLOCK

    tpos = t0 + _iota((1, L), 1) % tq
    cur = tpos // MOBA_BLOCK
    jb = _iota((nbp, L), 0)
    qs = _stack_heads(q_ref, R, dh, dh ** -0.5)
    block_mask = jnp.where(jb == cur, 1.0, 0.0)
    if n_top > 0:
        gm = jnp.where(jb < cur, _dot_nt(mean_sc[...].astype(BF16), qs), NEG)
        rank = jnp.zeros(gm.shape, F32)
        for c in range(nb):
            row = gm[c:c + 1, :]
            rank = rank + jnp.where(row > gm, 1.0, jnp.where(row == gm, jnp.where(jb > c, 1.0, 0.0), 0.0))
        block_mask = jnp.where(jb < cur, jnp.where(rank < n_top, 1.0, 0.0), block_mask)
    mask_sc[...] = block_mask

    def bias(kt, k0):
        parts = []
        for j in range(bpt):
            picked = mask_sc[pl.ds(kt * bpt + j, 1), :]
            kpos = k0 + j * MOBA_BLOCK + _iota((MOBA_BLOCK, 1), 0)
            parts.append(jnp.where(picked > 0.5, jnp.where(kpos <= tpos, 0.0, NEG), NEG))
        return parts[0] if bpt == 1 else jnp.concatenate(parts, axis=0)

    hi = (t0 + tq + tk - 1) // tk
    acc = _flash(qs, k2, slice(None), vt_sc, 0, m_sc, acc_sc, 0, hi, tk, bias, 1)
    o = acc[ONES_ROWS:] / acc[0:1]
    _write_heads_t(o_ref, [o[:, r * tq:(r + 1) * tq] for r in range(R)])


def _attn_moba(proj, B, T, tk=None):
    KH = MOBA_KV_HEADS
    R = MOBA_HEADS // KH
    dh = MOBA_HEAD_DIM
    tq = min(T, 256)
    tk = tk or min(T, 512)
    nq = T // tq
    nb = (T - 1) // MOBA_BLOCK
    n_top = min(MOBA_TOP, nb)
    qw = MOBA_HEADS * dh // KH
    kv0 = MOBA_HEADS * dh // LANES
    nbp = max(T // MOBA_BLOCK, 8)
    return pl.pallas_call(
        functools.partial(_attn_moba_kernel, T=T, tq=tq, tk=tk, nb=nb, n_top=n_top),
        out_shape=jax.ShapeDtypeStruct((B * T, MOBA_HEADS * dh), F32), grid=(B, KH, nq),
        in_specs=[pl.BlockSpec((tq, qw), lambda b, h, i: (b * nq + i, h)),
                  pl.BlockSpec((T, 2 * dh), lambda b, h, i: (b, kv0 + h))],
        out_specs=pl.BlockSpec((tq, qw), lambda b, h, i: (b * nq + i, h)),
        scratch_shapes=[pltpu.VMEM((T, 2 * dh), BF16), pltpu.VMEM((T // LANES, ONES_ROWS + dh, LANES), BF16),
                        pltpu.VMEM((nbp, 2 * dh), F32), pltpu.VMEM((nbp, R * tq), F32),
                        pltpu.VMEM((1, R * tq), F32), pltpu.VMEM((ONES_ROWS + dh, R * tq), F32)],
        compiler_params=_cp("parallel", "parallel", "arbitrary"), name="attn_moba",
    )(proj, proj)


def _moba_means_kernel(*refs, per_step):
    pages, o_ref = refs[1:1 + per_step], refs[1 + per_step]
    KH, dh = MOBA_KV_HEADS, MOBA_HEAD_DIM
    ppb = MOBA_BLOCK // PAGE_SIZE
    s = pl.program_id(1)

    @pl.when(s == 0)
    def _():
        o_ref[...] = jnp.zeros(o_ref.shape, F32)

    lane = _iota(o_ref.shape[1:], 1)
    out = o_ref[0]
    for j in range(per_step // ppb):
        cols = []
        for kh in range(KH):
            k_rows = slice(2 * kh * dh, (2 * kh + 1) * dh)
            tot = pages[j * ppb][0, k_rows, :]
            for p in range(1, ppb):
                tot = tot + pages[j * ppb + p][0, k_rows, :]
            cols.append(jnp.sum(tot, axis=1, keepdims=True))
        col = jnp.concatenate(cols, axis=0) / MOBA_BLOCK
        out = jnp.where(lane == s * (per_step // ppb) + j, col, out)
    o_ref[0] = out


def _moba_means(pages, cache_t, per_step):
    B, n_pages = pages.shape
    width = cache_t.shape[1]
    nk = MOBA_KV_HEADS * MOBA_HEAD_DIM
    assert n_pages * PAGE_SIZE // MOBA_BLOCK <= LANES
    grid_spec = pltpu.PrefetchScalarGridSpec(
        num_scalar_prefetch=1, grid=(B, n_pages // per_step),
        in_specs=[pl.BlockSpec((1, width, PAGE_SIZE), functools.partial(
            lambda b, s, pt, p: (pt[b, s * per_step + p], 0, 0), p=p)) for p in range(per_step)],
        out_specs=pl.BlockSpec((1, nk, LANES), lambda b, s, pt: (b, 0, 0)))
    return pl.pallas_call(
        functools.partial(_moba_means_kernel, per_step=per_step),
        out_shape=jax.ShapeDtypeStruct((B, nk, LANES), F32), grid_spec=grid_spec,
        compiler_params=_cp("parallel", "arbitrary"), name="moba_means",
    )(pages, *([cache_t] * per_step))


def _moba_dec_score_kernel(q_ref, mean_ref, o_ref, *, nb):
    KH = MOBA_KV_HEADS
    R = MOBA_HEADS // KH
    dh = MOBA_HEAD_DIM
    lane = _iota((R, LANES), 1)
    for kh in range(KH):
        mk = mean_ref[0, kh * dh:(kh + 1) * dh, :].astype(BF16)
        g = _dot(q_ref[0, kh * R:(kh + 1) * R, :].astype(BF16), mk)
        o_ref[0, kh * R:(kh + 1) * R, :] = jnp.where(lane < nb, g, -jnp.inf)


def _moba_dec_score(q3, means, nb):
    B, H, dh = q3.shape
    blk3 = lambda b: (b, 0, 0)
    return pl.pallas_call(
        functools.partial(_moba_dec_score_kernel, nb=nb),
        out_shape=jax.ShapeDtypeStruct((B, H, LANES), F32), grid=(B,),
        in_specs=[pl.BlockSpec((1, H, dh), blk3), pl.BlockSpec((1,) + means.shape[1:], blk3)],
        out_specs=pl.BlockSpec((1, H, LANES), blk3), compiler_params=_cp("parallel"), name="moba_dec_score",
    )(q3, means)


def _moba_dec_attn_kernel(*refs, n_blk):
    R = MOBA_HEADS // MOBA_KV_HEADS
    dh = MOBA_HEAD_DIM
    pages = refs[1:1 + R * n_blk]
    q_ref, new_ref, o_ref = refs[1 + R * n_blk:]
    k_new = new_ref[0, :, :dh]
    v_new = new_ref[0, :, dh:]
    for r in range(R):
        q = q_ref[0, 0, r:r + 1, :] * dh ** -0.5
        kt = jnp.concatenate([pages[r * n_blk + n][0, :dh, :] for n in range(n_blk)], axis=1).astype(BF16)
        vt = jnp.concatenate([pages[r * n_blk + n][0, dh:, :] for n in range(n_blk)], axis=1).astype(BF16)
        s = _dot(q.astype(BF16), kt)
        s_new = jnp.sum(q * k_new, axis=-1, keepdims=True)
        m = jnp.maximum(jnp.max(s, axis=-1, keepdims=True), s_new)
        e = jnp.exp(s - m)
        e_new = jnp.exp(s_new - m)
        o_ref[0, 0, r:r + 1, :] = ((_dot_nt(e.astype(BF16), vt) + e_new * v_new)
                                   / (jnp.sum(e, axis=-1, keepdims=True) + e_new))


def _moba_dec_attn(page_ids, cache_t, q4, new_kv):
    B, KH, R, dh = q4.shape
    n_blk = page_ids.shape[1] // (KH * R)
    page_specs = [pl.BlockSpec((1, 2 * dh, PAGE_SIZE), functools.partial(
        lambda b, h, pg, n: (pg[b, h * R * n_blk + n], h, 0), n=n)) for n in range(R * n_blk)]
    grid_spec = pltpu.PrefetchScalarGridSpec(
        num_scalar_prefetch=1, grid=(B, KH),
        in_specs=page_specs + [pl.BlockSpec((1, 1, R, dh), lambda b, h, pg: (b, h, 0, 0)),
                               pl.BlockSpec((1, 1, 2 * dh), lambda b, h, pg: (b, 0, h))],
        out_specs=pl.BlockSpec((1, 1, R, dh), lambda b, h, pg: (b, h, 0, 0)))
    return pl.pallas_call(
        functools.partial(_moba_dec_attn_kernel, n_blk=n_blk),
        out_shape=jax.ShapeDtypeStruct(q4.shape, F32), grid_spec=grid_spec,
        compiler_params=_cp("parallel", "parallel"), name="moba_dec_attn",
    )(page_ids, *([cache_t] * (R * n_blk)), q4, new_kv)


def _router_kernel(x_ref, w_ref, b_ref, o_ref):
    s = jax.nn.sigmoid(lax.dot_general(w_ref[...], x_ref[...], (((1,), (1,)), ((), ())),
                                       preferred_element_type=F32, precision=lax.Precision.HIGHEST))
    row = _iota(s.shape, 0)
    x = s + b_ref[...]
    chosen = jnp.zeros(s.shape, F32)
    for _ in range(TOP_K):
        m = jnp.max(x, axis=0, keepdims=True)
        idx = jnp.min(jnp.where(x == m, row, N_EXPERTS), axis=0, keepdims=True)
        hit = row == idx
        chosen = jnp.where(hit, 1.0, chosen)
        x = jnp.where(hit, -jnp.inf, x)
    w = chosen * s
    w = w / jnp.sum(w, axis=0, keepdims=True) * ROUTED_SCALE
    o_ref[...] = jnp.concatenate([w, jnp.zeros((LANES - N_EXPERTS, w.shape[1]), F32)], axis=0).T


def _moe_router(x, w_router_t, b_router):
    M, K = x.shape
    tm = min(M, 512)
    E = w_router_t.shape[0]
    row = lambda i: (i, 0)
    fix = lambda i: (0, 0)
    return pl.pallas_call(
        _router_kernel, out_shape=jax.ShapeDtypeStruct((M, LANES), F32), grid=(M // tm,),
        in_specs=[pl.BlockSpec((tm, K), row), pl.BlockSpec((E, K), fix), pl.BlockSpec((E, 1), fix)],
        out_specs=pl.BlockSpec((tm, LANES), row), compiler_params=_cp("parallel"), name="moe_router",
    )(x, w_router_t, b_router)


def _moe_kernel(x_ref, gate_ref, w1_ref, w2_ref, wsgu_ref, wsd_ref, g_ref, b_ref, o_ref, xb_sc, acc_sc):
    e = pl.program_id(1)
    nh = MOE_GROUP * D_EXPERT

    @pl.when(e == 0)
    def _():
        xb = x_ref[...].astype(BF16)
        xb_sc[...] = xb
        gu = _dot(xb, wsgu_ref[...])
        acc_sc[...] = _dot((_silu(gu[:, :D_SHARED]) * gu[:, D_SHARED:]).astype(BF16), wsd_ref[...])

    gu = _dot(xb_sc[...], w1_ref[0])
    src = _iota((2 * LANES, nh), 0) % LANES
    pick = jnp.where(src == MOE_GROUP * e + _iota((2 * LANES, nh), 1) // D_EXPERT, 1.0, 0.0).astype(BF16)
    gate = gate_ref[...]
    gate_hi = gate.astype(BF16)
    gate_lo = (gate - gate_hi.astype(F32)).astype(BF16)
    gw = _dot(jnp.concatenate([gate_hi, gate_lo], axis=1), pick)
    h = _silu(gu[:, :nh]) * gu[:, nh:] * gw
    acc_sc[...] += _dot(h.astype(BF16), w2_ref[0])

    @pl.when(e == pl.num_programs(1) - 1)
    def _():
        o_ref[...] = _layer_norm(ALPHA * x_ref[...] + acc_sc[...], g_ref[...], b_ref[...])


def _moe(x, gate, w1, w2, wsgu, wsd, g, b):
    M, D = x.shape
    tm = min(M, 1024)
    npair = w1.shape[0]
    row = lambda i, e: (i, 0)
    fix = lambda i, e: (0, 0)
    return pl.pallas_call(
        _moe_kernel, out_shape=jax.ShapeDtypeStruct((M, D), F32), grid=(M // tm, npair),
        in_specs=[pl.BlockSpec((tm, D), row), pl.BlockSpec((tm, LANES), row),
                  pl.BlockSpec((1,) + w1.shape[1:], lambda i, e: (e, 0, 0)),
                  pl.BlockSpec((1,) + w2.shape[1:], lambda i, e: (e, 0, 0)),
                  pl.BlockSpec(wsgu.shape, fix), pl.BlockSpec(wsd.shape, fix),
                  pl.BlockSpec((1, D), fix), pl.BlockSpec((1, D), fix)],
        out_specs=pl.BlockSpec((tm, D), row),
        scratch_shapes=[pltpu.VMEM((tm, D), BF16), pltpu.VMEM((tm, D), F32)],
        compiler_params=_cp("parallel", "arbitrary"), name="moe",
    )(x, gate, w1, w2, wsgu, wsd, g.reshape(1, D), b.reshape(1, D))


def _pad_cols(w, n):
    return jnp.pad(w, ((0, 0), (0, n - w.shape[1])))


def _block_diag(blocks):
    n, a, b = blocks.shape
    eye = jnp.eye(n, dtype=blocks.dtype)
    return (eye[:, None, :, None] * blocks[:, :, None, :]).reshape(n * a, n * b)


def _nsa_weights(w_in, cmp_pe, cmp_w1, cmp_b1, cmp_w2, cmp_b2, w_o):
    H, G, dh = NSA_HEADS, NSA_KV_HEADS, NSA_HEAD_DIM
    R = H // G
    nq, nkv = H * dh, 6 * G * dh
    wq, wkv = w_in[:, :nq], w_in[:, nq:nq + nkv]
    wg = w_in[:, nq + nkv:].reshape(-1, 3, G, R).transpose(0, 2, 1, 3).reshape(-1, G, 3 * R)
    wg = jnp.pad(wg, ((0, 0), (0, 0), (0, LANES - 3 * R))).reshape(-1, G * LANES)
    w_ext = jnp.concatenate([wq, wq, wkv, wg], axis=1).astype(BF16)
    w1 = cmp_w1.reshape(2, CMP_STRIDE, 2, dh, CMP_HIDDEN)
    eye_k = jnp.eye(2, dtype=F32)
    w_lohi = (w1.transpose(1, 2, 3, 0, 4)[:, :, :, :, None, :] * eye_k[None, :, None, None, :, None])
    w_lohi = w_lohi.reshape(CMP_STRIDE // 2, 2 * 2 * dh, 2 * 2 * CMP_HIDDEN).astype(BF16)
    pe_rows = jnp.broadcast_to(cmp_pe[:, None], (CMP_BLOCK, G, 2, dh)).reshape(1, CMP_BLOCK, G * 2 * dh)
    b1 = jnp.broadcast_to(cmp_b1[None], (G, 2, CMP_HIDDEN)).reshape(1, -1)
    b2 = jnp.broadcast_to(cmp_b2[None], (G, 2, dh)).reshape(1, -1)
    w2 = _block_diag(jnp.tile(cmp_w2, (G, 1, 1))).astype(BF16)
    return dict(w_ext=w_ext, w_lohi=w_lohi, pe_rows=pe_rows, b1=b1, b2=b2, w2=w2, w_o=w_o.astype(BF16))


def _nsa_tables(pos):
    H, G = NSA_HEADS, NSA_KV_HEADS
    dh = NSA_HEAD_DIM
    kv_rot = [('r', 1), ('n', dh)] * G
    layout = ([('n', H * dh), ('r', H), ('n', 2 * G * dh)] + kv_rot + kv_rot + [('n', G * LANES)])
    return _tables(pos, dh // 2, layout)


def _mla_weights(w_dn, g_q, w_uq, g_kv, w_uk, w_uv, w_o):
    H = MLA_HEADS
    pad = MLA_SLOT - QK_NOPE - QK_ROPE
    w_dn_p = _pad_cols(w_dn, Q_LORA + KV_LORA + LANES).astype(BF16)
    wq = jnp.pad(w_uq.reshape(Q_LORA, H, QK_NOPE + QK_ROPE), ((0, 0), (0, 0), (0, pad)))
    w_q = wq.reshape(Q_LORA, H * MLA_SLOT).astype(BF16)
    wk_c = jnp.pad(w_uk, ((0, 0), (0, 0), (0, MLA_SLOT - QK_NOPE))).reshape(KV_LORA, H * MLA_SLOT)
    eye = jnp.pad(jnp.eye(QK_ROPE, dtype=F32), ((0, 0), (QK_NOPE, pad)))
    wk_r = jnp.tile(eye, (1, H))
    wk = jnp.concatenate([wk_c, wk_r, jnp.zeros((LANES - QK_ROPE, H * MLA_SLOT), F32)], axis=0)
    wv = jnp.pad(w_uv.reshape(KV_LORA, H * V_DIM), ((0, LANES), (0, 0)))
    w_kv = jnp.concatenate([wk, wv], axis=1).astype(BF16)
    absorb = jnp.pad(w_uk.transpose(1, 2, 0), ((0, 0), (0, MLA_SLOT - QK_NOPE), (0, 0)))
    w_absorb = _block_diag(absorb).astype(BF16)
    w_unabsorb = _block_diag(w_uv.transpose(1, 0, 2)).astype(BF16)
    return dict(w_dn=w_dn_p, g_q=g_q, g_kv=g_kv, w_q=w_q, w_kv=w_kv, w_absorb=w_absorb,
                w_unabsorb=w_unabsorb, w_o=w_o.astype(BF16))


def _mla_q_tables(pos):
    half = QK_ROPE // 2
    ua, ub = _rope_unit(pos, half)
    T = pos.shape[0]
    pad = MLA_SLOT - QK_NOPE - QK_ROPE
    a = jnp.concatenate([jnp.ones((T, QK_NOPE), F32), ua, jnp.ones((T, pad), F32)], axis=1)
    b = jnp.concatenate([jnp.zeros((T, QK_NOPE), F32), ub, jnp.zeros((T, pad), F32)], axis=1)
    return jnp.tile(a, (1, MLA_HEADS)), jnp.tile(b, (1, MLA_HEADS))


def _mla_kr_tables(pos):
    return _tables(pos, QK_ROPE // 2, [('r', 1), ('n', LANES - QK_ROPE)])


def _moba_weights(w_in, w_o):
    return dict(w_in=w_in.astype(BF16), w_o=w_o.astype(BF16))


def _moba_tables(pos):
    dh = MOBA_HEAD_DIM
    return _tables(pos, dh // 2, [('r', MOBA_HEADS)] + [('r', 1), ('n', dh)] * MOBA_KV_HEADS)


def _moe_weights(w_router, b_router, w_gate, w_up, w_down, ws_gate, ws_up, ws_down):
    E, D, F = w_gate.shape
    n = MOE_GROUP
    wg = w_gate.astype(BF16).reshape(E // n, n, D, F).transpose(0, 2, 1, 3).reshape(E // n, D, n * F)
    wu = w_up.astype(BF16).reshape(E // n, n, D, F).transpose(0, 2, 1, 3).reshape(E // n, D, n * F)
    w1 = jnp.concatenate([wg, wu], axis=2)
    w2 = w_down.reshape(E // n, n * F, D).astype(BF16)
    return dict(w_router=w_router.T, b_router=b_router.reshape(E, 1),
                w1=w1, w2=w2, wsgu=jnp.concatenate([ws_gate, ws_up], axis=1).astype(BF16),
                wsd=ws_down.astype(BF16))


def _feature_major(cache):
    n, pool, rows = cache.shape[:3]
    nd = cache.ndim
    return cache.transpose(0, 1, *range(3, nd), 2).reshape(n * pool, -1, rows)


def _nsa_cmp_tokens(rows_lohi, w):
    pe_lohi = _cmp_lohi(w['pe_rows'], w['w_lohi'])
    return _cmp_combine(rows_lohi, pe_lohi, w['b1'], w['w2'], w['b2'])


def _nsa_prompt(h, B, T, w, tabs):
    G, dh = NSA_KV_HEADS, NSA_HEAD_DIM
    proj = _proj(h, w['w_ext'], tabs, dh // 2)
    kv0 = 2 * NSA_HEADS * dh
    width = 2 * G * dh
    lohi = _cmp_lohi(proj.reshape(B, T, -1), w['w_lohi'], col_block=kv0 // LANES)
    cmp_tok = _nsa_cmp_tokens(lohi, w)
    y = _attn_nsa(proj, cmp_tok, B, T)
    kv = proj[:, kv0:kv0 + 3 * width].reshape(B, T, 3, G, 2, dh)
    return y, (kv[:, :, 0], kv[:, :, 1], kv[:, -min(WINDOW, T):, 2])


def _nsa_sample(h, past_len, w, tabs, cache_cmp, cache_sel, win_state, slot, page_table):
    B = h.shape[0]
    G, H, dh = NSA_KV_HEADS, NSA_HEADS, NSA_HEAD_DIM
    R = H // G
    n_pool = cache_cmp.shape[1]
    width = 2 * G * dh
    proj = _proj(h, w['w_ext'], tabs, dh // 2)
    pages = page_table + slot * n_pool
    lohi = _cmp_lohi_paged(cache_cmp.reshape(-1, PAGE_SIZE, width), pages, w['w_lohi'],
                           per_step=min(32, page_table.shape[1]))
    cmp_tok = _nsa_cmp_tokens(lohi, w)
    n_sel = -(-(past_len + 1) // SEL_BLOCK)
    n_top = min(SEL_TOP, n_sel)
    q3 = proj[:, :H * dh].reshape(B, H, dh)
    qr3 = proj[:, H * dh:2 * H * dh].reshape(B, H, dh)
    o_cmp, imp = _nsa_dec_cmp(q3, cmp_tok, past_len, n_sel)
    idx = _topk_idx(imp.reshape(B * G, -1), n_top)[:, :n_top].reshape(B, G * n_top)
    n_past_blocks = past_len // SEL_BLOCK
    per_page = PAGE_SIZE // SEL_BLOCK
    sel_pages = jnp.take_along_axis(pages, jnp.minimum(idx, n_past_blocks - 1) // per_page, axis=1)
    kv0 = 2 * H * dh
    new_kv = proj[:, kv0:kv0 + 3 * width].reshape(B, 1, 3 * width)
    gl3 = proj[:, kv0 + 3 * width:].reshape(B, G, LANES)[:, :, :3 * R].reshape(B, G, 3, R)
    gl3 = gl3.transpose(0, 1, 3, 2).reshape(B, H, 3)
    Wn = win_state.shape[2]
    o = _nsa_dec_attn(idx, sel_pages, _feature_major(cache_sel), qr3, new_kv, _feature_major(win_state), slot * B,
                      o_cmp, gl3, n_past_blocks)
    kv = new_kv.reshape(B, 1, 3, G, 2, dh)
    new_win = jnp.concatenate([win_state[slot], kv[:, :, 2]], axis=1)[:, -Wn:]
    return o.reshape(B, H * dh), (kv[:, :, 0], kv[:, :, 1], new_win)


def _mla_front(h, w, q_tabs, kr_tabs):
    dn = _proj(h, w['w_dn'])
    c_q, ckr = _mla_post(dn, w['g_q'], w['g_kv'], kr_tabs)
    q_ext = _proj(c_q, w['w_q'], q_tabs, QK_ROPE // 2)
    return q_ext, ckr


def _mla_prompt(h, B, T, w, q_tabs, kr_tabs):
    q_ext, ckr = _mla_front(h, w, q_tabs, kr_tabs)
    kv_ext = _proj(ckr, w['w_kv'])
    y = _attn_mla(q_ext, kv_ext, B, T)
    return y, (ckr[:, :KV_LORA].reshape(B, T, KV_LORA), ckr[:, KV_LORA:KV_LORA + QK_ROPE].reshape(B, T, QK_ROPE))


def _mla_sample(h, w, q_tabs, kr_tabs, cache_ckv, cache_kr, slot, page_table):
    B = h.shape[0]
    H = MLA_HEADS
    n_pool = cache_ckv.shape[1]
    q_ext, ckr = _mla_front(h, w, q_tabs, kr_tabs)
    q_lat = _proj(q_ext, w['w_absorb']).reshape(B, H, KV_LORA)
    q_rope = q_ext.reshape(B, H, MLA_SLOT)[:, :, QK_NOPE:QK_NOPE + QK_ROPE]
    c_new = ckr[:, :KV_LORA].reshape(B, 1, KV_LORA)
    r_new = ckr[:, KV_LORA:KV_LORA + QK_ROPE].reshape(B, 1, QK_ROPE)
    pages = page_table + slot * n_pool
    o_lat = _mla_decode(pages, cache_ckv.reshape(-1, PAGE_SIZE, KV_LORA), _feature_major(cache_kr),
                        q_lat, q_rope, c_new, r_new, per_step=min(8, page_table.shape[1]))
    y = _proj(o_lat.reshape(B, H * KV_LORA), w['w_unabsorb'])
    return y, (c_new, r_new)


def _moba_prompt(h, B, T, w, tabs):
    KH, dh = MOBA_KV_HEADS, MOBA_HEAD_DIM
    proj = _proj(h, w['w_in'], tabs, dh // 2)
    y = _attn_moba(proj, B, T)
    return y, proj[:, MOBA_HEADS * dh:].reshape(B, T, KH, 2, dh)


def _moba_sample(h, past_len, w, tabs, cache, slot, page_table):
    B = h.shape[0]
    H, KH, dh = MOBA_HEADS, MOBA_KV_HEADS, MOBA_HEAD_DIM
    R = H // KH
    assert past_len % MOBA_BLOCK == 0 and past_len // MOBA_BLOCK >= MOBA_TOP
    n_pool = cache.shape[1]
    width = KH * 2 * dh
    ppb = MOBA_BLOCK // PAGE_SIZE
    proj = _proj(h, w['w_in'], tabs, dh // 2)
    pages = page_table + slot * n_pool
    cache_t = _feature_major(cache)
    means = _moba_means(pages, cache_t, per_step=min(16, page_table.shape[1]))
    q3 = proj[:, :H * dh].reshape(B, H, dh)
    scores = _moba_dec_score(q3, means, past_len // MOBA_BLOCK)
    idx = _topk_idx(scores.reshape(B * H, -1), MOBA_TOP)[:, :MOBA_TOP].reshape(B, H * MOBA_TOP)
    page_ids = jnp.take_along_axis(pages, (idx[:, :, None] * ppb + jnp.arange(ppb)).reshape(B, -1), axis=1)
    new_kv = proj[:, H * dh:].reshape(B, 1, width)
    o = _moba_dec_attn(page_ids, cache_t, q3.reshape(B, KH, R, dh), new_kv)
    return o.reshape(B, H * dh), new_kv.reshape(B, 1, KH, 2, dh)


def _moe_layer(h, w, g, b):
    gate = _moe_router(h, w['w_router'], w['b_router'])
    return _moe(h, gate, w['w1'], w['w2'], w['wsgu'], w['wsd'], g, b)


def kernel(x_prompt, x_sample, cache_nsa_cmp, cache_nsa_sel, state_nsa_win, cache_mla_ckv, cache_mla_krope,
           cache_moba_kv, page_table, nsa_w_in, nsa_cmp_pe, nsa_cmp_w1, nsa_cmp_b1, nsa_cmp_w2, nsa_cmp_b2,
           nsa_w_o, mla_w_dn, mla_g_q, mla_w_uq, mla_g_kv, mla_w_uk, mla_w_uv, mla_w_o, moba_w_in, moba_w_o,
           ln1_g, ln1_b, ln2_g, ln2_b, moe_w_router, moe_b_router, moe_w_gate, moe_w_up, moe_w_down,
           moe_ws_gate, moe_ws_up, moe_ws_down):
    B, T, D = x_prompt.shape
    Bs, Ts, _ = x_sample.shape
    assert Ts == 1
    past_len = page_table.shape[1] * PAGE_SIZE
    assert state_nsa_win.shape[2] == WINDOW and past_len >= WINDOW
    pos_p = jnp.arange(T, dtype=I32)
    pos_s = jnp.full((Bs,), past_len, dtype=I32)
    hp = x_prompt.reshape(B * T, D)
    hs = x_sample.reshape(Bs, D)
    outs = {k: [] for k in ('cmp_p', 'cmp_s', 'sel_p', 'sel_s', 'win_p', 'win_s',
                            'ckv_p', 'ckv_s', 'kr_p', 'kr_s', 'mb_p', 'mb_s')}
    for i in range(DEPTH):
        kind, slot = i % N_MIXERS, i // N_MIXERS
        if kind == MIX_NSA:
            w = _nsa_weights(nsa_w_in[slot], nsa_cmp_pe[slot], nsa_cmp_w1[slot], nsa_cmp_b1[slot],
                             nsa_cmp_w2[slot], nsa_cmp_b2[slot], nsa_w_o[slot])
            yp, (a_p, b_p, c_p) = _nsa_prompt(hp, B, T, w, _nsa_tables(pos_p))
            ys, (a_s, b_s, c_s) = _nsa_sample(hs, past_len, w, _nsa_tables(pos_s), cache_nsa_cmp, cache_nsa_sel,
                                              state_nsa_win, slot, page_table)
            outs['cmp_p'].append(a_p); outs['cmp_s'].append(a_s)
            outs['sel_p'].append(b_p); outs['sel_s'].append(b_s)
            outs['win_p'].append(c_p); outs['win_s'].append(c_s)
        elif kind == MIX_MLA:
            w = _mla_weights(mla_w_dn[slot], mla_g_q[slot], mla_w_uq[slot], mla_g_kv[slot], mla_w_uk[slot],
                             mla_w_uv[slot], mla_w_o[slot])
            yp, (a_p, b_p) = _mla_prompt(hp, B, T, w, _mla_q_tables(pos_p), _mla_kr_tables(pos_p))
            ys, (a_s, b_s) = _mla_sample(hs, w, _mla_q_tables(pos_s), _mla_kr_tables(pos_s), cache_mla_ckv,
                                         cache_mla_krope, slot, page_table)
            outs['ckv_p'].append(a_p); outs['ckv_s'].append(a_s)
            outs['kr_p'].append(b_p); outs['kr_s'].append(b_s)
        else:
            w = _moba_weights(moba_w_in[slot], moba_w_o[slot])
            yp, a_p = _moba_prompt(hp, B, T, w, _moba_tables(pos_p))
            ys, a_s = _moba_sample(hs, past_len, w, _moba_tables(pos_s), cache_moba_kv, slot, page_table)
            outs['mb_p'].append(a_p); outs['mb_s'].append(a_s)
        hp = _proj_ln(yp, w['w_o'], hp, ln1_g[i], ln1_b[i])
        hs = _proj_ln(ys, w['w_o'], hs, ln1_g[i], ln1_b[i])
        mw = _moe_weights(moe_w_router[i], moe_b_router[i], moe_w_gate[i], moe_w_up[i], moe_w_down[i],
                          moe_ws_gate[i], moe_ws_up[i], moe_ws_down[i])
        hp = _moe_layer(hp, mw, ln2_g[i], ln2_b[i])
        hs = _moe_layer(hs, mw, ln2_g[i], ln2_b[i])
    st = lambda k: jnp.stack(outs[k])
    return (hp.reshape(B, T, D), hs.reshape(Bs, Ts, D),
            st('cmp_p'), st('cmp_s'), st('sel_p'), st('sel_s'), st('win_p'), st('win_s'),
            st('ckv_p'), st('ckv_s'), st('kr_p'), st('kr_s'), st('mb_p'), st('mb_s'))
```

```python
import functools

import jax
import jax.numpy as jnp
from jax import lax
from jax.experimental import pallas as pl
from jax.experimental.pallas import tpu as pltpu

D_MODEL = 1024
DEPTH = 4
PAGE_SIZE = 128
N_MIXERS = 3
MIX_NSA, MIX_MLA, MIX_MOBA = 0, 1, 2

ALPHA = (2 * DEPTH) ** 0.25
LN_EPS = 1e-5
RMS_EPS = 1e-6
ROPE_THETA = 10000.0
NEG = -1e30
FORCE = 1e9

NSA_HEADS = 16
NSA_KV_HEADS = 2
NSA_HEAD_DIM = 64
CMP_BLOCK = 32
CMP_STRIDE = 16
CMP_HIDDEN = 128
SEL_BLOCK = 64
SEL_RATIO = SEL_BLOCK // CMP_STRIDE
SEL_TOP = 16
WINDOW = 512

MLA_HEADS = 16
Q_LORA = 384
KV_LORA = 256
QK_NOPE = 64
QK_ROPE = 32
V_DIM = 64
MLA_SLOT = 128

MOBA_HEADS = 16
MOBA_KV_HEADS = 4
MOBA_HEAD_DIM = 64
MOBA_BLOCK = 256
MOBA_TOP = 3

N_EXPERTS = 64
TOP_K = 8
D_EXPERT = 128
D_SHARED = 128
ROUTED_SCALE = 2.5

LANES = 128
FLASH_CHAINS = 1
MOE_GROUP = 4
ONES_ROWS = 16
VMEM_LIMIT = 48 * 1024 * 1024

F32 = jnp.float32
BF16 = jnp.bfloat16
I32 = jnp.int32


def _cp(*sem):
    return pltpu.CompilerParams(dimension_semantics=sem, vmem_limit_bytes=VMEM_LIMIT)


def _dot(a, b):
    return jnp.dot(a, b, preferred_element_type=F32)


def _dot_nt(a, b):
    return lax.dot_general(a, b, (((1,), (1,)), ((), ())), preferred_element_type=F32)


def _dot_exact(a, b):
    return jnp.dot(a, b, preferred_element_type=F32, precision=lax.Precision.HIGHEST)


def _iota(shape, axis):
    return lax.broadcasted_iota(I32, shape, axis)


def _layer_norm(z, g, b):
    mu = jnp.mean(z, axis=-1, keepdims=True)
    d = z - mu
    var = jnp.mean(d * d, axis=-1, keepdims=True)
    return d * lax.rsqrt(var + LN_EPS) * g + b


def _silu(x):
    return x * jax.nn.sigmoid(x)


def _softmax_rows(s, valid):
    s = jnp.where(valid, s, NEG)
    m = jnp.max(s, axis=-1, keepdims=True)
    e = jnp.where(valid, jnp.exp(s - m), 0.0)
    l = jnp.sum(e, axis=-1, keepdims=True)
    return e / jnp.where(l > 0.0, l, 1.0)


def _rank_before(x, n_cols, lane):
    rank = jnp.zeros(x.shape, F32)
    for c in range(n_cols):
        col = x[:, c:c + 1]
        rank = rank + jnp.where(col > x, 1.0, jnp.where(col == x, jnp.where(lane > c, 1.0, 0.0), 0.0))
    return rank


def _proj_kernel(x_ref, w_ref, o_ref):
    o_ref[...] = _dot(x_ref[...].astype(BF16), w_ref[...])


def _proj_rope_kernel(x_ref, w_ref, a_ref, b_ref, mask_ref, o_ref, *, half, tn):
    acc = _dot(x_ref[...].astype(BF16), w_ref[...])
    lane = _iota(acc.shape, 1)
    first = (lane % (2 * half)) < half
    partner = jnp.where(first, pltpu.roll(acc, tn - half, 1), pltpu.roll(acc, half, 1))
    rotary = mask_ref[...] > 0.5
    a = jnp.where(rotary, jnp.concatenate([a_ref[...]] * (tn // LANES), axis=1), 1.0)
    b = jnp.where(rotary, jnp.concatenate([b_ref[...]] * (tn // LANES), axis=1), 0.0)
    o_ref[...] = acc * a + partner * b


def _col_tile(n):
    return next(t for t in (512, 384, 256, 128) if n % t == 0)


def _proj(x, w, tabs=None, half=0):
    M, K = x.shape
    N = w.shape[1]
    tm = min(M, 512)
    tn = _col_tile(N)
    grid = (M // tm, N // tn)
    x_spec = pl.BlockSpec((tm, K), lambda i, j: (i, 0))
    w_spec = pl.BlockSpec((K, tn), lambda i, j: (0, j))
    o_spec = pl.BlockSpec((tm, tn), lambda i, j: (i, j))
    out_shape = jax.ShapeDtypeStruct((M, N), F32)
    if tabs is None:
        return pl.pallas_call(_proj_kernel, out_shape=out_shape, grid=grid, in_specs=[x_spec, w_spec],
                              out_specs=o_spec, compiler_params=_cp("parallel", "parallel"), name="proj")(x, w)
    a, bm, mask = tabs
    nrb = a.shape[0] // tm
    t_spec = pl.BlockSpec((tm, LANES), lambda i, j: (i % nrb, 0))
    m_spec = pl.BlockSpec((1, tn), lambda i, j: (0, j))
    return pl.pallas_call(functools.partial(_proj_rope_kernel, half=half, tn=tn), out_shape=out_shape, grid=grid,
                          in_specs=[x_spec, w_spec, t_spec, t_spec, m_spec], out_specs=o_spec,
                          compiler_params=_cp("parallel", "parallel"), name="proj_rope")(x, w, a, bm, mask)


def _proj_ln_kernel(x_ref, w_ref, r_ref, g_ref, b_ref, o_ref):
    y = _dot(x_ref[...].astype(BF16), w_ref[...])
    o_ref[...] = _layer_norm(ALPHA * r_ref[...] + y, g_ref[...], b_ref[...])


def _proj_ln(x, w, res, g, b):
    M, K = x.shape
    N = w.shape[1]
    tm = min(M, 256)
    row = lambda i: (i, 0)
    fix = lambda i: (0, 0)
    return pl.pallas_call(
        _proj_ln_kernel, out_shape=jax.ShapeDtypeStruct((M, N), F32), grid=(M // tm,),
        in_specs=[pl.BlockSpec((tm, K), row), pl.BlockSpec((K, N), fix), pl.BlockSpec((tm, N), row),
                  pl.BlockSpec((1, N), fix), pl.BlockSpec((1, N), fix)],
        out_specs=pl.BlockSpec((tm, N), row), compiler_params=_cp("parallel"), name="proj_ln",
    )(x, w, res, g.reshape(1, N), b.reshape(1, N))


def _rope_unit(pos, half):
    inv = jnp.power(ROPE_THETA, -jnp.arange(half, dtype=F32) / half)
    ang = pos.astype(F32)[:, None] * inv
    cos, sin = jnp.cos(ang), jnp.sin(ang)
    return jnp.concatenate([cos, cos], axis=1), jnp.concatenate([-sin, sin], axis=1)


def _unit_tables(pos, half):
    ua, ub = _rope_unit(pos, half)
    reps = LANES // (2 * half)
    return jnp.tile(ua, (1, reps)), jnp.tile(ub, (1, reps))


def _column_mask(layout):
    return jnp.concatenate([jnp.full((1, n), v, F32) for v, n in layout], axis=1)


def _tables(pos, half, layout):
    ua, ub = _rope_unit(pos, half)
    T = pos.shape[0]
    a_parts, b_parts = [], []
    for kind, n in layout:
        if kind == 'r':
            a_parts.append(jnp.tile(ua, (1, n)))
            b_parts.append(jnp.tile(ub, (1, n)))
        else:
            a_parts.append(jnp.ones((T, n), F32))
            b_parts.append(jnp.zeros((T, n), F32))
    return jnp.concatenate(a_parts, axis=1), jnp.concatenate(b_parts, axis=1)


def _add_bias(s, bias, heads):
    if heads == 1:
        return s + bias
    tq = s.shape[1] // heads
    return jnp.concatenate([s[:, h * tq:(h + 1) * tq] + bias for h in range(heads)], axis=1)


def _weighted_values(vt_ref, first, e):
    acc = _dot(vt_ref[first], e[:LANES])
    for j in range(1, e.shape[0] // LANES):
        acc = acc + _dot(vt_ref[first + j], e[j * LANES:(j + 1) * LANES])
    return acc


def _flash(qs, k_ref, k_cols, vt_ref, vt_base, m_sc, acc_sc, lo, hi, tk, bias_fn, heads):
    m_sc[...] = jnp.full(m_sc.shape, NEG, F32)
    acc_sc[...] = jnp.zeros(acc_sc.shape, F32)
    sub = tk // LANES

    def body(kt, carry):
        k0 = pl.multiple_of(kt * tk, tk)
        s = _add_bias(_dot_nt(k_ref[pl.ds(k0, tk), k_cols], qs), bias_fn(kt, k0), heads)
        m_old = m_sc[...]
        m_new = jnp.maximum(m_old, jnp.max(s, axis=0, keepdims=True))
        e = jnp.exp(s - m_new).astype(BF16)
        acc_sc[...] = jnp.exp(m_old - m_new) * acc_sc[...] + _weighted_values(vt_ref, vt_base + kt * sub, e)
        m_sc[...] = m_new
        return carry

    lax.fori_loop(lo, hi, body, 0)
    return acc_sc[...]


def _transpose_rows(x):
    eye = jnp.where(_iota((LANES, LANES), 0) == _iota((LANES, LANES), 1), 1.0, 0.0).astype(BF16)
    return _dot_nt(eye, x)


def _store_values_t(vt_ref, base, vt):
    n = vt.shape[1]
    full = jnp.concatenate([jnp.ones((ONES_ROWS, n), F32), vt], axis=0).astype(BF16)
    for j in range(n // LANES):
        vt_ref[base + j] = full[:, j * LANES:(j + 1) * LANES]


def _write_heads_t(o_ref, heads):
    tq = heads[0].shape[1]
    for p in range(len(heads) // 2):
        pair = jnp.concatenate([heads[2 * p], heads[2 * p + 1]], axis=0)
        for c in range(tq // LANES):
            o_ref[c * LANES:(c + 1) * LANES, p * LANES:(p + 1) * LANES] = pair[:, c * LANES:(c + 1) * LANES].T


def _stack_heads(ref, n_heads, dh, scale):
    parts = []
    for r in range(n_heads):
        pair = ref[:, (r // 2) * 2 * dh:(r // 2 + 1) * 2 * dh] * scale
        lane = _iota(pair.shape, 1)
        own = (lane < dh) if r % 2 == 0 else (lane >= dh)
        parts.append(jnp.where(own, pair, 0.0))
    return jnp.concatenate(parts, axis=0).astype(BF16)


def _dup_keys(kv, dh):
    return jnp.where(_iota(kv.shape, 1) < dh, kv, pltpu.roll(kv, dh, 1))


def _cmp_lohi_kernel(*refs, n_pages, rows_per_page, feature_major):
    G = NSA_KV_HEADS
    if feature_major:
        refs, x_sc = refs[1:-1], refs[-1]
    row_refs, w_ref, o_ref = refs[:G * n_pages], refs[G * n_pages], refs[G * n_pages + 1]
    cpp = rows_per_page // CMP_STRIDE
    nh = w_ref.shape[2] // 2

    def chunk_rows(g, r):
        if feature_major:
            return x_sc[pl.ds(r, n_pages * cpp, stride=CMP_STRIDE), :]
        return row_refs[g][0, pl.ds(r, cpp, stride=CMP_STRIDE), :]

    for g in range(G):
        if feature_major:
            for p in range(n_pages):
                x_sc[p * rows_per_page:(p + 1) * rows_per_page, :] = _transpose_rows(
                    row_refs[g * n_pages + p][0].astype(BF16))
        acc = jnp.zeros((n_pages * cpp, 2 * nh), F32)
        for r2 in range(CMP_STRIDE // 2):
            xr = jnp.concatenate([chunk_rows(g, 2 * r2), chunk_rows(g, 2 * r2 + 1)], axis=1)
            acc = acc + _dot(xr.astype(BF16), w_ref[r2])
        o_ref[0, :, g * nh:(g + 1) * nh] = acc[:, :nh]
        o_ref[0, :, (G + g) * nh:(G + g + 1) * nh] = acc[:, nh:]


def _cmp_lohi(rows, w_lohi, col_block=0):
    B, L = rows.shape[0], rows.shape[1]
    G = NSA_KV_HEADS
    nout = G * w_lohi.shape[2]
    row_specs = [pl.BlockSpec((1, L, LANES), functools.partial(lambda b, g: (b, 0, col_block + g), g=g))
                 for g in range(G)]
    return pl.pallas_call(
        functools.partial(_cmp_lohi_kernel, n_pages=1, rows_per_page=L, feature_major=False),
        out_shape=jax.ShapeDtypeStruct((B, L // CMP_STRIDE, nout), F32), grid=(B,),
        in_specs=row_specs + [pl.BlockSpec(w_lohi.shape, lambda b: (0, 0, 0))],
        out_specs=pl.BlockSpec((1, L // CMP_STRIDE, nout), lambda b: (b, 0, 0)),
        compiler_params=_cp("parallel"), name="cmp_lohi",
    )(*([rows] * G), w_lohi)


def _cmp_lohi_paged(cache_t, pages, w_lohi, per_step):
    B, n_pages = pages.shape
    G = NSA_KV_HEADS
    nout = G * w_lohi.shape[2]
    cpp = PAGE_SIZE // CMP_STRIDE
    page_specs = [pl.BlockSpec((1, LANES, PAGE_SIZE), functools.partial(
        lambda b, s, pt, p, g: (pt[b, s * per_step + p], g, 0), p=p, g=g))
        for g in range(G) for p in range(per_step)]
    grid_spec = pltpu.PrefetchScalarGridSpec(
        num_scalar_prefetch=1, grid=(B, n_pages // per_step),
        in_specs=page_specs + [pl.BlockSpec(w_lohi.shape, lambda b, s, pt: (0, 0, 0))],
        out_specs=pl.BlockSpec((1, per_step * cpp, nout), lambda b, s, pt: (b, s, 0)),
        scratch_shapes=[pltpu.VMEM((per_step * PAGE_SIZE, LANES), F32)])
    return pl.pallas_call(
        functools.partial(_cmp_lohi_kernel, n_pages=per_step, rows_per_page=PAGE_SIZE, feature_major=True),
        out_shape=jax.ShapeDtypeStruct((B, n_pages * cpp, nout), F32), grid_spec=grid_spec,
        compiler_params=_cp("parallel", "parallel"), name="cmp_lohi_paged",
    )(pages, *([cache_t] * (G * per_step)), w_lohi)


def _cmp_combine_kernel(lohi_ref, pe_ref, b1_ref, w2_ref, b2_ref, o_ref):
    nch = lohi_ref.shape[1]
    nh = lohi_ref.shape[2] // 2
    lo = lohi_ref[0, :, :nh]
    hi_next = pltpu.roll(lohi_ref[0, :, nh:], nch - 1, 0)
    pe = pe_ref[0, 0:1, :nh] + pe_ref[0, 1:2, nh:]
    hid = jax.nn.gelu(lo + hi_next + pe + b1_ref[...])
    o_ref[0] = _dot(hid.astype(BF16), w2_ref[...]) + b2_ref[...]


def _cmp_combine(lohi, pe_lohi, b1, w2, b2):
    B, nch, n2 = lohi.shape
    nh = n2 // 2
    nout = w2.shape[1]
    fix2 = lambda b: (0, 0)
    return pl.pallas_call(
        _cmp_combine_kernel, out_shape=jax.ShapeDtypeStruct((B, nch, nout), F32), grid=(B,),
        in_specs=[pl.BlockSpec((1, nch, n2), lambda b: (b, 0, 0)), pl.BlockSpec((1, 2, n2), lambda b: (0, 0, 0)),
                  pl.BlockSpec((1, nh), fix2), pl.BlockSpec((nh, nout), fix2), pl.BlockSpec((1, nout), fix2)],
        out_specs=pl.BlockSpec((1, nch, nout), lambda b: (b, 0, 0)),
        compiler_params=_cp("parallel"), name="cmp_combine",
    )(lohi, pe_lohi, b1, w2, b2)


def _sel_importance(p_sum, n_lanes):
    nc = p_sum.shape[1]
    c = _iota((nc, n_lanes), 0)
    j = _iota((nc, n_lanes), 1)
    a = jnp.where(c >= SEL_RATIO * j - 1, jnp.where(c <= SEL_RATIO * j + SEL_RATIO - 1, 1.0, 0.0), 0.0)
    return _dot_exact(p_sum, a)


def _attn_nsa_kernel(q_ref, qr_ref, cmp_ref, sel_ref, win_ref, gl_ref, o_ref,
                     k2s, vts, k2w, vtw, bias_ref, m_sc, acc_sc, *, T, tq, tk, n_sel, n_top):
    i = pl.program_id(2)
    t0 = i * tq
    R = NSA_HEADS // NSA_KV_HEADS
    dh = NSA_HEAD_DIM
    scale = dh ** -0.5
    wk = min(T, WINDOW + tq)

    @pl.when(i == 0)
    def _():
        for src, k_sc, vt_sc in ((sel_ref, k2s, vts), (win_ref, k2w, vtw)):
            kv = src[...]
            k_sc[...] = _dup_keys(kv, dh).astype(BF16)
            _store_values_t(vt_sc, 0, _transpose_rows(kv.astype(BF16))[dh:])

    tpos = t0 + _iota((1, tq), 1)
    cmp_tok = cmp_ref[0]
    nc = cmp_tok.shape[0]
    s = _dot_nt(_dup_keys(cmp_tok, dh).astype(BF16), _stack_heads(q_ref, R, dh, scale))
    cvalid = (_iota((nc, 1), 0) * CMP_STRIDE + CMP_BLOCK - 1) <= tpos
    probs = []
    for h in range(R):
        sh = jnp.where(cvalid, s[:, h * tq:(h + 1) * tq], NEG)
        e = jnp.where(cvalid, jnp.exp(sh - jnp.max(sh, axis=0, keepdims=True)), 0.0)
        l = jnp.sum(e, axis=0, keepdims=True)
        probs.append(e / jnp.where(l > 0.0, l, 1.0))
    p_sum = probs[0]
    for h in range(1, R):
        p_sum = p_sum + probs[h]
    vct = _transpose_rows(cmp_tok.astype(BF16))[dh:].astype(BF16)
    o_cmp = _dot(vct, jnp.concatenate(probs, axis=1).astype(BF16))

    ns = bias_ref.shape[0] // SEL_BLOCK
    blk = _iota((ns, tq), 0)
    cur = tpos // SEL_BLOCK
    c_idx = _iota((ns, nc), 1)
    lo_c = SEL_RATIO * _iota((ns, nc), 0) - 1
    spread = jnp.where(c_idx >= lo_c, jnp.where(c_idx <= lo_c + SEL_RATIO, 1.0, 0.0), 0.0)
    imp = _dot_exact(spread, p_sum)
    forced = jnp.where(blk == 0, 1.0, jnp.where(blk == cur, 1.0, jnp.where(blk == cur - 1, 1.0, 0.0)))
    imp = jnp.where(blk > cur, NEG, jnp.where(forced > 0.5, FORCE, imp))
    rank = jnp.zeros(imp.shape, F32)
    for c in range(n_sel):
        row = imp[c:c + 1, :]
        rank = rank + jnp.where(row > imp, 1.0, jnp.where(row == imp, jnp.where(blk > c, 1.0, 0.0), 0.0))
    chosen = jnp.where(blk < n_sel, jnp.where(rank < n_top, 1.0, 0.0), 0.0)
    for j in range(n_sel):
        picked = jnp.broadcast_to(chosen[j:j + 1, :], (SEL_BLOCK, tq))
        kpos = j * SEL_BLOCK + _iota((SEL_BLOCK, 1), 0)
        bias_ref[j * SEL_BLOCK:(j + 1) * SEL_BLOCK, :] = jnp.where(
            picked > 0.5, jnp.where(kpos <= tpos, 0.0, NEG), NEG)

    hi = (t0 + tq + tk - 1) // tk
    qrs = _stack_heads(qr_ref, R, dh, scale)
    acc = _flash(qrs, k2s, slice(None), vts, 0, m_sc, acc_sc, 0, hi, tk,
                 lambda kt, k0: bias_ref[pl.ds(k0, tk), :], R)
    o_sel = acc[ONES_ROWS:] / acc[0:1]
    kw0 = pl.multiple_of(jnp.clip(t0 - WINDOW, 0, T - wk), LANES)
    d = tpos - (kw0 + _iota((wk, 1), 0))
    win_bias = jnp.where(d >= 0, jnp.where(d <= WINDOW, 0.0, NEG), NEG)
    s = _add_bias(_dot_nt(k2w[pl.ds(kw0, wk), :], qrs), win_bias, R)
    e = jnp.exp(s - jnp.max(s, axis=0, keepdims=True)).astype(BF16)
    acc = _weighted_values(vtw, kw0 // LANES, e)
    o_win = acc[ONES_ROWS:] / acc[0:1]
    gate = jax.nn.sigmoid(gl_ref[...].T)
    heads = []
    for h in range(R):
        hs = slice(h * tq, (h + 1) * tq)
        heads.append(gate[h:h + 1] * o_cmp[:, hs] + gate[R + h:R + h + 1] * o_sel[:, hs]
                     + gate[2 * R + h:2 * R + h + 1] * o_win[:, hs])
    _write_heads_t(o_ref, heads)


def _attn_nsa(proj, cmp_tok, B, T, tk=None):
    G = NSA_KV_HEADS
    R = NSA_HEADS // G
    tq = LANES
    tk = tk or min(T, 512)
    nq = T // tq
    n_sel = -(-T // SEL_BLOCK)
    n_top = min(SEL_TOP, n_sel)
    qw = NSA_HEADS * NSA_HEAD_DIM // G
    kv0 = 2 * NSA_HEADS * NSA_HEAD_DIM // LANES
    gl0 = kv0 + 3 * G
    nch = cmp_tok.shape[1]
    dh = NSA_HEAD_DIM
    kern = functools.partial(_attn_nsa_kernel, T=T, tq=tq, tk=tk, n_sel=n_sel, n_top=n_top)
    return pl.pallas_call(
        kern, out_shape=jax.ShapeDtypeStruct((B * T, NSA_HEADS * dh), F32), grid=(B, G, nq),
        in_specs=[pl.BlockSpec((tq, qw), lambda b, g, i: (b * nq + i, g)),
                  pl.BlockSpec((tq, qw), lambda b, g, i: (b * nq + i, G + g)),
                  pl.BlockSpec((1, nch, 2 * dh), lambda b, g, i: (b, 0, g)),
                  pl.BlockSpec((T, 2 * dh), lambda b, g, i: (b, kv0 + G + g)),
                  pl.BlockSpec((T, 2 * dh), lambda b, g, i: (b, kv0 + 2 * G + g)),
                  pl.BlockSpec((tq, LANES), lambda b, g, i: (b * nq + i, gl0 + g))],
        out_specs=pl.BlockSpec((tq, qw), lambda b, g, i: (b * nq + i, g)),
        scratch_shapes=[pltpu.VMEM((T, 2 * dh), BF16), pltpu.VMEM((T // LANES, ONES_ROWS + dh, LANES), BF16)] * 2 + [
            pltpu.VMEM((T, tq), F32), pltpu.VMEM((1, R * tq), F32), pltpu.VMEM((ONES_ROWS + dh, R * tq), F32)],
        compiler_params=_cp("parallel", "parallel", "arbitrary"), name="attn_nsa",
    )(proj, proj, cmp_tok, proj, proj, proj)


def _nsa_dec_cmp_kernel(q_ref, cmp_ref, o_ref, imp_ref, *, t, n_sel):
    G = NSA_KV_HEADS
    R = NSA_HEADS // G
    dh = NSA_HEAD_DIM
    nc = cmp_ref.shape[1]
    nl = imp_ref.shape[2]
    cvalid = (_iota((1, nc), 1) * CMP_STRIDE + CMP_BLOCK - 1) <= t
    blk = _iota((1, nl), 1)
    cur = t // SEL_BLOCK
    for g in range(G):
        kc = cmp_ref[0, :, 2 * g * dh:(2 * g + 1) * dh].astype(BF16)
        vc = cmp_ref[0, :, (2 * g + 1) * dh:(2 * g + 2) * dh].astype(BF16)
        qg = (q_ref[0, g * R:(g + 1) * R, :] * dh ** -0.5).astype(BF16)
        p = _softmax_rows(_dot_nt(qg, kc), cvalid)
        o_ref[0, g * R:(g + 1) * R, :] = _dot(p.astype(BF16), vc)
        imp = _sel_importance(jnp.sum(p, axis=0, keepdims=True), nl)
        forced = jnp.where(blk == 0, 1.0, jnp.where(blk == cur, 1.0, jnp.where(blk == cur - 1, 1.0, 0.0)))
        imp = jnp.where(blk > cur, NEG, jnp.where(forced > 0.5, FORCE, imp))
        imp_ref[0, g:g + 1, :] = jnp.where(blk < n_sel, imp, -jnp.inf)


def _nsa_dec_cmp(q3, cmp_tok, t, n_sel):
    B = q3.shape[0]
    nl = -(-n_sel // LANES) * LANES
    nc = cmp_tok.shape[1]
    blk3 = lambda b: (b, 0, 0)
    return pl.pallas_call(
        functools.partial(_nsa_dec_cmp_kernel, t=t, n_sel=n_sel),
        out_shape=(jax.ShapeDtypeStruct(q3.shape, F32), jax.ShapeDtypeStruct((B, NSA_KV_HEADS, nl), F32)),
        grid=(B,),
        in_specs=[pl.BlockSpec((1,) + q3.shape[1:], blk3), pl.BlockSpec((1, nc, cmp_tok.shape[2]), blk3)],
        out_specs=(pl.BlockSpec((1,) + q3.shape[1:], blk3), pl.BlockSpec((1, NSA_KV_HEADS, nl), blk3)),
        compiler_params=_cp("parallel"), name="nsa_dec_cmp",
    )(q3, cmp_tok)


def _topk_idx_kernel(x_ref, o_ref, *, k):
    x = x_ref[...]
    lane = _iota(x.shape, 1)
    out_lane = _iota(o_ref.shape, 1)
    out = jnp.zeros(o_ref.shape, I32)
    big = x.shape[1]
    for n in range(k):
        m = jnp.max(x, axis=-1, keepdims=True)
        idx = jnp.min(jnp.where(x == m, lane, big), axis=-1, keepdims=True)
        out = jnp.where(out_lane == n, idx, out)
        x = jnp.where(lane == idx, -jnp.inf, x)
    o_ref[...] = out


def _topk_idx(x, k):
    rows = x.shape[0]
    return pl.pallas_call(functools.partial(_topk_idx_kernel, k=k),
                          out_shape=jax.ShapeDtypeStruct((rows, LANES), I32), name="topk_idx")(x)


def _nsa_dec_attn_kernel(*refs, n_top, n_past_blocks):
    idx_ref, rb_ref = refs[0], refs[1]
    G = NSA_KV_HEADS
    R = NSA_HEADS // G
    dh = NSA_HEAD_DIM
    blocks = refs[2:2 + G * n_top]
    qr_ref, new_ref, win_ref, ocmp_ref, gl_ref, o_ref = refs[2 + G * n_top:]
    b = pl.program_id(0)
    scale = dh ** -0.5
    per_page = PAGE_SIZE // SEL_BLOCK
    nk = n_top * PAGE_SIZE
    lane = _iota((1, nk), 1)
    for g in range(G):
        qg = qr_ref[0, g * R:(g + 1) * R, :] * scale
        qb = qg.astype(BF16)
        kt = jnp.concatenate([blocks[g * n_top + n][0, :dh, :] for n in range(n_top)], axis=1).astype(BF16)
        vt = jnp.concatenate([blocks[g * n_top + n][0, dh:, :] for n in range(n_top)], axis=1).astype(BF16)
        valid = jnp.zeros((1, nk), F32)
        for n in range(n_top):
            blk = idx_ref[b, g * n_top + n]
            flag = jnp.where(blk < n_past_blocks, 1.0, 0.0)
            in_block = jnp.where((lane % PAGE_SIZE) // SEL_BLOCK == blk % per_page, flag, 0.0)
            valid = jnp.where(lane // PAGE_SIZE == n, in_block, valid)
        valid = valid > 0.5
        k_new = new_ref[0, :, (2 * G + 2 * g) * dh:(2 * G + 2 * g + 1) * dh]
        v_new = new_ref[0, :, (2 * G + 2 * g + 1) * dh:(2 * G + 2 * g + 2) * dh]
        s = jnp.where(valid, _dot(qb, kt), NEG)
        s_new = jnp.sum(qg * k_new, axis=-1, keepdims=True)
        m = jnp.maximum(jnp.max(s, axis=-1, keepdims=True), s_new)
        e = jnp.where(valid, jnp.exp(s - m), 0.0)
        e_new = jnp.exp(s_new - m)
        o_sel = (_dot_nt(e.astype(BF16), vt) + e_new * v_new) / (jnp.sum(e, axis=-1, keepdims=True) + e_new)
        kwin = win_ref[0, 2 * g * dh:(2 * g + 1) * dh, :].astype(BF16)
        vwin = win_ref[0, (2 * g + 1) * dh:(2 * g + 2) * dh, :].astype(BF16)
        k_new = new_ref[0, :, (4 * G + 2 * g) * dh:(4 * G + 2 * g + 1) * dh]
        v_new = new_ref[0, :, (4 * G + 2 * g + 1) * dh:(4 * G + 2 * g + 2) * dh]
        s = _dot(qb, kwin)
        s_new = jnp.sum(qg * k_new, axis=-1, keepdims=True)
        m = jnp.maximum(jnp.max(s, axis=-1, keepdims=True), s_new)
        e = jnp.exp(s - m)
        e_new = jnp.exp(s_new - m)
        o_win = (_dot_nt(e.astype(BF16), vwin) + e_new * v_new) / (jnp.sum(e, axis=-1, keepdims=True) + e_new)
        gate = jax.nn.sigmoid(gl_ref[0, g * R:(g + 1) * R, :])
        o_ref[0, g * R:(g + 1) * R, :] = (gate[:, 0:1] * ocmp_ref[0, g * R:(g + 1) * R, :]
                                          + gate[:, 1:2] * o_sel + gate[:, 2:3] * o_win)


def _nsa_dec_attn(idx, sel_pages, cache_t, qr3, new_kv, win_buf, win_index0, o_cmp, gl3, n_past_blocks):
    B = qr3.shape[0]
    G = NSA_KV_HEADS
    n_top = idx.shape[1] // G
    dh = NSA_HEAD_DIM
    blk3 = lambda b, idx, rb: (b, 0, 0)
    block_specs = [pl.BlockSpec((1, 2 * dh, PAGE_SIZE), functools.partial(
        lambda b, idx, rb, n, g: (rb[b, n], g, 0), n=g * n_top + n, g=g)) for g in range(G) for n in range(n_top)]
    grid_spec = pltpu.PrefetchScalarGridSpec(
        num_scalar_prefetch=2, grid=(B,),
        in_specs=block_specs + [
            pl.BlockSpec((1,) + qr3.shape[1:], blk3), pl.BlockSpec((1,) + new_kv.shape[1:], blk3),
            pl.BlockSpec((1,) + win_buf.shape[1:], lambda b, idx, rb: (win_index0 + b, 0, 0)),
            pl.BlockSpec((1,) + o_cmp.shape[1:], blk3), pl.BlockSpec((1,) + gl3.shape[1:], blk3)],
        out_specs=pl.BlockSpec((1,) + qr3.shape[1:], blk3))
    return pl.pallas_call(
        functools.partial(_nsa_dec_attn_kernel, n_top=n_top, n_past_blocks=n_past_blocks),
        out_shape=jax.ShapeDtypeStruct(qr3.shape, F32), grid_spec=grid_spec,
        compiler_params=_cp("parallel"), name="nsa_dec_attn",
    )(idx, sel_pages, *([cache_t] * (G * n_top)), qr3, new_kv, win_buf, o_cmp, gl3)


def _mla_post_kernel(dn_ref, gq_ref, gkv_ref, a_ref, b_ref, cq_ref, ckr_ref):
    x = dn_ref[...]
    cq = x[:, :Q_LORA]
    cq_ref[...] = cq * lax.rsqrt(jnp.mean(cq * cq, axis=-1, keepdims=True) + RMS_EPS) * gq_ref[...]
    ckv = x[:, Q_LORA:Q_LORA + KV_LORA]
    ckv = ckv * lax.rsqrt(jnp.mean(ckv * ckv, axis=-1, keepdims=True) + RMS_EPS) * gkv_ref[...]
    kr = x[:, Q_LORA + KV_LORA:]
    half = QK_ROPE // 2
    first = (_iota(kr.shape, 1) % QK_ROPE) < half
    partner = jnp.where(first, pltpu.roll(kr, LANES - half, 1), pltpu.roll(kr, half, 1))
    ckr_ref[...] = jnp.concatenate([ckv, kr * a_ref[...] + partner * b_ref[...]], axis=1)


def _mla_post(dn, g_q, g_kv, tabs):
    M, N = dn.shape
    tm = min(M, 512)
    nrb = tabs[0].shape[0] // tm
    row = lambda i: (i, 0)
    fix = lambda i: (0, 0)
    tab = pl.BlockSpec((tm, LANES), lambda i: (i % nrb, 0))
    return pl.pallas_call(
        _mla_post_kernel,
        out_shape=(jax.ShapeDtypeStruct((M, Q_LORA), F32), jax.ShapeDtypeStruct((M, KV_LORA + LANES), F32)),
        grid=(M // tm,),
        in_specs=[pl.BlockSpec((tm, N), row), pl.BlockSpec((1, Q_LORA), fix), pl.BlockSpec((1, KV_LORA), fix),
                  tab, tab],
        out_specs=(pl.BlockSpec((tm, Q_LORA), row), pl.BlockSpec((tm, KV_LORA + LANES), row)),
        compiler_params=_cp("parallel"), name="mla_post",
    )(dn, g_q.reshape(1, -1), g_kv.reshape(1, -1), *tabs)


def _attn_mla_kernel(q_ref, k_ref, v_ref, o_ref, k_sc, v_sc, m_sc, acc_sc, *, tq, tk):
    i = pl.program_id(2)
    t0 = i * tq
    scale = (QK_NOPE + QK_ROPE) ** -0.5

    n_vt = k_ref.shape[0] // LANES

    @pl.when(i == 0)
    def _():
        k_sc[...] = k_ref[...].astype(BF16)
        vt = _transpose_rows(v_ref[...].astype(BF16))
        for hh in range(2):
            _store_values_t(v_sc, hh * n_vt, vt[hh * V_DIM:(hh + 1) * V_DIM])

    tpos = t0 + _iota((1, tq), 1)
    hi = (t0 + tq + tk - 1) // tk
    causal = lambda kt, k0: jnp.where(k0 + _iota((tk, 1), 0) <= tpos, 0.0, NEG)
    heads = []
    for hh in range(2):
        qh = (q_ref[:, hh * MLA_SLOT:(hh + 1) * MLA_SLOT] * scale).astype(BF16)
        acc = _flash(qh, k_sc, slice(hh * MLA_SLOT, (hh + 1) * MLA_SLOT), v_sc, hh * n_vt, m_sc, acc_sc, 0, hi, tk,
                     causal, 1)
        heads.append(acc[ONES_ROWS:] / acc[0:1])
    _write_heads_t(o_ref, heads)


def _attn_mla(q_ext, kv_ext, B, T, tk=None):
    H = MLA_HEADS
    tq = min(T, 512)
    tk = tk or min(T, 512)
    nq = T // tq
    v0 = H * MLA_SLOT // LANES
    return pl.pallas_call(
        functools.partial(_attn_mla_kernel, tq=tq, tk=tk),
        out_shape=jax.ShapeDtypeStruct((B * T, H * V_DIM), F32), grid=(B, H // 2, nq),
        in_specs=[pl.BlockSpec((tq, 2 * MLA_SLOT), lambda b, h, i: (b * nq + i, h)),
                  pl.BlockSpec((T, 2 * MLA_SLOT), lambda b, h, i: (b, h)),
                  pl.BlockSpec((T, 2 * V_DIM), lambda b, h, i: (b, v0 + h))],
        out_specs=pl.BlockSpec((tq, 2 * V_DIM), lambda b, h, i: (b * nq + i, h)),
        scratch_shapes=[pltpu.VMEM((T, 2 * MLA_SLOT), BF16),
                        pltpu.VMEM((2 * (T // LANES), ONES_ROWS + V_DIM, LANES), BF16),
                        pltpu.VMEM((1, tq), F32), pltpu.VMEM((ONES_ROWS + V_DIM, tq), F32)],
        compiler_params=_cp("parallel", "parallel", "arbitrary"), name="attn_mla",
    )(q_ext, kv_ext, kv_ext)


def _mla_decode_kernel(*refs, per_step):
    pt_ref = refs[0]
    ckv_pages = refs[1:1 + per_step]
    kr_pages = refs[1 + per_step:1 + 2 * per_step]
    ql_ref, qr_ref, cnew_ref, rnew_ref, o_ref, m_sc, l_sc, acc_sc = refs[1 + 2 * per_step:]
    s_idx = pl.program_id(1)
    scale = (QK_NOPE + QK_ROPE) ** -0.5
    ql = ql_ref[0] * scale
    qr = qr_ref[0] * scale

    @pl.when(s_idx == 0)
    def _():
        c_new = cnew_ref[0]
        s_new = (jnp.sum(ql * c_new, axis=-1, keepdims=True) + jnp.sum(qr * rnew_ref[0], axis=-1, keepdims=True))
        m_sc[...] = s_new
        l_sc[...] = jnp.ones(l_sc.shape, F32)
        acc_sc[...] = jnp.broadcast_to(c_new, acc_sc.shape)

    qlb = ql.astype(BF16)
    qrb = qr.astype(BF16)
    cs = [ref[0].astype(BF16) for ref in ckv_pages]
    s = jnp.concatenate([_dot_nt(qlb, c) + _dot(qrb, r[0].astype(BF16)) for c, r in zip(cs, kr_pages)], axis=1)
    m_old = m_sc[...]
    m_new = jnp.maximum(m_old, jnp.max(s, axis=-1, keepdims=True))
    alpha = jnp.exp(m_old - m_new)
    e = jnp.exp(s - m_new).astype(BF16)
    pv = _dot(e[:, :PAGE_SIZE], cs[0])
    for p in range(1, per_step):
        pv = pv + _dot(e[:, p * PAGE_SIZE:(p + 1) * PAGE_SIZE], cs[p])
    l_sc[...] = alpha * l_sc[...] + jnp.sum(e.astype(F32), axis=-1, keepdims=True)
    acc_sc[...] = alpha * acc_sc[...] + pv
    m_sc[...] = m_new

    @pl.when(s_idx == pl.num_programs(1) - 1)
    def _():
        o_ref[0] = acc_sc[...] / l_sc[...]


def _mla_decode(pages, cache_ckv, cache_kr, q_lat, q_rope, c_new, r_new, per_step):
    B, n_pages = pages.shape
    H = MLA_HEADS
    blk3 = lambda b, s, pt: (b, 0, 0)
    page_map = lambda p: functools.partial(lambda b, s, pt, p: (pt[b, s * per_step + p], 0, 0), p=p)
    grid_spec = pltpu.PrefetchScalarGridSpec(
        num_scalar_prefetch=1, grid=(B, n_pages // per_step),
        in_specs=([pl.BlockSpec((1, PAGE_SIZE, KV_LORA), page_map(p)) for p in range(per_step)]
                  + [pl.BlockSpec((1, QK_ROPE, PAGE_SIZE), page_map(p)) for p in range(per_step)]
                  + [pl.BlockSpec((1, H, KV_LORA), blk3), pl.BlockSpec((1, H, QK_ROPE), blk3),
                     pl.BlockSpec((1, 1, KV_LORA), blk3), pl.BlockSpec((1, 1, QK_ROPE), blk3)]),
        out_specs=pl.BlockSpec((1, H, KV_LORA), blk3),
        scratch_shapes=[pltpu.VMEM((H, 1), F32), pltpu.VMEM((H, 1), F32), pltpu.VMEM((H, KV_LORA), F32)])
    return pl.pallas_call(
        functools.partial(_mla_decode_kernel, per_step=per_step),
        out_shape=jax.ShapeDtypeStruct((B, H, KV_LORA), F32), grid_spec=grid_spec,
        compiler_params=_cp("parallel", "arbitrary"), name="mla_decode",
    )(pages, *([cache_ckv] * per_step), *([cache_kr] * per_step), q_lat, q_rope, c_new, r_new)


def _attn_moba_kernel(q_ref, kv_ref, o_ref, k2, vt_sc, mean_sc, mask_sc, m_sc, acc_sc, *, T, tq, tk, nb, n_top):
    i = pl.program_id(2)
    t0 = i * tq
    R = MOBA_HEADS // MOBA_KV_HEADS
    dh = MOBA_HEAD_DIM
    L = R * tq
    nbp = mean_sc.shape[0]
    bpt = tk // MOBA_BLOCK

    @pl.when(i == 0)
    def _():
        kv = kv_ref[...]
        k2[...] = _dup_keys(kv, dh).astype(BF16)
        _store_values_t(vt_sc, 0, _transpose_rows(kv.astype(BF16))[dh:])
        mean_sc[...] = jnp.zeros(mean_sc.shape, F32)
        for j in range(T // MOBA_BLOCK):
            blk_rows = _dup_keys(kv_ref[j * MOBA_BLOCK:(j + 1) * MOBA_BLOCK, :], dh)
            mean_sc[j:j + 1, :] = jnp.sum(blk_rows, axis=0, keepdims=True) / MOBA_BLOCK

    tpos = t0 + _iota((1, L), 1) % tq
    cur = tpos // MOBA_BLOCK
    jb = _iota((nbp, L), 0)
    qs = _stack_heads(q_ref, R, dh, dh ** -0.5)
    block_mask = jnp.where(jb == cur, 1.0, 0.0)
    if n_top > 0:
        gm = jnp.where(jb < cur, _dot_nt(mean_sc[...].astype(BF16), qs), NEG)
        rank = jnp.zeros(gm.shape, F32)
        for c in range(nb):
            row = gm[c:c + 1, :]
            rank = rank + jnp.where(row > gm, 1.0, jnp.where(row == gm, jnp.where(jb > c, 1.0, 0.0), 0.0))
        block_mask = jnp.where(jb < cur, jnp.where(rank < n_top, 1.0, 0.0), block_mask)
    mask_sc[...] = block_mask

    def bias(kt, k0):
        parts = []
        for j in range(bpt):
            picked = mask_sc[pl.ds(kt * bpt + j, 1), :]
            kpos = k0 + j * MOBA_BLOCK + _iota((MOBA_BLOCK, 1), 0)
            parts.append(jnp.where(picked > 0.5, jnp.where(kpos <= tpos, 0.0, NEG), NEG))
        return parts[0] if bpt == 1 else jnp.concatenate(parts, axis=0)

    hi = (t0 + tq + tk - 1) // tk
    acc = _flash(qs, k2, slice(None), vt_sc, 0, m_sc, acc_sc, 0, hi, tk, bias, 1)
    o = acc[ONES_ROWS:] / acc[0:1]
    _write_heads_t(o_ref, [o[:, r * tq:(r + 1) * tq] for r in range(R)])


def _attn_moba(proj, B, T, tk=None):
    KH = MOBA_KV_HEADS
    R = MOBA_HEADS // KH
    dh = MOBA_HEAD_DIM
    tq = min(T, 256)
    tk = tk or min(T, 512)
    nq = T // tq
    nb = (T - 1) // MOBA_BLOCK
    n_top = min(MOBA_TOP, nb)
    qw = MOBA_HEADS * dh // KH
    kv0 = MOBA_HEADS * dh // LANES
    nbp = max(T // MOBA_BLOCK, 8)
    return pl.pallas_call(
        functools.partial(_attn_moba_kernel, T=T, tq=tq, tk=tk, nb=nb, n_top=n_top),
        out_shape=jax.ShapeDtypeStruct((B * T, MOBA_HEADS * dh), F32), grid=(B, KH, nq),
        in_specs=[pl.BlockSpec((tq, qw), lambda b, h, i: (b * nq + i, h)),
                  pl.BlockSpec((T, 2 * dh), lambda b, h, i: (b, kv0 + h))],
        out_specs=pl.BlockSpec((tq, qw), lambda b, h, i: (b * nq + i, h)),
        scratch_shapes=[pltpu.VMEM((T, 2 * dh), BF16), pltpu.VMEM((T // LANES, ONES_ROWS + dh, LANES), BF16),
                        pltpu.VMEM((nbp, 2 * dh), F32), pltpu.VMEM((nbp, R * tq), F32),
                        pltpu.VMEM((1, R * tq), F32), pltpu.VMEM((ONES_ROWS + dh, R * tq), F32)],
        compiler_params=_cp("parallel", "parallel", "arbitrary"), name="attn_moba",
    )(proj, proj)


def _moba_means_kernel(*refs, per_step):
    pages, o_ref = refs[1:1 + per_step], refs[1 + per_step]
    KH, dh = MOBA_KV_HEADS, MOBA_HEAD_DIM
    ppb = MOBA_BLOCK // PAGE_SIZE
    s = pl.program_id(1)

    @pl.when(s == 0)
    def _():
        o_ref[...] = jnp.zeros(o_ref.shape, F32)

    lane = _iota(o_ref.shape[1:], 1)
    out = o_ref[0]
    for j in range(per_step // ppb):
        cols = []
        for kh in range(KH):
            k_rows = slice(2 * kh * dh, (2 * kh + 1) * dh)
            tot = pages[j * ppb][0, k_rows, :]
            for p in range(1, ppb):
                tot = tot + pages[j * ppb + p][0, k_rows, :]
            cols.append(jnp.sum(tot, axis=1, keepdims=True))
        col = jnp.concatenate(cols, axis=0) / MOBA_BLOCK
        out = jnp.where(lane == s * (per_step // ppb) + j, col, out)
    o_ref[0] = out


def _moba_means(pages, cache_t, per_step):
    B, n_pages = pages.shape
    width = cache_t.shape[1]
    nk = MOBA_KV_HEADS * MOBA_HEAD_DIM
    assert n_pages * PAGE_SIZE // MOBA_BLOCK <= LANES
    grid_spec = pltpu.PrefetchScalarGridSpec(
        num_scalar_prefetch=1, grid=(B, n_pages // per_step),
        in_specs=[pl.BlockSpec((1, width, PAGE_SIZE), functools.partial(
            lambda b, s, pt, p: (pt[b, s * per_step + p], 0, 0), p=p)) for p in range(per_step)],
        out_specs=pl.BlockSpec((1, nk, LANES), lambda b, s, pt: (b, 0, 0)))
    return pl.pallas_call(
        functools.partial(_moba_means_kernel, per_step=per_step),
        out_shape=jax.ShapeDtypeStruct((B, nk, LANES), F32), grid_spec=grid_spec,
        compiler_params=_cp("parallel", "arbitrary"), name="moba_means",
    )(pages, *([cache_t] * per_step))


def _moba_dec_score_kernel(q_ref, mean_ref, o_ref, *, nb):
    KH = MOBA_KV_HEADS
    R = MOBA_HEADS // KH
    dh = MOBA_HEAD_DIM
    lane = _iota((R, LANES), 1)
    for kh in range(KH):
        mk = mean_ref[0, kh * dh:(kh + 1) * dh, :].astype(BF16)
        g = _dot(q_ref[0, kh * R:(kh + 1) * R, :].astype(BF16), mk)
        o_ref[0, kh * R:(kh + 1) * R, :] = jnp.where(lane < nb, g, -jnp.inf)


def _moba_dec_score(q3, means, nb):
    B, H, dh = q3.shape
    blk3 = lambda b: (b, 0, 0)
    return pl.pallas_call(
        functools.partial(_moba_dec_score_kernel, nb=nb),
        out_shape=jax.ShapeDtypeStruct((B, H, LANES), F32), grid=(B,),
        in_specs=[pl.BlockSpec((1, H, dh), blk3), pl.BlockSpec((1,) + means.shape[1:], blk3)],
        out_specs=pl.BlockSpec((1, H, LANES), blk3), compiler_params=_cp("parallel"), name="moba_dec_score",
    )(q3, means)


def _moba_dec_attn_kernel(*refs, n_blk):
    R = MOBA_HEADS // MOBA_KV_HEADS
    dh = MOBA_HEAD_DIM
    pages = refs[1:1 + R * n_blk]
    q_ref, new_ref, o_ref = refs[1 + R * n_blk:]
    k_new = new_ref[0, :, :dh]
    v_new = new_ref[0, :, dh:]
    for r in range(R):
        q = q_ref[0, 0, r:r + 1, :] * dh ** -0.5
        kt = jnp.concatenate([pages[r * n_blk + n][0, :dh, :] for n in range(n_blk)], axis=1).astype(BF16)
        vt = jnp.concatenate([pages[r * n_blk + n][0, dh:, :] for n in range(n_blk)], axis=1).astype(BF16)
        s = _dot(q.astype(BF16), kt)
        s_new = jnp.sum(q * k_new, axis=-1, keepdims=True)
        m = jnp.maximum(jnp.max(s, axis=-1, keepdims=True), s_new)
        e = jnp.exp(s - m)
        e_new = jnp.exp(s_new - m)
        o_ref[0, 0, r:r + 1, :] = ((_dot_nt(e.astype(BF16), vt) + e_new * v_new)
                                   / (jnp.sum(e, axis=-1, keepdims=True) + e_new))


def _moba_dec_attn(page_ids, cache_t, q4, new_kv):
    B, KH, R, dh = q4.shape
    n_blk = page_ids.shape[1] // (KH * R)
    page_specs = [pl.BlockSpec((1, 2 * dh, PAGE_SIZE), functools.partial(
        lambda b, h, pg, n: (pg[b, h * R * n_blk + n], h, 0), n=n)) for n in range(R * n_blk)]
    grid_spec = pltpu.PrefetchScalarGridSpec(
        num_scalar_prefetch=1, grid=(B, KH),
        in_specs=page_specs + [pl.BlockSpec((1, 1, R, dh), lambda b, h, pg: (b, h, 0, 0)),
                               pl.BlockSpec((1, 1, 2 * dh), lambda b, h, pg: (b, 0, h))],
        out_specs=pl.BlockSpec((1, 1, R, dh), lambda b, h, pg: (b, h, 0, 0)))
    return pl.pallas_call(
        functools.partial(_moba_dec_attn_kernel, n_blk=n_blk),
        out_shape=jax.ShapeDtypeStruct(q4.shape, F32), grid_spec=grid_spec,
        compiler_params=_cp("parallel", "parallel"), name="moba_dec_attn",
    )(page_ids, *([cache_t] * (R * n_blk)), q4, new_kv)


def _router_kernel(x_ref, w_ref, b_ref, o_ref):
    s = jax.nn.sigmoid(lax.dot_general(w_ref[...], x_ref[...], (((1,), (1,)), ((), ())),
                                       preferred_element_type=F32, precision=lax.Precision.HIGHEST))
    row = _iota(s.shape, 0)
    x = s + b_ref[...]
    chosen = jnp.zeros(s.shape, F32)
    for _ in range(TOP_K):
        m = jnp.max(x, axis=0, keepdims=True)
        idx = jnp.min(jnp.where(x == m, row, N_EXPERTS), axis=0, keepdims=True)
        hit = row == idx
        chosen = jnp.where(hit, 1.0, chosen)
        x = jnp.where(hit, -jnp.inf, x)
    w = chosen * s
    w = w / jnp.sum(w, axis=0, keepdims=True) * ROUTED_SCALE
    o_ref[...] = jnp.concatenate([w, jnp.zeros((LANES - N_EXPERTS, w.shape[1]), F32)], axis=0).T


def _moe_router(x, w_router_t, b_router):
    M, K = x.shape
    tm = min(M, 512)
    E = w_router_t.shape[0]
    row = lambda i: (i, 0)
    fix = lambda i: (0, 0)
    return pl.pallas_call(
        _router_kernel, out_shape=jax.ShapeDtypeStruct((M, LANES), F32), grid=(M // tm,),
        in_specs=[pl.BlockSpec((tm, K), row), pl.BlockSpec((E, K), fix), pl.BlockSpec((E, 1), fix)],
        out_specs=pl.BlockSpec((tm, LANES), row), compiler_params=_cp("parallel"), name="moe_router",
    )(x, w_router_t, b_router)


def _moe_kernel(x_ref, gate_ref, wg_ref, wu_ref, wd_ref, wsg_ref, wsu_ref, wsd_ref, g_ref, b_ref, o_ref,
                xb_sc, acc_sc):
    e = pl.program_id(1)
    nh = MOE_GROUP * D_EXPERT

    @pl.when(e == 0)
    def _():
        xb = x_ref[...].astype(BF16)
        xb_sc[...] = xb
        hs = _silu(_dot(xb, wsg_ref[0].astype(BF16))) * _dot(xb, wsu_ref[0].astype(BF16))
        acc_sc[...] = _dot(hs.astype(BF16), wsd_ref[0].astype(BF16))

    xb = xb_sc[...]
    w_gate = jnp.concatenate([wg_ref[0, k].astype(BF16) for k in range(MOE_GROUP)], axis=1)
    w_up = jnp.concatenate([wu_ref[0, k].astype(BF16) for k in range(MOE_GROUP)], axis=1)
    src = _iota((2 * LANES, nh), 0) % LANES
    pick = jnp.where(src == MOE_GROUP * e + _iota((2 * LANES, nh), 1) // D_EXPERT, 1.0, 0.0).astype(BF16)
    gate = gate_ref[...]
    gate_hi = gate.astype(BF16)
    gate_lo = (gate - gate_hi.astype(F32)).astype(BF16)
    gw = _dot(jnp.concatenate([gate_hi, gate_lo], axis=1), pick)
    h = _silu(_dot(xb, w_gate)) * _dot(xb, w_up) * gw
    w_down = jnp.concatenate([wd_ref[0, k].astype(BF16) for k in range(MOE_GROUP)], axis=0)
    acc_sc[...] += _dot(h.astype(BF16), w_down)

    @pl.when(e == pl.num_programs(1) - 1)
    def _():
        o_ref[...] = _layer_norm(ALPHA * x_ref[...] + acc_sc[...], g_ref[...], b_ref[...])


def _moe(x, gate, w, g, b):
    M, D = x.shape
    tm = min(M, 1024)
    layer = w['layer']
    E, _, F = w['w_gate'].shape[1:]
    n = MOE_GROUP
    row = lambda i, e: (i, 0)
    fix = lambda i, e: (0, 0)
    grp = lambda i, e: (layer, e, 0, 0)
    lay = lambda i, e: (layer, 0, 0)
    return pl.pallas_call(
        _moe_kernel, out_shape=jax.ShapeDtypeStruct((M, D), F32), grid=(M // tm, E // n),
        in_specs=[pl.BlockSpec((tm, D), row), pl.BlockSpec((tm, LANES), row),
                  pl.BlockSpec((1, n, D, F), grp), pl.BlockSpec((1, n, D, F), grp), pl.BlockSpec((1, n, F, D), grp),
                  pl.BlockSpec((1, D, D_SHARED), lay), pl.BlockSpec((1, D, D_SHARED), lay),
                  pl.BlockSpec((1, D_SHARED, D), lay),
                  pl.BlockSpec((1, D), fix), pl.BlockSpec((1, D), fix)],
        out_specs=pl.BlockSpec((tm, D), row),
        scratch_shapes=[pltpu.VMEM((tm, D), BF16), pltpu.VMEM((tm, D), F32)],
        compiler_params=_cp("parallel", "arbitrary"), name="moe",
    )(x, gate, w['w_gate'], w['w_up'], w['w_down'], w['ws_gate'], w['ws_up'], w['ws_down'],
      g.reshape(1, D), b.reshape(1, D))


def _pad_cols(w, n):
    return jnp.pad(w, ((0, 0), (0, n - w.shape[1])))


def _block_diag(blocks):
    n, a, b = blocks.shape
    eye = jnp.eye(n, dtype=blocks.dtype)
    return (eye[:, None, :, None] * blocks[:, :, None, :]).reshape(n * a, n * b)


def _nsa_weights(w_in, cmp_pe, cmp_w1, cmp_b1, cmp_w2, cmp_b2, w_o):
    H, G, dh = NSA_HEADS, NSA_KV_HEADS, NSA_HEAD_DIM
    R = H // G
    nq, nkv = H * dh, 6 * G * dh
    wq, wkv = w_in[:, :nq], w_in[:, nq:nq + nkv]
    wg = w_in[:, nq + nkv:].reshape(-1, 3, G, R).transpose(0, 2, 1, 3).reshape(-1, G, 3 * R)
    wg = jnp.pad(wg, ((0, 0), (0, 0), (0, LANES - 3 * R))).reshape(-1, G * LANES)
    w_ext = jnp.concatenate([wq, wq, wkv, wg], axis=1).astype(BF16)
    w1 = cmp_w1.reshape(2, CMP_STRIDE, 2, dh, CMP_HIDDEN)
    eye_k = jnp.eye(2, dtype=F32)
    w_lohi = (w1.transpose(1, 2, 3, 0, 4)[:, :, :, :, None, :] * eye_k[None, :, None, None, :, None])
    w_lohi = w_lohi.reshape(CMP_STRIDE // 2, 2 * 2 * dh, 2 * 2 * CMP_HIDDEN).astype(BF16)
    pe_rows = jnp.broadcast_to(cmp_pe[:, None], (CMP_BLOCK, G, 2, dh)).reshape(1, CMP_BLOCK, G * 2 * dh)
    b1 = jnp.broadcast_to(cmp_b1[None], (G, 2, CMP_HIDDEN)).reshape(1, -1)
    b2 = jnp.broadcast_to(cmp_b2[None], (G, 2, dh)).reshape(1, -1)
    w2 = _block_diag(jnp.tile(cmp_w2, (G, 1, 1))).astype(BF16)
    return dict(w_ext=w_ext, w_lohi=w_lohi, pe_rows=pe_rows, b1=b1, b2=b2, w2=w2, w_o=w_o.astype(BF16))


def _nsa_tables(pos):
    H, G = NSA_HEADS, NSA_KV_HEADS
    dh = NSA_HEAD_DIM
    kv_rot = [(1.0, dh), (0.0, dh)] * G
    layout = [(0.0, H * dh), (1.0, H * dh), (0.0, 2 * G * dh)] + kv_rot + kv_rot + [(0.0, G * LANES)]
    return _unit_tables(pos, dh // 2) + (_column_mask(layout),)


def _mla_weights(w_dn, g_q, w_uq, g_kv, w_uk, w_uv, w_o):
    H = MLA_HEADS
    pad = MLA_SLOT - QK_NOPE - QK_ROPE
    w_dn_p = _pad_cols(w_dn, Q_LORA + KV_LORA + LANES).astype(BF16)
    wq = jnp.pad(w_uq.reshape(Q_LORA, H, QK_NOPE + QK_ROPE), ((0, 0), (0, 0), (0, pad)))
    w_q = wq.reshape(Q_LORA, H * MLA_SLOT).astype(BF16)
    wk_c = jnp.pad(w_uk, ((0, 0), (0, 0), (0, MLA_SLOT - QK_NOPE))).reshape(KV_LORA, H * MLA_SLOT)
    eye = jnp.pad(jnp.eye(QK_ROPE, dtype=F32), ((0, 0), (QK_NOPE, pad)))
    wk_r = jnp.tile(eye, (1, H))
    wk = jnp.concatenate([wk_c, wk_r, jnp.zeros((LANES - QK_ROPE, H * MLA_SLOT), F32)], axis=0)
    wv = jnp.pad(w_uv.reshape(KV_LORA, H * V_DIM), ((0, LANES), (0, 0)))
    w_kv = jnp.concatenate([wk, wv], axis=1).astype(BF16)
    absorb = jnp.pad(w_uk.transpose(1, 2, 0), ((0, 0), (0, MLA_SLOT - QK_NOPE), (0, 0)))
    w_absorb = _block_diag(absorb).astype(BF16)
    w_unabsorb = _block_diag(w_uv.transpose(1, 0, 2)).astype(BF16)
    return dict(w_dn=w_dn_p, g_q=g_q, g_kv=g_kv, w_q=w_q, w_kv=w_kv, w_absorb=w_absorb,
                w_unabsorb=w_unabsorb, w_o=w_o.astype(BF16))


def _mla_q_tables(pos):
    half = QK_ROPE // 2
    ua, ub = _rope_unit(pos, half)
    T = pos.shape[0]
    pad = MLA_SLOT - QK_NOPE - QK_ROPE
    a = jnp.concatenate([jnp.ones((T, QK_NOPE), F32), ua, jnp.ones((T, pad), F32)], axis=1)
    b = jnp.concatenate([jnp.zeros((T, QK_NOPE), F32), ub, jnp.zeros((T, pad), F32)], axis=1)
    return a, b, jnp.ones((1, MLA_HEADS * MLA_SLOT), F32)


def _mla_kr_tables(pos):
    return _tables(pos, QK_ROPE // 2, [('r', 1), ('n', LANES - QK_ROPE)])


def _moba_weights(w_in, w_o):
    return dict(w_in=w_in.astype(BF16), w_o=w_o.astype(BF16))


def _moba_tables(pos):
    dh = MOBA_HEAD_DIM
    layout = [(1.0, MOBA_HEADS * dh)] + [(1.0, dh), (0.0, dh)] * MOBA_KV_HEADS
    return _unit_tables(pos, dh // 2) + (_column_mask(layout),)


def _moe_weights(w_router, b_router, w_gate, w_up, w_down, ws_gate, ws_up, ws_down, layer=None):
    stacked = (w_router, b_router, w_gate, w_up, w_down, ws_gate, ws_up, ws_down)
    if layer is None:
        stacked, layer = tuple(a[None] for a in stacked), 0
    w_router, b_router, w_gate, w_up, w_down, ws_gate, ws_up, ws_down = stacked
    E = w_gate.shape[1]
    return dict(layer=layer, w_router=w_router[layer].T, b_router=b_router[layer].reshape(E, 1),
                w_gate=w_gate, w_up=w_up, w_down=w_down, ws_gate=ws_gate, ws_up=ws_up, ws_down=ws_down)


def _feature_major(cache):
    n, pool, rows = cache.shape[:3]
    nd = cache.ndim
    return cache.transpose(0, 1, *range(3, nd), 2).reshape(n * pool, -1, rows)


def _nsa_cmp_tokens(rows_lohi, w):
    pe_lohi = _cmp_lohi(w['pe_rows'], w['w_lohi'])
    return _cmp_combine(rows_lohi, pe_lohi, w['b1'], w['w2'], w['b2'])


def _nsa_prompt(h, B, T, w, tabs):
    G, dh = NSA_KV_HEADS, NSA_HEAD_DIM
    proj = _proj(h, w['w_ext'], tabs, dh // 2)
    kv0 = 2 * NSA_HEADS * dh
    width = 2 * G * dh
    lohi = _cmp_lohi(proj.reshape(B, T, -1), w['w_lohi'], col_block=kv0 // LANES)
    cmp_tok = _nsa_cmp_tokens(lohi, w)
    y = _attn_nsa(proj, cmp_tok, B, T)
    kv = proj[:, kv0:kv0 + 3 * width].reshape(B, T, 3, G, 2, dh)
    return y, (kv[:, :, 0], kv[:, :, 1], kv[:, -min(WINDOW, T):, 2])


def _nsa_sample(h, past_len, w, tabs, cache_cmp, cache_sel, win_state, slot, page_table):
    B = h.shape[0]
    G, H, dh = NSA_KV_HEADS, NSA_HEADS, NSA_HEAD_DIM
    R = H // G
    n_pool = cache_cmp.shape[1]
    width = 2 * G * dh
    proj = _proj(h, w['w_ext'], tabs, dh // 2)
    pages = page_table + slot * n_pool
    lohi = _cmp_lohi_paged(_feature_major(cache_cmp), pages, w['w_lohi'], per_step=min(32, page_table.shape[1]))
    cmp_tok = _nsa_cmp_tokens(lohi, w)
    n_sel = -(-(past_len + 1) // SEL_BLOCK)
    n_top = min(SEL_TOP, n_sel)
    q3 = proj[:, :H * dh].reshape(B, H, dh)
    qr3 = proj[:, H * dh:2 * H * dh].reshape(B, H, dh)
    o_cmp, imp = _nsa_dec_cmp(q3, cmp_tok, past_len, n_sel)
    idx = _topk_idx(imp.reshape(B * G, -1), n_top)[:, :n_top].reshape(B, G * n_top)
    n_past_blocks = past_len // SEL_BLOCK
    per_page = PAGE_SIZE // SEL_BLOCK
    sel_pages = jnp.take_along_axis(pages, jnp.minimum(idx, n_past_blocks - 1) // per_page, axis=1)
    kv0 = 2 * H * dh
    new_kv = proj[:, kv0:kv0 + 3 * width].reshape(B, 1, 3 * width)
    gl3 = proj[:, kv0 + 3 * width:].reshape(B, G, LANES)[:, :, :3 * R].reshape(B, G, 3, R)
    gl3 = gl3.transpose(0, 1, 3, 2).reshape(B, H, 3)
    Wn = win_state.shape[2]
    o = _nsa_dec_attn(idx, sel_pages, _feature_major(cache_sel), qr3, new_kv, _feature_major(win_state), slot * B,
                      o_cmp, gl3, n_past_blocks)
    kv = new_kv.reshape(B, 1, 3, G, 2, dh)
    new_win = jnp.concatenate([win_state[slot], kv[:, :, 2]], axis=1)[:, -Wn:]
    return o.reshape(B, H * dh), (kv[:, :, 0], kv[:, :, 1], new_win)


def _mla_front(h, w, q_tabs, kr_tabs):
    dn = _proj(h, w['w_dn'])
    c_q, ckr = _mla_post(dn, w['g_q'], w['g_kv'], kr_tabs)
    q_ext = _proj(c_q, w['w_q'], q_tabs, QK_ROPE // 2)
    return q_ext, ckr


def _mla_prompt(h, B, T, w, q_tabs, kr_tabs):
    q_ext, ckr = _mla_front(h, w, q_tabs, kr_tabs)
    kv_ext = _proj(ckr, w['w_kv'])
    y = _attn_mla(q_ext, kv_ext, B, T)
    return y, (ckr[:, :KV_LORA].reshape(B, T, KV_LORA), ckr[:, KV_LORA:KV_LORA + QK_ROPE].reshape(B, T, QK_ROPE))


def _mla_sample(h, w, q_tabs, kr_tabs, cache_ckv, cache_kr, slot, page_table):
    B = h.shape[0]
    H = MLA_HEADS
    n_pool = cache_ckv.shape[1]
    q_ext, ckr = _mla_front(h, w, q_tabs, kr_tabs)
    q_lat = _proj(q_ext, w['w_absorb']).reshape(B, H, KV_LORA)
    q_rope = q_ext.reshape(B, H, MLA_SLOT)[:, :, QK_NOPE:QK_NOPE + QK_ROPE]
    c_new = ckr[:, :KV_LORA].reshape(B, 1, KV_LORA)
    r_new = ckr[:, KV_LORA:KV_LORA + QK_ROPE].reshape(B, 1, QK_ROPE)
    pages = page_table + slot * n_pool
    o_lat = _mla_decode(pages, cache_ckv.reshape(-1, PAGE_SIZE, KV_LORA), _feature_major(cache_kr),
                        q_lat, q_rope, c_new, r_new, per_step=min(32, page_table.shape[1]))
    y = _proj(o_lat.reshape(B, H * KV_LORA), w['w_unabsorb'])
    return y, (c_new, r_new)


def _moba_prompt(h, B, T, w, tabs):
    KH, dh = MOBA_KV_HEADS, MOBA_HEAD_DIM
    proj = _proj(h, w['w_in'], tabs, dh // 2)
    y = _attn_moba(proj, B, T)
    return y, proj[:, MOBA_HEADS * dh:].reshape(B, T, KH, 2, dh)


def _moba_sample(h, past_len, w, tabs, cache, slot, page_table):
    B = h.shape[0]
    H, KH, dh = MOBA_HEADS, MOBA_KV_HEADS, MOBA_HEAD_DIM
    R = H // KH
    assert past_len % MOBA_BLOCK == 0 and past_len // MOBA_BLOCK >= MOBA_TOP
    n_pool = cache.shape[1]
    width = KH * 2 * dh
    ppb = MOBA_BLOCK // PAGE_SIZE
    proj = _proj(h, w['w_in'], tabs, dh // 2)
    pages = page_table + slot * n_pool
    cache_t = _feature_major(cache)
    means = _moba_means(pages, cache_t, per_step=min(16, page_table.shape[1]))
    q3 = proj[:, :H * dh].reshape(B, H, dh)
    scores = _moba_dec_score(q3, means, past_len // MOBA_BLOCK)
    idx = _topk_idx(scores.reshape(B * H, -1), MOBA_TOP)[:, :MOBA_TOP].reshape(B, H * MOBA_TOP)
    page_ids = jnp.take_along_axis(pages, (idx[:, :, None] * ppb + jnp.arange(ppb)).reshape(B, -1), axis=1)
    new_kv = proj[:, H * dh:].reshape(B, 1, width)
    o = _moba_dec_attn(page_ids, cache_t, q3.reshape(B, KH, R, dh), new_kv)
    return o.reshape(B, H * dh), new_kv.reshape(B, 1, KH, 2, dh)


def _moe_layer(h, w, g, b):
    gate = _moe_router(h, w['w_router'], w['b_router'])
    return _moe(h, gate, w, g, b)


def kernel(x_prompt, x_sample, cache_nsa_cmp, cache_nsa_sel, state_nsa_win, cache_mla_ckv, cache_mla_krope,
           cache_moba_kv, page_table, nsa_w_in, nsa_cmp_pe, nsa_cmp_w1, nsa_cmp_b1, nsa_cmp_w2, nsa_cmp_b2,
           nsa_w_o, mla_w_dn, mla_g_q, mla_w_uq, mla_g_kv, mla_w_uk, mla_w_uv, mla_w_o, moba_w_in, moba_w_o,
           ln1_g, ln1_b, ln2_g, ln2_b, moe_w_router, moe_b_router, moe_w_gate, moe_w_up, moe_w_down,
           moe_ws_gate, moe_ws_up, moe_ws_down):
    B, T, D = x_prompt.shape
    Bs, Ts, _ = x_sample.shape
    assert Ts == 1
    past_len = page_table.shape[1] * PAGE_SIZE
    assert state_nsa_win.shape[2] == WINDOW and past_len >= WINDOW
    pos_p = jnp.arange(T, dtype=I32)
    pos_s = jnp.full((Bs,), past_len, dtype=I32)
    hp = x_prompt.reshape(B * T, D)
    hs = x_sample.reshape(Bs, D)
    outs = {k: [] for k in ('cmp_p', 'cmp_s', 'sel_p', 'sel_s', 'win_p', 'win_s',
                            'ckv_p', 'ckv_s', 'kr_p', 'kr_s', 'mb_p', 'mb_s')}
    for i in range(DEPTH):
        kind, slot = i % N_MIXERS, i // N_MIXERS
        if kind == MIX_NSA:
            w = _nsa_weights(nsa_w_in[slot], nsa_cmp_pe[slot], nsa_cmp_w1[slot], nsa_cmp_b1[slot],
                             nsa_cmp_w2[slot], nsa_cmp_b2[slot], nsa_w_o[slot])
            yp, (a_p, b_p, c_p) = _nsa_prompt(hp, B, T, w, _nsa_tables(pos_p))
            ys, (a_s, b_s, c_s) = _nsa_sample(hs, past_len, w, _nsa_tables(pos_s), cache_nsa_cmp, cache_nsa_sel,
                                              state_nsa_win, slot, page_table)
            outs['cmp_p'].append(a_p); outs['cmp_s'].append(a_s)
            outs['sel_p'].append(b_p); outs['sel_s'].append(b_s)
            outs['win_p'].append(c_p); outs['win_s'].append(c_s)
        elif kind == MIX_MLA:
            w = _mla_weights(mla_w_dn[slot], mla_g_q[slot], mla_w_uq[slot], mla_g_kv[slot], mla_w_uk[slot],
                             mla_w_uv[slot], mla_w_o[slot])
            yp, (a_p, b_p) = _mla_prompt(hp, B, T, w, _mla_q_tables(pos_p), _mla_kr_tables(pos_p))
            ys, (a_s, b_s) = _mla_sample(hs, w, _mla_q_tables(pos_s), _mla_kr_tables(pos_s), cache_mla_ckv,
                                         cache_mla_krope, slot, page_table)
            outs['ckv_p'].append(a_p); outs['ckv_s'].append(a_s)
            outs['kr_p'].append(b_p); outs['kr_s'].append(b_s)
        else:
            w = _moba_weights(moba_w_in[slot], moba_w_o[slot])
            yp, a_p = _moba_prompt(hp, B, T, w, _moba_tables(pos_p))
            ys, a_s = _moba_sample(hs, past_len, w, _moba_tables(pos_s), cache_moba_kv, slot, page_table)
            outs['mb_p'].append(a_p); outs['mb_s'].append(a_s)
        hp = _proj_ln(yp, w['w_o'], hp, ln1_g[i], ln1_b[i])
        hs = _proj_ln(ys, w['w_o'], hs, ln1_g[i], ln1_b[i])
        mw = _moe_weights(moe_w_router, moe_b_router, moe_w_gate, moe_w_up, moe_w_down,
                          moe_ws_gate, moe_ws_up, moe_ws_down, layer=i)
        hp = _moe_layer(hp, mw, ln2_g[i], ln2_b[i])
        hs = _moe_layer(hs, mw, ln2_g[i], ln2_b[i])
    st = lambda k: jnp.stack(outs[k])
    return (hp.reshape(B, T, D), hs.reshape(Bs, Ts, D),
            st('cmp_p'), st('cmp_s'), st('sel_p'), st('sel_s'), st('win_p'), st('win_s'),
            st('ckv_p'), st('ckv_s'), st('kr_p'), st('kr_s'), st('mb_p'), st('mb_s'))
```

```python
import functools

import jax
import jax.numpy as jnp
from jax import lax
from jax.experimental import pallas as pl
from jax.experimental.pallas import tpu as pltpu

D_MODEL = 1024
DEPTH = 4
PAGE_SIZE = 128
N_MIXERS = 3
MIX_NSA, MIX_MLA, MIX_MOBA = 0, 1, 2

ALPHA = (2 * DEPTH) ** 0.25
LN_EPS = 1e-5
RMS_EPS = 1e-6
ROPE_THETA = 10000.0
NEG = -1e30
FORCE = 1e9

NSA_HEADS = 16
NSA_KV_HEADS = 2
NSA_HEAD_DIM = 64
CMP_BLOCK = 32
CMP_STRIDE = 16
CMP_HIDDEN = 128
SEL_BLOCK = 64
SEL_RATIO = SEL_BLOCK // CMP_STRIDE
SEL_TOP = 16
WINDOW = 512

MLA_HEADS = 16
Q_LORA = 384
KV_LORA = 256
QK_NOPE = 64
QK_ROPE = 32
V_DIM = 64
MLA_SLOT = 128

MOBA_HEADS = 16
MOBA_KV_HEADS = 4
MOBA_HEAD_DIM = 64
MOBA_BLOCK = 256
MOBA_TOP = 3

N_EXPERTS = 64
TOP_K = 8
D_EXPERT = 128
D_SHARED = 128
ROUTED_SCALE = 2.5

LANES = 128
FLASH_CHAINS = 1
MOE_GROUP = 4
ONES_ROWS = 16
VMEM_LIMIT = 48 * 1024 * 1024

F32 = jnp.float32
BF16 = jnp.bfloat16
I32 = jnp.int32


def _cp(*sem):
    return pltpu.CompilerParams(dimension_semantics=sem, vmem_limit_bytes=VMEM_LIMIT)


def _dot(a, b):
    return jnp.dot(a, b, preferred_element_type=F32)


def _dot_nt(a, b):
    return lax.dot_general(a, b, (((1,), (1,)), ((), ())), preferred_element_type=F32)


def _dot_exact(a, b):
    return jnp.dot(a, b, preferred_element_type=F32, precision=lax.Precision.HIGHEST)


def _iota(shape, axis):
    return lax.broadcasted_iota(I32, shape, axis)


def _layer_norm(z, g, b):
    mu = jnp.mean(z, axis=-1, keepdims=True)
    d = z - mu
    var = jnp.mean(d * d, axis=-1, keepdims=True)
    return d * lax.rsqrt(var + LN_EPS) * g + b


def _silu(x):
    return x * jax.nn.sigmoid(x)


def _softmax_rows(s, valid):
    s = jnp.where(valid, s, NEG)
    m = jnp.max(s, axis=-1, keepdims=True)
    e = jnp.where(valid, jnp.exp(s - m), 0.0)
    l = jnp.sum(e, axis=-1, keepdims=True)
    return e / jnp.where(l > 0.0, l, 1.0)


def _rank_before(x, n_cols, lane):
    rank = jnp.zeros(x.shape, F32)
    for c in range(n_cols):
        col = x[:, c:c + 1]
        rank = rank + jnp.where(col > x, 1.0, jnp.where(col == x, jnp.where(lane > c, 1.0, 0.0), 0.0))
    return rank


def _proj_kernel(x_ref, w_ref, o_ref):
    o_ref[...] = _dot(x_ref[...].astype(BF16), w_ref[...])


def _proj_rope_kernel(x_ref, w_ref, a_ref, b_ref, mask_ref, o_ref, *, half, tn):
    acc = _dot(x_ref[...].astype(BF16), w_ref[...])
    lane = _iota(acc.shape, 1)
    first = (lane % (2 * half)) < half
    partner = jnp.where(first, pltpu.roll(acc, tn - half, 1), pltpu.roll(acc, half, 1))
    rotary = mask_ref[...] > 0.5
    a = jnp.where(rotary, jnp.concatenate([a_ref[...]] * (tn // LANES), axis=1), 1.0)
    b = jnp.where(rotary, jnp.concatenate([b_ref[...]] * (tn // LANES), axis=1), 0.0)
    o_ref[...] = acc * a + partner * b


def _col_tile(n):
    return next(t for t in (512, 384, 256, 128) if n % t == 0)


def _proj(x, w, tabs=None, half=0):
    M, K = x.shape
    N = w.shape[1]
    tm = min(M, 512)
    tn = _col_tile(N)
    grid = (M // tm, N // tn)
    x_spec = pl.BlockSpec((tm, K), lambda i, j: (i, 0))
    w_spec = pl.BlockSpec((K, tn), lambda i, j: (0, j))
    o_spec = pl.BlockSpec((tm, tn), lambda i, j: (i, j))
    out_shape = jax.ShapeDtypeStruct((M, N), F32)
    if tabs is None:
        return pl.pallas_call(_proj_kernel, out_shape=out_shape, grid=grid, in_specs=[x_spec, w_spec],
                              out_specs=o_spec, compiler_params=_cp("parallel", "parallel"), name="proj")(x, w)
    a, bm, mask = tabs
    nrb = a.shape[0] // tm
    t_spec = pl.BlockSpec((tm, LANES), lambda i, j: (i % nrb, 0))
    m_spec = pl.BlockSpec((1, tn), lambda i, j: (0, j))
    return pl.pallas_call(functools.partial(_proj_rope_kernel, half=half, tn=tn), out_shape=out_shape, grid=grid,
                          in_specs=[x_spec, w_spec, t_spec, t_spec, m_spec], out_specs=o_spec,
                          compiler_params=_cp("parallel", "parallel"), name="proj_rope")(x, w, a, bm, mask)


def _proj_ln_kernel(x_ref, w_ref, r_ref, g_ref, b_ref, o_ref):
    y = _dot(x_ref[...].astype(BF16), w_ref[...])
    o_ref[...] = _layer_norm(ALPHA * r_ref[...] + y, g_ref[...], b_ref[...])


def _proj_ln(x, w, res, g, b):
    M, K = x.shape
    N = w.shape[1]
    tm = min(M, 512)
    row = lambda i: (i, 0)
    fix = lambda i: (0, 0)
    return pl.pallas_call(
        _proj_ln_kernel, out_shape=jax.ShapeDtypeStruct((M, N), F32), grid=(M // tm,),
        in_specs=[pl.BlockSpec((tm, K), row), pl.BlockSpec((K, N), fix), pl.BlockSpec((tm, N), row),
                  pl.BlockSpec((1, N), fix), pl.BlockSpec((1, N), fix)],
        out_specs=pl.BlockSpec((tm, N), row), compiler_params=_cp("parallel"), name="proj_ln",
    )(x, w, res, g.reshape(1, N), b.reshape(1, N))


def _rope_unit(pos, half):
    inv = jnp.power(ROPE_THETA, -jnp.arange(half, dtype=F32) / half)
    ang = pos.astype(F32)[:, None] * inv
    cos, sin = jnp.cos(ang), jnp.sin(ang)
    return jnp.concatenate([cos, cos], axis=1), jnp.concatenate([-sin, sin], axis=1)


def _unit_tables(pos, half):
    ua, ub = _rope_unit(pos, half)
    reps = LANES // (2 * half)
    return jnp.tile(ua, (1, reps)), jnp.tile(ub, (1, reps))


def _column_mask(layout):
    return jnp.concatenate([jnp.full((1, n), v, F32) for v, n in layout], axis=1)


def _tables(pos, half, layout):
    ua, ub = _rope_unit(pos, half)
    T = pos.shape[0]
    a_parts, b_parts = [], []
    for kind, n in layout:
        if kind == 'r':
            a_parts.append(jnp.tile(ua, (1, n)))
            b_parts.append(jnp.tile(ub, (1, n)))
        else:
            a_parts.append(jnp.ones((T, n), F32))
            b_parts.append(jnp.zeros((T, n), F32))
    return jnp.concatenate(a_parts, axis=1), jnp.concatenate(b_parts, axis=1)


def _add_bias(s, bias, heads):
    if heads == 1:
        return s + bias
    tq = s.shape[1] // heads
    return jnp.concatenate([s[:, h * tq:(h + 1) * tq] + bias for h in range(heads)], axis=1)


def _weighted_values(vt_ref, first, e):
    acc = _dot(vt_ref[first], e[:LANES])
    for j in range(1, e.shape[0] // LANES):
        acc = acc + _dot(vt_ref[first + j], e[j * LANES:(j + 1) * LANES])
    return acc


def _flash(qs, k_ref, k_cols, vt_ref, vt_base, m_sc, acc_sc, lo, hi, tk, bias_fn, heads):
    m_sc[...] = jnp.full(m_sc.shape, NEG, F32)
    acc_sc[...] = jnp.zeros(acc_sc.shape, F32)
    sub = tk // LANES

    def body(kt, carry):
        k0 = pl.multiple_of(kt * tk, tk)
        s = _add_bias(_dot_nt(k_ref[pl.ds(k0, tk), k_cols], qs), bias_fn(kt, k0), heads)
        m_old = m_sc[...]
        m_new = jnp.maximum(m_old, jnp.max(s, axis=0, keepdims=True))
        e = jnp.exp(s - m_new).astype(BF16)
        acc_sc[...] = jnp.exp(m_old - m_new) * acc_sc[...] + _weighted_values(vt_ref, vt_base + kt * sub, e)
        m_sc[...] = m_new
        return carry

    lax.fori_loop(lo, hi, body, 0)
    return acc_sc[...]


def _transpose_rows(x):
    eye = jnp.where(_iota((LANES, LANES), 0) == _iota((LANES, LANES), 1), 1.0, 0.0).astype(BF16)
    return _dot_nt(eye, x)


def _store_values_t(vt_ref, base, vt):
    n = vt.shape[1]
    full = jnp.concatenate([jnp.ones((ONES_ROWS, n), F32), vt], axis=0).astype(BF16)
    for j in range(n // LANES):
        vt_ref[base + j] = full[:, j * LANES:(j + 1) * LANES]


def _write_heads_t(o_ref, heads):
    tq = heads[0].shape[1]
    for p in range(len(heads) // 2):
        pair = jnp.concatenate([heads[2 * p], heads[2 * p + 1]], axis=0)
        for c in range(tq // LANES):
            o_ref[c * LANES:(c + 1) * LANES, p * LANES:(p + 1) * LANES] = pair[:, c * LANES:(c + 1) * LANES].T


def _stack_heads(ref, n_heads, dh, scale):
    parts = []
    for r in range(n_heads):
        pair = ref[:, (r // 2) * 2 * dh:(r // 2 + 1) * 2 * dh] * scale
        lane = _iota(pair.shape, 1)
        own = (lane < dh) if r % 2 == 0 else (lane >= dh)
        parts.append(jnp.where(own, pair, 0.0))
    return jnp.concatenate(parts, axis=0).astype(BF16)


def _dup_keys(kv, dh):
    return jnp.where(_iota(kv.shape, 1) < dh, kv, pltpu.roll(kv, dh, 1))


def _cmp_lohi_kernel(*refs, n_pages, rows_per_page, feature_major):
    G = NSA_KV_HEADS
    if feature_major:
        refs, x_sc = refs[1:-1], refs[-1]
    n_in = n_pages if feature_major else G
    row_refs, w_ref, o_ref = refs[:n_in], refs[n_in], refs[n_in + 1]
    cpp = rows_per_page // CMP_STRIDE
    nh = w_ref.shape[2] // 2

    def chunk_rows(g, r):
        if feature_major:
            return x_sc[pl.ds(r, n_pages * cpp, stride=CMP_STRIDE), :]
        return row_refs[g][0, pl.ds(r, cpp, stride=CMP_STRIDE), :]

    for g in range(G):
        if feature_major:
            for p in range(n_pages):
                x_sc[p * rows_per_page:(p + 1) * rows_per_page, :] = _transpose_rows(
                    row_refs[p][0, g * LANES:(g + 1) * LANES, :].astype(BF16))
        acc = jnp.zeros((n_pages * cpp, 2 * nh), F32)
        for r2 in range(CMP_STRIDE // 2):
            xr = jnp.concatenate([chunk_rows(g, 2 * r2), chunk_rows(g, 2 * r2 + 1)], axis=1)
            acc = acc + _dot(xr.astype(BF16), w_ref[r2])
        o_ref[0, :, g * nh:(g + 1) * nh] = acc[:, :nh]
        o_ref[0, :, (G + g) * nh:(G + g + 1) * nh] = acc[:, nh:]


def _cmp_lohi(rows, w_lohi, col_block=0):
    B, L = rows.shape[0], rows.shape[1]
    G = NSA_KV_HEADS
    nout = G * w_lohi.shape[2]
    row_specs = [pl.BlockSpec((1, L, LANES), functools.partial(lambda b, g: (b, 0, col_block + g), g=g))
                 for g in range(G)]
    return pl.pallas_call(
        functools.partial(_cmp_lohi_kernel, n_pages=1, rows_per_page=L, feature_major=False),
        out_shape=jax.ShapeDtypeStruct((B, L // CMP_STRIDE, nout), F32), grid=(B,),
        in_specs=row_specs + [pl.BlockSpec(w_lohi.shape, lambda b: (0, 0, 0))],
        out_specs=pl.BlockSpec((1, L // CMP_STRIDE, nout), lambda b: (b, 0, 0)),
        compiler_params=_cp("parallel"), name="cmp_lohi",
    )(*([rows] * G), w_lohi)


def _cmp_lohi_paged(cache_t, pages, w_lohi, per_step):
    B, n_pages = pages.shape
    G = NSA_KV_HEADS
    nout = G * w_lohi.shape[2]
    cpp = PAGE_SIZE // CMP_STRIDE
    page_specs = [pl.BlockSpec((1, G * LANES, PAGE_SIZE), functools.partial(
        lambda b, s, pt, p: (pt[b, s * per_step + p], 0, 0), p=p)) for p in range(per_step)]
    grid_spec = pltpu.PrefetchScalarGridSpec(
        num_scalar_prefetch=1, grid=(B, n_pages // per_step),
        in_specs=page_specs + [pl.BlockSpec(w_lohi.shape, lambda b, s, pt: (0, 0, 0))],
        out_specs=pl.BlockSpec((1, per_step * cpp, nout), lambda b, s, pt: (b, s, 0)),
        scratch_shapes=[pltpu.VMEM((per_step * PAGE_SIZE, LANES), F32)])
    return pl.pallas_call(
        functools.partial(_cmp_lohi_kernel, n_pages=per_step, rows_per_page=PAGE_SIZE, feature_major=True),
        out_shape=jax.ShapeDtypeStruct((B, n_pages * cpp, nout), F32), grid_spec=grid_spec,
        compiler_params=_cp("parallel", "parallel"), name="cmp_lohi_paged",
    )(pages, *([cache_t] * per_step), w_lohi)


def _cmp_combine_kernel(lohi_ref, pe_ref, b1_ref, w2_ref, b2_ref, o_ref):
    nch = lohi_ref.shape[1]
    nh = lohi_ref.shape[2] // 2
    lo = lohi_ref[0, :, :nh]
    hi_next = pltpu.roll(lohi_ref[0, :, nh:], nch - 1, 0)
    pe = pe_ref[0, 0:1, :nh] + pe_ref[0, 1:2, nh:]
    hid = jax.nn.gelu(lo + hi_next + pe + b1_ref[...])
    o_ref[0] = _dot(hid.astype(BF16), w2_ref[...]) + b2_ref[...]


def _cmp_combine(lohi, pe_lohi, b1, w2, b2):
    B, nch, n2 = lohi.shape
    nh = n2 // 2
    nout = w2.shape[1]
    fix2 = lambda b: (0, 0)
    return pl.pallas_call(
        _cmp_combine_kernel, out_shape=jax.ShapeDtypeStruct((B, nch, nout), F32), grid=(B,),
        in_specs=[pl.BlockSpec((1, nch, n2), lambda b: (b, 0, 0)), pl.BlockSpec((1, 2, n2), lambda b: (0, 0, 0)),
                  pl.BlockSpec((1, nh), fix2), pl.BlockSpec((nh, nout), fix2), pl.BlockSpec((1, nout), fix2)],
        out_specs=pl.BlockSpec((1, nch, nout), lambda b: (b, 0, 0)),
        compiler_params=_cp("parallel"), name="cmp_combine",
    )(lohi, pe_lohi, b1, w2, b2)


def _sel_importance(p_sum, n_lanes):
    nc = p_sum.shape[1]
    c = _iota((nc, n_lanes), 0)
    j = _iota((nc, n_lanes), 1)
    a = jnp.where(c >= SEL_RATIO * j - 1, jnp.where(c <= SEL_RATIO * j + SEL_RATIO - 1, 1.0, 0.0), 0.0)
    return _dot_exact(p_sum, a)


def _attn_nsa_kernel(q_ref, qr_ref, cmp_ref, sel_ref, win_ref, gl_ref, o_ref,
                     k2s, vts, k2w, vtw, bias_ref, m_sc, acc_sc, *, T, tq, tk, n_sel, n_top):
    i = pl.program_id(2)
    t0 = i * tq
    R = NSA_HEADS // NSA_KV_HEADS
    dh = NSA_HEAD_DIM
    scale = dh ** -0.5
    wk = min(T, WINDOW + tq)

    @pl.when(i == 0)
    def _():
        for src, k_sc, vt_sc in ((sel_ref, k2s, vts), (win_ref, k2w, vtw)):
            kv = src[...]
            k_sc[...] = _dup_keys(kv, dh).astype(BF16)
            _store_values_t(vt_sc, 0, _transpose_rows(kv.astype(BF16))[dh:])

    tpos = t0 + _iota((1, tq), 1)
    cmp_tok = cmp_ref[0]
    nc = cmp_tok.shape[0]
    s = _dot_nt(_dup_keys(cmp_tok, dh).astype(BF16), _stack_heads(q_ref, R, dh, scale))
    cvalid = (_iota((nc, 1), 0) * CMP_STRIDE + CMP_BLOCK - 1) <= tpos
    probs = []
    for h in range(R):
        sh = jnp.where(cvalid, s[:, h * tq:(h + 1) * tq], NEG)
        e = jnp.where(cvalid, jnp.exp(sh - jnp.max(sh, axis=0, keepdims=True)), 0.0)
        l = jnp.sum(e, axis=0, keepdims=True)
        probs.append(e / jnp.where(l > 0.0, l, 1.0))
    p_sum = probs[0]
    for h in range(1, R):
        p_sum = p_sum + probs[h]
    vct = _transpose_rows(cmp_tok.astype(BF16))[dh:].astype(BF16)
    o_cmp = _dot(vct, jnp.concatenate(probs, axis=1).astype(BF16))

    ns = bias_ref.shape[0] // SEL_BLOCK
    blk = _iota((ns, tq), 0)
    cur = tpos // SEL_BLOCK
    c_idx = _iota((ns, nc), 1)
    lo_c = SEL_RATIO * _iota((ns, nc), 0) - 1
    spread = jnp.where(c_idx >= lo_c, jnp.where(c_idx <= lo_c + SEL_RATIO, 1.0, 0.0), 0.0)
    imp = _dot_exact(spread, p_sum)
    forced = jnp.where(blk == 0, 1.0, jnp.where(blk == cur, 1.0, jnp.where(blk == cur - 1, 1.0, 0.0)))
    imp = jnp.where(blk > cur, NEG, jnp.where(forced > 0.5, FORCE, imp))
    rank = jnp.zeros(imp.shape, F32)
    for c in range(n_sel):
        row = imp[c:c + 1, :]
        rank = rank + jnp.where(row > imp, 1.0, jnp.where(row == imp, jnp.where(blk > c, 1.0, 0.0), 0.0))
    chosen = jnp.where(blk < n_sel, jnp.where(rank < n_top, 1.0, 0.0), 0.0)
    for j in range(n_sel):
        picked = jnp.broadcast_to(chosen[j:j + 1, :], (SEL_BLOCK, tq))
        kpos = j * SEL_BLOCK + _iota((SEL_BLOCK, 1), 0)
        bias_ref[j * SEL_BLOCK:(j + 1) * SEL_BLOCK, :] = jnp.where(
            picked > 0.5, jnp.where(kpos <= tpos, 0.0, NEG), NEG)

    hi = (t0 + tq + tk - 1) // tk
    qrs = _stack_heads(qr_ref, R, dh, scale)
    acc = _flash(qrs, k2s, slice(None), vts, 0, m_sc, acc_sc, 0, hi, tk,
                 lambda kt, k0: bias_ref[pl.ds(k0, tk), :], R)
    o_sel = acc[ONES_ROWS:] / acc[0:1]
    kw0 = pl.multiple_of(jnp.clip(t0 - WINDOW, 0, T - wk), LANES)
    d = tpos - (kw0 + _iota((wk, 1), 0))
    win_bias = jnp.where(d >= 0, jnp.where(d <= WINDOW, 0.0, NEG), NEG)
    s = _add_bias(_dot_nt(k2w[pl.ds(kw0, wk), :], qrs), win_bias, R)
    e = jnp.exp(s - jnp.max(s, axis=0, keepdims=True)).astype(BF16)
    acc = _weighted_values(vtw, kw0 // LANES, e)
    o_win = acc[ONES_ROWS:] / acc[0:1]
    gate = jax.nn.sigmoid(gl_ref[...].T)
    heads = []
    for h in range(R):
        hs = slice(h * tq, (h + 1) * tq)
        heads.append(gate[h:h + 1] * o_cmp[:, hs] + gate[R + h:R + h + 1] * o_sel[:, hs]
                     + gate[2 * R + h:2 * R + h + 1] * o_win[:, hs])
    _write_heads_t(o_ref, heads)


def _attn_nsa(proj, cmp_tok, B, T, tk=None):
    G = NSA_KV_HEADS
    R = NSA_HEADS // G
    tq = LANES
    tk = tk or min(T, 512)
    nq = T // tq
    n_sel = -(-T // SEL_BLOCK)
    n_top = min(SEL_TOP, n_sel)
    qw = NSA_HEADS * NSA_HEAD_DIM // G
    kv0 = 2 * NSA_HEADS * NSA_HEAD_DIM // LANES
    gl0 = kv0 + 3 * G
    nch = cmp_tok.shape[1]
    dh = NSA_HEAD_DIM
    kern = functools.partial(_attn_nsa_kernel, T=T, tq=tq, tk=tk, n_sel=n_sel, n_top=n_top)
    return pl.pallas_call(
        kern, out_shape=jax.ShapeDtypeStruct((B * T, NSA_HEADS * dh), F32), grid=(B, G, nq),
        in_specs=[pl.BlockSpec((tq, qw), lambda b, g, i: (b * nq + i, g)),
                  pl.BlockSpec((tq, qw), lambda b, g, i: (b * nq + i, G + g)),
                  pl.BlockSpec((1, nch, 2 * dh), lambda b, g, i: (b, 0, g)),
                  pl.BlockSpec((T, 2 * dh), lambda b, g, i: (b, kv0 + G + g)),
                  pl.BlockSpec((T, 2 * dh), lambda b, g, i: (b, kv0 + 2 * G + g)),
                  pl.BlockSpec((tq, LANES), lambda b, g, i: (b * nq + i, gl0 + g))],
        out_specs=pl.BlockSpec((tq, qw), lambda b, g, i: (b * nq + i, g)),
        scratch_shapes=[pltpu.VMEM((T, 2 * dh), BF16), pltpu.VMEM((T // LANES, ONES_ROWS + dh, LANES), BF16)] * 2 + [
            pltpu.VMEM((T, tq), F32), pltpu.VMEM((1, R * tq), F32), pltpu.VMEM((ONES_ROWS + dh, R * tq), F32)],
        compiler_params=_cp("parallel", "parallel", "arbitrary"), name="attn_nsa",
    )(proj, proj, cmp_tok, proj, proj, proj)


def _nsa_dec_cmp_kernel(q_ref, cmp_ref, o_ref, imp_ref, *, t, n_sel):
    G = NSA_KV_HEADS
    R = NSA_HEADS // G
    dh = NSA_HEAD_DIM
    nc = cmp_ref.shape[1]
    nl = imp_ref.shape[2]
    cvalid = (_iota((1, nc), 1) * CMP_STRIDE + CMP_BLOCK - 1) <= t
    blk = _iota((1, nl), 1)
    cur = t // SEL_BLOCK
    for g in range(G):
        kc = cmp_ref[0, :, 2 * g * dh:(2 * g + 1) * dh].astype(BF16)
        vc = cmp_ref[0, :, (2 * g + 1) * dh:(2 * g + 2) * dh].astype(BF16)
        qg = (q_ref[0, g * R:(g + 1) * R, :] * dh ** -0.5).astype(BF16)
        p = _softmax_rows(_dot_nt(qg, kc), cvalid)
        o_ref[0, g * R:(g + 1) * R, :] = _dot(p.astype(BF16), vc)
        imp = _sel_importance(jnp.sum(p, axis=0, keepdims=True), nl)
        forced = jnp.where(blk == 0, 1.0, jnp.where(blk == cur, 1.0, jnp.where(blk == cur - 1, 1.0, 0.0)))
        imp = jnp.where(blk > cur, NEG, jnp.where(forced > 0.5, FORCE, imp))
        imp_ref[0, g:g + 1, :] = jnp.where(blk < n_sel, imp, -jnp.inf)


def _nsa_dec_cmp(q3, cmp_tok, t, n_sel):
    B = q3.shape[0]
    nl = -(-n_sel // LANES) * LANES
    nc = cmp_tok.shape[1]
    blk3 = lambda b: (b, 0, 0)
    return pl.pallas_call(
        functools.partial(_nsa_dec_cmp_kernel, t=t, n_sel=n_sel),
        out_shape=(jax.ShapeDtypeStruct(q3.shape, F32), jax.ShapeDtypeStruct((B, NSA_KV_HEADS, nl), F32)),
        grid=(B,),
        in_specs=[pl.BlockSpec((1,) + q3.shape[1:], blk3), pl.BlockSpec((1, nc, cmp_tok.shape[2]), blk3)],
        out_specs=(pl.BlockSpec((1,) + q3.shape[1:], blk3), pl.BlockSpec((1, NSA_KV_HEADS, nl), blk3)),
        compiler_params=_cp("parallel"), name="nsa_dec_cmp",
    )(q3, cmp_tok)


def _topk_idx_kernel(x_ref, o_ref, *, k):
    x = x_ref[...]
    lane = _iota(x.shape, 1)
    out_lane = _iota(o_ref.shape, 1)
    out = jnp.zeros(o_ref.shape, I32)
    big = x.shape[1]
    for n in range(k):
        m = jnp.max(x, axis=-1, keepdims=True)
        idx = jnp.min(jnp.where(x == m, lane, big), axis=-1, keepdims=True)
        out = jnp.where(out_lane == n, idx, out)
        x = jnp.where(lane == idx, -jnp.inf, x)
    o_ref[...] = out


def _topk_idx(x, k):
    rows = x.shape[0]
    return pl.pallas_call(functools.partial(_topk_idx_kernel, k=k),
                          out_shape=jax.ShapeDtypeStruct((rows, LANES), I32), name="topk_idx")(x)


def _nsa_dec_attn_kernel(*refs, n_top, n_past_blocks):
    idx_ref, rb_ref = refs[0], refs[1]
    G = NSA_KV_HEADS
    R = NSA_HEADS // G
    dh = NSA_HEAD_DIM
    blocks = refs[2:2 + G * n_top]
    qr_ref, new_ref, win_ref, ocmp_ref, gl_ref, o_ref = refs[2 + G * n_top:]
    b = pl.program_id(0)
    scale = dh ** -0.5
    per_page = PAGE_SIZE // SEL_BLOCK
    nk = n_top * PAGE_SIZE
    lane = _iota((1, nk), 1)
    for g in range(G):
        qg = qr_ref[0, g * R:(g + 1) * R, :] * scale
        qb = qg.astype(BF16)
        kt = jnp.concatenate([blocks[g * n_top + n][0, :dh, :] for n in range(n_top)], axis=1).astype(BF16)
        vt = jnp.concatenate([blocks[g * n_top + n][0, dh:, :] for n in range(n_top)], axis=1).astype(BF16)
        valid = jnp.zeros((1, nk), F32)
        for n in range(n_top):
            blk = idx_ref[b, g * n_top + n]
            flag = jnp.where(blk < n_past_blocks, 1.0, 0.0)
            in_block = jnp.where((lane % PAGE_SIZE) // SEL_BLOCK == blk % per_page, flag, 0.0)
            valid = jnp.where(lane // PAGE_SIZE == n, in_block, valid)
        valid = valid > 0.5
        k_new = new_ref[0, :, (2 * G + 2 * g) * dh:(2 * G + 2 * g + 1) * dh]
        v_new = new_ref[0, :, (2 * G + 2 * g + 1) * dh:(2 * G + 2 * g + 2) * dh]
        s = jnp.where(valid, _dot(qb, kt), NEG)
        s_new = jnp.sum(qg * k_new, axis=-1, keepdims=True)
        m = jnp.maximum(jnp.max(s, axis=-1, keepdims=True), s_new)
        e = jnp.where(valid, jnp.exp(s - m), 0.0)
        e_new = jnp.exp(s_new - m)
        o_sel = (_dot_nt(e.astype(BF16), vt) + e_new * v_new) / (jnp.sum(e, axis=-1, keepdims=True) + e_new)
        kwin = win_ref[0, 2 * g * dh:(2 * g + 1) * dh, :].astype(BF16)
        vwin = win_ref[0, (2 * g + 1) * dh:(2 * g + 2) * dh, :].astype(BF16)
        k_new = new_ref[0, :, (4 * G + 2 * g) * dh:(4 * G + 2 * g + 1) * dh]
        v_new = new_ref[0, :, (4 * G + 2 * g + 1) * dh:(4 * G + 2 * g + 2) * dh]
        s = _dot(qb, kwin)
        s_new = jnp.sum(qg * k_new, axis=-1, keepdims=True)
        m = jnp.maximum(jnp.max(s, axis=-1, keepdims=True), s_new)
        e = jnp.exp(s - m)
        e_new = jnp.exp(s_new - m)
        o_win = (_dot_nt(e.astype(BF16), vwin) + e_new * v_new) / (jnp.sum(e, axis=-1, keepdims=True) + e_new)
        gate = jax.nn.sigmoid(gl_ref[0, g * R:(g + 1) * R, :])
        o_ref[0, g * R:(g + 1) * R, :] = (gate[:, 0:1] * ocmp_ref[0, g * R:(g + 1) * R, :]
                                          + gate[:, 1:2] * o_sel + gate[:, 2:3] * o_win)


def _nsa_dec_attn(idx, sel_pages, cache_t, qr3, new_kv, win_buf, win_index0, o_cmp, gl3, n_past_blocks):
    B = qr3.shape[0]
    G = NSA_KV_HEADS
    n_top = idx.shape[1] // G
    dh = NSA_HEAD_DIM
    blk3 = lambda b, idx, rb: (b, 0, 0)
    block_specs = [pl.BlockSpec((1, 2 * dh, PAGE_SIZE), functools.partial(
        lambda b, idx, rb, n, g: (rb[b, n], g, 0), n=g * n_top + n, g=g)) for g in range(G) for n in range(n_top)]
    grid_spec = pltpu.PrefetchScalarGridSpec(
        num_scalar_prefetch=2, grid=(B,),
        in_specs=block_specs + [
            pl.BlockSpec((1,) + qr3.shape[1:], blk3), pl.BlockSpec((1,) + new_kv.shape[1:], blk3),
            pl.BlockSpec((1,) + win_buf.shape[1:], lambda b, idx, rb: (win_index0 + b, 0, 0)),
            pl.BlockSpec((1,) + o_cmp.shape[1:], blk3), pl.BlockSpec((1,) + gl3.shape[1:], blk3)],
        out_specs=pl.BlockSpec((1,) + qr3.shape[1:], blk3))
    return pl.pallas_call(
        functools.partial(_nsa_dec_attn_kernel, n_top=n_top, n_past_blocks=n_past_blocks),
        out_shape=jax.ShapeDtypeStruct(qr3.shape, F32), grid_spec=grid_spec,
        compiler_params=_cp("parallel"), name="nsa_dec_attn",
    )(idx, sel_pages, *([cache_t] * (G * n_top)), qr3, new_kv, win_buf, o_cmp, gl3)


def _mla_post_kernel(dn_ref, gq_ref, gkv_ref, a_ref, b_ref, cq_ref, ckr_ref):
    x = dn_ref[...]
    cq = x[:, :Q_LORA]
    cq_ref[...] = cq * lax.rsqrt(jnp.mean(cq * cq, axis=-1, keepdims=True) + RMS_EPS) * gq_ref[...]
    ckv = x[:, Q_LORA:Q_LORA + KV_LORA]
    ckv = ckv * lax.rsqrt(jnp.mean(ckv * ckv, axis=-1, keepdims=True) + RMS_EPS) * gkv_ref[...]
    kr = x[:, Q_LORA + KV_LORA:]
    half = QK_ROPE // 2
    first = (_iota(kr.shape, 1) % QK_ROPE) < half
    partner = jnp.where(first, pltpu.roll(kr, LANES - half, 1), pltpu.roll(kr, half, 1))
    ckr_ref[...] = jnp.concatenate([ckv, kr * a_ref[...] + partner * b_ref[...]], axis=1)


def _mla_post(dn, g_q, g_kv, tabs):
    M, N = dn.shape
    tm = min(M, 512)
    nrb = tabs[0].shape[0] // tm
    row = lambda i: (i, 0)
    fix = lambda i: (0, 0)
    tab = pl.BlockSpec((tm, LANES), lambda i: (i % nrb, 0))
    return pl.pallas_call(
        _mla_post_kernel,
        out_shape=(jax.ShapeDtypeStruct((M, Q_LORA), F32), jax.ShapeDtypeStruct((M, KV_LORA + LANES), F32)),
        grid=(M // tm,),
        in_specs=[pl.BlockSpec((tm, N), row), pl.BlockSpec((1, Q_LORA), fix), pl.BlockSpec((1, KV_LORA), fix),
                  tab, tab],
        out_specs=(pl.BlockSpec((tm, Q_LORA), row), pl.BlockSpec((tm, KV_LORA + LANES), row)),
        compiler_params=_cp("parallel"), name="mla_post",
    )(dn, g_q.reshape(1, -1), g_kv.reshape(1, -1), *tabs)


def _attn_mla_kernel(q_ref, k_ref, v_ref, o_ref, k_sc, v_sc, m_sc, acc_sc, *, tq, tk):
    i = pl.program_id(2)
    t0 = i * tq
    scale = (QK_NOPE + QK_ROPE) ** -0.5

    n_vt = k_ref.shape[0] // LANES

    @pl.when(i == 0)
    def _():
        k_sc[...] = k_ref[...].astype(BF16)
        vt = _transpose_rows(v_ref[...].astype(BF16))
        for hh in range(2):
            _store_values_t(v_sc, hh * n_vt, vt[hh * V_DIM:(hh + 1) * V_DIM])

    tpos = t0 + _iota((1, tq), 1)
    hi = (t0 + tq + tk - 1) // tk
    causal = lambda kt, k0: jnp.where(k0 + _iota((tk, 1), 0) <= tpos, 0.0, NEG)
    heads = []
    for hh in range(2):
        qh = (q_ref[:, hh * MLA_SLOT:(hh + 1) * MLA_SLOT] * scale).astype(BF16)
        acc = _flash(qh, k_sc, slice(hh * MLA_SLOT, (hh + 1) * MLA_SLOT), v_sc, hh * n_vt, m_sc, acc_sc, 0, hi, tk,
                     causal, 1)
        heads.append(acc[ONES_ROWS:] / acc[0:1])
    _write_heads_t(o_ref, heads)


def _attn_mla(q_ext, kv_ext, B, T, tk=None):
    H = MLA_HEADS
    tq = min(T, 1024)
    tk = tk or min(T, 512)
    nq = T // tq
    v0 = H * MLA_SLOT // LANES
    return pl.pallas_call(
        functools.partial(_attn_mla_kernel, tq=tq, tk=tk),
        out_shape=jax.ShapeDtypeStruct((B * T, H * V_DIM), F32), grid=(B, H // 2, nq),
        in_specs=[pl.BlockSpec((tq, 2 * MLA_SLOT), lambda b, h, i: (b * nq + i, h)),
                  pl.BlockSpec((T, 2 * MLA_SLOT), lambda b, h, i: (b, h)),
                  pl.BlockSpec((T, 2 * V_DIM), lambda b, h, i: (b, v0 + h))],
        out_specs=pl.BlockSpec((tq, 2 * V_DIM), lambda b, h, i: (b * nq + i, h)),
        scratch_shapes=[pltpu.VMEM((T, 2 * MLA_SLOT), BF16),
                        pltpu.VMEM((2 * (T // LANES), ONES_ROWS + V_DIM, LANES), BF16),
                        pltpu.VMEM((1, tq), F32), pltpu.VMEM((ONES_ROWS + V_DIM, tq), F32)],
        compiler_params=_cp("parallel", "parallel", "arbitrary"), name="attn_mla",
    )(q_ext, kv_ext, kv_ext)


def _mla_decode_kernel(*refs, per_step):
    pt_ref = refs[0]
    ckv_pages = refs[1:1 + per_step]
    kr_pages = refs[1 + per_step:1 + 2 * per_step]
    ql_ref, qr_ref, cnew_ref, rnew_ref, o_ref, m_sc, l_sc, acc_sc = refs[1 + 2 * per_step:]
    s_idx = pl.program_id(1)
    scale = (QK_NOPE + QK_ROPE) ** -0.5
    ql = ql_ref[0] * scale
    qr = qr_ref[0] * scale

    @pl.when(s_idx == 0)
    def _():
        c_new = cnew_ref[0]
        s_new = (jnp.sum(ql * c_new, axis=-1, keepdims=True) + jnp.sum(qr * rnew_ref[0], axis=-1, keepdims=True))
        m_sc[...] = s_new
        l_sc[...] = jnp.ones(l_sc.shape, F32)
        acc_sc[...] = jnp.broadcast_to(c_new, acc_sc.shape)

    qlb = ql.astype(BF16)
    qrb = qr.astype(BF16)
    cs = [ref[0].astype(BF16) for ref in ckv_pages]
    s = jnp.concatenate([_dot_nt(qlb, c) + _dot(qrb, r[0].astype(BF16)) for c, r in zip(cs, kr_pages)], axis=1)
    m_old = m_sc[...]
    m_new = jnp.maximum(m_old, jnp.max(s, axis=-1, keepdims=True))
    alpha = jnp.exp(m_old - m_new)
    e = jnp.exp(s - m_new).astype(BF16)
    pv = _dot(e[:, :PAGE_SIZE], cs[0])
    for p in range(1, per_step):
        pv = pv + _dot(e[:, p * PAGE_SIZE:(p + 1) * PAGE_SIZE], cs[p])
    l_sc[...] = alpha * l_sc[...] + jnp.sum(e.astype(F32), axis=-1, keepdims=True)
    acc_sc[...] = alpha * acc_sc[...] + pv
    m_sc[...] = m_new

    @pl.when(s_idx == pl.num_programs(1) - 1)
    def _():
        o_ref[0] = acc_sc[...] / l_sc[...]


def _mla_decode(pages, cache_ckv, cache_kr, q_lat, q_rope, c_new, r_new, per_step):
    B, n_pages = pages.shape
    H = MLA_HEADS
    blk3 = lambda b, s, pt: (b, 0, 0)
    page_map = lambda p: functools.partial(lambda b, s, pt, p: (pt[b, s * per_step + p], 0, 0), p=p)
    grid_spec = pltpu.PrefetchScalarGridSpec(
        num_scalar_prefetch=1, grid=(B, n_pages // per_step),
        in_specs=([pl.BlockSpec((1, PAGE_SIZE, KV_LORA), page_map(p)) for p in range(per_step)]
                  + [pl.BlockSpec((1, QK_ROPE, PAGE_SIZE), page_map(p)) for p in range(per_step)]
                  + [pl.BlockSpec((1, H, KV_LORA), blk3), pl.BlockSpec((1, H, QK_ROPE), blk3),
                     pl.BlockSpec((1, 1, KV_LORA), blk3), pl.BlockSpec((1, 1, QK_ROPE), blk3)]),
        out_specs=pl.BlockSpec((1, H, KV_LORA), blk3),
        scratch_shapes=[pltpu.VMEM((H, 1), F32), pltpu.VMEM((H, 1), F32), pltpu.VMEM((H, KV_LORA), F32)])
    return pl.pallas_call(
        functools.partial(_mla_decode_kernel, per_step=per_step),
        out_shape=jax.ShapeDtypeStruct((B, H, KV_LORA), F32), grid_spec=grid_spec,
        compiler_params=_cp("parallel", "arbitrary"), name="mla_decode",
    )(pages, *([cache_ckv] * per_step), *([cache_kr] * per_step), q_lat, q_rope, c_new, r_new)


def _attn_moba_kernel(q_ref, kv_ref, o_ref, k2, vt_sc, mean_sc, mask_sc, m_sc, acc_sc, *, T, tq, tk, nb, n_top):
    i = pl.program_id(2)
    t0 = i * tq
    R = MOBA_HEADS // MOBA_KV_HEADS
    dh = MOBA_HEAD_DIM
    L = R * tq
    nbp = mean_sc.shape[0]
    bpt = tk // MOBA_BLOCK

    @pl.when(i == 0)
    def _():
        kv = kv_ref[...]
        k2[...] = _dup_keys(kv, dh).astype(BF16)
        _store_values_t(vt_sc, 0, _transpose_rows(kv.astype(BF16))[dh:])
        mean_sc[...] = jnp.zeros(mean_sc.shape, F32)
        for j in range(T // MOBA_BLOCK):
            blk_rows = _dup_keys(kv_ref[j * MOBA_BLOCK:(j + 1) * MOBA_BLOCK, :], dh)
            mean_sc[j:j + 1, :] = jnp.sum(blk_rows, axis=0, keepdims=True) / MOBA_BLOCK

    tpos = t0 + _iota((1, L), 1) % tq
    cur = tpos // MOBA_BLOCK
    jb = _iota((nbp, L), 0)
    qs = _stack_heads(q_ref, R, dh, dh ** -0.5)
    block_mask = jnp.where(jb == cur, 1.0, 0.0)
    if n_top > 0:
        gm = jnp.where(jb < cur, _dot_nt(mean_sc[...].astype(BF16), qs), NEG)
        rank = jnp.zeros(gm.shape, F32)
        for c in range(nb):
            row = gm[c:c + 1, :]
            rank = rank + jnp.where(row > gm, 1.0, jnp.where(row == gm, jnp.where(jb > c, 1.0, 0.0), 0.0))
        block_mask = jnp.where(jb < cur, jnp.where(rank < n_top, 1.0, 0.0), block_mask)
    mask_sc[...] = block_mask

    def bias(kt, k0):
        parts = []
        for j in range(bpt):
            picked = mask_sc[pl.ds(kt * bpt + j, 1), :]
            kpos = k0 + j * MOBA_BLOCK + _iota((MOBA_BLOCK, 1), 0)
            parts.append(jnp.where(picked > 0.5, jnp.where(kpos <= tpos, 0.0, NEG), NEG))
        return parts[0] if bpt == 1 else jnp.concatenate(parts, axis=0)

    hi = (t0 + tq + tk - 1) // tk
    acc = _flash(qs, k2, slice(None), vt_sc, 0, m_sc, acc_sc, 0, hi, tk, bias, 1)
    o = acc[ONES_ROWS:] / acc[0:1]
    _write_heads_t(o_ref, [o[:, r * tq:(r + 1) * tq] for r in range(R)])


def _attn_moba(proj, B, T, tk=None):
    KH = MOBA_KV_HEADS
    R = MOBA_HEADS // KH
    dh = MOBA_HEAD_DIM
    tq = min(T, 256)
    tk = tk or min(T, 512)
    nq = T // tq
    nb = (T - 1) // MOBA_BLOCK
    n_top = min(MOBA_TOP, nb)
    qw = MOBA_HEADS * dh // KH
    kv0 = MOBA_HEADS * dh // LANES
    nbp = max(T // MOBA_BLOCK, 8)
    return pl.pallas_call(
        functools.partial(_attn_moba_kernel, T=T, tq=tq, tk=tk, nb=nb, n_top=n_top),
        out_shape=jax.ShapeDtypeStruct((B * T, MOBA_HEADS * dh), F32), grid=(B, KH, nq),
        in_specs=[pl.BlockSpec((tq, qw), lambda b, h, i: (b * nq + i, h)),
                  pl.BlockSpec((T, 2 * dh), lambda b, h, i: (b, kv0 + h))],
        out_specs=pl.BlockSpec((tq, qw), lambda b, h, i: (b * nq + i, h)),
        scratch_shapes=[pltpu.VMEM((T, 2 * dh), BF16), pltpu.VMEM((T // LANES, ONES_ROWS + dh, LANES), BF16),
                        pltpu.VMEM((nbp, 2 * dh), F32), pltpu.VMEM((nbp, R * tq), F32),
                        pltpu.VMEM((1, R * tq), F32), pltpu.VMEM((ONES_ROWS + dh, R * tq), F32)],
        compiler_params=_cp("parallel", "parallel", "arbitrary"), name="attn_moba",
    )(proj, proj)


def _moba_means_kernel(*refs, per_step):
    pages, o_ref = refs[1:1 + per_step], refs[1 + per_step]
    KH, dh = MOBA_KV_HEADS, MOBA_HEAD_DIM
    ppb = MOBA_BLOCK // PAGE_SIZE
    s = pl.program_id(1)

    @pl.when(s == 0)
    def _():
        o_ref[...] = jnp.zeros(o_ref.shape, F32)

    lane = _iota(o_ref.shape[1:], 1)
    out = o_ref[0]
    for j in range(per_step // ppb):
        cols = []
        for kh in range(KH):
            tot = pages[j * ppb][0, kh, 0]
            for p in range(1, ppb):
                tot = tot + pages[j * ppb + p][0, kh, 0]
            cols.append(jnp.sum(tot, axis=1, keepdims=True))
        col = jnp.concatenate(cols, axis=0) / MOBA_BLOCK
        out = jnp.where(lane == s * (per_step // ppb) + j, col, out)
    o_ref[0] = out


def _moba_means(pages, cache_t, per_step):
    B, n_pages = pages.shape
    KH, dh = MOBA_KV_HEADS, MOBA_HEAD_DIM
    nk = KH * dh
    assert n_pages * PAGE_SIZE // MOBA_BLOCK <= LANES
    cache_t = cache_t.reshape(-1, KH, 2, dh, PAGE_SIZE)
    grid_spec = pltpu.PrefetchScalarGridSpec(
        num_scalar_prefetch=1, grid=(B, n_pages // per_step),
        in_specs=[pl.BlockSpec((1, KH, 1, dh, PAGE_SIZE), functools.partial(
            lambda b, s, pt, p: (pt[b, s * per_step + p], 0, 0, 0, 0), p=p)) for p in range(per_step)],
        out_specs=pl.BlockSpec((1, nk, LANES), lambda b, s, pt: (b, 0, 0)))
    return pl.pallas_call(
        functools.partial(_moba_means_kernel, per_step=per_step),
        out_shape=jax.ShapeDtypeStruct((B, nk, LANES), F32), grid_spec=grid_spec,
        compiler_params=_cp("parallel", "arbitrary"), name="moba_means",
    )(pages, *([cache_t] * per_step))


def _moba_dec_score_kernel(q_ref, mean_ref, o_ref, *, nb):
    KH = MOBA_KV_HEADS
    R = MOBA_HEADS // KH
    dh = MOBA_HEAD_DIM
    lane = _iota((R, LANES), 1)
    for kh in range(KH):
        mk = mean_ref[0, kh * dh:(kh + 1) * dh, :].astype(BF16)
        g = _dot(q_ref[0, kh * R:(kh + 1) * R, :].astype(BF16), mk)
        o_ref[0, kh * R:(kh + 1) * R, :] = jnp.where(lane < nb, g, -jnp.inf)


def _moba_dec_score(q3, means, nb):
    B, H, dh = q3.shape
    blk3 = lambda b: (b, 0, 0)
    return pl.pallas_call(
        functools.partial(_moba_dec_score_kernel, nb=nb),
        out_shape=jax.ShapeDtypeStruct((B, H, LANES), F32), grid=(B,),
        in_specs=[pl.BlockSpec((1, H, dh), blk3), pl.BlockSpec((1,) + means.shape[1:], blk3)],
        out_specs=pl.BlockSpec((1, H, LANES), blk3), compiler_params=_cp("parallel"), name="moba_dec_score",
    )(q3, means)


def _moba_dec_attn_kernel(*refs, n_blk):
    R = MOBA_HEADS // MOBA_KV_HEADS
    dh = MOBA_HEAD_DIM
    pages = refs[1:1 + R * n_blk]
    q_ref, new_ref, o_ref = refs[1 + R * n_blk:]
    k_new = new_ref[0, :, :dh]
    v_new = new_ref[0, :, dh:]
    for r in range(R):
        q = q_ref[0, 0, r:r + 1, :] * dh ** -0.5
        kt = jnp.concatenate([pages[r * n_blk + n][0, :dh, :] for n in range(n_blk)], axis=1).astype(BF16)
        vt = jnp.concatenate([pages[r * n_blk + n][0, dh:, :] for n in range(n_blk)], axis=1).astype(BF16)
        s = _dot(q.astype(BF16), kt)
        s_new = jnp.sum(q * k_new, axis=-1, keepdims=True)
        m = jnp.maximum(jnp.max(s, axis=-1, keepdims=True), s_new)
        e = jnp.exp(s - m)
        e_new = jnp.exp(s_new - m)
        o_ref[0, 0, r:r + 1, :] = ((_dot_nt(e.astype(BF16), vt) + e_new * v_new)
                                   / (jnp.sum(e, axis=-1, keepdims=True) + e_new))


def _moba_dec_attn(page_ids, cache_t, q4, new_kv):
    B, KH, R, dh = q4.shape
    n_blk = page_ids.shape[1] // (KH * R)
    page_specs = [pl.BlockSpec((1, 2 * dh, PAGE_SIZE), functools.partial(
        lambda b, h, pg, n: (pg[b, h * R * n_blk + n], h, 0), n=n)) for n in range(R * n_blk)]
    grid_spec = pltpu.PrefetchScalarGridSpec(
        num_scalar_prefetch=1, grid=(B, KH),
        in_specs=page_specs + [pl.BlockSpec((1, 1, R, dh), lambda b, h, pg: (b, h, 0, 0)),
                               pl.BlockSpec((1, 1, 2 * dh), lambda b, h, pg: (b, 0, h))],
        out_specs=pl.BlockSpec((1, 1, R, dh), lambda b, h, pg: (b, h, 0, 0)))
    return pl.pallas_call(
        functools.partial(_moba_dec_attn_kernel, n_blk=n_blk),
        out_shape=jax.ShapeDtypeStruct(q4.shape, F32), grid_spec=grid_spec,
        compiler_params=_cp("parallel", "parallel"), name="moba_dec_attn",
    )(page_ids, *([cache_t] * (R * n_blk)), q4, new_kv)


def _router_kernel(x_ref, w_ref, b_ref, o_ref):
    s = jax.nn.sigmoid(lax.dot_general(w_ref[...], x_ref[...], (((1,), (1,)), ((), ())),
                                       preferred_element_type=F32, precision=lax.Precision.HIGHEST))
    row = _iota(s.shape, 0)
    x = s + b_ref[...]
    chosen = jnp.zeros(s.shape, F32)
    for _ in range(TOP_K):
        m = jnp.max(x, axis=0, keepdims=True)
        idx = jnp.min(jnp.where(x == m, row, N_EXPERTS), axis=0, keepdims=True)
        hit = row == idx
        chosen = jnp.where(hit, 1.0, chosen)
        x = jnp.where(hit, -jnp.inf, x)
    w = chosen * s
    w = w / jnp.sum(w, axis=0, keepdims=True) * ROUTED_SCALE
    o_ref[...] = jnp.concatenate([w, jnp.zeros((LANES - N_EXPERTS, w.shape[1]), F32)], axis=0).T


def _moe_router(x, w_router_t, b_router):
    M, K = x.shape
    tm = min(M, 512)
    E = w_router_t.shape[0]
    row = lambda i: (i, 0)
    fix = lambda i: (0, 0)
    return pl.pallas_call(
        _router_kernel, out_shape=jax.ShapeDtypeStruct((M, LANES), F32), grid=(M // tm,),
        in_specs=[pl.BlockSpec((tm, K), row), pl.BlockSpec((E, K), fix), pl.BlockSpec((E, 1), fix)],
        out_specs=pl.BlockSpec((tm, LANES), row), compiler_params=_cp("parallel"), name="moe_router",
    )(x, w_router_t, b_router)


def _moe_kernel(x_ref, gate_ref, wg_ref, wu_ref, wd_ref, wsg_ref, wsu_ref, wsd_ref, g_ref, b_ref, o_ref,
                xb_sc, acc_sc):
    e = pl.program_id(1)
    nh = MOE_GROUP * D_EXPERT

    @pl.when(e == 0)
    def _():
        xb = x_ref[...].astype(BF16)
        xb_sc[...] = xb
        hs = _silu(_dot(xb, wsg_ref[0].astype(BF16))) * _dot(xb, wsu_ref[0].astype(BF16))
        acc_sc[...] = _dot(hs.astype(BF16), wsd_ref[0].astype(BF16))

    xb = xb_sc[...]
    w_gate = jnp.concatenate([wg_ref[0, k].astype(BF16) for k in range(MOE_GROUP)], axis=1)
    w_up = jnp.concatenate([wu_ref[0, k].astype(BF16) for k in range(MOE_GROUP)], axis=1)
    src = _iota((2 * LANES, nh), 0) % LANES
    pick = jnp.where(src == MOE_GROUP * e + _iota((2 * LANES, nh), 1) // D_EXPERT, 1.0, 0.0).astype(BF16)
    gate = gate_ref[...]
    gate_hi = gate.astype(BF16)
    gate_lo = (gate - gate_hi.astype(F32)).astype(BF16)
    gw = _dot(jnp.concatenate([gate_hi, gate_lo], axis=1), pick)
    h = _silu(_dot(xb, w_gate)) * _dot(xb, w_up) * gw
    w_down = jnp.concatenate([wd_ref[0, k].astype(BF16) for k in range(MOE_GROUP)], axis=0)
    acc_sc[...] += _dot(h.astype(BF16), w_down)

    @pl.when(e == pl.num_programs(1) - 1)
    def _():
        o_ref[...] = _layer_norm(ALPHA * x_ref[...] + acc_sc[...], g_ref[...], b_ref[...])


def _moe(x, gate, w, g, b):
    M, D = x.shape
    tm = min(M, 1024)
    layer = w['layer']
    E, _, F = w['w_gate'].shape[1:]
    n = MOE_GROUP
    row = lambda i, e: (i, 0)
    fix = lambda i, e: (0, 0)
    grp = lambda i, e: (layer, e, 0, 0)
    lay = lambda i, e: (layer, 0, 0)
    return pl.pallas_call(
        _moe_kernel, out_shape=jax.ShapeDtypeStruct((M, D), F32), grid=(M // tm, E // n),
        in_specs=[pl.BlockSpec((tm, D), row), pl.BlockSpec((tm, LANES), row),
                  pl.BlockSpec((1, n, D, F), grp), pl.BlockSpec((1, n, D, F), grp), pl.BlockSpec((1, n, F, D), grp),
                  pl.BlockSpec((1, D, D_SHARED), lay), pl.BlockSpec((1, D, D_SHARED), lay),
                  pl.BlockSpec((1, D_SHARED, D), lay),
                  pl.BlockSpec((1, D), fix), pl.BlockSpec((1, D), fix)],
        out_specs=pl.BlockSpec((tm, D), row),
        scratch_shapes=[pltpu.VMEM((tm, D), BF16), pltpu.VMEM((tm, D), F32)],
        compiler_params=_cp("parallel", "arbitrary"), name="moe",
    )(x, gate, w['w_gate'], w['w_up'], w['w_down'], w['ws_gate'], w['ws_up'], w['ws_down'],
      g.reshape(1, D), b.reshape(1, D))


def _pad_cols(w, n):
    return jnp.pad(w, ((0, 0), (0, n - w.shape[1])))


def _block_diag(blocks):
    n, a, b = blocks.shape
    eye = jnp.eye(n, dtype=blocks.dtype)
    return (eye[:, None, :, None] * blocks[:, :, None, :]).reshape(n * a, n * b)


def _nsa_weights(w_in, cmp_pe, cmp_w1, cmp_b1, cmp_w2, cmp_b2, w_o):
    H, G, dh = NSA_HEADS, NSA_KV_HEADS, NSA_HEAD_DIM
    R = H // G
    nq, nkv = H * dh, 6 * G * dh
    wq, wkv = w_in[:, :nq], w_in[:, nq:nq + nkv]
    wg = w_in[:, nq + nkv:].reshape(-1, 3, G, R).transpose(0, 2, 1, 3).reshape(-1, G, 3 * R)
    wg = jnp.pad(wg, ((0, 0), (0, 0), (0, LANES - 3 * R))).reshape(-1, G * LANES)
    w_ext = jnp.concatenate([wq, wq, wkv, wg], axis=1).astype(BF16)
    w1 = cmp_w1.reshape(2, CMP_STRIDE, 2, dh, CMP_HIDDEN)
    eye_k = jnp.eye(2, dtype=F32)
    w_lohi = (w1.transpose(1, 2, 3, 0, 4)[:, :, :, :, None, :] * eye_k[None, :, None, None, :, None])
    w_lohi = w_lohi.reshape(CMP_STRIDE // 2, 2 * 2 * dh, 2 * 2 * CMP_HIDDEN).astype(BF16)
    pe_rows = jnp.broadcast_to(cmp_pe[:, None], (CMP_BLOCK, G, 2, dh)).reshape(1, CMP_BLOCK, G * 2 * dh)
    b1 = jnp.broadcast_to(cmp_b1[None], (G, 2, CMP_HIDDEN)).reshape(1, -1)
    b2 = jnp.broadcast_to(cmp_b2[None], (G, 2, dh)).reshape(1, -1)
    w2 = _block_diag(jnp.tile(cmp_w2, (G, 1, 1))).astype(BF16)
    return dict(w_ext=w_ext, w_lohi=w_lohi, pe_rows=pe_rows, b1=b1, b2=b2, w2=w2, w_o=w_o.astype(BF16))


def _nsa_tables(pos):
    H, G = NSA_HEADS, NSA_KV_HEADS
    dh = NSA_HEAD_DIM
    kv_rot = [(1.0, dh), (0.0, dh)] * G
    layout = [(0.0, H * dh), (1.0, H * dh), (0.0, 2 * G * dh)] + kv_rot + kv_rot + [(0.0, G * LANES)]
    return _unit_tables(pos, dh // 2) + (_column_mask(layout),)


def _mla_weights(w_dn, g_q, w_uq, g_kv, w_uk, w_uv, w_o):
    H = MLA_HEADS
    pad = MLA_SLOT - QK_NOPE - QK_ROPE
    w_dn_p = _pad_cols(w_dn, Q_LORA + KV_LORA + LANES).astype(BF16)
    wq = jnp.pad(w_uq.reshape(Q_LORA, H, QK_NOPE + QK_ROPE), ((0, 0), (0, 0), (0, pad)))
    w_q = wq.reshape(Q_LORA, H * MLA_SLOT).astype(BF16)
    wk_c = jnp.pad(w_uk, ((0, 0), (0, 0), (0, MLA_SLOT - QK_NOPE))).reshape(KV_LORA, H * MLA_SLOT)
    eye = jnp.pad(jnp.eye(QK_ROPE, dtype=F32), ((0, 0), (QK_NOPE, pad)))
    wk_r = jnp.tile(eye, (1, H))
    wk = jnp.concatenate([wk_c, wk_r, jnp.zeros((LANES - QK_ROPE, H * MLA_SLOT), F32)], axis=0)
    wv = jnp.pad(w_uv.reshape(KV_LORA, H * V_DIM), ((0, LANES), (0, 0)))
    w_kv = jnp.concatenate([wk, wv], axis=1).astype(BF16)
    absorb = jnp.pad(w_uk.transpose(1, 2, 0), ((0, 0), (0, MLA_SLOT - QK_NOPE), (0, 0)))
    w_absorb = _block_diag(absorb).astype(BF16)
    w_unabsorb = _block_diag(w_uv.transpose(1, 0, 2)).astype(BF16)
    return dict(w_dn=w_dn_p, g_q=g_q, g_kv=g_kv, w_q=w_q, w_kv=w_kv, w_absorb=w_absorb,
                w_unabsorb=w_unabsorb, w_o=w_o.astype(BF16))


def _mla_q_tables(pos):
    half = QK_ROPE // 2
    ua, ub = _rope_unit(pos, half)
    T = pos.shape[0]
    pad = MLA_SLOT - QK_NOPE - QK_ROPE
    a = jnp.concatenate([jnp.ones((T, QK_NOPE), F32), ua, jnp.ones((T, pad), F32)], axis=1)
    b = jnp.concatenate([jnp.zeros((T, QK_NOPE), F32), ub, jnp.zeros((T, pad), F32)], axis=1)
    return a, b, jnp.ones((1, MLA_HEADS * MLA_SLOT), F32)


def _mla_kr_tables(pos):
    return _tables(pos, QK_ROPE // 2, [('r', 1), ('n', LANES - QK_ROPE)])


def _moba_weights(w_in, w_o):
    return dict(w_in=w_in.astype(BF16), w_o=w_o.astype(BF16))


def _moba_tables(pos):
    dh = MOBA_HEAD_DIM
    layout = [(1.0, MOBA_HEADS * dh)] + [(1.0, dh), (0.0, dh)] * MOBA_KV_HEADS
    return _unit_tables(pos, dh // 2) + (_column_mask(layout),)


def _moe_weights(w_router, b_router, w_gate, w_up, w_down, ws_gate, ws_up, ws_down, layer=None):
    stacked = (w_router, b_router, w_gate, w_up, w_down, ws_gate, ws_up, ws_down)
    if layer is None:
        stacked, layer = tuple(a[None] for a in stacked), 0
    w_router, b_router, w_gate, w_up, w_down, ws_gate, ws_up, ws_down = stacked
    E = w_gate.shape[1]
    return dict(layer=layer, w_router=w_router[layer].T, b_router=b_router[layer].reshape(E, 1),
                w_gate=w_gate, w_up=w_up, w_down=w_down, ws_gate=ws_gate, ws_up=ws_up, ws_down=ws_down)


def _feature_major(cache):
    n, pool, rows = cache.shape[:3]
    nd = cache.ndim
    return cache.transpose(0, 1, *range(3, nd), 2).reshape(n * pool, -1, rows)


def _nsa_cmp_tokens(rows_lohi, w):
    pe_lohi = _cmp_lohi(w['pe_rows'], w['w_lohi'])
    return _cmp_combine(rows_lohi, pe_lohi, w['b1'], w['w2'], w['b2'])


def _nsa_prompt(h, B, T, w, tabs):
    G, dh = NSA_KV_HEADS, NSA_HEAD_DIM
    proj = _proj(h, w['w_ext'], tabs, dh // 2)
    kv0 = 2 * NSA_HEADS * dh
    width = 2 * G * dh
    lohi = _cmp_lohi(proj.reshape(B, T, -1), w['w_lohi'], col_block=kv0 // LANES)
    cmp_tok = _nsa_cmp_tokens(lohi, w)
    y = _attn_nsa(proj, cmp_tok, B, T)
    kv = proj[:, kv0:kv0 + 3 * width].reshape(B, T, 3, G, 2, dh)
    return y, (kv[:, :, 0], kv[:, :, 1], kv[:, -min(WINDOW, T):, 2])


def _nsa_sample(h, past_len, w, tabs, cache_cmp, cache_sel, win_state, slot, page_table):
    B = h.shape[0]
    G, H, dh = NSA_KV_HEADS, NSA_HEADS, NSA_HEAD_DIM
    R = H // G
    n_pool = cache_cmp.shape[1]
    width = 2 * G * dh
    proj = _proj(h, w['w_ext'], tabs, dh // 2)
    pages = page_table + slot * n_pool
    lohi = _cmp_lohi_paged(_feature_major(cache_cmp), pages, w['w_lohi'], per_step=min(32, page_table.shape[1]))
    cmp_tok = _nsa_cmp_tokens(lohi, w)
    n_sel = -(-(past_len + 1) // SEL_BLOCK)
    n_top = min(SEL_TOP, n_sel)
    q3 = proj[:, :H * dh].reshape(B, H, dh)
    qr3 = proj[:, H * dh:2 * H * dh].reshape(B, H, dh)
    o_cmp, imp = _nsa_dec_cmp(q3, cmp_tok, past_len, n_sel)
    idx = _topk_idx(imp.reshape(B * G, -1), n_top)[:, :n_top].reshape(B, G * n_top)
    n_past_blocks = past_len // SEL_BLOCK
    per_page = PAGE_SIZE // SEL_BLOCK
    sel_pages = jnp.take_along_axis(pages, jnp.minimum(idx, n_past_blocks - 1) // per_page, axis=1)
    kv0 = 2 * H * dh
    new_kv = proj[:, kv0:kv0 + 3 * width].reshape(B, 1, 3 * width)
    gl3 = proj[:, kv0 + 3 * width:].reshape(B, G, LANES)[:, :, :3 * R].reshape(B, G, 3, R)
    gl3 = gl3.transpose(0, 1, 3, 2).reshape(B, H, 3)
    Wn = win_state.shape[2]
    o = _nsa_dec_attn(idx, sel_pages, _feature_major(cache_sel), qr3, new_kv, _feature_major(win_state), slot * B,
                      o_cmp, gl3, n_past_blocks)
    kv = new_kv.reshape(B, 1, 3, G, 2, dh)
    new_win = jnp.concatenate([win_state[slot], kv[:, :, 2]], axis=1)[:, -Wn:]
    return o.reshape(B, H * dh), (kv[:, :, 0], kv[:, :, 1], new_win)


def _mla_front(h, w, q_tabs, kr_tabs):
    dn = _proj(h, w['w_dn'])
    c_q, ckr = _mla_post(dn, w['g_q'], w['g_kv'], kr_tabs)
    q_ext = _proj(c_q, w['w_q'], q_tabs, QK_ROPE // 2)
    return q_ext, ckr


def _mla_prompt(h, B, T, w, q_tabs, kr_tabs):
    q_ext, ckr = _mla_front(h, w, q_tabs, kr_tabs)
    kv_ext = _proj(ckr, w['w_kv'])
    y = _attn_mla(q_ext, kv_ext, B, T)
    return y, (ckr[:, :KV_LORA].reshape(B, T, KV_LORA), ckr[:, KV_LORA:KV_LORA + QK_ROPE].reshape(B, T, QK_ROPE))


def _mla_sample(h, w, q_tabs, kr_tabs, cache_ckv, cache_kr, slot, page_table):
    B = h.shape[0]
    H = MLA_HEADS
    n_pool = cache_ckv.shape[1]
    q_ext, ckr = _mla_front(h, w, q_tabs, kr_tabs)
    q_lat = _proj(q_ext, w['w_absorb']).reshape(B, H, KV_LORA)
    q_rope = q_ext.reshape(B, H, MLA_SLOT)[:, :, QK_NOPE:QK_NOPE + QK_ROPE]
    c_new = ckr[:, :KV_LORA].reshape(B, 1, KV_LORA)
    r_new = ckr[:, KV_LORA:KV_LORA + QK_ROPE].reshape(B, 1, QK_ROPE)
    pages = page_table + slot * n_pool
    o_lat = _mla_decode(pages, cache_ckv.reshape(-1, PAGE_SIZE, KV_LORA), _feature_major(cache_kr),
                        q_lat, q_rope, c_new, r_new, per_step=min(32, page_table.shape[1]))
    y = _proj(o_lat.reshape(B, H * KV_LORA), w['w_unabsorb'])
    return y, (c_new, r_new)


def _moba_prompt(h, B, T, w, tabs):
    KH, dh = MOBA_KV_HEADS, MOBA_HEAD_DIM
    proj = _proj(h, w['w_in'], tabs, dh // 2)
    y = _attn_moba(proj, B, T)
    return y, proj[:, MOBA_HEADS * dh:].reshape(B, T, KH, 2, dh)


def _moba_sample(h, past_len, w, tabs, cache, slot, page_table):
    B = h.shape[0]
    H, KH, dh = MOBA_HEADS, MOBA_KV_HEADS, MOBA_HEAD_DIM
    R = H // KH
    assert past_len % MOBA_BLOCK == 0 and past_len // MOBA_BLOCK >= MOBA_TOP
    n_pool = cache.shape[1]
    width = KH * 2 * dh
    ppb = MOBA_BLOCK // PAGE_SIZE
    proj = _proj(h, w['w_in'], tabs, dh // 2)
    pages = page_table + slot * n_pool
    cache_t = _feature_major(cache)
    means = _moba_means(pages, cache_t, per_step=min(16, page_table.shape[1]))
    q3 = proj[:, :H * dh].reshape(B, H, dh)
    scores = _moba_dec_score(q3, means, past_len // MOBA_BLOCK)
    idx = _topk_idx(scores.reshape(B * H, -1), MOBA_TOP)[:, :MOBA_TOP].reshape(B, H * MOBA_TOP)
    page_ids = jnp.take_along_axis(pages, (idx[:, :, None] * ppb + jnp.arange(ppb)).reshape(B, -1), axis=1)
    new_kv = proj[:, H * dh:].reshape(B, 1, width)
    o = _moba_dec_attn(page_ids, cache_t, q3.reshape(B, KH, R, dh), new_kv)
    return o.reshape(B, H * dh), new_kv.reshape(B, 1, KH, 2, dh)


def _moe_layer(h, w, g, b):
    gate = _moe_router(h, w['w_router'], w['b_router'])
    return _moe(h, gate, w, g, b)


def kernel(x_prompt, x_sample, cache_nsa_cmp, cache_nsa_sel, state_nsa_win, cache_mla_ckv, cache_mla_krope,
           cache_moba_kv, page_table, nsa_w_in, nsa_cmp_pe, nsa_cmp_w1, nsa_cmp_b1, nsa_cmp_w2, nsa_cmp_b2,
           nsa_w_o, mla_w_dn, mla_g_q, mla_w_uq, mla_g_kv, mla_w_uk, mla_w_uv, mla_w_o, moba_w_in, moba_w_o,
           ln1_g, ln1_b, ln2_g, ln2_b, moe_w_router, moe_b_router, moe_w_gate, moe_w_up, moe_w_down,
           moe_ws_gate, moe_ws_up, moe_ws_down):
    B, T, D = x_prompt.shape
    Bs, Ts, _ = x_sample.shape
    assert Ts == 1
    past_len = page_table.shape[1] * PAGE_SIZE
    assert state_nsa_win.shape[2] == WINDOW and past_len >= WINDOW
    pos_p = jnp.arange(T, dtype=I32)
    pos_s = jnp.full((Bs,), past_len, dtype=I32)
    hp = x_prompt.reshape(B * T, D)
    hs = x_sample.reshape(Bs, D)
    outs = {k: [] for k in ('cmp_p', 'cmp_s', 'sel_p', 'sel_s', 'win_p', 'win_s',
                            'ckv_p', 'ckv_s', 'kr_p', 'kr_s', 'mb_p', 'mb_s')}
    for i in range(DEPTH):
        kind, slot = i % N_MIXERS, i // N_MIXERS
        if kind == MIX_NSA:
            w = _nsa_weights(nsa_w_in[slot], nsa_cmp_pe[slot], nsa_cmp_w1[slot], nsa_cmp_b1[slot],
                             nsa_cmp_w2[slot], nsa_cmp_b2[slot], nsa_w_o[slot])
            yp, (a_p, b_p, c_p) = _nsa_prompt(hp, B, T, w, _nsa_tables(pos_p))
            ys, (a_s, b_s, c_s) = _nsa_sample(hs, past_len, w, _nsa_tables(pos_s), cache_nsa_cmp, cache_nsa_sel,
                                              state_nsa_win, slot, page_table)
            outs['cmp_p'].append(a_p); outs['cmp_s'].append(a_s)
            outs['sel_p'].append(b_p); outs['sel_s'].append(b_s)
            outs['win_p'].append(c_p); outs['win_s'].append(c_s)
        elif kind == MIX_MLA:
            w = _mla_weights(mla_w_dn[slot], mla_g_q[slot], mla_w_uq[slot], mla_g_kv[slot], mla_w_uk[slot],
                             mla_w_uv[slot], mla_w_o[slot])
            yp, (a_p, b_p) = _mla_prompt(hp, B, T, w, _mla_q_tables(pos_p), _mla_kr_tables(pos_p))
            ys, (a_s, b_s) = _mla_sample(hs, w, _mla_q_tables(pos_s), _mla_kr_tables(pos_s), cache_mla_ckv,
                                         cache_mla_krope, slot, page_table)
            outs['ckv_p'].append(a_p); outs['ckv_s'].append(a_s)
            outs['kr_p'].append(b_p); outs['kr_s'].append(b_s)
        else:
            w = _moba_weights(moba_w_in[slot], moba_w_o[slot])
            yp, a_p = _moba_prompt(hp, B, T, w, _moba_tables(pos_p))
            ys, a_s = _moba_sample(hs, past_len, w, _moba_tables(pos_s), cache_moba_kv, slot, page_table)
            outs['mb_p'].append(a_p); outs['mb_s'].append(a_s)
        hp = _proj_ln(yp, w['w_o'], hp, ln1_g[i], ln1_b[i])
        hs = _proj_ln(ys, w['w_o'], hs, ln1_g[i], ln1_b[i])
        mw = _moe_weights(moe_w_router, moe_b_router, moe_w_gate, moe_w_up, moe_w_down,
                          moe_ws_gate, moe_ws_up, moe_ws_down, layer=i)
        hp = _moe_layer(hp, mw, ln2_g[i], ln2_b[i])
        hs = _moe_layer(hs, mw, ln2_g[i], ln2_b[i])
    st = lambda k: jnp.stack(outs[k])
    return (hp.reshape(B, T, D), hs.reshape(Bs, Ts, D),
            st('cmp_p'), st('cmp_s'), st('sel_p'), st('sel_s'), st('win_p'), st('win_s'),
            st('ckv_p'), st('ckv_s'), st('kr_p'), st('kr_s'), st('mb_p'), st('mb_s'))
```

```python
import functools

import jax
import jax.numpy as jnp
from jax import lax
from jax.experimental import pallas as pl
from jax.experimental.pallas import tpu as pltpu

D_MODEL = 1024
DEPTH = 4
PAGE_SIZE = 128
N_MIXERS = 3
MIX_NSA, MIX_MLA, MIX_MOBA = 0, 1, 2

ALPHA = (2 * DEPTH) ** 0.25
LN_EPS = 1e-5
RMS_EPS = 1e-6
ROPE_THETA = 10000.0
NEG = -1e30
FORCE = 1e9

NSA_HEADS = 16
NSA_KV_HEADS = 2
NSA_HEAD_DIM = 64
CMP_BLOCK = 32
CMP_STRIDE = 16
CMP_HIDDEN = 128
SEL_BLOCK = 64
SEL_RATIO = SEL_BLOCK // CMP_STRIDE
SEL_TOP = 16
WINDOW = 512

MLA_HEADS = 16
Q_LORA = 384
KV_LORA = 256
QK_NOPE = 64
QK_ROPE = 32
V_DIM = 64
MLA_SLOT = 128

MOBA_HEADS = 16
MOBA_KV_HEADS = 4
MOBA_HEAD_DIM = 64
MOBA_BLOCK = 256
MOBA_TOP = 3

N_EXPERTS = 64
TOP_K = 8
D_EXPERT = 128
D_SHARED = 128
ROUTED_SCALE = 2.5

LANES = 128
FLASH_CHAINS = 1
MOE_GROUP = 4
ONES_ROWS = 16
VMEM_LIMIT = 48 * 1024 * 1024

F32 = jnp.float32
BF16 = jnp.bfloat16
I32 = jnp.int32


def _cp(*sem):
    return pltpu.CompilerParams(dimension_semantics=sem, vmem_limit_bytes=VMEM_LIMIT)


def _dot(a, b):
    return jnp.dot(a, b, preferred_element_type=F32)


def _dot_nt(a, b):
    return lax.dot_general(a, b, (((1,), (1,)), ((), ())), preferred_element_type=F32)


def _dot_exact(a, b):
    return jnp.dot(a, b, preferred_element_type=F32, precision=lax.Precision.HIGHEST)


def _iota(shape, axis):
    return lax.broadcasted_iota(I32, shape, axis)


def _layer_norm(z, g, b):
    mu = jnp.mean(z, axis=-1, keepdims=True)
    d = z - mu
    var = jnp.mean(d * d, axis=-1, keepdims=True)
    return d * lax.rsqrt(var + LN_EPS) * g + b


def _silu(x):
    return x * jax.nn.sigmoid(x)


def _softmax_rows(s, valid):
    s = jnp.where(valid, s, NEG)
    m = jnp.max(s, axis=-1, keepdims=True)
    e = jnp.where(valid, jnp.exp(s - m), 0.0)
    l = jnp.sum(e, axis=-1, keepdims=True)
    return e / jnp.where(l > 0.0, l, 1.0)


def _rank_before(x, n_cols, lane):
    rank = jnp.zeros(x.shape, F32)
    for c in range(n_cols):
        col = x[:, c:c + 1]
        rank = rank + jnp.where(col > x, 1.0, jnp.where(col == x, jnp.where(lane > c, 1.0, 0.0), 0.0))
    return rank


def _proj_kernel(x_ref, w_ref, o_ref):
    o_ref[...] = _dot(x_ref[...].astype(BF16), w_ref[...])


def _proj_rope_kernel(x_ref, w_ref, a_ref, b_ref, mask_ref, o_ref, *, half, tn):
    acc = _dot(x_ref[...].astype(BF16), w_ref[...])
    lane = _iota(acc.shape, 1)
    first = (lane % (2 * half)) < half
    partner = jnp.where(first, pltpu.roll(acc, tn - half, 1), pltpu.roll(acc, half, 1))
    rotary = mask_ref[...] > 0.5
    a = jnp.where(rotary, jnp.concatenate([a_ref[...]] * (tn // LANES), axis=1), 1.0)
    b = jnp.where(rotary, jnp.concatenate([b_ref[...]] * (tn // LANES), axis=1), 0.0)
    o_ref[...] = acc * a + partner * b


def _col_tile(n):
    return next(t for t in (512, 384, 256, 128) if n % t == 0)


def _proj(x, w, tabs=None, half=0):
    M, K = x.shape
    N = w.shape[1]
    tm = min(M, 512)
    tn = _col_tile(N)
    grid = (M // tm, N // tn)
    x_spec = pl.BlockSpec((tm, K), lambda i, j: (i, 0))
    w_spec = pl.BlockSpec((K, tn), lambda i, j: (0, j))
    o_spec = pl.BlockSpec((tm, tn), lambda i, j: (i, j))
    out_shape = jax.ShapeDtypeStruct((M, N), F32)
    if tabs is None:
        return pl.pallas_call(_proj_kernel, out_shape=out_shape, grid=grid, in_specs=[x_spec, w_spec],
                              out_specs=o_spec, compiler_params=_cp("parallel", "parallel"), name="proj")(x, w)
    a, bm, mask = tabs
    nrb = a.shape[0] // tm
    t_spec = pl.BlockSpec((tm, LANES), lambda i, j: (i % nrb, 0))
    m_spec = pl.BlockSpec((1, tn), lambda i, j: (0, j))
    return pl.pallas_call(functools.partial(_proj_rope_kernel, half=half, tn=tn), out_shape=out_shape, grid=grid,
                          in_specs=[x_spec, w_spec, t_spec, t_spec, m_spec], out_specs=o_spec,
                          compiler_params=_cp("parallel", "parallel"), name="proj_rope")(x, w, a, bm, mask)


def _proj_ln_kernel(x_ref, w_ref, r_ref, g_ref, b_ref, o_ref):
    y = _dot(x_ref[...].astype(BF16), w_ref[...])
    o_ref[...] = _layer_norm(ALPHA * r_ref[...] + y, g_ref[...], b_ref[...])


def _proj_ln(x, w, res, g, b):
    M, K = x.shape
    N = w.shape[1]
    tm = min(M, 512)
    row = lambda i: (i, 0)
    fix = lambda i: (0, 0)
    return pl.pallas_call(
        _proj_ln_kernel, out_shape=jax.ShapeDtypeStruct((M, N), F32), grid=(M // tm,),
        in_specs=[pl.BlockSpec((tm, K), row), pl.BlockSpec((K, N), fix), pl.BlockSpec((tm, N), row),
                  pl.BlockSpec((1, N), fix), pl.BlockSpec((1, N), fix)],
        out_specs=pl.BlockSpec((tm, N), row), compiler_params=_cp("parallel"), name="proj_ln",
    )(x, w, res, g.reshape(1, N), b.reshape(1, N))


def _rope_unit(pos, half):
    inv = jnp.power(ROPE_THETA, -jnp.arange(half, dtype=F32) / half)
    ang = pos.astype(F32)[:, None] * inv
    cos, sin = jnp.cos(ang), jnp.sin(ang)
    return jnp.concatenate([cos, cos], axis=1), jnp.concatenate([-sin, sin], axis=1)


def _unit_tables(pos, half):
    ua, ub = _rope_unit(pos, half)
    reps = LANES // (2 * half)
    return jnp.tile(ua, (1, reps)), jnp.tile(ub, (1, reps))


def _column_mask(layout):
    return jnp.concatenate([jnp.full((1, n), v, F32) for v, n in layout], axis=1)


def _tables(pos, half, layout):
    ua, ub = _rope_unit(pos, half)
    T = pos.shape[0]
    a_parts, b_parts = [], []
    for kind, n in layout:
        if kind == 'r':
            a_parts.append(jnp.tile(ua, (1, n)))
            b_parts.append(jnp.tile(ub, (1, n)))
        else:
            a_parts.append(jnp.ones((T, n), F32))
            b_parts.append(jnp.zeros((T, n), F32))
    return jnp.concatenate(a_parts, axis=1), jnp.concatenate(b_parts, axis=1)


def _add_bias(s, bias, heads):
    if heads == 1:
        return s + bias
    tq = s.shape[1] // heads
    return jnp.concatenate([s[:, h * tq:(h + 1) * tq] + bias for h in range(heads)], axis=1)


def _weighted_values(vt_ref, first, e):
    acc = _dot(vt_ref[first], e[:LANES])
    for j in range(1, e.shape[0] // LANES):
        acc = acc + _dot(vt_ref[first + j], e[j * LANES:(j + 1) * LANES])
    return acc


def _flash(qs, k_ref, k_cols, vt_ref, vt_base, m_sc, acc_sc, lo, hi, tk, bias_fn, heads):
    m_sc[...] = jnp.full(m_sc.shape, NEG, F32)
    acc_sc[...] = jnp.zeros(acc_sc.shape, F32)
    sub = tk // LANES

    def body(kt, carry):
        k0 = pl.multiple_of(kt * tk, tk)
        s = _add_bias(_dot_nt(k_ref[pl.ds(k0, tk), k_cols], qs), bias_fn(kt, k0), heads)
        m_old = m_sc[...]
        m_new = jnp.maximum(m_old, jnp.max(s, axis=0, keepdims=True))
        e = jnp.exp(s - m_new).astype(BF16)
        acc_sc[...] = jnp.exp(m_old - m_new) * acc_sc[...] + _weighted_values(vt_ref, vt_base + kt * sub, e)
        m_sc[...] = m_new
        return carry

    lax.fori_loop(lo, hi, body, 0)
    return acc_sc[...]


def _transpose_rows(x):
    eye = jnp.where(_iota((LANES, LANES), 0) == _iota((LANES, LANES), 1), 1.0, 0.0).astype(BF16)
    return _dot_nt(eye, x)


def _store_values_t(vt_ref, base, vt):
    n = vt.shape[1]
    full = jnp.concatenate([jnp.ones((ONES_ROWS, n), F32), vt], axis=0).astype(BF16)
    for j in range(n // LANES):
        vt_ref[base + j] = full[:, j * LANES:(j + 1) * LANES]


def _write_heads_t(o_ref, heads):
    tq = heads[0].shape[1]
    for p in range(len(heads) // 2):
        pair = jnp.concatenate([heads[2 * p], heads[2 * p + 1]], axis=0)
        for c in range(tq // LANES):
            o_ref[c * LANES:(c + 1) * LANES, p * LANES:(p + 1) * LANES] = pair[:, c * LANES:(c + 1) * LANES].T


def _stack_heads(ref, n_heads, dh, scale):
    parts = []
    for r in range(n_heads):
        pair = ref[:, (r // 2) * 2 * dh:(r // 2 + 1) * 2 * dh] * scale
        lane = _iota(pair.shape, 1)
        own = (lane < dh) if r % 2 == 0 else (lane >= dh)
        parts.append(jnp.where(own, pair, 0.0))
    return jnp.concatenate(parts, axis=0).astype(BF16)


def _dup_keys(kv, dh):
    return jnp.where(_iota(kv.shape, 1) < dh, kv, pltpu.roll(kv, dh, 1))


def _cmp_lohi_kernel(*refs, n_pages, rows_per_page, feature_major):
    G = NSA_KV_HEADS
    if feature_major:
        refs, x_sc = refs[1:-1], refs[-1]
    n_in = n_pages if feature_major else G
    row_refs, w_ref, o_ref = refs[:n_in], refs[n_in], refs[n_in + 1]
    cpp = rows_per_page // CMP_STRIDE
    nh = w_ref.shape[2] // 2

    def chunk_rows(g, r):
        if feature_major:
            return x_sc[pl.ds(r, n_pages * cpp, stride=CMP_STRIDE), :]
        return row_refs[g][0, pl.ds(r, cpp, stride=CMP_STRIDE), :]

    for g in range(G):
        if feature_major:
            for p in range(n_pages):
                x_sc[p * rows_per_page:(p + 1) * rows_per_page, :] = row_refs[p][0, g * LANES:(g + 1) * LANES, :].T
        acc = jnp.zeros((n_pages * cpp, 2 * nh), F32)
        for r2 in range(CMP_STRIDE // 2):
            xr = jnp.concatenate([chunk_rows(g, 2 * r2), chunk_rows(g, 2 * r2 + 1)], axis=1)
            acc = acc + _dot(xr.astype(BF16), w_ref[r2])
        o_ref[0, :, g * nh:(g + 1) * nh] = acc[:, :nh]
        o_ref[0, :, (G + g) * nh:(G + g + 1) * nh] = acc[:, nh:]


def _cmp_lohi(rows, w_lohi, col_block=0):
    B, L = rows.shape[0], rows.shape[1]
    G = NSA_KV_HEADS
    nout = G * w_lohi.shape[2]
    row_specs = [pl.BlockSpec((1, L, LANES), functools.partial(lambda b, g: (b, 0, col_block + g), g=g))
                 for g in range(G)]
    return pl.pallas_call(
        functools.partial(_cmp_lohi_kernel, n_pages=1, rows_per_page=L, feature_major=False),
        out_shape=jax.ShapeDtypeStruct((B, L // CMP_STRIDE, nout), F32), grid=(B,),
        in_specs=row_specs + [pl.BlockSpec(w_lohi.shape, lambda b: (0, 0, 0))],
        out_specs=pl.BlockSpec((1, L // CMP_STRIDE, nout), lambda b: (b, 0, 0)),
        compiler_params=_cp("parallel"), name="cmp_lohi",
    )(*([rows] * G), w_lohi)


def _cmp_lohi_paged(cache_t, pages, w_lohi, per_step):
    B, n_pages = pages.shape
    G = NSA_KV_HEADS
    nout = G * w_lohi.shape[2]
    cpp = PAGE_SIZE // CMP_STRIDE
    page_specs = [pl.BlockSpec((1, G * LANES, PAGE_SIZE), functools.partial(
        lambda b, s, pt, p: (pt[b, s * per_step + p], 0, 0), p=p)) for p in range(per_step)]
    grid_spec = pltpu.PrefetchScalarGridSpec(
        num_scalar_prefetch=1, grid=(B, n_pages // per_step),
        in_specs=page_specs + [pl.BlockSpec(w_lohi.shape, lambda b, s, pt: (0, 0, 0))],
        out_specs=pl.BlockSpec((1, per_step * cpp, nout), lambda b, s, pt: (b, s, 0)),
        scratch_shapes=[pltpu.VMEM((per_step * PAGE_SIZE, LANES), F32)])
    return pl.pallas_call(
        functools.partial(_cmp_lohi_kernel, n_pages=per_step, rows_per_page=PAGE_SIZE, feature_major=True),
        out_shape=jax.ShapeDtypeStruct((B, n_pages * cpp, nout), F32), grid_spec=grid_spec,
        compiler_params=_cp("parallel", "parallel"), name="cmp_lohi_paged",
    )(pages, *([cache_t] * per_step), w_lohi)


def _cmp_combine_kernel(lohi_ref, pe_ref, b1_ref, w2_ref, b2_ref, o_ref):
    nch = lohi_ref.shape[1]
    nh = lohi_ref.shape[2] // 2
    lo = lohi_ref[0, :, :nh]
    hi_next = pltpu.roll(lohi_ref[0, :, nh:], nch - 1, 0)
    pe = pe_ref[0, 0:1, :nh] + pe_ref[0, 1:2, nh:]
    hid = jax.nn.gelu(lo + hi_next + pe + b1_ref[...])
    o_ref[0] = _dot(hid.astype(BF16), w2_ref[...]) + b2_ref[...]


def _cmp_combine(lohi, pe_lohi, b1, w2, b2):
    B, nch, n2 = lohi.shape
    nh = n2 // 2
    nout = w2.shape[1]
    fix2 = lambda b: (0, 0)
    return pl.pallas_call(
        _cmp_combine_kernel, out_shape=jax.ShapeDtypeStruct((B, nch, nout), F32), grid=(B,),
        in_specs=[pl.BlockSpec((1, nch, n2), lambda b: (b, 0, 0)), pl.BlockSpec((1, 2, n2), lambda b: (0, 0, 0)),
                  pl.BlockSpec((1, nh), fix2), pl.BlockSpec((nh, nout), fix2), pl.BlockSpec((1, nout), fix2)],
        out_specs=pl.BlockSpec((1, nch, nout), lambda b: (b, 0, 0)),
        compiler_params=_cp("parallel"), name="cmp_combine",
    )(lohi, pe_lohi, b1, w2, b2)


def _sel_importance(p_sum, n_lanes):
    nc = p_sum.shape[1]
    c = _iota((nc, n_lanes), 0)
    j = _iota((nc, n_lanes), 1)
    a = jnp.where(c >= SEL_RATIO * j - 1, jnp.where(c <= SEL_RATIO * j + SEL_RATIO - 1, 1.0, 0.0), 0.0)
    return _dot_exact(p_sum, a)


def _attn_nsa_kernel(q_ref, qr_ref, cmp_ref, sel_ref, win_ref, gl_ref, o_ref,
                     k2s, vts, k2w, vtw, bias_ref, m_sc, acc_sc, *, T, tq, tk, n_sel, n_top):
    i = pl.program_id(2)
    t0 = i * tq
    R = NSA_HEADS // NSA_KV_HEADS
    dh = NSA_HEAD_DIM
    scale = dh ** -0.5
    wk = min(T, WINDOW + tq)

    @pl.when(i == 0)
    def _():
        for src, k_sc, vt_sc in ((sel_ref, k2s, vts), (win_ref, k2w, vtw)):
            kv = src[...]
            k_sc[...] = _dup_keys(kv, dh).astype(BF16)
            _store_values_t(vt_sc, 0, _transpose_rows(kv.astype(BF16))[dh:])

    tpos = t0 + _iota((1, tq), 1)
    cmp_tok = cmp_ref[0]
    nc = cmp_tok.shape[0]
    s = _dot_nt(_dup_keys(cmp_tok, dh).astype(BF16), _stack_heads(q_ref, R, dh, scale))
    cvalid = (_iota((nc, 1), 0) * CMP_STRIDE + CMP_BLOCK - 1) <= tpos
    probs = []
    for h in range(R):
        sh = jnp.where(cvalid, s[:, h * tq:(h + 1) * tq], NEG)
        e = jnp.where(cvalid, jnp.exp(sh - jnp.max(sh, axis=0, keepdims=True)), 0.0)
        l = jnp.sum(e, axis=0, keepdims=True)
        probs.append(e / jnp.where(l > 0.0, l, 1.0))
    p_sum = probs[0]
    for h in range(1, R):
        p_sum = p_sum + probs[h]
    vct = _transpose_rows(cmp_tok.astype(BF16))[dh:].astype(BF16)
    o_cmp = _dot(vct, jnp.concatenate(probs, axis=1).astype(BF16))

    ns = bias_ref.shape[0] // SEL_BLOCK
    blk = _iota((ns, tq), 0)
    cur = tpos // SEL_BLOCK
    c_idx = _iota((ns, nc), 1)
    lo_c = SEL_RATIO * _iota((ns, nc), 0) - 1
    spread = jnp.where(c_idx >= lo_c, jnp.where(c_idx <= lo_c + SEL_RATIO, 1.0, 0.0), 0.0)
    imp = _dot_exact(spread, p_sum)
    forced = jnp.where(blk == 0, 1.0, jnp.where(blk == cur, 1.0, jnp.where(blk == cur - 1, 1.0, 0.0)))
    imp = jnp.where(blk > cur, NEG, jnp.where(forced > 0.5, FORCE, imp))
    rank = jnp.zeros(imp.shape, F32)
    for c in range(n_sel):
        row = imp[c:c + 1, :]
        rank = rank + jnp.where(row > imp, 1.0, jnp.where(row == imp, jnp.where(blk > c, 1.0, 0.0), 0.0))
    chosen = jnp.where(blk < n_sel, jnp.where(rank < n_top, 1.0, 0.0), 0.0)
    for j in range(n_sel):
        picked = jnp.broadcast_to(chosen[j:j + 1, :], (SEL_BLOCK, tq))
        kpos = j * SEL_BLOCK + _iota((SEL_BLOCK, 1), 0)
        bias_ref[j * SEL_BLOCK:(j + 1) * SEL_BLOCK, :] = jnp.where(
            picked > 0.5, jnp.where(kpos <= tpos, 0.0, NEG), NEG)

    hi = (t0 + tq + tk - 1) // tk
    qrs = _stack_heads(qr_ref, R, dh, scale)
    acc = _flash(qrs, k2s, slice(None), vts, 0, m_sc, acc_sc, 0, hi, tk,
                 lambda kt, k0: bias_ref[pl.ds(k0, tk), :], R)
    o_sel = acc[ONES_ROWS:] / acc[0:1]
    kw0 = pl.multiple_of(jnp.clip(t0 - WINDOW, 0, T - wk), LANES)
    d = tpos - (kw0 + _iota((wk, 1), 0))
    win_bias = jnp.where(d >= 0, jnp.where(d <= WINDOW, 0.0, NEG), NEG)
    s = _add_bias(_dot_nt(k2w[pl.ds(kw0, wk), :], qrs), win_bias, R)
    e = jnp.exp(s - jnp.max(s, axis=0, keepdims=True)).astype(BF16)
    acc = _weighted_values(vtw, kw0 // LANES, e)
    o_win = acc[ONES_ROWS:] / acc[0:1]
    gate = jax.nn.sigmoid(gl_ref[...].T)
    heads = []
    for h in range(R):
        hs = slice(h * tq, (h + 1) * tq)
        heads.append(gate[h:h + 1] * o_cmp[:, hs] + gate[R + h:R + h + 1] * o_sel[:, hs]
                     + gate[2 * R + h:2 * R + h + 1] * o_win[:, hs])
    _write_heads_t(o_ref, heads)


def _attn_nsa(proj, cmp_tok, B, T, tk=None):
    G = NSA_KV_HEADS
    R = NSA_HEADS // G
    tq = 2 * LANES
    tk = tk or min(T, 512)
    nq = T // tq
    n_sel = -(-T // SEL_BLOCK)
    n_top = min(SEL_TOP, n_sel)
    qw = NSA_HEADS * NSA_HEAD_DIM // G
    kv0 = 2 * NSA_HEADS * NSA_HEAD_DIM // LANES
    gl0 = kv0 + 3 * G
    nch = cmp_tok.shape[1]
    dh = NSA_HEAD_DIM
    kern = functools.partial(_attn_nsa_kernel, T=T, tq=tq, tk=tk, n_sel=n_sel, n_top=n_top)
    return pl.pallas_call(
        kern, out_shape=jax.ShapeDtypeStruct((B * T, NSA_HEADS * dh), F32), grid=(B, G, nq),
        in_specs=[pl.BlockSpec((tq, qw), lambda b, g, i: (b * nq + i, g)),
                  pl.BlockSpec((tq, qw), lambda b, g, i: (b * nq + i, G + g)),
                  pl.BlockSpec((1, nch, 2 * dh), lambda b, g, i: (b, 0, g)),
                  pl.BlockSpec((T, 2 * dh), lambda b, g, i: (b, kv0 + G + g)),
                  pl.BlockSpec((T, 2 * dh), lambda b, g, i: (b, kv0 + 2 * G + g)),
                  pl.BlockSpec((tq, LANES), lambda b, g, i: (b * nq + i, gl0 + g))],
        out_specs=pl.BlockSpec((tq, qw), lambda b, g, i: (b * nq + i, g)),
        scratch_shapes=[pltpu.VMEM((T, 2 * dh), BF16), pltpu.VMEM((T // LANES, ONES_ROWS + dh, LANES), BF16)] * 2 + [
            pltpu.VMEM((T, tq), F32), pltpu.VMEM((1, R * tq), F32), pltpu.VMEM((ONES_ROWS + dh, R * tq), F32)],
        compiler_params=_cp("parallel", "parallel", "arbitrary"), name="attn_nsa",
    )(proj, proj, cmp_tok, proj, proj, proj)


def _nsa_dec_cmp_kernel(q_ref, cmp_ref, o_ref, imp_ref, *, t, n_sel):
    G = NSA_KV_HEADS
    R = NSA_HEADS // G
    dh = NSA_HEAD_DIM
    nc = cmp_ref.shape[1]
    nl = imp_ref.shape[2]
    cvalid = (_iota((1, nc), 1) * CMP_STRIDE + CMP_BLOCK - 1) <= t
    blk = _iota((1, nl), 1)
    cur = t // SEL_BLOCK
    for g in range(G):
        kc = cmp_ref[0, :, 2 * g * dh:(2 * g + 1) * dh].astype(BF16)
        vc = cmp_ref[0, :, (2 * g + 1) * dh:(2 * g + 2) * dh].astype(BF16)
        qg = (q_ref[0, g * R:(g + 1) * R, :] * dh ** -0.5).astype(BF16)
        p = _softmax_rows(_dot_nt(qg, kc), cvalid)
        o_ref[0, g * R:(g + 1) * R, :] = _dot(p.astype(BF16), vc)
        imp = _sel_importance(jnp.sum(p, axis=0, keepdims=True), nl)
        forced = jnp.where(blk == 0, 1.0, jnp.where(blk == cur, 1.0, jnp.where(blk == cur - 1, 1.0, 0.0)))
        imp = jnp.where(blk > cur, NEG, jnp.where(forced > 0.5, FORCE, imp))
        imp_ref[0, g:g + 1, :] = jnp.where(blk < n_sel, imp, -jnp.inf)


def _nsa_dec_cmp(q3, cmp_tok, t, n_sel):
    B = q3.shape[0]
    nl = -(-n_sel // LANES) * LANES
    nc = cmp_tok.shape[1]
    blk3 = lambda b: (b, 0, 0)
    return pl.pallas_call(
        functools.partial(_nsa_dec_cmp_kernel, t=t, n_sel=n_sel),
        out_shape=(jax.ShapeDtypeStruct(q3.shape, F32), jax.ShapeDtypeStruct((B, NSA_KV_HEADS, nl), F32)),
        grid=(B,),
        in_specs=[pl.BlockSpec((1,) + q3.shape[1:], blk3), pl.BlockSpec((1, nc, cmp_tok.shape[2]), blk3)],
        out_specs=(pl.BlockSpec((1,) + q3.shape[1:], blk3), pl.BlockSpec((1, NSA_KV_HEADS, nl), blk3)),
        compiler_params=_cp("parallel"), name="nsa_dec_cmp",
    )(q3, cmp_tok)


def _topk_idx_kernel(x_ref, o_ref, *, k):
    x = x_ref[...]
    lane = _iota(x.shape, 1)
    out_lane = _iota(o_ref.shape, 1)
    out = jnp.zeros(o_ref.shape, I32)
    big = x.shape[1]
    for n in range(k):
        m = jnp.max(x, axis=-1, keepdims=True)
        idx = jnp.min(jnp.where(x == m, lane, big), axis=-1, keepdims=True)
        out = jnp.where(out_lane == n, idx, out)
        x = jnp.where(lane == idx, -jnp.inf, x)
    o_ref[...] = out


def _topk_idx(x, k):
    rows = x.shape[0]
    return pl.pallas_call(functools.partial(_topk_idx_kernel, k=k),
                          out_shape=jax.ShapeDtypeStruct((rows, LANES), I32), name="topk_idx")(x)


def _nsa_dec_attn_kernel(*refs, n_top, n_past_blocks):
    idx_ref, rb_ref = refs[0], refs[1]
    G = NSA_KV_HEADS
    R = NSA_HEADS // G
    dh = NSA_HEAD_DIM
    blocks = refs[2:2 + G * n_top]
    qr_ref, new_ref, win_ref, ocmp_ref, gl_ref, o_ref = refs[2 + G * n_top:]
    b = pl.program_id(0)
    scale = dh ** -0.5
    per_page = PAGE_SIZE // SEL_BLOCK
    nk = n_top * PAGE_SIZE
    lane = _iota((1, nk), 1)
    for g in range(G):
        qg = qr_ref[0, g * R:(g + 1) * R, :] * scale
        qb = qg.astype(BF16)
        kt = jnp.concatenate([blocks[g * n_top + n][0, :dh, :] for n in range(n_top)], axis=1).astype(BF16)
        vt = jnp.concatenate([blocks[g * n_top + n][0, dh:, :] for n in range(n_top)], axis=1).astype(BF16)
        valid = jnp.zeros((1, nk), F32)
        for n in range(n_top):
            blk = idx_ref[b, g * n_top + n]
            flag = jnp.where(blk < n_past_blocks, 1.0, 0.0)
            in_block = jnp.where((lane % PAGE_SIZE) // SEL_BLOCK == blk % per_page, flag, 0.0)
            valid = jnp.where(lane // PAGE_SIZE == n, in_block, valid)
        valid = valid > 0.5
        k_new = new_ref[0, :, (2 * G + 2 * g) * dh:(2 * G + 2 * g + 1) * dh]
        v_new = new_ref[0, :, (2 * G + 2 * g + 1) * dh:(2 * G + 2 * g + 2) * dh]
        s = jnp.where(valid, _dot(qb, kt), NEG)
        s_new = jnp.sum(qg * k_new, axis=-1, keepdims=True)
        m = jnp.maximum(jnp.max(s, axis=-1, keepdims=True), s_new)
        e = jnp.where(valid, jnp.exp(s - m), 0.0)
        e_new = jnp.exp(s_new - m)
        o_sel = (_dot_nt(e.astype(BF16), vt) + e_new * v_new) / (jnp.sum(e, axis=-1, keepdims=True) + e_new)
        kwin = win_ref[0, 2 * g * dh:(2 * g + 1) * dh, :].astype(BF16)
        vwin = win_ref[0, (2 * g + 1) * dh:(2 * g + 2) * dh, :].astype(BF16)
        k_new = new_ref[0, :, (4 * G + 2 * g) * dh:(4 * G + 2 * g + 1) * dh]
        v_new = new_ref[0, :, (4 * G + 2 * g + 1) * dh:(4 * G + 2 * g + 2) * dh]
        s = _dot(qb, kwin)
        s_new = jnp.sum(qg * k_new, axis=-1, keepdims=True)
        m = jnp.maximum(jnp.max(s, axis=-1, keepdims=True), s_new)
        e = jnp.exp(s - m)
        e_new = jnp.exp(s_new - m)
        o_win = (_dot_nt(e.astype(BF16), vwin) + e_new * v_new) / (jnp.sum(e, axis=-1, keepdims=True) + e_new)
        gate = jax.nn.sigmoid(gl_ref[0, g * R:(g + 1) * R, :])
        o_ref[0, g * R:(g + 1) * R, :] = (gate[:, 0:1] * ocmp_ref[0, g * R:(g + 1) * R, :]
                                          + gate[:, 1:2] * o_sel + gate[:, 2:3] * o_win)


def _nsa_dec_attn(idx, sel_pages, cache_t, qr3, new_kv, win_buf, win_index0, o_cmp, gl3, n_past_blocks):
    B = qr3.shape[0]
    G = NSA_KV_HEADS
    n_top = idx.shape[1] // G
    dh = NSA_HEAD_DIM
    blk3 = lambda b, idx, rb: (b, 0, 0)
    block_specs = [pl.BlockSpec((1, 2 * dh, PAGE_SIZE), functools.partial(
        lambda b, idx, rb, n, g: (rb[b, n], g, 0), n=g * n_top + n, g=g)) for g in range(G) for n in range(n_top)]
    grid_spec = pltpu.PrefetchScalarGridSpec(
        num_scalar_prefetch=2, grid=(B,),
        in_specs=block_specs + [
            pl.BlockSpec((1,) + qr3.shape[1:], blk3), pl.BlockSpec((1,) + new_kv.shape[1:], blk3),
            pl.BlockSpec((1,) + win_buf.shape[1:], lambda b, idx, rb: (win_index0 + b, 0, 0)),
            pl.BlockSpec((1,) + o_cmp.shape[1:], blk3), pl.BlockSpec((1,) + gl3.shape[1:], blk3)],
        out_specs=pl.BlockSpec((1,) + qr3.shape[1:], blk3))
    return pl.pallas_call(
        functools.partial(_nsa_dec_attn_kernel, n_top=n_top, n_past_blocks=n_past_blocks),
        out_shape=jax.ShapeDtypeStruct(qr3.shape, F32), grid_spec=grid_spec,
        compiler_params=_cp("parallel"), name="nsa_dec_attn",
    )(idx, sel_pages, *([cache_t] * (G * n_top)), qr3, new_kv, win_buf, o_cmp, gl3)


def _mla_post_kernel(dn_ref, gq_ref, gkv_ref, a_ref, b_ref, cq_ref, ckr_ref):
    x = dn_ref[...]
    cq = x[:, :Q_LORA]
    cq_ref[...] = cq * lax.rsqrt(jnp.mean(cq * cq, axis=-1, keepdims=True) + RMS_EPS) * gq_ref[...]
    ckv = x[:, Q_LORA:Q_LORA + KV_LORA]
    ckv = ckv * lax.rsqrt(jnp.mean(ckv * ckv, axis=-1, keepdims=True) + RMS_EPS) * gkv_ref[...]
    kr = x[:, Q_LORA + KV_LORA:]
    half = QK_ROPE // 2
    first = (_iota(kr.shape, 1) % QK_ROPE) < half
    partner = jnp.where(first, pltpu.roll(kr, LANES - half, 1), pltpu.roll(kr, half, 1))
    ckr_ref[...] = jnp.concatenate([ckv, kr * a_ref[...] + partner * b_ref[...]], axis=1)


def _mla_post(dn, g_q, g_kv, tabs):
    M, N = dn.shape
    tm = min(M, 512)
    nrb = tabs[0].shape[0] // tm
    row = lambda i: (i, 0)
    fix = lambda i: (0, 0)
    tab = pl.BlockSpec((tm, LANES), lambda i: (i % nrb, 0))
    return pl.pallas_call(
        _mla_post_kernel,
        out_shape=(jax.ShapeDtypeStruct((M, Q_LORA), F32), jax.ShapeDtypeStruct((M, KV_LORA + LANES), F32)),
        grid=(M // tm,),
        in_specs=[pl.BlockSpec((tm, N), row), pl.BlockSpec((1, Q_LORA), fix), pl.BlockSpec((1, KV_LORA), fix),
                  tab, tab],
        out_specs=(pl.BlockSpec((tm, Q_LORA), row), pl.BlockSpec((tm, KV_LORA + LANES), row)),
        compiler_params=_cp("parallel"), name="mla_post",
    )(dn, g_q.reshape(1, -1), g_kv.reshape(1, -1), *tabs)


def _attn_mla_kernel(q_ref, k_ref, v_ref, o_ref, k_sc, v_sc, m_sc, acc_sc, *, tq, tk):
    i = pl.program_id(2)
    t0 = i * tq
    scale = (QK_NOPE + QK_ROPE) ** -0.5

    n_vt = k_ref.shape[0] // LANES

    @pl.when(i == 0)
    def _():
        k_sc[...] = k_ref[...].astype(BF16)
        vt = _transpose_rows(v_ref[...].astype(BF16))
        for hh in range(2):
            _store_values_t(v_sc, hh * n_vt, vt[hh * V_DIM:(hh + 1) * V_DIM])

    tpos = t0 + _iota((1, tq), 1)
    hi = (t0 + tq + tk - 1) // tk
    causal = lambda kt, k0: jnp.where(k0 + _iota((tk, 1), 0) <= tpos, 0.0, NEG)
    heads = []
    for hh in range(2):
        qh = (q_ref[:, hh * MLA_SLOT:(hh + 1) * MLA_SLOT] * scale).astype(BF16)
        acc = _flash(qh, k_sc, slice(hh * MLA_SLOT, (hh + 1) * MLA_SLOT), v_sc, hh * n_vt, m_sc, acc_sc, 0, hi, tk,
                     causal, 1)
        heads.append(acc[ONES_ROWS:] / acc[0:1])
    _write_heads_t(o_ref, heads)


def _attn_mla(q_ext, kv_ext, B, T, tk=None):
    H = MLA_HEADS
    tq = min(T, 1024)
    tk = tk or min(T, 512)
    nq = T // tq
    v0 = H * MLA_SLOT // LANES
    return pl.pallas_call(
        functools.partial(_attn_mla_kernel, tq=tq, tk=tk),
        out_shape=jax.ShapeDtypeStruct((B * T, H * V_DIM), F32), grid=(B, H // 2, nq),
        in_specs=[pl.BlockSpec((tq, 2 * MLA_SLOT), lambda b, h, i: (b * nq + i, h)),
                  pl.BlockSpec((T, 2 * MLA_SLOT), lambda b, h, i: (b, h)),
                  pl.BlockSpec((T, 2 * V_DIM), lambda b, h, i: (b, v0 + h))],
        out_specs=pl.BlockSpec((tq, 2 * V_DIM), lambda b, h, i: (b * nq + i, h)),
        scratch_shapes=[pltpu.VMEM((T, 2 * MLA_SLOT), BF16),
                        pltpu.VMEM((2 * (T // LANES), ONES_ROWS + V_DIM, LANES), BF16),
                        pltpu.VMEM((1, tq), F32), pltpu.VMEM((ONES_ROWS + V_DIM, tq), F32)],
        compiler_params=_cp("parallel", "parallel", "arbitrary"), name="attn_mla",
    )(q_ext, kv_ext, kv_ext)


def _mla_decode_kernel(*refs, per_step):
    pt_ref = refs[0]
    ckv_pages = refs[1:1 + per_step]
    kr_pages = refs[1 + per_step:1 + 2 * per_step]
    ql_ref, qr_ref, cnew_ref, rnew_ref, o_ref, m_sc, l_sc, acc_sc = refs[1 + 2 * per_step:]
    s_idx = pl.program_id(1)
    scale = (QK_NOPE + QK_ROPE) ** -0.5
    ql = ql_ref[0] * scale
    qr = qr_ref[0] * scale

    @pl.when(s_idx == 0)
    def _():
        c_new = cnew_ref[0]
        s_new = (jnp.sum(ql * c_new, axis=-1, keepdims=True) + jnp.sum(qr * rnew_ref[0], axis=-1, keepdims=True))
        m_sc[...] = s_new
        l_sc[...] = jnp.ones(l_sc.shape, F32)
        acc_sc[...] = jnp.broadcast_to(c_new, acc_sc.shape)

    qlb = ql.astype(BF16)
    qrb = qr.astype(BF16)
    cs = [ref[0].astype(BF16) for ref in ckv_pages]
    s = jnp.concatenate([_dot_nt(qlb, c) + _dot(qrb, r[0].astype(BF16)) for c, r in zip(cs, kr_pages)], axis=1)
    m_old = m_sc[...]
    m_new = jnp.maximum(m_old, jnp.max(s, axis=-1, keepdims=True))
    alpha = jnp.exp(m_old - m_new)
    e = jnp.exp(s - m_new).astype(BF16)
    pv = _dot(e[:, :PAGE_SIZE], cs[0])
    for p in range(1, per_step):
        pv = pv + _dot(e[:, p * PAGE_SIZE:(p + 1) * PAGE_SIZE], cs[p])
    l_sc[...] = alpha * l_sc[...] + jnp.sum(e.astype(F32), axis=-1, keepdims=True)
    acc_sc[...] = alpha * acc_sc[...] + pv
    m_sc[...] = m_new

    @pl.when(s_idx == pl.num_programs(1) - 1)
    def _():
        o_ref[0] = acc_sc[...] / l_sc[...]


def _mla_decode(pages, cache_ckv, cache_kr, q_lat, q_rope, c_new, r_new, per_step):
    B, n_pages = pages.shape
    H = MLA_HEADS
    blk3 = lambda b, s, pt: (b, 0, 0)
    page_map = lambda p: functools.partial(lambda b, s, pt, p: (pt[b, s * per_step + p], 0, 0), p=p)
    grid_spec = pltpu.PrefetchScalarGridSpec(
        num_scalar_prefetch=1, grid=(B, n_pages // per_step),
        in_specs=([pl.BlockSpec((1, PAGE_SIZE, KV_LORA), page_map(p)) for p in range(per_step)]
                  + [pl.BlockSpec((1, QK_ROPE, PAGE_SIZE), page_map(p)) for p in range(per_step)]
                  + [pl.BlockSpec((1, H, KV_LORA), blk3), pl.BlockSpec((1, H, QK_ROPE), blk3),
                     pl.BlockSpec((1, 1, KV_LORA), blk3), pl.BlockSpec((1, 1, QK_ROPE), blk3)]),
        out_specs=pl.BlockSpec((1, H, KV_LORA), blk3),
        scratch_shapes=[pltpu.VMEM((H, 1), F32), pltpu.VMEM((H, 1), F32), pltpu.VMEM((H, KV_LORA), F32)])
    return pl.pallas_call(
        functools.partial(_mla_decode_kernel, per_step=per_step),
        out_shape=jax.ShapeDtypeStruct((B, H, KV_LORA), F32), grid_spec=grid_spec,
        compiler_params=_cp("parallel", "arbitrary"), name="mla_decode",
    )(pages, *([cache_ckv] * per_step), *([cache_kr] * per_step), q_lat, q_rope, c_new, r_new)


def _attn_moba_kernel(q_ref, kv_ref, o_ref, k2, vt_sc, mean_sc, mask_sc, m_sc, acc_sc, *, T, tq, tk, nb, n_top):
    i = pl.program_id(2)
    t0 = i * tq
    R = MOBA_HEADS // MOBA_KV_HEADS
    dh = MOBA_HEAD_DIM
    L = R * tq
    nbp = mean_sc.shape[0]
    bpt = tk // MOBA_BLOCK

    @pl.when(i == 0)
    def _():
        kv = kv_ref[...]
        k2[...] = _dup_keys(kv, dh).astype(BF16)
        _store_values_t(vt_sc, 0, _transpose_rows(kv.astype(BF16))[dh:])
        mean_sc[...] = jnp.zeros(mean_sc.shape, F32)
        for j in range(T // MOBA_BLOCK):
            blk_rows = _dup_keys(kv_ref[j * MOBA_BLOCK:(j + 1) * MOBA_BLOCK, :], dh)
            mean_sc[j:j + 1, :] = jnp.sum(blk_rows, axis=0, keepdims=True) / MOBA_BLOCK

    tpos = t0 + _iota((1, L), 1) % tq
    cur = tpos // MOBA_BLOCK
    jb = _iota((nbp, L), 0)
    qs = _stack_heads(q_ref, R, dh, dh ** -0.5)
    block_mask = jnp.where(jb == cur, 1.0, 0.0)
    if n_top > 0:
        gm = jnp.where(jb < cur, _dot_nt(mean_sc[...].astype(BF16), qs), NEG)
        rank = jnp.zeros(gm.shape, F32)
        for c in range(nb):
            row = gm[c:c + 1, :]
            rank = rank + jnp.where(row > gm, 1.0, jnp.where(row == gm, jnp.where(jb > c, 1.0, 0.0), 0.0))
        block_mask = jnp.where(jb < cur, jnp.where(rank < n_top, 1.0, 0.0), block_mask)
    mask_sc[...] = block_mask

    def bias(kt, k0):
        parts = []
        for j in range(bpt):
            picked = mask_sc[pl.ds(kt * bpt + j, 1), :]
            kpos = k0 + j * MOBA_BLOCK + _iota((MOBA_BLOCK, 1), 0)
            parts.append(jnp.where(picked > 0.5, jnp.where(kpos <= tpos, 0.0, NEG), NEG))
        return parts[0] if bpt == 1 else jnp.concatenate(parts, axis=0)

    hi = (t0 + tq + tk - 1) // tk
    acc = _flash(qs, k2, slice(None), vt_sc, 0, m_sc, acc_sc, 0, hi, tk, bias, 1)
    o = acc[ONES_ROWS:] / acc[0:1]
    _write_heads_t(o_ref, [o[:, r * tq:(r + 1) * tq] for r in range(R)])


def _attn_moba(proj, B, T, tk=None):
    KH = MOBA_KV_HEADS
    R = MOBA_HEADS // KH
    dh = MOBA_HEAD_DIM
    tq = min(T, 512)
    tk = tk or min(T, 512)
    nq = T // tq
    nb = (T - 1) // MOBA_BLOCK
    n_top = min(MOBA_TOP, nb)
    qw = MOBA_HEADS * dh // KH
    kv0 = MOBA_HEADS * dh // LANES
    nbp = max(T // MOBA_BLOCK, 8)
    return pl.pallas_call(
        functools.partial(_attn_moba_kernel, T=T, tq=tq, tk=tk, nb=nb, n_top=n_top),
        out_shape=jax.ShapeDtypeStruct((B * T, MOBA_HEADS * dh), F32), grid=(B, KH, nq),
        in_specs=[pl.BlockSpec((tq, qw), lambda b, h, i: (b * nq + i, h)),
                  pl.BlockSpec((T, 2 * dh), lambda b, h, i: (b, kv0 + h))],
        out_specs=pl.BlockSpec((tq, qw), lambda b, h, i: (b * nq + i, h)),
        scratch_shapes=[pltpu.VMEM((T, 2 * dh), BF16), pltpu.VMEM((T // LANES, ONES_ROWS + dh, LANES), BF16),
                        pltpu.VMEM((nbp, 2 * dh), F32), pltpu.VMEM((nbp, R * tq), F32),
                        pltpu.VMEM((1, R * tq), F32), pltpu.VMEM((ONES_ROWS + dh, R * tq), F32)],
        compiler_params=_cp("parallel", "parallel", "arbitrary"), name="attn_moba",
    )(proj, proj)


def _moba_means_kernel(*refs, per_step):
    pages, o_ref = refs[1:1 + per_step], refs[1 + per_step]
    KH, dh = MOBA_KV_HEADS, MOBA_HEAD_DIM
    ppb = MOBA_BLOCK // PAGE_SIZE
    s = pl.program_id(1)

    @pl.when(s == 0)
    def _():
        o_ref[...] = jnp.zeros(o_ref.shape, F32)

    lane = _iota(o_ref.shape[1:], 1)
    out = o_ref[0]
    for j in range(per_step // ppb):
        cols = []
        for kh in range(KH):
            tot = pages[j * ppb][0, kh, 0]
            for p in range(1, ppb):
                tot = tot + pages[j * ppb + p][0, kh, 0]
            cols.append(jnp.sum(tot, axis=1, keepdims=True))
        col = jnp.concatenate(cols, axis=0) / MOBA_BLOCK
        out = jnp.where(lane == s * (per_step // ppb) + j, col, out)
    o_ref[0] = out


def _moba_means(pages, cache_t, per_step):
    B, n_pages = pages.shape
    KH, dh = MOBA_KV_HEADS, MOBA_HEAD_DIM
    nk = KH * dh
    assert n_pages * PAGE_SIZE // MOBA_BLOCK <= LANES
    cache_t = cache_t.reshape(-1, KH, 2, dh, PAGE_SIZE)
    grid_spec = pltpu.PrefetchScalarGridSpec(
        num_scalar_prefetch=1, grid=(B, n_pages // per_step),
        in_specs=[pl.BlockSpec((1, KH, 1, dh, PAGE_SIZE), functools.partial(
            lambda b, s, pt, p: (pt[b, s * per_step + p], 0, 0, 0, 0), p=p)) for p in range(per_step)],
        out_specs=pl.BlockSpec((1, nk, LANES), lambda b, s, pt: (b, 0, 0)))
    return pl.pallas_call(
        functools.partial(_moba_means_kernel, per_step=per_step),
        out_shape=jax.ShapeDtypeStruct((B, nk, LANES), F32), grid_spec=grid_spec,
        compiler_params=_cp("parallel", "arbitrary"), name="moba_means",
    )(pages, *([cache_t] * per_step))


def _moba_dec_score_kernel(q_ref, mean_ref, o_ref, *, nb):
    KH = MOBA_KV_HEADS
    R = MOBA_HEADS // KH
    dh = MOBA_HEAD_DIM
    lane = _iota((R, LANES), 1)
    for kh in range(KH):
        mk = mean_ref[0, kh * dh:(kh + 1) * dh, :].astype(BF16)
        g = _dot(q_ref[0, kh * R:(kh + 1) * R, :].astype(BF16), mk)
        o_ref[0, kh * R:(kh + 1) * R, :] = jnp.where(lane < nb, g, -jnp.inf)


def _moba_dec_score(q3, means, nb):
    B, H, dh = q3.shape
    blk3 = lambda b: (b, 0, 0)
    return pl.pallas_call(
        functools.partial(_moba_dec_score_kernel, nb=nb),
        out_shape=jax.ShapeDtypeStruct((B, H, LANES), F32), grid=(B,),
        in_specs=[pl.BlockSpec((1, H, dh), blk3), pl.BlockSpec((1,) + means.shape[1:], blk3)],
        out_specs=pl.BlockSpec((1, H, LANES), blk3), compiler_params=_cp("parallel"), name="moba_dec_score",
    )(q3, means)


def _moba_dec_attn_kernel(*refs, n_blk):
    R = MOBA_HEADS // MOBA_KV_HEADS
    dh = MOBA_HEAD_DIM
    pages = refs[1:1 + R * n_blk]
    q_ref, new_ref, o_ref = refs[1 + R * n_blk:]
    k_new = new_ref[0, :, :dh]
    v_new = new_ref[0, :, dh:]
    for r in range(R):
        q = q_ref[0, 0, r:r + 1, :] * dh ** -0.5
        kt = jnp.concatenate([pages[r * n_blk + n][0, :dh, :] for n in range(n_blk)], axis=1).astype(BF16)
        vt = jnp.concatenate([pages[r * n_blk + n][0, dh:, :] for n in range(n_blk)], axis=1).astype(BF16)
        s = _dot(q.astype(BF16), kt)
        s_new = jnp.sum(q * k_new, axis=-1, keepdims=True)
        m = jnp.maximum(jnp.max(s, axis=-1, keepdims=True), s_new)
        e = jnp.exp(s - m)
        e_new = jnp.exp(s_new - m)
        o_ref[0, 0, r:r + 1, :] = ((_dot_nt(e.astype(BF16), vt) + e_new * v_new)
                                   / (jnp.sum(e, axis=-1, keepdims=True) + e_new))


def _moba_dec_attn(page_ids, cache_t, q4, new_kv):
    B, KH, R, dh = q4.shape
    n_blk = page_ids.shape[1] // (KH * R)
    page_specs = [pl.BlockSpec((1, 2 * dh, PAGE_SIZE), functools.partial(
        lambda b, h, pg, n: (pg[b, h * R * n_blk + n], h, 0), n=n)) for n in range(R * n_blk)]
    grid_spec = pltpu.PrefetchScalarGridSpec(
        num_scalar_prefetch=1, grid=(B, KH),
        in_specs=page_specs + [pl.BlockSpec((1, 1, R, dh), lambda b, h, pg: (b, h, 0, 0)),
                               pl.BlockSpec((1, 1, 2 * dh), lambda b, h, pg: (b, 0, h))],
        out_specs=pl.BlockSpec((1, 1, R, dh), lambda b, h, pg: (b, h, 0, 0)))
    return pl.pallas_call(
        functools.partial(_moba_dec_attn_kernel, n_blk=n_blk),
        out_shape=jax.ShapeDtypeStruct(q4.shape, F32), grid_spec=grid_spec,
        compiler_params=_cp("parallel", "parallel"), name="moba_dec_attn",
    )(page_ids, *([cache_t] * (R * n_blk)), q4, new_kv)


def _router_kernel(x_ref, w_ref, b_ref, o_ref):
    s = jax.nn.sigmoid(lax.dot_general(w_ref[...], x_ref[...], (((1,), (1,)), ((), ())),
                                       preferred_element_type=F32, precision=lax.Precision.HIGHEST))
    row = _iota(s.shape, 0)
    x = s + b_ref[...]
    chosen = jnp.zeros(s.shape, F32)
    for _ in range(TOP_K):
        m = jnp.max(x, axis=0, keepdims=True)
        idx = jnp.min(jnp.where(x == m, row, N_EXPERTS), axis=0, keepdims=True)
        hit = row == idx
        chosen = jnp.where(hit, 1.0, chosen)
        x = jnp.where(hit, -jnp.inf, x)
    w = chosen * s
    w = w / jnp.sum(w, axis=0, keepdims=True) * ROUTED_SCALE
    o_ref[...] = jnp.concatenate([w, jnp.zeros((LANES - N_EXPERTS, w.shape[1]), F32)], axis=0).T


def _moe_router(x, w_router_t, b_router):
    M, K = x.shape
    tm = min(M, 512)
    E = w_router_t.shape[0]
    row = lambda i: (i, 0)
    fix = lambda i: (0, 0)
    return pl.pallas_call(
        _router_kernel, out_shape=jax.ShapeDtypeStruct((M, LANES), F32), grid=(M // tm,),
        in_specs=[pl.BlockSpec((tm, K), row), pl.BlockSpec((E, K), fix), pl.BlockSpec((E, 1), fix)],
        out_specs=pl.BlockSpec((tm, LANES), row), compiler_params=_cp("parallel"), name="moe_router",
    )(x, w_router_t, b_router)


def _moe_kernel(x_ref, gate_ref, wg_ref, wu_ref, wd_ref, wsg_ref, wsu_ref, wsd_ref, g_ref, b_ref, o_ref,
                xb_sc, acc_sc):
    e = pl.program_id(1)
    nh = MOE_GROUP * D_EXPERT

    @pl.when(e == 0)
    def _():
        xb = x_ref[...].astype(BF16)
        xb_sc[...] = xb
        hs = _silu(_dot(xb, wsg_ref[0].astype(BF16))) * _dot(xb, wsu_ref[0].astype(BF16))
        acc_sc[...] = _dot(hs.astype(BF16), wsd_ref[0].astype(BF16))

    xb = xb_sc[...]
    w_gate = jnp.concatenate([wg_ref[0, k].astype(BF16) for k in range(MOE_GROUP)], axis=1)
    w_up = jnp.concatenate([wu_ref[0, k].astype(BF16) for k in range(MOE_GROUP)], axis=1)
    src = _iota((2 * LANES, nh), 0) % LANES
    pick = jnp.where(src == MOE_GROUP * e + _iota((2 * LANES, nh), 1) // D_EXPERT, 1.0, 0.0).astype(BF16)
    gate = gate_ref[...]
    gate_hi = gate.astype(BF16)
    gate_lo = (gate - gate_hi.astype(F32)).astype(BF16)
    gw = _dot(jnp.concatenate([gate_hi, gate_lo], axis=1), pick)
    h = _silu(_dot(xb, w_gate)) * _dot(xb, w_up) * gw
    w_down = jnp.concatenate([wd_ref[0, k].astype(BF16) for k in range(MOE_GROUP)], axis=0)
    acc_sc[...] += _dot(h.astype(BF16), w_down)

    @pl.when(e == pl.num_programs(1) - 1)
    def _():
        o_ref[...] = _layer_norm(ALPHA * x_ref[...] + acc_sc[...], g_ref[...], b_ref[...])


def _moe(x, gate, w, g, b):
    M, D = x.shape
    tm = min(M, 1024)
    layer = w['layer']
    E, _, F = w['w_gate'].shape[1:]
    n = MOE_GROUP
    row = lambda i, e: (i, 0)
    fix = lambda i, e: (0, 0)
    grp = lambda i, e: (layer, e, 0, 0)
    lay = lambda i, e: (layer, 0, 0)
    return pl.pallas_call(
        _moe_kernel, out_shape=jax.ShapeDtypeStruct((M, D), F32), grid=(M // tm, E // n),
        in_specs=[pl.BlockSpec((tm, D), row), pl.BlockSpec((tm, LANES), row),
                  pl.BlockSpec((1, n, D, F), grp), pl.BlockSpec((1, n, D, F), grp), pl.BlockSpec((1, n, F, D), grp),
                  pl.BlockSpec((1, D, D_SHARED), lay), pl.BlockSpec((1, D, D_SHARED), lay),
                  pl.BlockSpec((1, D_SHARED, D), lay),
                  pl.BlockSpec((1, D), fix), pl.BlockSpec((1, D), fix)],
        out_specs=pl.BlockSpec((tm, D), row),
        scratch_shapes=[pltpu.VMEM((tm, D), BF16), pltpu.VMEM((tm, D), F32)],
        compiler_params=_cp("parallel", "arbitrary"), name="moe",
    )(x, gate, w['w_gate'], w['w_up'], w['w_down'], w['ws_gate'], w['ws_up'], w['ws_down'],
      g.reshape(1, D), b.reshape(1, D))


def _pad_cols(w, n):
    return jnp.pad(w, ((0, 0), (0, n - w.shape[1])))


def _block_diag(blocks):
    n, a, b = blocks.shape
    eye = jnp.eye(n, dtype=blocks.dtype)
    return (eye[:, None, :, None] * blocks[:, :, None, :]).reshape(n * a, n * b)


def _nsa_weights(w_in, cmp_pe, cmp_w1, cmp_b1, cmp_w2, cmp_b2, w_o):
    H, G, dh = NSA_HEADS, NSA_KV_HEADS, NSA_HEAD_DIM
    R = H // G
    nq, nkv = H * dh, 6 * G * dh
    wq, wkv = w_in[:, :nq], w_in[:, nq:nq + nkv]
    wg = w_in[:, nq + nkv:].reshape(-1, 3, G, R).transpose(0, 2, 1, 3).reshape(-1, G, 3 * R)
    wg = jnp.pad(wg, ((0, 0), (0, 0), (0, LANES - 3 * R))).reshape(-1, G * LANES)
    w_ext = jnp.concatenate([wq, wq, wkv, wg], axis=1).astype(BF16)
    w1 = cmp_w1.reshape(2, CMP_STRIDE, 2, dh, CMP_HIDDEN)
    eye_k = jnp.eye(2, dtype=F32)
    w_lohi = (w1.transpose(1, 2, 3, 0, 4)[:, :, :, :, None, :] * eye_k[None, :, None, None, :, None])
    w_lohi = w_lohi.reshape(CMP_STRIDE // 2, 2 * 2 * dh, 2 * 2 * CMP_HIDDEN).astype(BF16)
    pe_rows = jnp.broadcast_to(cmp_pe[:, None], (CMP_BLOCK, G, 2, dh)).reshape(1, CMP_BLOCK, G * 2 * dh)
    b1 = jnp.broadcast_to(cmp_b1[None], (G, 2, CMP_HIDDEN)).reshape(1, -1)
    b2 = jnp.broadcast_to(cmp_b2[None], (G, 2, dh)).reshape(1, -1)
    w2 = _block_diag(jnp.tile(cmp_w2, (G, 1, 1))).astype(BF16)
    return dict(w_ext=w_ext, w_lohi=w_lohi, pe_rows=pe_rows, b1=b1, b2=b2, w2=w2, w_o=w_o.astype(BF16))


def _nsa_tables(pos):
    H, G = NSA_HEADS, NSA_KV_HEADS
    dh = NSA_HEAD_DIM
    kv_rot = [(1.0, dh), (0.0, dh)] * G
    layout = [(0.0, H * dh), (1.0, H * dh), (0.0, 2 * G * dh)] + kv_rot + kv_rot + [(0.0, G * LANES)]
    return _unit_tables(pos, dh // 2) + (_column_mask(layout),)


def _mla_weights(w_dn, g_q, w_uq, g_kv, w_uk, w_uv, w_o):
    H = MLA_HEADS
    pad = MLA_SLOT - QK_NOPE - QK_ROPE
    w_dn_p = _pad_cols(w_dn, Q_LORA + KV_LORA + LANES).astype(BF16)
    wq = jnp.pad(w_uq.reshape(Q_LORA, H, QK_NOPE + QK_ROPE), ((0, 0), (0, 0), (0, pad)))
    w_q = wq.reshape(Q_LORA, H * MLA_SLOT).astype(BF16)
    wk_c = jnp.pad(w_uk, ((0, 0), (0, 0), (0, MLA_SLOT - QK_NOPE))).reshape(KV_LORA, H * MLA_SLOT)
    eye = jnp.pad(jnp.eye(QK_ROPE, dtype=F32), ((0, 0), (QK_NOPE, pad)))
    wk_r = jnp.tile(eye, (1, H))
    wk = jnp.concatenate([wk_c, wk_r, jnp.zeros((LANES - QK_ROPE, H * MLA_SLOT), F32)], axis=0)
    wv = jnp.pad(w_uv.reshape(KV_LORA, H * V_DIM), ((0, LANES), (0, 0)))
    w_kv = jnp.concatenate([wk, wv], axis=1).astype(BF16)
    absorb = jnp.pad(w_uk.transpose(1, 2, 0), ((0, 0), (0, MLA_SLOT - QK_NOPE), (0, 0)))
    w_absorb = _block_diag(absorb).astype(BF16)
    w_unabsorb = _block_diag(w_uv.transpose(1, 0, 2)).astype(BF16)
    return dict(w_dn=w_dn_p, g_q=g_q, g_kv=g_kv, w_q=w_q, w_kv=w_kv, w_absorb=w_absorb,
                w_unabsorb=w_unabsorb, w_o=w_o.astype(BF16))


def _mla_q_tables(pos):
    half = QK_ROPE // 2
    ua, ub = _rope_unit(pos, half)
    T = pos.shape[0]
    pad = MLA_SLOT - QK_NOPE - QK_ROPE
    a = jnp.concatenate([jnp.ones((T, QK_NOPE), F32), ua, jnp.ones((T, pad), F32)], axis=1)
    b = jnp.concatenate([jnp.zeros((T, QK_NOPE), F32), ub, jnp.zeros((T, pad), F32)], axis=1)
    return a, b, jnp.ones((1, MLA_HEADS * MLA_SLOT), F32)


def _mla_kr_tables(pos):
    return _tables(pos, QK_ROPE // 2, [('r', 1), ('n', LANES - QK_ROPE)])


def _moba_weights(w_in, w_o):
    return dict(w_in=w_in.astype(BF16), w_o=w_o.astype(BF16))


def _moba_tables(pos):
    dh = MOBA_HEAD_DIM
    layout = [(1.0, MOBA_HEADS * dh)] + [(1.0, dh), (0.0, dh)] * MOBA_KV_HEADS
    return _unit_tables(pos, dh // 2) + (_column_mask(layout),)


def _moe_weights(w_router, b_router, w_gate, w_up, w_down, ws_gate, ws_up, ws_down, layer=None):
    stacked = (w_router, b_router, w_gate, w_up, w_down, ws_gate, ws_up, ws_down)
    if layer is None:
        stacked, layer = tuple(a[None] for a in stacked), 0
    w_router, b_router, w_gate, w_up, w_down, ws_gate, ws_up, ws_down = stacked
    E = w_gate.shape[1]
    return dict(layer=layer, w_router=w_router[layer].T, b_router=b_router[layer].reshape(E, 1),
                w_gate=w_gate, w_up=w_up, w_down=w_down, ws_gate=ws_gate, ws_up=ws_up, ws_down=ws_down)


def _feature_major(cache):
    n, pool, rows = cache.shape[:3]
    nd = cache.ndim
    return cache.transpose(0, 1, *range(3, nd), 2).reshape(n * pool, -1, rows)


def _nsa_cmp_tokens(rows_lohi, w):
    pe_lohi = _cmp_lohi(w['pe_rows'], w['w_lohi'])
    return _cmp_combine(rows_lohi, pe_lohi, w['b1'], w['w2'], w['b2'])


def _nsa_prompt(h, B, T, w, tabs):
    G, dh = NSA_KV_HEADS, NSA_HEAD_DIM
    proj = _proj(h, w['w_ext'], tabs, dh // 2)
    kv0 = 2 * NSA_HEADS * dh
    width = 2 * G * dh
    lohi = _cmp_lohi(proj.reshape(B, T, -1), w['w_lohi'], col_block=kv0 // LANES)
    cmp_tok = _nsa_cmp_tokens(lohi, w)
    y = _attn_nsa(proj, cmp_tok, B, T)
    kv = proj[:, kv0:kv0 + 3 * width].reshape(B, T, 3, G, 2, dh)
    return y, (kv[:, :, 0], kv[:, :, 1], kv[:, -min(WINDOW, T):, 2])


def _nsa_sample(h, past_len, w, tabs, cache_cmp, cache_sel, win_state, slot, page_table):
    B = h.shape[0]
    G, H, dh = NSA_KV_HEADS, NSA_HEADS, NSA_HEAD_DIM
    R = H // G
    n_pool = cache_cmp.shape[1]
    width = 2 * G * dh
    proj = _proj(h, w['w_ext'], tabs, dh // 2)
    pages = page_table + slot * n_pool
    lohi = _cmp_lohi_paged(_feature_major(cache_cmp), pages, w['w_lohi'], per_step=min(32, page_table.shape[1]))
    cmp_tok = _nsa_cmp_tokens(lohi, w)
    n_sel = -(-(past_len + 1) // SEL_BLOCK)
    n_top = min(SEL_TOP, n_sel)
    q3 = proj[:, :H * dh].reshape(B, H, dh)
    qr3 = proj[:, H * dh:2 * H * dh].reshape(B, H, dh)
    o_cmp, imp = _nsa_dec_cmp(q3, cmp_tok, past_len, n_sel)
    idx = _topk_idx(imp.reshape(B * G, -1), n_top)[:, :n_top].reshape(B, G * n_top)
    n_past_blocks = past_len // SEL_BLOCK
    per_page = PAGE_SIZE // SEL_BLOCK
    sel_pages = jnp.take_along_axis(pages, jnp.minimum(idx, n_past_blocks - 1) // per_page, axis=1)
    kv0 = 2 * H * dh
    new_kv = proj[:, kv0:kv0 + 3 * width].reshape(B, 1, 3 * width)
    gl3 = proj[:, kv0 + 3 * width:].reshape(B, G, LANES)[:, :, :3 * R].reshape(B, G, 3, R)
    gl3 = gl3.transpose(0, 1, 3, 2).reshape(B, H, 3)
    Wn = win_state.shape[2]
    o = _nsa_dec_attn(idx, sel_pages, _feature_major(cache_sel), qr3, new_kv, _feature_major(win_state), slot * B,
                      o_cmp, gl3, n_past_blocks)
    kv = new_kv.reshape(B, 1, 3, G, 2, dh)
    new_win = jnp.concatenate([win_state[slot], kv[:, :, 2]], axis=1)[:, -Wn:]
    return o.reshape(B, H * dh), (kv[:, :, 0], kv[:, :, 1], new_win)


def _mla_front(h, w, q_tabs, kr_tabs):
    dn = _proj(h, w['w_dn'])
    c_q, ckr = _mla_post(dn, w['g_q'], w['g_kv'], kr_tabs)
    q_ext = _proj(c_q, w['w_q'], q_tabs, QK_ROPE // 2)
    return q_ext, ckr


def _mla_prompt(h, B, T, w, q_tabs, kr_tabs):
    q_ext, ckr = _mla_front(h, w, q_tabs, kr_tabs)
    kv_ext = _proj(ckr, w['w_kv'])
    y = _attn_mla(q_ext, kv_ext, B, T)
    return y, (ckr[:, :KV_LORA].reshape(B, T, KV_LORA), ckr[:, KV_LORA:KV_LORA + QK_ROPE].reshape(B, T, QK_ROPE))


def _mla_sample(h, w, q_tabs, kr_tabs, cache_ckv, cache_kr, slot, page_table):
    B = h.shape[0]
    H = MLA_HEADS
    n_pool = cache_ckv.shape[1]
    q_ext, ckr = _mla_front(h, w, q_tabs, kr_tabs)
    q_lat = _proj(q_ext, w['w_absorb']).reshape(B, H, KV_LORA)
    q_rope = q_ext.reshape(B, H, MLA_SLOT)[:, :, QK_NOPE:QK_NOPE + QK_ROPE]
    c_new = ckr[:, :KV_LORA].reshape(B, 1, KV_LORA)
    r_new = ckr[:, KV_LORA:KV_LORA + QK_ROPE].reshape(B, 1, QK_ROPE)
    pages = page_table + slot * n_pool
    o_lat = _mla_decode(pages, cache_ckv.reshape(-1, PAGE_SIZE, KV_LORA), _feature_major(cache_kr),
                        q_lat, q_rope, c_new, r_new, per_step=min(32, page_table.shape[1]))
    y = _proj(o_lat.reshape(B, H * KV_LORA), w['w_unabsorb'])
    return y, (c_new, r_new)


def _moba_prompt(h, B, T, w, tabs):
    KH, dh = MOBA_KV_HEADS, MOBA_HEAD_DIM
    proj = _proj(h, w['w_in'], tabs, dh // 2)
    y = _attn_moba(proj, B, T)
    return y, proj[:, MOBA_HEADS * dh:].reshape(B, T, KH, 2, dh)


def _moba_sample(h, past_len, w, tabs, cache, slot, page_table):
    B = h.shape[0]
    H, KH, dh = MOBA_HEADS, MOBA_KV_HEADS, MOBA_HEAD_DIM
    R = H // KH
    assert past_len % MOBA_BLOCK == 0 and past_len // MOBA_BLOCK >= MOBA_TOP
    n_pool = cache.shape[1]
    width = KH * 2 * dh
    ppb = MOBA_BLOCK // PAGE_SIZE
    proj = _proj(h, w['w_in'], tabs, dh // 2)
    pages = page_table + slot * n_pool
    cache_t = _feature_major(cache)
    means = _moba_means(pages, cache_t, per_step=min(16, page_table.shape[1]))
    q3 = proj[:, :H * dh].reshape(B, H, dh)
    scores = _moba_dec_score(q3, means, past_len // MOBA_BLOCK)
    idx = _topk_idx(scores.reshape(B * H, -1), MOBA_TOP)[:, :MOBA_TOP].reshape(B, H * MOBA_TOP)
    page_ids = jnp.take_along_axis(pages, (idx[:, :, None] * ppb + jnp.arange(ppb)).reshape(B, -1), axis=1)
    new_kv = proj[:, H * dh:].reshape(B, 1, width)
    o = _moba_dec_attn(page_ids, cache_t, q3.reshape(B, KH, R, dh), new_kv)
    return o.reshape(B, H * dh), new_kv.reshape(B, 1, KH, 2, dh)


def _moe_layer(h, w, g, b):
    gate = _moe_router(h, w['w_router'], w['b_router'])
    return _moe(h, gate, w, g, b)


def kernel(x_prompt, x_sample, cache_nsa_cmp, cache_nsa_sel, state_nsa_win, cache_mla_ckv, cache_mla_krope,
           cache_moba_kv, page_table, nsa_w_in, nsa_cmp_pe, nsa_cmp_w1, nsa_cmp_b1, nsa_cmp_w2, nsa_cmp_b2,
           nsa_w_o, mla_w_dn, mla_g_q, mla_w_uq, mla_g_kv, mla_w_uk, mla_w_uv, mla_w_o, moba_w_in, moba_w_o,
           ln1_g, ln1_b, ln2_g, ln2_b, moe_w_router, moe_b_router, moe_w_gate, moe_w_up, moe_w_down,
           moe_ws_gate, moe_ws_up, moe_ws_down):
    B, T, D = x_prompt.shape
    Bs, Ts, _ = x_sample.shape
    assert Ts == 1
    past_len = page_table.shape[1] * PAGE_SIZE
    assert state_nsa_win.shape[2] == WINDOW and past_len >= WINDOW
    pos_p = jnp.arange(T, dtype=I32)
    pos_s = jnp.full((Bs,), past_len, dtype=I32)
    hp = x_prompt.reshape(B * T, D)
    hs = x_sample.reshape(Bs, D)
    outs = {k: [] for k in ('cmp_p', 'cmp_s', 'sel_p', 'sel_s', 'win_p', 'win_s',
                            'ckv_p', 'ckv_s', 'kr_p', 'kr_s', 'mb_p', 'mb_s')}
    for i in range(DEPTH):
        kind, slot = i % N_MIXERS, i // N_MIXERS
        if kind == MIX_NSA:
            w = _nsa_weights(nsa_w_in[slot], nsa_cmp_pe[slot], nsa_cmp_w1[slot], nsa_cmp_b1[slot],
                             nsa_cmp_w2[slot], nsa_cmp_b2[slot], nsa_w_o[slot])
            yp, (a_p, b_p, c_p) = _nsa_prompt(hp, B, T, w, _nsa_tables(pos_p))
            ys, (a_s, b_s, c_s) = _nsa_sample(hs, past_len, w, _nsa_tables(pos_s), cache_nsa_cmp, cache_nsa_sel,
                                              state_nsa_win, slot, page_table)
            outs['cmp_p'].append(a_p); outs['cmp_s'].append(a_s)
            outs['sel_p'].append(b_p); outs['sel_s'].append(b_s)
            outs['win_p'].append(c_p); outs['win_s'].append(c_s)
        elif kind == MIX_MLA:
            w = _mla_weights(mla_w_dn[slot], mla_g_q[slot], mla_w_uq[slot], mla_g_kv[slot], mla_w_uk[slot],
                             mla_w_uv[slot], mla_w_o[slot])
            yp, (a_p, b_p) = _mla_prompt(hp, B, T, w, _mla_q_tables(pos_p), _mla_kr_tables(pos_p))
            ys, (a_s, b_s) = _mla_sample(hs, w, _mla_q_tables(pos_s), _mla_kr_tables(pos_s), cache_mla_ckv,
                                         cache_mla_krope, slot, page_table)
            outs['ckv_p'].append(a_p); outs['ckv_s'].append(a_s)
            outs['kr_p'].append(b_p); outs['kr_s'].append(b_s)
        else:
            w = _moba_weights(moba_w_in[slot], moba_w_o[slot])
            yp, a_p = _moba_prompt(hp, B, T, w, _moba_tables(pos_p))
            ys, a_s = _moba_sample(hs, past_len, w, _moba_tables(pos_s), cache_moba_kv, slot, page_table)
            outs['mb_p'].append(a_p); outs['mb_s'].append(a_s)
        hp = _proj_ln(yp, w['w_o'], hp, ln1_g[i], ln1_b[i])
        hs = _proj_ln(ys, w['w_o'], hs, ln1_g[i], ln1_b[i])
        mw = _moe_weights(moe_w_router, moe_b_router, moe_w_gate, moe_w_up, moe_w_down,
                          moe_ws_gate, moe_ws_up, moe_ws_down, layer=i)
        hp = _moe_layer(hp, mw, ln2_g[i], ln2_b[i])
        hs = _moe_layer(hs, mw, ln2_g[i], ln2_b[i])
    st = lambda k: jnp.stack(outs[k])
    return (hp.reshape(B, T, D), hs.reshape(Bs, Ts, D),
            st('cmp_p'), st('cmp_s'), st('sel_p'), st('sel_s'), st('win_p'), st('win_s'),
            st('ckv_p'), st('ckv_s'), st('kr_p'), st('kr_s'), st('mb_p'), st('mb_s'))
```

```python
import functools

import jax
import jax.numpy as jnp
from jax import lax
from jax.experimental import pallas as pl
from jax.experimental.pallas import tpu as pltpu

D_MODEL = 1024
DEPTH = 4
PAGE_SIZE = 128
N_MIXERS = 3
MIX_NSA, MIX_MLA, MIX_MOBA = 0, 1, 2

ALPHA = (2 * DEPTH) ** 0.25
LN_EPS = 1e-5
RMS_EPS = 1e-6
ROPE_THETA = 10000.0
NEG = -1e30
FORCE = 1e9

NSA_HEADS = 16
NSA_KV_HEADS = 2
NSA_HEAD_DIM = 64
CMP_BLOCK = 32
CMP_STRIDE = 16
CMP_HIDDEN = 128
SEL_BLOCK = 64
SEL_RATIO = SEL_BLOCK // CMP_STRIDE
SEL_TOP = 16
WINDOW = 512

MLA_HEADS = 16
Q_LORA = 384
KV_LORA = 256
QK_NOPE = 64
QK_ROPE = 32
V_DIM = 64
MLA_SLOT = 128

MOBA_HEADS = 16
MOBA_KV_HEADS = 4
MOBA_HEAD_DIM = 64
MOBA_BLOCK = 256
MOBA_TOP = 3

N_EXPERTS = 64
TOP_K = 8
D_EXPERT = 128
D_SHARED = 128
ROUTED_SCALE = 2.5

LANES = 128
FLASH_CHAINS = 1
MOE_GROUP = 4
ONES_ROWS = 16
VMEM_LIMIT = 48 * 1024 * 1024

F32 = jnp.float32
BF16 = jnp.bfloat16
I32 = jnp.int32


def _cp(*sem):
    return pltpu.CompilerParams(dimension_semantics=sem, vmem_limit_bytes=VMEM_LIMIT)


def _dot(a, b):
    return jnp.dot(a, b, preferred_element_type=F32)


def _dot_nt(a, b):
    return lax.dot_general(a, b, (((1,), (1,)), ((), ())), preferred_element_type=F32)


def _dot_exact(a, b):
    return jnp.dot(a, b, preferred_element_type=F32, precision=lax.Precision.HIGHEST)


def _iota(shape, axis):
    return lax.broadcasted_iota(I32, shape, axis)


def _layer_norm(z, g, b):
    mu = jnp.mean(z, axis=-1, keepdims=True)
    d = z - mu
    var = jnp.mean(d * d, axis=-1, keepdims=True)
    return d * lax.rsqrt(var + LN_EPS) * g + b


def _silu(x):
    return x * jax.nn.sigmoid(x)


def _softmax_rows(s, valid):
    s = jnp.where(valid, s, NEG)
    m = jnp.max(s, axis=-1, keepdims=True)
    e = jnp.where(valid, jnp.exp(s - m), 0.0)
    l = jnp.sum(e, axis=-1, keepdims=True)
    return e / jnp.where(l > 0.0, l, 1.0)


def _rank_before(x, n_cols, lane):
    rank = jnp.zeros(x.shape, F32)
    for c in range(n_cols):
        col = x[:, c:c + 1]
        rank = rank + jnp.where(col > x, 1.0, jnp.where(col == x, jnp.where(lane > c, 1.0, 0.0), 0.0))
    return rank


def _proj_kernel(x_ref, w_ref, o_ref):
    o_ref[...] = _dot(x_ref[...].astype(BF16), w_ref[...])


def _proj_rope_kernel(x_ref, w_ref, a_ref, b_ref, mask_ref, o_ref, *, half, tn):
    acc = _dot(x_ref[...].astype(BF16), w_ref[...])
    lane = _iota(acc.shape, 1)
    first = (lane % (2 * half)) < half
    partner = jnp.where(first, pltpu.roll(acc, tn - half, 1), pltpu.roll(acc, half, 1))
    rotary = mask_ref[...] > 0.5
    a = jnp.where(rotary, jnp.concatenate([a_ref[...]] * (tn // LANES), axis=1), 1.0)
    b = jnp.where(rotary, jnp.concatenate([b_ref[...]] * (tn // LANES), axis=1), 0.0)
    o_ref[...] = acc * a + partner * b


def _col_tile(n):
    return next(t for t in (512, 384, 256, 128) if n % t == 0)


def _proj(x, w, tabs=None, half=0):
    M, K = x.shape
    N = w.shape[1]
    tm = min(M, 512)
    tn = _col_tile(N)
    grid = (M // tm, N // tn)
    x_spec = pl.BlockSpec((tm, K), lambda i, j: (i, 0))
    w_spec = pl.BlockSpec((K, tn), lambda i, j: (0, j))
    o_spec = pl.BlockSpec((tm, tn), lambda i, j: (i, j))
    out_shape = jax.ShapeDtypeStruct((M, N), F32)
    if tabs is None:
        return pl.pallas_call(_proj_kernel, out_shape=out_shape, grid=grid, in_specs=[x_spec, w_spec],
                              out_specs=o_spec, compiler_params=_cp("parallel", "parallel"), name="proj")(x, w)
    a, bm, mask = tabs
    nrb = a.shape[0] // tm
    t_spec = pl.BlockSpec((tm, LANES), lambda i, j: (i % nrb, 0))
    m_spec = pl.BlockSpec((1, tn), lambda i, j: (0, j))
    return pl.pallas_call(functools.partial(_proj_rope_kernel, half=half, tn=tn), out_shape=out_shape, grid=grid,
                          in_specs=[x_spec, w_spec, t_spec, t_spec, m_spec], out_specs=o_spec,
                          compiler_params=_cp("parallel", "parallel"), name="proj_rope")(x, w, a, bm, mask)


def _proj_ln_kernel(x_ref, w_ref, r_ref, g_ref, b_ref, o_ref):
    y = _dot(x_ref[...].astype(BF16), w_ref[...])
    o_ref[...] = _layer_norm(ALPHA * r_ref[...] + y, g_ref[...], b_ref[...])


def _proj_ln(x, w, res, g, b):
    M, K = x.shape
    N = w.shape[1]
    tm = min(M, 512)
    row = lambda i: (i, 0)
    fix = lambda i: (0, 0)
    return pl.pallas_call(
        _proj_ln_kernel, out_shape=jax.ShapeDtypeStruct((M, N), F32), grid=(M // tm,),
        in_specs=[pl.BlockSpec((tm, K), row), pl.BlockSpec((K, N), fix), pl.BlockSpec((tm, N), row),
                  pl.BlockSpec((1, N), fix), pl.BlockSpec((1, N), fix)],
        out_specs=pl.BlockSpec((tm, N), row), compiler_params=_cp("parallel"), name="proj_ln",
    )(x, w, res, g.reshape(1, N), b.reshape(1, N))


def _rope_unit(pos, half):
    inv = jnp.power(ROPE_THETA, -jnp.arange(half, dtype=F32) / half)
    ang = pos.astype(F32)[:, None] * inv
    cos, sin = jnp.cos(ang), jnp.sin(ang)
    return jnp.concatenate([cos, cos], axis=1), jnp.concatenate([-sin, sin], axis=1)


def _unit_tables(pos, half):
    ua, ub = _rope_unit(pos, half)
    reps = LANES // (2 * half)
    return jnp.tile(ua, (1, reps)), jnp.tile(ub, (1, reps))


def _column_mask(layout):
    return jnp.concatenate([jnp.full((1, n), v, F32) for v, n in layout], axis=1)


def _tables(pos, half, layout):
    ua, ub = _rope_unit(pos, half)
    T = pos.shape[0]
    a_parts, b_parts = [], []
    for kind, n in layout:
        if kind == 'r':
            a_parts.append(jnp.tile(ua, (1, n)))
            b_parts.append(jnp.tile(ub, (1, n)))
        else:
            a_parts.append(jnp.ones((T, n), F32))
            b_parts.append(jnp.zeros((T, n), F32))
    return jnp.concatenate(a_parts, axis=1), jnp.concatenate(b_parts, axis=1)


def _add_bias(s, bias, heads):
    if heads == 1:
        return s + bias
    tq = s.shape[1] // heads
    return jnp.concatenate([s[:, h * tq:(h + 1) * tq] + bias for h in range(heads)], axis=1)


def _weighted_values(vt_ref, first, e):
    acc = _dot(vt_ref[first], e[:LANES])
    for j in range(1, e.shape[0] // LANES):
        acc = acc + _dot(vt_ref[first + j], e[j * LANES:(j + 1) * LANES])
    return acc


def _flash(qs, k_ref, k_cols, vt_ref, vt_base, m_sc, acc_sc, lo, hi, tk, bias_fn, heads):
    m_sc[...] = jnp.full(m_sc.shape, NEG, F32)
    acc_sc[...] = jnp.zeros(acc_sc.shape, F32)
    sub = tk // LANES

    def body(kt, carry):
        k0 = pl.multiple_of(kt * tk, tk)
        s = _add_bias(_dot_nt(k_ref[pl.ds(k0, tk), k_cols], qs), bias_fn(kt, k0), heads)
        m_old = m_sc[...]
        m_new = jnp.maximum(m_old, jnp.max(s, axis=0, keepdims=True))
        e = jnp.exp(s - m_new).astype(BF16)
        acc_sc[...] = jnp.exp(m_old - m_new) * acc_sc[...] + _weighted_values(vt_ref, vt_base + kt * sub, e)
        m_sc[...] = m_new
        return carry

    lax.fori_loop(lo, hi, body, 0)
    return acc_sc[...]


def _transpose_rows(x):
    eye = jnp.where(_iota((LANES, LANES), 0) == _iota((LANES, LANES), 1), 1.0, 0.0).astype(BF16)
    return _dot_nt(eye, x)


def _store_values_t(vt_ref, base, vt):
    n = vt.shape[1]
    full = jnp.concatenate([jnp.ones((ONES_ROWS, n), F32), vt], axis=0).astype(BF16)
    for j in range(n // LANES):
        vt_ref[base + j] = full[:, j * LANES:(j + 1) * LANES]


def _write_heads_t(o_ref, heads):
    tq = heads[0].shape[1]
    for p in range(len(heads) // 2):
        pair = jnp.concatenate([heads[2 * p], heads[2 * p + 1]], axis=0)
        for c in range(tq // LANES):
            o_ref[c * LANES:(c + 1) * LANES, p * LANES:(p + 1) * LANES] = pair[:, c * LANES:(c + 1) * LANES].T


def _stack_heads(ref, n_heads, dh, scale):
    parts = []
    for r in range(n_heads):
        pair = ref[:, (r // 2) * 2 * dh:(r // 2 + 1) * 2 * dh] * scale
        lane = _iota(pair.shape, 1)
        own = (lane < dh) if r % 2 == 0 else (lane >= dh)
        parts.append(jnp.where(own, pair, 0.0))
    return jnp.concatenate(parts, axis=0).astype(BF16)


def _dup_keys(kv, dh):
    return jnp.where(_iota(kv.shape, 1) < dh, kv, pltpu.roll(kv, dh, 1))


def _cmp_lohi_kernel(*refs, n_pages, rows_per_page, feature_major):
    G = NSA_KV_HEADS
    if feature_major:
        refs, x_sc = refs[1:-1], refs[-1]
    n_in = n_pages if feature_major else G
    row_refs, w_ref, o_ref = refs[:n_in], refs[n_in], refs[n_in + 1]
    cpp = rows_per_page // CMP_STRIDE
    nh = w_ref.shape[2] // 2

    def chunk_rows(g, r):
        if feature_major:
            return x_sc[pl.ds(r, n_pages * cpp, stride=CMP_STRIDE), :]
        return row_refs[g][0, pl.ds(r, cpp, stride=CMP_STRIDE), :]

    for g in range(G):
        if feature_major:
            for p in range(n_pages):
                x_sc[p * rows_per_page:(p + 1) * rows_per_page, :] = row_refs[p][0, g * LANES:(g + 1) * LANES, :].T
        acc = jnp.zeros((n_pages * cpp, 2 * nh), F32)
        for r2 in range(CMP_STRIDE // 2):
            xr = jnp.concatenate([chunk_rows(g, 2 * r2), chunk_rows(g, 2 * r2 + 1)], axis=1)
            acc = acc + _dot(xr.astype(BF16), w_ref[r2])
        o_ref[0, :, g * nh:(g + 1) * nh] = acc[:, :nh]
        o_ref[0, :, (G + g) * nh:(G + g + 1) * nh] = acc[:, nh:]


def _cmp_lohi(rows, w_lohi, col_block=0):
    B, L = rows.shape[0], rows.shape[1]
    G = NSA_KV_HEADS
    nout = G * w_lohi.shape[2]
    row_specs = [pl.BlockSpec((1, L, LANES), functools.partial(lambda b, g: (b, 0, col_block + g), g=g))
                 for g in range(G)]
    return pl.pallas_call(
        functools.partial(_cmp_lohi_kernel, n_pages=1, rows_per_page=L, feature_major=False),
        out_shape=jax.ShapeDtypeStruct((B, L // CMP_STRIDE, nout), F32), grid=(B,),
        in_specs=row_specs + [pl.BlockSpec(w_lohi.shape, lambda b: (0, 0, 0))],
        out_specs=pl.BlockSpec((1, L // CMP_STRIDE, nout), lambda b: (b, 0, 0)),
        compiler_params=_cp("parallel"), name="cmp_lohi",
    )(*([rows] * G), w_lohi)


def _cmp_lohi_paged(cache_t, pages, w_lohi, per_step):
    B, n_pages = pages.shape
    G = NSA_KV_HEADS
    nout = G * w_lohi.shape[2]
    cpp = PAGE_SIZE // CMP_STRIDE
    page_specs = [pl.BlockSpec((1, G * LANES, PAGE_SIZE), functools.partial(
        lambda b, s, pt, p: (pt[b, s * per_step + p], 0, 0), p=p)) for p in range(per_step)]
    grid_spec = pltpu.PrefetchScalarGridSpec(
        num_scalar_prefetch=1, grid=(B, n_pages // per_step),
        in_specs=page_specs + [pl.BlockSpec(w_lohi.shape, lambda b, s, pt: (0, 0, 0))],
        out_specs=pl.BlockSpec((1, per_step * cpp, nout), lambda b, s, pt: (b, s, 0)),
        scratch_shapes=[pltpu.VMEM((per_step * PAGE_SIZE, LANES), F32)])
    return pl.pallas_call(
        functools.partial(_cmp_lohi_kernel, n_pages=per_step, rows_per_page=PAGE_SIZE, feature_major=True),
        out_shape=jax.ShapeDtypeStruct((B, n_pages * cpp, nout), F32), grid_spec=grid_spec,
        compiler_params=_cp("parallel", "parallel"), name="cmp_lohi_paged",
    )(pages, *([cache_t] * per_step), w_lohi)


def _cmp_combine_kernel(lohi_ref, pe_ref, b1_ref, w2_ref, b2_ref, o_ref):
    nch = lohi_ref.shape[1]
    nh = lohi_ref.shape[2] // 2
    lo = lohi_ref[0, :, :nh]
    hi_next = pltpu.roll(lohi_ref[0, :, nh:], nch - 1, 0)
    pe = pe_ref[0, 0:1, :nh] + pe_ref[0, 1:2, nh:]
    hid = jax.nn.gelu(lo + hi_next + pe + b1_ref[...])
    o_ref[0] = _dot(hid.astype(BF16), w2_ref[...]) + b2_ref[...]


def _cmp_combine(lohi, pe_lohi, b1, w2, b2):
    B, nch, n2 = lohi.shape
    nh = n2 // 2
    nout = w2.shape[1]
    fix2 = lambda b: (0, 0)
    return pl.pallas_call(
        _cmp_combine_kernel, out_shape=jax.ShapeDtypeStruct((B, nch, nout), F32), grid=(B,),
        in_specs=[pl.BlockSpec((1, nch, n2), lambda b: (b, 0, 0)), pl.BlockSpec((1, 2, n2), lambda b: (0, 0, 0)),
                  pl.BlockSpec((1, nh), fix2), pl.BlockSpec((nh, nout), fix2), pl.BlockSpec((1, nout), fix2)],
        out_specs=pl.BlockSpec((1, nch, nout), lambda b: (b, 0, 0)),
        compiler_params=_cp("parallel"), name="cmp_combine",
    )(lohi, pe_lohi, b1, w2, b2)


def _sel_importance(p_sum, n_lanes):
    nc = p_sum.shape[1]
    c = _iota((nc, n_lanes), 0)
    j = _iota((nc, n_lanes), 1)
    a = jnp.where(c >= SEL_RATIO * j - 1, jnp.where(c <= SEL_RATIO * j + SEL_RATIO - 1, 1.0, 0.0), 0.0)
    return _dot_exact(p_sum, a)


def _attn_nsa_kernel(q_ref, qr_ref, cmp_ref, sel_ref, win_ref, gl_ref, o_ref,
                     k2s, vts, k2w, vtw, bias_ref, m_sc, acc_sc, *, T, tq, tk, n_sel, n_top):
    i = pl.program_id(2)
    t0 = i * tq
    R = NSA_HEADS // NSA_KV_HEADS
    dh = NSA_HEAD_DIM
    scale = dh ** -0.5
    wk = min(T, WINDOW + tq)

    @pl.when(i == 0)
    def _():
        for src, k_sc, vt_sc in ((sel_ref, k2s, vts), (win_ref, k2w, vtw)):
            kv = src[...]
            k_sc[...] = _dup_keys(kv, dh).astype(BF16)
            _store_values_t(vt_sc, 0, _transpose_rows(kv.astype(BF16))[dh:])

    tpos = t0 + _iota((1, tq), 1)
    cmp_tok = cmp_ref[0]
    nc = cmp_tok.shape[0]
    s = _dot_nt(_dup_keys(cmp_tok, dh).astype(BF16), _stack_heads(q_ref, R, dh, scale))
    cvalid = (_iota((nc, 1), 0) * CMP_STRIDE + CMP_BLOCK - 1) <= tpos
    probs = []
    for h in range(R):
        sh = jnp.where(cvalid, s[:, h * tq:(h + 1) * tq], NEG)
        e = jnp.where(cvalid, jnp.exp(sh - jnp.max(sh, axis=0, keepdims=True)), 0.0)
        l = jnp.sum(e, axis=0, keepdims=True)
        probs.append(e / jnp.where(l > 0.0, l, 1.0))
    p_sum = probs[0]
    for h in range(1, R):
        p_sum = p_sum + probs[h]
    vct = _transpose_rows(cmp_tok.astype(BF16))[dh:].astype(BF16)
    o_cmp = _dot(vct, jnp.concatenate(probs, axis=1).astype(BF16))

    ns = bias_ref.shape[0] // SEL_BLOCK
    blk = _iota((ns, tq), 0)
    cur = tpos // SEL_BLOCK
    c_idx = _iota((ns, nc), 1)
    lo_c = SEL_RATIO * _iota((ns, nc), 0) - 1
    spread = jnp.where(c_idx >= lo_c, jnp.where(c_idx <= lo_c + SEL_RATIO, 1.0, 0.0), 0.0)
    imp = _dot_exact(spread, p_sum)
    forced = jnp.where(blk == 0, 1.0, jnp.where(blk == cur, 1.0, jnp.where(blk == cur - 1, 1.0, 0.0)))
    imp = jnp.where(blk > cur, NEG, jnp.where(forced > 0.5, FORCE, imp))
    rank = jnp.zeros(imp.shape, F32)
    for c in range(n_sel):
        row = imp[c:c + 1, :]
        rank = rank + jnp.where(row > imp, 1.0, jnp.where(row == imp, jnp.where(blk > c, 1.0, 0.0), 0.0))
    chosen = jnp.where(blk < n_sel, jnp.where(rank < n_top, 1.0, 0.0), 0.0)
    for j in range(n_sel):
        picked = jnp.broadcast_to(chosen[j:j + 1, :], (SEL_BLOCK, tq))
        kpos = j * SEL_BLOCK + _iota((SEL_BLOCK, 1), 0)
        bias_ref[j * SEL_BLOCK:(j + 1) * SEL_BLOCK, :] = jnp.where(
            picked > 0.5, jnp.where(kpos <= tpos, 0.0, NEG), NEG)

    hi = (t0 + tq + tk - 1) // tk
    qrs = _stack_heads(qr_ref, R, dh, scale)
    acc = _flash(qrs, k2s, slice(None), vts, 0, m_sc, acc_sc, 0, hi, tk,
                 lambda kt, k0: bias_ref[pl.ds(k0, tk), :], R)
    o_sel = acc[ONES_ROWS:] / acc[0:1]
    kw0 = pl.multiple_of(jnp.clip(t0 - WINDOW, 0, T - wk), LANES)
    d = tpos - (kw0 + _iota((wk, 1), 0))
    win_bias = jnp.where(d >= 0, jnp.where(d <= WINDOW, 0.0, NEG), NEG)
    s = _add_bias(_dot_nt(k2w[pl.ds(kw0, wk), :], qrs), win_bias, R)
    e = jnp.exp(s - jnp.max(s, axis=0, keepdims=True)).astype(BF16)
    acc = _weighted_values(vtw, kw0 // LANES, e)
    o_win = acc[ONES_ROWS:] / acc[0:1]
    gate = jax.nn.sigmoid(gl_ref[...].T)
    heads = []
    for h in range(R):
        hs = slice(h * tq, (h + 1) * tq)
        heads.append(gate[h:h + 1] * o_cmp[:, hs] + gate[R + h:R + h + 1] * o_sel[:, hs]
                     + gate[2 * R + h:2 * R + h + 1] * o_win[:, hs])
    _write_heads_t(o_ref, heads)


def _attn_nsa(proj, cmp_tok, B, T, tk=None):
    G = NSA_KV_HEADS
    R = NSA_HEADS // G
    tq = 2 * LANES
    tk = tk or min(T, 512)
    nq = T // tq
    n_sel = -(-T // SEL_BLOCK)
    n_top = min(SEL_TOP, n_sel)
    qw = NSA_HEADS * NSA_HEAD_DIM // G
    kv0 = 2 * NSA_HEADS * NSA_HEAD_DIM // LANES
    gl0 = kv0 + 3 * G
    nch = cmp_tok.shape[1]
    dh = NSA_HEAD_DIM
    kern = functools.partial(_attn_nsa_kernel, T=T, tq=tq, tk=tk, n_sel=n_sel, n_top=n_top)
    return pl.pallas_call(
        kern, out_shape=jax.ShapeDtypeStruct((B * T, NSA_HEADS * dh), F32), grid=(B, G, nq),
        in_specs=[pl.BlockSpec((tq, qw), lambda b, g, i: (b * nq + i, g)),
                  pl.BlockSpec((tq, qw), lambda b, g, i: (b * nq + i, G + g)),
                  pl.BlockSpec((1, nch, 2 * dh), lambda b, g, i: (b, 0, g)),
                  pl.BlockSpec((T, 2 * dh), lambda b, g, i: (b, kv0 + G + g)),
                  pl.BlockSpec((T, 2 * dh), lambda b, g, i: (b, kv0 + 2 * G + g)),
                  pl.BlockSpec((tq, LANES), lambda b, g, i: (b * nq + i, gl0 + g))],
        out_specs=pl.BlockSpec((tq, qw), lambda b, g, i: (b * nq + i, g)),
        scratch_shapes=[pltpu.VMEM((T, 2 * dh), BF16), pltpu.VMEM((T // LANES, ONES_ROWS + dh, LANES), BF16)] * 2 + [
            pltpu.VMEM((T, tq), F32), pltpu.VMEM((1, R * tq), F32), pltpu.VMEM((ONES_ROWS + dh, R * tq), F32)],
        compiler_params=_cp("parallel", "parallel", "arbitrary"), name="attn_nsa",
    )(proj, proj, cmp_tok, proj, proj, proj)


def _nsa_dec_cmp_kernel(q_ref, cmp_ref, o_ref, imp_ref, *, t, n_sel):
    G = NSA_KV_HEADS
    R = NSA_HEADS // G
    dh = NSA_HEAD_DIM
    nc = cmp_ref.shape[1]
    nl = imp_ref.shape[2]
    cvalid = (_iota((1, nc), 1) * CMP_STRIDE + CMP_BLOCK - 1) <= t
    blk = _iota((1, nl), 1)
    cur = t // SEL_BLOCK
    for g in range(G):
        kc = cmp_ref[0, :, 2 * g * dh:(2 * g + 1) * dh].astype(BF16)
        vc = cmp_ref[0, :, (2 * g + 1) * dh:(2 * g + 2) * dh].astype(BF16)
        qg = (q_ref[0, g * R:(g + 1) * R, :] * dh ** -0.5).astype(BF16)
        p = _softmax_rows(_dot_nt(qg, kc), cvalid)
        o_ref[0, g * R:(g + 1) * R, :] = _dot(p.astype(BF16), vc)
        imp = _sel_importance(jnp.sum(p, axis=0, keepdims=True), nl)
        forced = jnp.where(blk == 0, 1.0, jnp.where(blk == cur, 1.0, jnp.where(blk == cur - 1, 1.0, 0.0)))
        imp = jnp.where(blk > cur, NEG, jnp.where(forced > 0.5, FORCE, imp))
        imp_ref[0, g:g + 1, :] = jnp.where(blk < n_sel, imp, -jnp.inf)


def _nsa_dec_cmp(q3, cmp_tok, t, n_sel):
    B = q3.shape[0]
    nl = -(-n_sel // LANES) * LANES
    nc = cmp_tok.shape[1]
    blk3 = lambda b: (b, 0, 0)
    return pl.pallas_call(
        functools.partial(_nsa_dec_cmp_kernel, t=t, n_sel=n_sel),
        out_shape=(jax.ShapeDtypeStruct(q3.shape, F32), jax.ShapeDtypeStruct((B, NSA_KV_HEADS, nl), F32)),
        grid=(B,),
        in_specs=[pl.BlockSpec((1,) + q3.shape[1:], blk3), pl.BlockSpec((1, nc, cmp_tok.shape[2]), blk3)],
        out_specs=(pl.BlockSpec((1,) + q3.shape[1:], blk3), pl.BlockSpec((1, NSA_KV_HEADS, nl), blk3)),
        compiler_params=_cp("parallel"), name="nsa_dec_cmp",
    )(q3, cmp_tok)


def _topk_idx_kernel(x_ref, o_ref, *, k):
    x = x_ref[...]
    lane = _iota(x.shape, 1)
    out_lane = _iota(o_ref.shape, 1)
    out = jnp.zeros(o_ref.shape, I32)
    big = x.shape[1]
    for n in range(k):
        m = jnp.max(x, axis=-1, keepdims=True)
        idx = jnp.min(jnp.where(x == m, lane, big), axis=-1, keepdims=True)
        out = jnp.where(out_lane == n, idx, out)
        x = jnp.where(lane == idx, -jnp.inf, x)
    o_ref[...] = out


def _topk_idx(x, k):
    rows = x.shape[0]
    return pl.pallas_call(functools.partial(_topk_idx_kernel, k=k),
                          out_shape=jax.ShapeDtypeStruct((rows, LANES), I32), name="topk_idx")(x)


def _nsa_dec_attn_kernel(*refs, n_top, n_past_blocks):
    idx_ref, rb_ref = refs[0], refs[1]
    G = NSA_KV_HEADS
    R = NSA_HEADS // G
    dh = NSA_HEAD_DIM
    blocks = refs[2:2 + G * n_top]
    qr_ref, new_ref, win_ref, ocmp_ref, gl_ref, o_ref = refs[2 + G * n_top:]
    b = pl.program_id(0)
    scale = dh ** -0.5
    per_page = PAGE_SIZE // SEL_BLOCK
    nk = n_top * PAGE_SIZE
    lane = _iota((1, nk), 1)
    for g in range(G):
        qg = qr_ref[0, g * R:(g + 1) * R, :] * scale
        qb = qg.astype(BF16)
        kt = jnp.concatenate([blocks[g * n_top + n][0, :dh, :] for n in range(n_top)], axis=1).astype(BF16)
        vt = jnp.concatenate([blocks[g * n_top + n][0, dh:, :] for n in range(n_top)], axis=1).astype(BF16)
        valid = jnp.zeros((1, nk), F32)
        for n in range(n_top):
            blk = idx_ref[b, g * n_top + n]
            flag = jnp.where(blk < n_past_blocks, 1.0, 0.0)
            in_block = jnp.where((lane % PAGE_SIZE) // SEL_BLOCK == blk % per_page, flag, 0.0)
            valid = jnp.where(lane // PAGE_SIZE == n, in_block, valid)
        valid = valid > 0.5
        k_new = new_ref[0, :, (2 * G + 2 * g) * dh:(2 * G + 2 * g + 1) * dh]
        v_new = new_ref[0, :, (2 * G + 2 * g + 1) * dh:(2 * G + 2 * g + 2) * dh]
        s = jnp.where(valid, _dot(qb, kt), NEG)
        s_new = jnp.sum(qg * k_new, axis=-1, keepdims=True)
        m = jnp.maximum(jnp.max(s, axis=-1, keepdims=True), s_new)
        e = jnp.where(valid, jnp.exp(s - m), 0.0)
        e_new = jnp.exp(s_new - m)
        o_sel = (_dot_nt(e.astype(BF16), vt) + e_new * v_new) / (jnp.sum(e, axis=-1, keepdims=True) + e_new)
        kwin = win_ref[0, 2 * g * dh:(2 * g + 1) * dh, :].astype(BF16)
        vwin = win_ref[0, (2 * g + 1) * dh:(2 * g + 2) * dh, :].astype(BF16)
        k_new = new_ref[0, :, (4 * G + 2 * g) * dh:(4 * G + 2 * g + 1) * dh]
        v_new = new_ref[0, :, (4 * G + 2 * g + 1) * dh:(4 * G + 2 * g + 2) * dh]
        s = _dot(qb, kwin)
        s_new = jnp.sum(qg * k_new, axis=-1, keepdims=True)
        m = jnp.maximum(jnp.max(s, axis=-1, keepdims=True), s_new)
        e = jnp.exp(s - m)
        e_new = jnp.exp(s_new - m)
        o_win = (_dot_nt(e.astype(BF16), vwin) + e_new * v_new) / (jnp.sum(e, axis=-1, keepdims=True) + e_new)
        gate = jax.nn.sigmoid(gl_ref[0, g * R:(g + 1) * R, :])
        o_ref[0, g * R:(g + 1) * R, :] = (gate[:, 0:1] * ocmp_ref[0, g * R:(g + 1) * R, :]
                                          + gate[:, 1:2] * o_sel + gate[:, 2:3] * o_win)


def _nsa_dec_attn(idx, sel_pages, cache_t, qr3, new_kv, win_buf, win_index0, o_cmp, gl3, n_past_blocks):
    B = qr3.shape[0]
    G = NSA_KV_HEADS
    n_top = idx.shape[1] // G
    dh = NSA_HEAD_DIM
    blk3 = lambda b, idx, rb: (b, 0, 0)
    block_specs = [pl.BlockSpec((1, 2 * dh, PAGE_SIZE), functools.partial(
        lambda b, idx, rb, n, g: (rb[b, n], g, 0), n=g * n_top + n, g=g)) for g in range(G) for n in range(n_top)]
    grid_spec = pltpu.PrefetchScalarGridSpec(
        num_scalar_prefetch=2, grid=(B,),
        in_specs=block_specs + [
            pl.BlockSpec((1,) + qr3.shape[1:], blk3), pl.BlockSpec((1,) + new_kv.shape[1:], blk3),
            pl.BlockSpec((1,) + win_buf.shape[1:], lambda b, idx, rb: (win_index0 + b, 0, 0)),
            pl.BlockSpec((1,) + o_cmp.shape[1:], blk3), pl.BlockSpec((1,) + gl3.shape[1:], blk3)],
        out_specs=pl.BlockSpec((1,) + qr3.shape[1:], blk3))
    return pl.pallas_call(
        functools.partial(_nsa_dec_attn_kernel, n_top=n_top, n_past_blocks=n_past_blocks),
        out_shape=jax.ShapeDtypeStruct(qr3.shape, F32), grid_spec=grid_spec,
        compiler_params=_cp("parallel"), name="nsa_dec_attn",
    )(idx, sel_pages, *([cache_t] * (G * n_top)), qr3, new_kv, win_buf, o_cmp, gl3)


def _mla_post_kernel(dn_ref, gq_ref, gkv_ref, a_ref, b_ref, cq_ref, ckr_ref):
    x = dn_ref[...]
    cq = x[:, :Q_LORA]
    cq_ref[...] = cq * lax.rsqrt(jnp.mean(cq * cq, axis=-1, keepdims=True) + RMS_EPS) * gq_ref[...]
    ckv = x[:, Q_LORA:Q_LORA + KV_LORA]
    ckv = ckv * lax.rsqrt(jnp.mean(ckv * ckv, axis=-1, keepdims=True) + RMS_EPS) * gkv_ref[...]
    kr = x[:, Q_LORA + KV_LORA:]
    half = QK_ROPE // 2
    first = (_iota(kr.shape, 1) % QK_ROPE) < half
    partner = jnp.where(first, pltpu.roll(kr, LANES - half, 1), pltpu.roll(kr, half, 1))
    ckr_ref[...] = jnp.concatenate([ckv, kr * a_ref[...] + partner * b_ref[...]], axis=1)


def _mla_post(dn, g_q, g_kv, tabs):
    M, N = dn.shape
    tm = min(M, 512)
    nrb = tabs[0].shape[0] // tm
    row = lambda i: (i, 0)
    fix = lambda i: (0, 0)
    tab = pl.BlockSpec((tm, LANES), lambda i: (i % nrb, 0))
    return pl.pallas_call(
        _mla_post_kernel,
        out_shape=(jax.ShapeDtypeStruct((M, Q_LORA), F32), jax.ShapeDtypeStruct((M, KV_LORA + LANES), F32)),
        grid=(M // tm,),
        in_specs=[pl.BlockSpec((tm, N), row), pl.BlockSpec((1, Q_LORA), fix), pl.BlockSpec((1, KV_LORA), fix),
                  tab, tab],
        out_specs=(pl.BlockSpec((tm, Q_LORA), row), pl.BlockSpec((tm, KV_LORA + LANES), row)),
        compiler_params=_cp("parallel"), name="mla_post",
    )(dn, g_q.reshape(1, -1), g_kv.reshape(1, -1), *tabs)


def _attn_mla_kernel(q_ref, k_ref, v_ref, o_ref, k_sc, v_sc, m_sc, acc_sc, *, tq, tk):
    i = pl.program_id(2)
    t0 = i * tq
    scale = (QK_NOPE + QK_ROPE) ** -0.5

    n_vt = k_ref.shape[0] // LANES

    @pl.when(i == 0)
    def _():
        k_sc[...] = k_ref[...].astype(BF16)
        vt = _transpose_rows(v_ref[...].astype(BF16))
        for hh in range(2):
            _store_values_t(v_sc, hh * n_vt, vt[hh * V_DIM:(hh + 1) * V_DIM])

    tpos = t0 + _iota((1, tq), 1)
    hi = (t0 + tq + tk - 1) // tk
    causal = lambda kt, k0: jnp.where(k0 + _iota((tk, 1), 0) <= tpos, 0.0, NEG)
    heads = []
    for hh in range(2):
        qh = (q_ref[:, hh * MLA_SLOT:(hh + 1) * MLA_SLOT] * scale).astype(BF16)
        acc = _flash(qh, k_sc, slice(hh * MLA_SLOT, (hh + 1) * MLA_SLOT), v_sc, hh * n_vt, m_sc, acc_sc, 0, hi, tk,
                     causal, 1)
        heads.append(acc[ONES_ROWS:] / acc[0:1])
    _write_heads_t(o_ref, heads)


def _attn_mla(q_ext, kv_ext, B, T, tk=None):
    H = MLA_HEADS
    tq = min(T, 1024)
    tk = tk or min(T, 512)
    nq = T // tq
    v0 = H * MLA_SLOT // LANES
    return pl.pallas_call(
        functools.partial(_attn_mla_kernel, tq=tq, tk=tk),
        out_shape=jax.ShapeDtypeStruct((B * T, H * V_DIM), F32), grid=(B, H // 2, nq),
        in_specs=[pl.BlockSpec((tq, 2 * MLA_SLOT), lambda b, h, i: (b * nq + i, h)),
                  pl.BlockSpec((T, 2 * MLA_SLOT), lambda b, h, i: (b, h)),
                  pl.BlockSpec((T, 2 * V_DIM), lambda b, h, i: (b, v0 + h))],
        out_specs=pl.BlockSpec((tq, 2 * V_DIM), lambda b, h, i: (b * nq + i, h)),
        scratch_shapes=[pltpu.VMEM((T, 2 * MLA_SLOT), BF16),
                        pltpu.VMEM((2 * (T // LANES), ONES_ROWS + V_DIM, LANES), BF16),
                        pltpu.VMEM((1, tq), F32), pltpu.VMEM((ONES_ROWS + V_DIM, tq), F32)],
        compiler_params=_cp("parallel", "parallel", "arbitrary"), name="attn_mla",
    )(q_ext, kv_ext, kv_ext)


def _mla_decode_kernel(*refs, per_step):
    pt_ref = refs[0]
    ckv_pages = refs[1:1 + per_step]
    kr_pages = refs[1 + per_step:1 + 2 * per_step]
    ql_ref, qr_ref, cnew_ref, rnew_ref, o_ref, m_sc, l_sc, acc_sc = refs[1 + 2 * per_step:]
    s_idx = pl.program_id(1)
    scale = (QK_NOPE + QK_ROPE) ** -0.5
    ql = ql_ref[0] * scale
    qr = qr_ref[0] * scale

    @pl.when(s_idx == 0)
    def _():
        c_new = cnew_ref[0]
        s_new = (jnp.sum(ql * c_new, axis=-1, keepdims=True) + jnp.sum(qr * rnew_ref[0], axis=-1, keepdims=True))
        m_sc[...] = s_new
        l_sc[...] = jnp.ones(l_sc.shape, F32)
        acc_sc[...] = jnp.broadcast_to(c_new, acc_sc.shape)

    qlb = ql.astype(BF16)
    qrb = qr.astype(BF16)
    cs = [ref[0].astype(BF16) for ref in ckv_pages]
    s = jnp.concatenate([_dot_nt(qlb, c) + _dot(qrb, r[0].astype(BF16)) for c, r in zip(cs, kr_pages)], axis=1)
    m_old = m_sc[...]
    m_new = jnp.maximum(m_old, jnp.max(s, axis=-1, keepdims=True))
    alpha = jnp.exp(m_old - m_new)
    e = jnp.exp(s - m_new).astype(BF16)
    pv = _dot(e[:, :PAGE_SIZE], cs[0])
    for p in range(1, per_step):
        pv = pv + _dot(e[:, p * PAGE_SIZE:(p + 1) * PAGE_SIZE], cs[p])
    l_sc[...] = alpha * l_sc[...] + jnp.sum(e.astype(F32), axis=-1, keepdims=True)
    acc_sc[...] = alpha * acc_sc[...] + pv
    m_sc[...] = m_new

    @pl.when(s_idx == pl.num_programs(1) - 1)
    def _():
        o_ref[0] = acc_sc[...] / l_sc[...]


def _mla_decode(pages, cache_ckv, cache_kr, q_lat, q_rope, c_new, r_new, per_step):
    B, n_pages = pages.shape
    H = MLA_HEADS
    blk3 = lambda b, s, pt: (b, 0, 0)
    page_map = lambda p: functools.partial(lambda b, s, pt, p: (pt[b, s * per_step + p], 0, 0), p=p)
    grid_spec = pltpu.PrefetchScalarGridSpec(
        num_scalar_prefetch=1, grid=(B, n_pages // per_step),
        in_specs=([pl.BlockSpec((1, PAGE_SIZE, KV_LORA), page_map(p)) for p in range(per_step)]
                  + [pl.BlockSpec((1, QK_ROPE, PAGE_SIZE), page_map(p)) for p in range(per_step)]
                  + [pl.BlockSpec((1, H, KV_LORA), blk3), pl.BlockSpec((1, H, QK_ROPE), blk3),
                     pl.BlockSpec((1, 1, KV_LORA), blk3), pl.BlockSpec((1, 1, QK_ROPE), blk3)]),
        out_specs=pl.BlockSpec((1, H, KV_LORA), blk3),
        scratch_shapes=[pltpu.VMEM((H, 1), F32), pltpu.VMEM((H, 1), F32), pltpu.VMEM((H, KV_LORA), F32)])
    return pl.pallas_call(
        functools.partial(_mla_decode_kernel, per_step=per_step),
        out_shape=jax.ShapeDtypeStruct((B, H, KV_LORA), F32), grid_spec=grid_spec,
        compiler_params=_cp("parallel", "arbitrary"), name="mla_decode",
    )(pages, *([cache_ckv] * per_step), *([cache_kr] * per_step), q_lat, q_rope, c_new, r_new)


def _attn_moba_kernel(q_ref, kv_ref, o_ref, k2, vt_sc, mean_sc, mask_sc, m_sc, acc_sc, *, T, tq, tk, nb, n_top):
    i = pl.program_id(2)
    t0 = i * tq
    R = MOBA_HEADS // MOBA_KV_HEADS
    dh = MOBA_HEAD_DIM
    L = R * tq
    nbp = mean_sc.shape[0]
    bpt = tk // MOBA_BLOCK

    @pl.when(i == 0)
    def _():
        kv = kv_ref[...]
        k2[...] = _dup_keys(kv, dh).astype(BF16)
        _store_values_t(vt_sc, 0, _transpose_rows(kv.astype(BF16))[dh:])
        mean_sc[...] = jnp.zeros(mean_sc.shape, F32)
        for j in range(T // MOBA_BLOCK):
            blk_rows = _dup_keys(kv_ref[j * MOBA_BLOCK:(j + 1) * MOBA_BLOCK, :], dh)
            mean_sc[j:j + 1, :] = jnp.sum(blk_rows, axis=0, keepdims=True) / MOBA_BLOCK

    tpos = t0 + _iota((1, L), 1) % tq
    cur = tpos // MOBA_BLOCK
    jb = _iota((nbp, L), 0)
    qs = _stack_heads(q_ref, R, dh, dh ** -0.5)
    block_mask = jnp.where(jb == cur, 1.0, 0.0)
    if n_top > 0:
        gm = jnp.where(jb < cur, _dot_nt(mean_sc[...].astype(BF16), qs), NEG)
        rank = jnp.zeros(gm.shape, F32)
        for c in range(nb):
            row = gm[c:c + 1, :]
            rank = rank + jnp.where(row > gm, 1.0, jnp.where(row == gm, jnp.where(jb > c, 1.0, 0.0), 0.0))
        block_mask = jnp.where(jb < cur, jnp.where(rank < n_top, 1.0, 0.0), block_mask)
    mask_sc[...] = block_mask

    def bias(kt, k0):
        parts = []
        for j in range(bpt):
            picked = mask_sc[pl.ds(kt * bpt + j, 1), :]
            kpos = k0 + j * MOBA_BLOCK + _iota((MOBA_BLOCK, 1), 0)
            parts.append(jnp.where(picked > 0.5, jnp.where(kpos <= tpos, 0.0, NEG), NEG))
        return parts[0] if bpt == 1 else jnp.concatenate(parts, axis=0)

    hi = (t0 + tq + tk - 1) // tk
    acc = _flash(qs, k2, slice(None), vt_sc, 0, m_sc, acc_sc, 0, hi, tk, bias, 1)
    o = acc[ONES_ROWS:] / acc[0:1]
    _write_heads_t(o_ref, [o[:, r * tq:(r + 1) * tq] for r in range(R)])


def _attn_moba(proj, B, T, tk=None):
    KH = MOBA_KV_HEADS
    R = MOBA_HEADS // KH
    dh = MOBA_HEAD_DIM
    tq = min(T, 512)
    tk = tk or min(T, 512)
    nq = T // tq
    nb = (T - 1) // MOBA_BLOCK
    n_top = min(MOBA_TOP, nb)
    qw = MOBA_HEADS * dh // KH
    kv0 = MOBA_HEADS * dh // LANES
    nbp = max(T // MOBA_BLOCK, 8)
    return pl.pallas_call(
        functools.partial(_attn_moba_kernel, T=T, tq=tq, tk=tk, nb=nb, n_top=n_top),
        out_shape=jax.ShapeDtypeStruct((B * T, MOBA_HEADS * dh), F32), grid=(B, KH, nq),
        in_specs=[pl.BlockSpec((tq, qw), lambda b, h, i: (b * nq + i, h)),
                  pl.BlockSpec((T, 2 * dh), lambda b, h, i: (b, kv0 + h))],
        out_specs=pl.BlockSpec((tq, qw), lambda b, h, i: (b * nq + i, h)),
        scratch_shapes=[pltpu.VMEM((T, 2 * dh), BF16), pltpu.VMEM((T // LANES, ONES_ROWS + dh, LANES), BF16),
                        pltpu.VMEM((nbp, 2 * dh), F32), pltpu.VMEM((nbp, R * tq), F32),
                        pltpu.VMEM((1, R * tq), F32), pltpu.VMEM((ONES_ROWS + dh, R * tq), F32)],
        compiler_params=_cp("parallel", "parallel", "arbitrary"), name="attn_moba",
    )(proj, proj)


def _moba_means_kernel(*refs, per_step):
    pages, o_ref = refs[1:1 + per_step], refs[1 + per_step]
    KH, dh = MOBA_KV_HEADS, MOBA_HEAD_DIM
    ppb = MOBA_BLOCK // PAGE_SIZE
    s = pl.program_id(1)

    @pl.when(s == 0)
    def _():
        o_ref[...] = jnp.zeros(o_ref.shape, F32)

    lane = _iota(o_ref.shape[1:], 1)
    out = o_ref[0]
    for j in range(per_step // ppb):
        cols = []
        for kh in range(KH):
            tot = pages[j * ppb][0, kh, 0]
            for p in range(1, ppb):
                tot = tot + pages[j * ppb + p][0, kh, 0]
            cols.append(jnp.sum(tot, axis=1, keepdims=True))
        col = jnp.concatenate(cols, axis=0) / MOBA_BLOCK
        out = jnp.where(lane == s * (per_step // ppb) + j, col, out)
    o_ref[0] = out


def _moba_means(pages, cache_t, per_step):
    B, n_pages = pages.shape
    KH, dh = MOBA_KV_HEADS, MOBA_HEAD_DIM
    nk = KH * dh
    assert n_pages * PAGE_SIZE // MOBA_BLOCK <= LANES
    cache_t = cache_t.reshape(-1, KH, 2, dh, PAGE_SIZE)
    grid_spec = pltpu.PrefetchScalarGridSpec(
        num_scalar_prefetch=1, grid=(B, n_pages // per_step),
        in_specs=[pl.BlockSpec((1, KH, 1, dh, PAGE_SIZE), functools.partial(
            lambda b, s, pt, p: (pt[b, s * per_step + p], 0, 0, 0, 0), p=p)) for p in range(per_step)],
        out_specs=pl.BlockSpec((1, nk, LANES), lambda b, s, pt: (b, 0, 0)))
    return pl.pallas_call(
        functools.partial(_moba_means_kernel, per_step=per_step),
        out_shape=jax.ShapeDtypeStruct((B, nk, LANES), F32), grid_spec=grid_spec,
        compiler_params=_cp("parallel", "arbitrary"), name="moba_means",
    )(pages, *([cache_t] * per_step))


def _moba_dec_score_kernel(q_ref, mean_ref, o_ref, *, nb):
    KH = MOBA_KV_HEADS
    R = MOBA_HEADS // KH
    dh = MOBA_HEAD_DIM
    lane = _iota((R, LANES), 1)
    for kh in range(KH):
        mk = mean_ref[0, kh * dh:(kh + 1) * dh, :].astype(BF16)
        g = _dot(q_ref[0, kh * R:(kh + 1) * R, :].astype(BF16), mk)
        o_ref[0, kh * R:(kh + 1) * R, :] = jnp.where(lane < nb, g, -jnp.inf)


def _moba_dec_score(q3, means, nb):
    B, H, dh = q3.shape
    blk3 = lambda b: (b, 0, 0)
    return pl.pallas_call(
        functools.partial(_moba_dec_score_kernel, nb=nb),
        out_shape=jax.ShapeDtypeStruct((B, H, LANES), F32), grid=(B,),
        in_specs=[pl.BlockSpec((1, H, dh), blk3), pl.BlockSpec((1,) + means.shape[1:], blk3)],
        out_specs=pl.BlockSpec((1, H, LANES), blk3), compiler_params=_cp("parallel"), name="moba_dec_score",
    )(q3, means)


def _moba_dec_attn_kernel(*refs, n_blk):
    R = MOBA_HEADS // MOBA_KV_HEADS
    dh = MOBA_HEAD_DIM
    pages = refs[1:1 + R * n_blk]
    q_ref, new_ref, o_ref = refs[1 + R * n_blk:]
    k_new = new_ref[0, :, :dh]
    v_new = new_ref[0, :, dh:]
    for r in range(R):
        q = q_ref[0, 0, r:r + 1, :] * dh ** -0.5
        kt = jnp.concatenate([pages[r * n_blk + n][0, :dh, :] for n in range(n_blk)], axis=1).astype(BF16)
        vt = jnp.concatenate([pages[r * n_blk + n][0, dh:, :] for n in range(n_blk)], axis=1).astype(BF16)
        s = _dot(q.astype(BF16), kt)
        s_new = jnp.sum(q * k_new, axis=-1, keepdims=True)
        m = jnp.maximum(jnp.max(s, axis=-1, keepdims=True), s_new)
        e = jnp.exp(s - m)
        e_new = jnp.exp(s_new - m)
        o_ref[0, 0, r:r + 1, :] = ((_dot_nt(e.astype(BF16), vt) + e_new * v_new)
                                   / (jnp.sum(e, axis=-1, keepdims=True) + e_new))


def _moba_dec_attn(page_ids, cache_t, q4, new_kv):
    B, KH, R, dh = q4.shape
    n_blk = page_ids.shape[1] // (KH * R)
    page_specs = [pl.BlockSpec((1, 2 * dh, PAGE_SIZE), functools.partial(
        lambda b, h, pg, n: (pg[b, h * R * n_blk + n], h, 0), n=n)) for n in range(R * n_blk)]
    grid_spec = pltpu.PrefetchScalarGridSpec(
        num_scalar_prefetch=1, grid=(B, KH),
        in_specs=page_specs + [pl.BlockSpec((1, 1, R, dh), lambda b, h, pg: (b, h, 0, 0)),
                               pl.BlockSpec((1, 1, 2 * dh), lambda b, h, pg: (b, 0, h))],
        out_specs=pl.BlockSpec((1, 1, R, dh), lambda b, h, pg: (b, h, 0, 0)))
    return pl.pallas_call(
        functools.partial(_moba_dec_attn_kernel, n_blk=n_blk),
        out_shape=jax.ShapeDtypeStruct(q4.shape, F32), grid_spec=grid_spec,
        compiler_params=_cp("parallel", "parallel"), name="moba_dec_attn",
    )(page_ids, *([cache_t] * (R * n_blk)), q4, new_kv)


def _router_kernel(x_ref, w_ref, b_ref, o_ref):
    s = jax.nn.sigmoid(lax.dot_general(w_ref[...], x_ref[...], (((1,), (1,)), ((), ())),
                                       preferred_element_type=F32, precision=lax.Precision.HIGHEST))
    row = _iota(s.shape, 0)
    x = s + b_ref[...]
    chosen = jnp.zeros(s.shape, F32)
    for _ in range(TOP_K):
        m = jnp.max(x, axis=0, keepdims=True)
        idx = jnp.min(jnp.where(x == m, row, N_EXPERTS), axis=0, keepdims=True)
        hit = row == idx
        chosen = jnp.where(hit, 1.0, chosen)
        x = jnp.where(hit, -jnp.inf, x)
    w = chosen * s
    w = w / jnp.sum(w, axis=0, keepdims=True) * ROUTED_SCALE
    o_ref[...] = jnp.concatenate([w, jnp.zeros((LANES - N_EXPERTS, w.shape[1]), F32)], axis=0).T


def _moe_router(x, w_router_t, b_router):
    M, K = x.shape
    tm = min(M, 512)
    E = w_router_t.shape[0]
    row = lambda i: (i, 0)
    fix = lambda i: (0, 0)
    return pl.pallas_call(
        _router_kernel, out_shape=jax.ShapeDtypeStruct((M, LANES), F32), grid=(M // tm,),
        in_specs=[pl.BlockSpec((tm, K), row), pl.BlockSpec((E, K), fix), pl.BlockSpec((E, 1), fix)],
        out_specs=pl.BlockSpec((tm, LANES), row), compiler_params=_cp("parallel"), name="moe_router",
    )(x, w_router_t, b_router)


def _moe_kernel(x_ref, gate_ref, wg_ref, wu_ref, wd_ref, wsg_ref, wsu_ref, wsd_ref, g_ref, b_ref, o_ref,
                xb_sc, acc_sc):
    e = pl.program_id(1)
    nh = MOE_GROUP * D_EXPERT

    @pl.when(e == 0)
    def _():
        xb = x_ref[...].astype(BF16)
        xb_sc[...] = xb
        hs = _silu(_dot(xb, wsg_ref[0].astype(BF16))) * _dot(xb, wsu_ref[0].astype(BF16))
        acc_sc[...] = _dot(hs.astype(BF16), wsd_ref[0].astype(BF16))

    xb = xb_sc[...]
    w_gate = jnp.concatenate([wg_ref[0, k].astype(BF16) for k in range(MOE_GROUP)], axis=1)
    w_up = jnp.concatenate([wu_ref[0, k].astype(BF16) for k in range(MOE_GROUP)], axis=1)
    gate = gate_ref[...]
    lane = _iota(gate.shape, 1)
    act = _silu(_dot(xb, w_gate)) * _dot(xb, w_up)
    h = jnp.concatenate(
        [act[:, k * D_EXPERT:(k + 1) * D_EXPERT]
         * jnp.sum(jnp.where(lane == MOE_GROUP * e + k, gate, 0.0), axis=1, keepdims=True)
         for k in range(MOE_GROUP)], axis=1)
    w_down = jnp.concatenate([wd_ref[0, k].astype(BF16) for k in range(MOE_GROUP)], axis=0)
    acc_sc[...] += _dot(h.astype(BF16), w_down)

    @pl.when(e == pl.num_programs(1) - 1)
    def _():
        o_ref[...] = _layer_norm(ALPHA * x_ref[...] + acc_sc[...], g_ref[...], b_ref[...])


def _moe(x, gate, w, g, b):
    M, D = x.shape
    tm = min(M, 1024)
    layer = w['layer']
    E, _, F = w['w_gate'].shape[1:]
    n = MOE_GROUP
    row = lambda i, e: (i, 0)
    fix = lambda i, e: (0, 0)
    grp = lambda i, e: (layer, e, 0, 0)
    lay = lambda i, e: (layer, 0, 0)
    return pl.pallas_call(
        _moe_kernel, out_shape=jax.ShapeDtypeStruct((M, D), F32), grid=(M // tm, E // n),
        in_specs=[pl.BlockSpec((tm, D), row), pl.BlockSpec((tm, LANES), row),
                  pl.BlockSpec((1, n, D, F), grp), pl.BlockSpec((1, n, D, F), grp), pl.BlockSpec((1, n, F, D), grp),
                  pl.BlockSpec((1, D, D_SHARED), lay), pl.BlockSpec((1, D, D_SHARED), lay),
                  pl.BlockSpec((1, D_SHARED, D), lay),
                  pl.BlockSpec((1, D), fix), pl.BlockSpec((1, D), fix)],
        out_specs=pl.BlockSpec((tm, D), row),
        scratch_shapes=[pltpu.VMEM((tm, D), BF16), pltpu.VMEM((tm, D), F32)],
        compiler_params=_cp("parallel", "arbitrary"), name="moe",
    )(x, gate, w['w_gate'], w['w_up'], w['w_down'], w['ws_gate'], w['ws_up'], w['ws_down'],
      g.reshape(1, D), b.reshape(1, D))


def _pad_cols(w, n):
    return jnp.pad(w, ((0, 0), (0, n - w.shape[1])))


def _block_diag(blocks):
    n, a, b = blocks.shape
    eye = jnp.eye(n, dtype=blocks.dtype)
    return (eye[:, None, :, None] * blocks[:, :, None, :]).reshape(n * a, n * b)


def _nsa_weights(w_in, cmp_pe, cmp_w1, cmp_b1, cmp_w2, cmp_b2, w_o):
    H, G, dh = NSA_HEADS, NSA_KV_HEADS, NSA_HEAD_DIM
    R = H // G
    nq, nkv = H * dh, 6 * G * dh
    wq, wkv = w_in[:, :nq], w_in[:, nq:nq + nkv]
    wg = w_in[:, nq + nkv:].reshape(-1, 3, G, R).transpose(0, 2, 1, 3).reshape(-1, G, 3 * R)
    wg = jnp.pad(wg, ((0, 0), (0, 0), (0, LANES - 3 * R))).reshape(-1, G * LANES)
    w_ext = jnp.concatenate([wq, wq, wkv, wg], axis=1).astype(BF16)
    w1 = cmp_w1.reshape(2, CMP_STRIDE, 2, dh, CMP_HIDDEN)
    eye_k = jnp.eye(2, dtype=F32)
    w_lohi = (w1.transpose(1, 2, 3, 0, 4)[:, :, :, :, None, :] * eye_k[None, :, None, None, :, None])
    w_lohi = w_lohi.reshape(CMP_STRIDE // 2, 2 * 2 * dh, 2 * 2 * CMP_HIDDEN).astype(BF16)
    pe_rows = jnp.broadcast_to(cmp_pe[:, None], (CMP_BLOCK, G, 2, dh)).reshape(1, CMP_BLOCK, G * 2 * dh)
    b1 = jnp.broadcast_to(cmp_b1[None], (G, 2, CMP_HIDDEN)).reshape(1, -1)
    b2 = jnp.broadcast_to(cmp_b2[None], (G, 2, dh)).reshape(1, -1)
    w2 = _block_diag(jnp.tile(cmp_w2, (G, 1, 1))).astype(BF16)
    return dict(w_ext=w_ext, w_lohi=w_lohi, pe_rows=pe_rows, b1=b1, b2=b2, w2=w2, w_o=w_o.astype(BF16))


def _nsa_tables(pos):
    H, G = NSA_HEADS, NSA_KV_HEADS
    dh = NSA_HEAD_DIM
    kv_rot = [(1.0, dh), (0.0, dh)] * G
    layout = [(0.0, H * dh), (1.0, H * dh), (0.0, 2 * G * dh)] + kv_rot + kv_rot + [(0.0, G * LANES)]
    return _unit_tables(pos, dh // 2) + (_column_mask(layout),)


def _mla_weights(w_dn, g_q, w_uq, g_kv, w_uk, w_uv, w_o):
    H = MLA_HEADS
    pad = MLA_SLOT - QK_NOPE - QK_ROPE
    w_dn_p = _pad_cols(w_dn, Q_LORA + KV_LORA + LANES).astype(BF16)
    wq = jnp.pad(w_uq.reshape(Q_LORA, H, QK_NOPE + QK_ROPE), ((0, 0), (0, 0), (0, pad)))
    w_q = wq.reshape(Q_LORA, H * MLA_SLOT).astype(BF16)
    wk_c = jnp.pad(w_uk, ((0, 0), (0, 0), (0, MLA_SLOT - QK_NOPE))).reshape(KV_LORA, H * MLA_SLOT)
    eye = jnp.pad(jnp.eye(QK_ROPE, dtype=F32), ((0, 0), (QK_NOPE, pad)))
    wk_r = jnp.tile(eye, (1, H))
    wk = jnp.concatenate([wk_c, wk_r, jnp.zeros((LANES - QK_ROPE, H * MLA_SLOT), F32)], axis=0)
    wv = jnp.pad(w_uv.reshape(KV_LORA, H * V_DIM), ((0, LANES), (0, 0)))
    w_kv = jnp.concatenate([wk, wv], axis=1).astype(BF16)
    absorb = jnp.pad(w_uk.transpose(1, 2, 0), ((0, 0), (0, MLA_SLOT - QK_NOPE), (0, 0)))
    w_absorb = _block_diag(absorb).astype(BF16)
    w_unabsorb = _block_diag(w_uv.transpose(1, 0, 2)).astype(BF16)
    return dict(w_dn=w_dn_p, g_q=g_q, g_kv=g_kv, w_q=w_q, w_kv=w_kv, w_absorb=w_absorb,
                w_unabsorb=w_unabsorb, w_o=w_o.astype(BF16))


def _mla_q_tables(pos):
    half = QK_ROPE // 2
    ua, ub = _rope_unit(pos, half)
    T = pos.shape[0]
    pad = MLA_SLOT - QK_NOPE - QK_ROPE
    a = jnp.concatenate([jnp.ones((T, QK_NOPE), F32), ua, jnp.ones((T, pad), F32)], axis=1)
    b = jnp.concatenate([jnp.zeros((T, QK_NOPE), F32), ub, jnp.zeros((T, pad), F32)], axis=1)
    return a, b, jnp.ones((1, MLA_HEADS * MLA_SLOT), F32)


def _mla_kr_tables(pos):
    return _tables(pos, QK_ROPE // 2, [('r', 1), ('n', LANES - QK_ROPE)])


def _moba_weights(w_in, w_o):
    return dict(w_in=w_in.astype(BF16), w_o=w_o.astype(BF16))


def _moba_tables(pos):
    dh = MOBA_HEAD_DIM
    layout = [(1.0, MOBA_HEADS * dh)] + [(1.0, dh), (0.0, dh)] * MOBA_KV_HEADS
    return _unit_tables(pos, dh // 2) + (_column_mask(layout),)


def _moe_weights(w_router, b_router, w_gate, w_up, w_down, ws_gate, ws_up, ws_down, layer=None):
    stacked = (w_router, b_router, w_gate, w_up, w_down, ws_gate, ws_up, ws_down)
    if layer is None:
        stacked, layer = tuple(a[None] for a in stacked), 0
    w_router, b_router, w_gate, w_up, w_down, ws_gate, ws_up, ws_down = stacked
    E = w_gate.shape[1]
    return dict(layer=layer, w_router=w_router[layer].T, b_router=b_router[layer].reshape(E, 1),
                w_gate=w_gate, w_up=w_up, w_down=w_down, ws_gate=ws_gate, ws_up=ws_up, ws_down=ws_down)


def _feature_major(cache):
    n, pool, rows = cache.shape[:3]
    nd = cache.ndim
    return cache.transpose(0, 1, *range(3, nd), 2).reshape(n * pool, -1, rows)


def _nsa_cmp_tokens(rows_lohi, w):
    pe_lohi = _cmp_lohi(w['pe_rows'], w['w_lohi'])
    return _cmp_combine(rows_lohi, pe_lohi, w['b1'], w['w2'], w['b2'])


def _nsa_prompt(h, B, T, w, tabs):
    G, dh = NSA_KV_HEADS, NSA_HEAD_DIM
    proj = _proj(h, w['w_ext'], tabs, dh // 2)
    kv0 = 2 * NSA_HEADS * dh
    width = 2 * G * dh
    lohi = _cmp_lohi(proj.reshape(B, T, -1), w['w_lohi'], col_block=kv0 // LANES)
    cmp_tok = _nsa_cmp_tokens(lohi, w)
    y = _attn_nsa(proj, cmp_tok, B, T)
    kv = proj[:, kv0:kv0 + 3 * width].reshape(B, T, 3, G, 2, dh)
    return y, (kv[:, :, 0], kv[:, :, 1], kv[:, -min(WINDOW, T):, 2])


def _nsa_sample(h, past_len, w, tabs, cache_cmp, cache_sel, win_state, slot, page_table):
    B = h.shape[0]
    G, H, dh = NSA_KV_HEADS, NSA_HEADS, NSA_HEAD_DIM
    R = H // G
    n_pool = cache_cmp.shape[1]
    width = 2 * G * dh
    proj = _proj(h, w['w_ext'], tabs, dh // 2)
    pages = page_table + slot * n_pool
    lohi = _cmp_lohi_paged(_feature_major(cache_cmp), pages, w['w_lohi'], per_step=min(32, page_table.shape[1]))
    cmp_tok = _nsa_cmp_tokens(lohi, w)
    n_sel = -(-(past_len + 1) // SEL_BLOCK)
    n_top = min(SEL_TOP, n_sel)
    q3 = proj[:, :H * dh].reshape(B, H, dh)
    qr3 = proj[:, H * dh:2 * H * dh].reshape(B, H, dh)
    o_cmp, imp = _nsa_dec_cmp(q3, cmp_tok, past_len, n_sel)
    idx = _topk_idx(imp.reshape(B * G, -1), n_top)[:, :n_top].reshape(B, G * n_top)
    n_past_blocks = past_len // SEL_BLOCK
    per_page = PAGE_SIZE // SEL_BLOCK
    sel_pages = jnp.take_along_axis(pages, jnp.minimum(idx, n_past_blocks - 1) // per_page, axis=1)
    kv0 = 2 * H * dh
    new_kv = proj[:, kv0:kv0 + 3 * width].reshape(B, 1, 3 * width)
    gl3 = proj[:, kv0 + 3 * width:].reshape(B, G, LANES)[:, :, :3 * R].reshape(B, G, 3, R)
    gl3 = gl3.transpose(0, 1, 3, 2).reshape(B, H, 3)
    Wn = win_state.shape[2]
    o = _nsa_dec_attn(idx, sel_pages, _feature_major(cache_sel), qr3, new_kv, _feature_major(win_state), slot * B,
                      o_cmp, gl3, n_past_blocks)
    kv = new_kv.reshape(B, 1, 3, G, 2, dh)
    new_win = jnp.concatenate([win_state[slot], kv[:, :, 2]], axis=1)[:, -Wn:]
    return o.reshape(B, H * dh), (kv[:, :, 0], kv[:, :, 1], new_win)


def _mla_front(h, w, q_tabs, kr_tabs):
    dn = _proj(h, w['w_dn'])
    c_q, ckr = _mla_post(dn, w['g_q'], w['g_kv'], kr_tabs)
    q_ext = _proj(c_q, w['w_q'], q_tabs, QK_ROPE // 2)
    return q_ext, ckr


def _mla_prompt(h, B, T, w, q_tabs, kr_tabs):
    q_ext, ckr = _mla_front(h, w, q_tabs, kr_tabs)
    kv_ext = _proj(ckr, w['w_kv'])
    y = _attn_mla(q_ext, kv_ext, B, T)
    return y, (ckr[:, :KV_LORA].reshape(B, T, KV_LORA), ckr[:, KV_LORA:KV_LORA + QK_ROPE].reshape(B, T, QK_ROPE))


def _mla_sample(h, w, q_tabs, kr_tabs, cache_ckv, cache_kr, slot, page_table):
    B = h.shape[0]
    H = MLA_HEADS
    n_pool = cache_ckv.shape[1]
    q_ext, ckr = _mla_front(h, w, q_tabs, kr_tabs)
    q_lat = _proj(q_ext, w['w_absorb']).reshape(B, H, KV_LORA)
    q_rope = q_ext.reshape(B, H, MLA_SLOT)[:, :, QK_NOPE:QK_NOPE + QK_ROPE]
    c_new = ckr[:, :KV_LORA].reshape(B, 1, KV_LORA)
    r_new = ckr[:, KV_LORA:KV_LORA + QK_ROPE].reshape(B, 1, QK_ROPE)
    pages = page_table + slot * n_pool
    o_lat = _mla_decode(pages, cache_ckv.reshape(-1, PAGE_SIZE, KV_LORA), _feature_major(cache_kr),
                        q_lat, q_rope, c_new, r_new, per_step=min(32, page_table.shape[1]))
    y = _proj(o_lat.reshape(B, H * KV_LORA), w['w_unabsorb'])
    return y, (c_new, r_new)


def _moba_prompt(h, B, T, w, tabs):
    KH, dh = MOBA_KV_HEADS, MOBA_HEAD_DIM
    proj = _proj(h, w['w_in'], tabs, dh // 2)
    y = _attn_moba(proj, B, T)
    return y, proj[:, MOBA_HEADS * dh:].reshape(B, T, KH, 2, dh)


def _moba_sample(h, past_len, w, tabs, cache, slot, page_table):
    B = h.shape[0]
    H, KH, dh = MOBA_HEADS, MOBA_KV_HEADS, MOBA_HEAD_DIM
    R = H // KH
    assert past_len % MOBA_BLOCK == 0 and past_len // MOBA_BLOCK >= MOBA_TOP
    n_pool = cache.shape[1]
    width = KH * 2 * dh
    ppb = MOBA_BLOCK // PAGE_SIZE
    proj = _proj(h, w['w_in'], tabs, dh // 2)
    pages = page_table + slot * n_pool
    cache_t = _feature_major(cache)
    means = _moba_means(pages, cache_t, per_step=min(16, page_table.shape[1]))
    q3 = proj[:, :H * dh].reshape(B, H, dh)
    scores = _moba_dec_score(q3, means, past_len // MOBA_BLOCK)
    idx = _topk_idx(scores.reshape(B * H, -1), MOBA_TOP)[:, :MOBA_TOP].reshape(B, H * MOBA_TOP)
    page_ids = jnp.take_along_axis(pages, (idx[:, :, None] * ppb + jnp.arange(ppb)).reshape(B, -1), axis=1)
    new_kv = proj[:, H * dh:].reshape(B, 1, width)
    o = _moba_dec_attn(page_ids, cache_t, q3.reshape(B, KH, R, dh), new_kv)
    return o.reshape(B, H * dh), new_kv.reshape(B, 1, KH, 2, dh)


def _moe_layer(h, w, g, b):
    gate = _moe_router(h, w['w_router'], w['b_router'])
    return _moe(h, gate, w, g, b)


def kernel(x_prompt, x_sample, cache_nsa_cmp, cache_nsa_sel, state_nsa_win, cache_mla_ckv, cache_mla_krope,
           cache_moba_kv, page_table, nsa_w_in, nsa_cmp_pe, nsa_cmp_w1, nsa_cmp_b1, nsa_cmp_w2, nsa_cmp_b2,
           nsa_w_o, mla_w_dn, mla_g_q, mla_w_uq, mla_g_kv, mla_w_uk, mla_w_uv, mla_w_o, moba_w_in, moba_w_o,
           ln1_g, ln1_b, ln2_g, ln2_b, moe_w_router, moe_b_router, moe_w_gate, moe_w_up, moe_w_down,
           moe_ws_gate, moe_ws_up, moe_ws_down):
    B, T, D = x_prompt.shape
    Bs, Ts, _ = x_sample.shape
    assert Ts == 1
    past_len = page_table.shape[1] * PAGE_SIZE
    assert state_nsa_win.shape[2] == WINDOW and past_len >= WINDOW
    pos_p = jnp.arange(T, dtype=I32)
    pos_s = jnp.full((Bs,), past_len, dtype=I32)
    hp = x_prompt.reshape(B * T, D)
    hs = x_sample.reshape(Bs, D)
    outs = {k: [] for k in ('cmp_p', 'cmp_s', 'sel_p', 'sel_s', 'win_p', 'win_s',
                            'ckv_p', 'ckv_s', 'kr_p', 'kr_s', 'mb_p', 'mb_s')}
    for i in range(DEPTH):
        kind, slot = i % N_MIXERS, i // N_MIXERS
        if kind == MIX_NSA:
            w = _nsa_weights(nsa_w_in[slot], nsa_cmp_pe[slot], nsa_cmp_w1[slot], nsa_cmp_b1[slot],
                             nsa_cmp_w2[slot], nsa_cmp_b2[slot], nsa_w_o[slot])
            yp, (a_p, b_p, c_p) = _nsa_prompt(hp, B, T, w, _nsa_tables(pos_p))
            ys, (a_s, b_s, c_s) = _nsa_sample(hs, past_len, w, _nsa_tables(pos_s), cache_nsa_cmp, cache_nsa_sel,
                                              state_nsa_win, slot, page_table)
            outs['cmp_p'].append(a_p); outs['cmp_s'].append(a_s)
            outs['sel_p'].append(b_p); outs['sel_s'].append(b_s)
            outs['win_p'].append(c_p); outs['win_s'].append(c_s)
        elif kind == MIX_MLA:
            w = _mla_weights(mla_w_dn[slot], mla_g_q[slot], mla_w_uq[slot], mla_g_kv[slot], mla_w_uk[slot],
                             mla_w_uv[slot], mla_w_o[slot])
            yp, (a_p, b_p) = _mla_prompt(hp, B, T, w, _mla_q_tables(pos_p), _mla_kr_tables(pos_p))
            ys, (a_s, b_s) = _mla_sample(hs, w, _mla_q_tables(pos_s), _mla_kr_tables(pos_s), cache_mla_ckv,
                                         cache_mla_krope, slot, page_table)
            outs['ckv_p'].append(a_p); outs['ckv_s'].append(a_s)
            outs['kr_p'].append(b_p); outs['kr_s'].append(b_s)
        else:
            w = _moba_weights(moba_w_in[slot], moba_w_o[slot])
            yp, a_p = _moba_prompt(hp, B, T, w, _moba_tables(pos_p))
            ys, a_s = _moba_sample(hs, past_len, w, _moba_tables(pos_s), cache_moba_kv, slot, page_table)
            outs['mb_p'].append(a_p); outs['mb_s'].append(a_s)
        hp = _proj_ln(yp, w['w_o'], hp, ln1_g[i], ln1_b[i])
        hs = _proj_ln(ys, w['w_o'], hs, ln1_g[i], ln1_b[i])
        mw = _moe_weights(moe_w_router, moe_b_router, moe_w_gate, moe_w_up, moe_w_down,
                          moe_ws_gate, moe_ws_up, moe_ws_down, layer=i)
        hp = _moe_layer(hp, mw, ln2_g[i], ln2_b[i])
        hs = _moe_layer(hs, mw, ln2_g[i], ln2_b[i])
    st = lambda k: jnp.stack(outs[k])
    return (hp.reshape(B, T, D), hs.reshape(Bs, Ts, D),
            st('cmp_p'), st('cmp_s'), st('sel_p'), st('sel_s'), st('win_p'), st('win_s'),
            st('ckv_p'), st('ckv_s'), st('kr_p'), st('kr_s'), st('mb_p'), st('mb_s'))
```

```python
import functools

import jax
import jax.numpy as jnp
from jax import lax
from jax.experimental import pallas as pl
from jax.experimental.pallas import tpu as pltpu

D_MODEL = 1024
DEPTH = 4
PAGE_SIZE = 128
N_MIXERS = 3
MIX_NSA, MIX_MLA, MIX_MOBA = 0, 1, 2

ALPHA = (2 * DEPTH) ** 0.25
LN_EPS = 1e-5
RMS_EPS = 1e-6
ROPE_THETA = 10000.0
NEG = -1e30
FORCE = 1e9

NSA_HEADS = 16
NSA_KV_HEADS = 2
NSA_HEAD_DIM = 64
CMP_BLOCK = 32
CMP_STRIDE = 16
CMP_HIDDEN = 128
SEL_BLOCK = 64
SEL_RATIO = SEL_BLOCK // CMP_STRIDE
SEL_TOP = 16
WINDOW = 512

MLA_HEADS = 16
Q_LORA = 384
KV_LORA = 256
QK_NOPE = 64
QK_ROPE = 32
V_DIM = 64
MLA_SLOT = 128

MOBA_HEADS = 16
MOBA_KV_HEADS = 4
MOBA_HEAD_DIM = 64
MOBA_BLOCK = 256
MOBA_TOP = 3

N_EXPERTS = 64
TOP_K = 8
D_EXPERT = 128
D_SHARED = 128
ROUTED_SCALE = 2.5

LANES = 128
FLASH_CHAINS = 1
MOE_GROUP = 4
ONES_ROWS = 16
VMEM_LIMIT = 48 * 1024 * 1024

F32 = jnp.float32
BF16 = jnp.bfloat16
I32 = jnp.int32


def _cp(*sem):
    return pltpu.CompilerParams(dimension_semantics=sem, vmem_limit_bytes=VMEM_LIMIT)


def _dot(a, b):
    return jnp.dot(a, b, preferred_element_type=F32)


def _dot_nt(a, b):
    return lax.dot_general(a, b, (((1,), (1,)), ((), ())), preferred_element_type=F32)


def _dot_exact(a, b):
    return jnp.dot(a, b, preferred_element_type=F32, precision=lax.Precision.HIGHEST)


def _iota(shape, axis):
    return lax.broadcasted_iota(I32, shape, axis)


def _layer_norm(z, g, b):
    mu = jnp.mean(z, axis=-1, keepdims=True)
    d = z - mu
    var = jnp.mean(d * d, axis=-1, keepdims=True)
    return d * lax.rsqrt(var + LN_EPS) * g + b


def _silu(x):
    return x * jax.nn.sigmoid(x)


def _softmax_rows(s, valid):
    s = jnp.where(valid, s, NEG)
    m = jnp.max(s, axis=-1, keepdims=True)
    e = jnp.where(valid, jnp.exp(s - m), 0.0)
    l = jnp.sum(e, axis=-1, keepdims=True)
    return e / jnp.where(l > 0.0, l, 1.0)


def _rank_before(x, n_cols, lane):
    rank = jnp.zeros(x.shape, F32)
    for c in range(n_cols):
        col = x[:, c:c + 1]
        rank = rank + jnp.where(col > x, 1.0, jnp.where(col == x, jnp.where(lane > c, 1.0, 0.0), 0.0))
    return rank


def _proj_kernel(x_ref, w_ref, o_ref):
    o_ref[...] = _dot(x_ref[...].astype(BF16), w_ref[...])


def _proj_rope_kernel(x_ref, w_ref, a_ref, b_ref, mask_ref, o_ref, *, half, tn):
    acc = _dot(x_ref[...].astype(BF16), w_ref[...])
    lane = _iota(acc.shape, 1)
    first = (lane % (2 * half)) < half
    partner = jnp.where(first, pltpu.roll(acc, tn - half, 1), pltpu.roll(acc, half, 1))
    rotary = mask_ref[...] > 0.5
    a = jnp.where(rotary, jnp.concatenate([a_ref[...]] * (tn // LANES), axis=1), 1.0)
    b = jnp.where(rotary, jnp.concatenate([b_ref[...]] * (tn // LANES), axis=1), 0.0)
    o_ref[...] = acc * a + partner * b


def _col_tile(n):
    return next(t for t in (512, 384, 256, 128) if n % t == 0)


def _proj(x, w, tabs=None, half=0):
    M, K = x.shape
    N = w.shape[1]
    tm = min(M, 1024)
    tn = _col_tile(N)
    grid = (M // tm, N // tn)
    x_spec = pl.BlockSpec((tm, K), lambda i, j: (i, 0))
    w_spec = pl.BlockSpec((K, tn), lambda i, j: (0, j))
    o_spec = pl.BlockSpec((tm, tn), lambda i, j: (i, j))
    out_shape = jax.ShapeDtypeStruct((M, N), F32)
    if tabs is None:
        return pl.pallas_call(_proj_kernel, out_shape=out_shape, grid=grid, in_specs=[x_spec, w_spec],
                              out_specs=o_spec, compiler_params=_cp("parallel", "parallel"), name="proj")(x, w)
    a, bm, mask = tabs
    nrb = a.shape[0] // tm
    t_spec = pl.BlockSpec((tm, LANES), lambda i, j: (i % nrb, 0))
    m_spec = pl.BlockSpec((1, tn), lambda i, j: (0, j))
    return pl.pallas_call(functools.partial(_proj_rope_kernel, half=half, tn=tn), out_shape=out_shape, grid=grid,
                          in_specs=[x_spec, w_spec, t_spec, t_spec, m_spec], out_specs=o_spec,
                          compiler_params=_cp("parallel", "parallel"), name="proj_rope")(x, w, a, bm, mask)


def _proj_ln_kernel(x_ref, w_ref, r_ref, g_ref, b_ref, o_ref):
    y = _dot(x_ref[...].astype(BF16), w_ref[...])
    o_ref[...] = _layer_norm(ALPHA * r_ref[...] + y, g_ref[...], b_ref[...])


def _proj_ln(x, w, res, g, b):
    M, K = x.shape
    N = w.shape[1]
    tm = min(M, 512)
    row = lambda i: (i, 0)
    fix = lambda i: (0, 0)
    return pl.pallas_call(
        _proj_ln_kernel, out_shape=jax.ShapeDtypeStruct((M, N), F32), grid=(M // tm,),
        in_specs=[pl.BlockSpec((tm, K), row), pl.BlockSpec((K, N), fix), pl.BlockSpec((tm, N), row),
                  pl.BlockSpec((1, N), fix), pl.BlockSpec((1, N), fix)],
        out_specs=pl.BlockSpec((tm, N), row), compiler_params=_cp("parallel"), name="proj_ln",
    )(x, w, res, g.reshape(1, N), b.reshape(1, N))


def _rope_unit(pos, half):
    inv = jnp.power(ROPE_THETA, -jnp.arange(half, dtype=F32) / half)
    ang = pos.astype(F32)[:, None] * inv
    cos, sin = jnp.cos(ang), jnp.sin(ang)
    return jnp.concatenate([cos, cos], axis=1), jnp.concatenate([-sin, sin], axis=1)


def _unit_tables(pos, half):
    ua, ub = _rope_unit(pos, half)
    reps = LANES // (2 * half)
    return jnp.tile(ua, (1, reps)), jnp.tile(ub, (1, reps))


def _column_mask(layout):
    return jnp.concatenate([jnp.full((1, n), v, F32) for v, n in layout], axis=1)


def _tables(pos, half, layout):
    ua, ub = _rope_unit(pos, half)
    T = pos.shape[0]
    a_parts, b_parts = [], []
    for kind, n in layout:
        if kind == 'r':
            a_parts.append(jnp.tile(ua, (1, n)))
            b_parts.append(jnp.tile(ub, (1, n)))
        else:
            a_parts.append(jnp.ones((T, n), F32))
            b_parts.append(jnp.zeros((T, n), F32))
    return jnp.concatenate(a_parts, axis=1), jnp.concatenate(b_parts, axis=1)


def _add_bias(s, bias, heads):
    if heads == 1:
        return s + bias
    tq = s.shape[1] // heads
    return jnp.concatenate([s[:, h * tq:(h + 1) * tq] + bias for h in range(heads)], axis=1)


def _weighted_values(vt_ref, first, e):
    acc = _dot(vt_ref[first], e[:LANES])
    for j in range(1, e.shape[0] // LANES):
        acc = acc + _dot(vt_ref[first + j], e[j * LANES:(j + 1) * LANES])
    return acc


def _flash(qs, k_ref, k_cols, vt_ref, vt_base, m_sc, acc_sc, lo, hi, tk, bias_fn, heads):
    m_sc[...] = jnp.full(m_sc.shape, NEG, F32)
    acc_sc[...] = jnp.zeros(acc_sc.shape, F32)
    sub = tk // LANES

    def body(kt, carry):
        k0 = pl.multiple_of(kt * tk, tk)
        s = _add_bias(_dot_nt(k_ref[pl.ds(k0, tk), k_cols], qs), bias_fn(kt, k0), heads)
        m_old = m_sc[...]
        m_new = jnp.maximum(m_old, jnp.max(s, axis=0, keepdims=True))
        e = jnp.exp(s - m_new).astype(BF16)
        acc_sc[...] = jnp.exp(m_old - m_new) * acc_sc[...] + _weighted_values(vt_ref, vt_base + kt * sub, e)
        m_sc[...] = m_new
        return carry

    lax.fori_loop(lo, hi, body, 0)
    return acc_sc[...]


def _transpose_rows(x):
    eye = jnp.where(_iota((LANES, LANES), 0) == _iota((LANES, LANES), 1), 1.0, 0.0).astype(BF16)
    return _dot_nt(eye, x)


def _store_values_t(vt_ref, base, vt):
    n = vt.shape[1]
    full = jnp.concatenate([jnp.ones((ONES_ROWS, n), F32), vt], axis=0).astype(BF16)
    for j in range(n // LANES):
        vt_ref[base + j] = full[:, j * LANES:(j + 1) * LANES]


def _write_heads_t(o_ref, heads):
    tq = heads[0].shape[1]
    for p in range(len(heads) // 2):
        pair = jnp.concatenate([heads[2 * p], heads[2 * p + 1]], axis=0)
        for c in range(tq // LANES):
            o_ref[c * LANES:(c + 1) * LANES, p * LANES:(p + 1) * LANES] = pair[:, c * LANES:(c + 1) * LANES].T


def _stack_heads(ref, n_heads, dh, scale):
    parts = []
    for r in range(n_heads):
        pair = ref[:, (r // 2) * 2 * dh:(r // 2 + 1) * 2 * dh] * scale
        lane = _iota(pair.shape, 1)
        own = (lane < dh) if r % 2 == 0 else (lane >= dh)
        parts.append(jnp.where(own, pair, 0.0))
    return jnp.concatenate(parts, axis=0).astype(BF16)


def _dup_keys(kv, dh):
    return jnp.where(_iota(kv.shape, 1) < dh, kv, pltpu.roll(kv, dh, 1))


def _cmp_lohi_kernel(*refs, n_pages, rows_per_page, feature_major):
    G = NSA_KV_HEADS
    if feature_major:
        refs, x_sc = refs[1:-1], refs[-1]
    n_in = n_pages if feature_major else G
    row_refs, w_ref, o_ref = refs[:n_in], refs[n_in], refs[n_in + 1]
    cpp = rows_per_page // CMP_STRIDE
    nh = w_ref.shape[2] // 2

    def chunk_rows(g, r):
        if feature_major:
            return x_sc[pl.ds(r, n_pages * cpp, stride=CMP_STRIDE), :]
        return row_refs[g][0, pl.ds(r, cpp, stride=CMP_STRIDE), :]

    for g in range(G):
        if feature_major:
            for p in range(n_pages):
                x_sc[p * rows_per_page:(p + 1) * rows_per_page, :] = row_refs[p][0, g * LANES:(g + 1) * LANES, :].T
        acc = jnp.zeros((n_pages * cpp, 2 * nh), F32)
        for r2 in range(CMP_STRIDE // 2):
            xr = jnp.concatenate([chunk_rows(g, 2 * r2), chunk_rows(g, 2 * r2 + 1)], axis=1)
            acc = acc + _dot(xr.astype(BF16), w_ref[r2])
        o_ref[0, :, g * nh:(g + 1) * nh] = acc[:, :nh]
        o_ref[0, :, (G + g) * nh:(G + g + 1) * nh] = acc[:, nh:]


def _cmp_lohi(rows, w_lohi, col_block=0):
    B, L = rows.shape[0], rows.shape[1]
    G = NSA_KV_HEADS
    nout = G * w_lohi.shape[2]
    row_specs = [pl.BlockSpec((1, L, LANES), functools.partial(lambda b, g: (b, 0, col_block + g), g=g))
                 for g in range(G)]
    return pl.pallas_call(
        functools.partial(_cmp_lohi_kernel, n_pages=1, rows_per_page=L, feature_major=False),
        out_shape=jax.ShapeDtypeStruct((B, L // CMP_STRIDE, nout), F32), grid=(B,),
        in_specs=row_specs + [pl.BlockSpec(w_lohi.shape, lambda b: (0, 0, 0))],
        out_specs=pl.BlockSpec((1, L // CMP_STRIDE, nout), lambda b: (b, 0, 0)),
        compiler_params=_cp("parallel"), name="cmp_lohi",
    )(*([rows] * G), w_lohi)


def _cmp_lohi_paged(cache_t, pages, w_lohi, per_step):
    B, n_pages = pages.shape
    G = NSA_KV_HEADS
    nout = G * w_lohi.shape[2]
    cpp = PAGE_SIZE // CMP_STRIDE
    page_specs = [pl.BlockSpec((1, G * LANES, PAGE_SIZE), functools.partial(
        lambda b, s, pt, p: (pt[b, s * per_step + p], 0, 0), p=p)) for p in range(per_step)]
    grid_spec = pltpu.PrefetchScalarGridSpec(
        num_scalar_prefetch=1, grid=(B, n_pages // per_step),
        in_specs=page_specs + [pl.BlockSpec(w_lohi.shape, lambda b, s, pt: (0, 0, 0))],
        out_specs=pl.BlockSpec((1, per_step * cpp, nout), lambda b, s, pt: (b, s, 0)),
        scratch_shapes=[pltpu.VMEM((per_step * PAGE_SIZE, LANES), F32)])
    return pl.pallas_call(
        functools.partial(_cmp_lohi_kernel, n_pages=per_step, rows_per_page=PAGE_SIZE, feature_major=True),
        out_shape=jax.ShapeDtypeStruct((B, n_pages * cpp, nout), F32), grid_spec=grid_spec,
        compiler_params=_cp("parallel", "parallel"), name="cmp_lohi_paged",
    )(pages, *([cache_t] * per_step), w_lohi)


def _cmp_combine_kernel(lohi_ref, pe_ref, b1_ref, w2_ref, b2_ref, o_ref):
    nch = lohi_ref.shape[1]
    nh = lohi_ref.shape[2] // 2
    lo = lohi_ref[0, :, :nh]
    hi_next = pltpu.roll(lohi_ref[0, :, nh:], nch - 1, 0)
    pe = pe_ref[0, 0:1, :nh] + pe_ref[0, 1:2, nh:]
    hid = jax.nn.gelu(lo + hi_next + pe + b1_ref[...])
    o_ref[0] = _dot(hid.astype(BF16), w2_ref[...]) + b2_ref[...]


def _cmp_combine(lohi, pe_lohi, b1, w2, b2):
    B, nch, n2 = lohi.shape
    nh = n2 // 2
    nout = w2.shape[1]
    fix2 = lambda b: (0, 0)
    return pl.pallas_call(
        _cmp_combine_kernel, out_shape=jax.ShapeDtypeStruct((B, nch, nout), F32), grid=(B,),
        in_specs=[pl.BlockSpec((1, nch, n2), lambda b: (b, 0, 0)), pl.BlockSpec((1, 2, n2), lambda b: (0, 0, 0)),
                  pl.BlockSpec((1, nh), fix2), pl.BlockSpec((nh, nout), fix2), pl.BlockSpec((1, nout), fix2)],
        out_specs=pl.BlockSpec((1, nch, nout), lambda b: (b, 0, 0)),
        compiler_params=_cp("parallel"), name="cmp_combine",
    )(lohi, pe_lohi, b1, w2, b2)


def _sel_importance(p_sum, n_lanes):
    nc = p_sum.shape[1]
    c = _iota((nc, n_lanes), 0)
    j = _iota((nc, n_lanes), 1)
    a = jnp.where(c >= SEL_RATIO * j - 1, jnp.where(c <= SEL_RATIO * j + SEL_RATIO - 1, 1.0, 0.0), 0.0)
    return _dot_exact(p_sum, a)


def _attn_nsa_kernel(q_ref, qr_ref, cmp_ref, sel_ref, win_ref, gl_ref, o_ref,
                     k2s, vts, k2w, vtw, bias_ref, m_sc, acc_sc, *, T, tq, tk, n_sel, n_top):
    i = pl.program_id(2)
    t0 = i * tq
    R = NSA_HEADS // NSA_KV_HEADS
    dh = NSA_HEAD_DIM
    scale = dh ** -0.5
    wk = min(T, WINDOW + tq)

    @pl.when(i == 0)
    def _():
        for src, k_sc, vt_sc in ((sel_ref, k2s, vts), (win_ref, k2w, vtw)):
            kv = src[...]
            k_sc[...] = _dup_keys(kv, dh).astype(BF16)
            _store_values_t(vt_sc, 0, _transpose_rows(kv.astype(BF16))[dh:])

    tpos = t0 + _iota((1, tq), 1)
    cmp_tok = cmp_ref[0]
    nc = cmp_tok.shape[0]
    s = _dot_nt(_dup_keys(cmp_tok, dh).astype(BF16), _stack_heads(q_ref, R, dh, scale))
    cvalid = (_iota((nc, 1), 0) * CMP_STRIDE + CMP_BLOCK - 1) <= tpos
    probs = []
    for h in range(R):
        sh = jnp.where(cvalid, s[:, h * tq:(h + 1) * tq], NEG)
        e = jnp.where(cvalid, jnp.exp(sh - jnp.max(sh, axis=0, keepdims=True)), 0.0)
        l = jnp.sum(e, axis=0, keepdims=True)
        probs.append(e / jnp.where(l > 0.0, l, 1.0))
    p_sum = probs[0]
    for h in range(1, R):
        p_sum = p_sum + probs[h]
    vct = _transpose_rows(cmp_tok.astype(BF16))[dh:].astype(BF16)
    o_cmp = _dot(vct, jnp.concatenate(probs, axis=1).astype(BF16))

    ns = bias_ref.shape[0] // SEL_BLOCK
    blk = _iota((ns, tq), 0)
    cur = tpos // SEL_BLOCK
    c_idx = _iota((ns, nc), 1)
    lo_c = SEL_RATIO * _iota((ns, nc), 0) - 1
    spread = jnp.where(c_idx >= lo_c, jnp.where(c_idx <= lo_c + SEL_RATIO, 1.0, 0.0), 0.0)
    imp = _dot_exact(spread, p_sum)
    forced = jnp.where(blk == 0, 1.0, jnp.where(blk == cur, 1.0, jnp.where(blk == cur - 1, 1.0, 0.0)))
    imp = jnp.where(blk > cur, NEG, jnp.where(forced > 0.5, FORCE, imp))
    rank = jnp.zeros(imp.shape, F32)
    for c in range(n_sel):
        row = imp[c:c + 1, :]
        rank = rank + jnp.where(row > imp, 1.0, jnp.where(row == imp, jnp.where(blk > c, 1.0, 0.0), 0.0))
    chosen = jnp.where(blk < n_sel, jnp.where(rank < n_top, 1.0, 0.0), 0.0)
    for j in range(n_sel):
        picked = jnp.broadcast_to(chosen[j:j + 1, :], (SEL_BLOCK, tq))
        kpos = j * SEL_BLOCK + _iota((SEL_BLOCK, 1), 0)
        bias_ref[j * SEL_BLOCK:(j + 1) * SEL_BLOCK, :] = jnp.where(
            picked > 0.5, jnp.where(kpos <= tpos, 0.0, NEG), NEG)

    hi = (t0 + tq + tk - 1) // tk
    qrs = _stack_heads(qr_ref, R, dh, scale)
    acc = _flash(qrs, k2s, slice(None), vts, 0, m_sc, acc_sc, 0, hi, tk,
                 lambda kt, k0: bias_ref[pl.ds(k0, tk), :], R)
    o_sel = acc[ONES_ROWS:] / acc[0:1]
    kw0 = pl.multiple_of(jnp.clip(t0 - WINDOW, 0, T - wk), LANES)
    d = tpos - (kw0 + _iota((wk, 1), 0))
    win_bias = jnp.where(d >= 0, jnp.where(d <= WINDOW, 0.0, NEG), NEG)
    s = _add_bias(_dot_nt(k2w[pl.ds(kw0, wk), :], qrs), win_bias, R)
    e = jnp.exp(s - jnp.max(s, axis=0, keepdims=True)).astype(BF16)
    acc = _weighted_values(vtw, kw0 // LANES, e)
    o_win = acc[ONES_ROWS:] / acc[0:1]
    gate = jax.nn.sigmoid(gl_ref[...].T)
    heads = []
    for h in range(R):
        hs = slice(h * tq, (h + 1) * tq)
        heads.append(gate[h:h + 1] * o_cmp[:, hs] + gate[R + h:R + h + 1] * o_sel[:, hs]
                     + gate[2 * R + h:2 * R + h + 1] * o_win[:, hs])
    _write_heads_t(o_ref, heads)


def _attn_nsa(proj, cmp_tok, B, T, tk=None):
    G = NSA_KV_HEADS
    R = NSA_HEADS // G
    tq = 2 * LANES
    tk = tk or min(T, 512)
    nq = T // tq
    n_sel = -(-T // SEL_BLOCK)
    n_top = min(SEL_TOP, n_sel)
    qw = NSA_HEADS * NSA_HEAD_DIM // G
    kv0 = 2 * NSA_HEADS * NSA_HEAD_DIM // LANES
    gl0 = kv0 + 3 * G
    nch = cmp_tok.shape[1]
    dh = NSA_HEAD_DIM
    kern = functools.partial(_attn_nsa_kernel, T=T, tq=tq, tk=tk, n_sel=n_sel, n_top=n_top)
    return pl.pallas_call(
        kern, out_shape=jax.ShapeDtypeStruct((B * T, NSA_HEADS * dh), F32), grid=(B, G, nq),
        in_specs=[pl.BlockSpec((tq, qw), lambda b, g, i: (b * nq + i, g)),
                  pl.BlockSpec((tq, qw), lambda b, g, i: (b * nq + i, G + g)),
                  pl.BlockSpec((1, nch, 2 * dh), lambda b, g, i: (b, 0, g)),
                  pl.BlockSpec((T, 2 * dh), lambda b, g, i: (b, kv0 + G + g)),
                  pl.BlockSpec((T, 2 * dh), lambda b, g, i: (b, kv0 + 2 * G + g)),
                  pl.BlockSpec((tq, LANES), lambda b, g, i: (b * nq + i, gl0 + g))],
        out_specs=pl.BlockSpec((tq, qw), lambda b, g, i: (b * nq + i, g)),
        scratch_shapes=[pltpu.VMEM((T, 2 * dh), BF16), pltpu.VMEM((T // LANES, ONES_ROWS + dh, LANES), BF16)] * 2 + [
            pltpu.VMEM((T, tq), F32), pltpu.VMEM((1, R * tq), F32), pltpu.VMEM((ONES_ROWS + dh, R * tq), F32)],
        compiler_params=_cp("parallel", "parallel", "arbitrary"), name="attn_nsa",
    )(proj, proj, cmp_tok, proj, proj, proj)


def _nsa_dec_cmp_kernel(q_ref, cmp_ref, o_ref, imp_ref, *, t, n_sel):
    G = NSA_KV_HEADS
    R = NSA_HEADS // G
    dh = NSA_HEAD_DIM
    nc = cmp_ref.shape[1]
    nl = imp_ref.shape[2]
    cvalid = (_iota((1, nc), 1) * CMP_STRIDE + CMP_BLOCK - 1) <= t
    blk = _iota((1, nl), 1)
    cur = t // SEL_BLOCK
    for g in range(G):
        kc = cmp_ref[0, :, 2 * g * dh:(2 * g + 1) * dh].astype(BF16)
        vc = cmp_ref[0, :, (2 * g + 1) * dh:(2 * g + 2) * dh].astype(BF16)
        qg = (q_ref[0, g * R:(g + 1) * R, :] * dh ** -0.5).astype(BF16)
        p = _softmax_rows(_dot_nt(qg, kc), cvalid)
        o_ref[0, g * R:(g + 1) * R, :] = _dot(p.astype(BF16), vc)
        imp = _sel_importance(jnp.sum(p, axis=0, keepdims=True), nl)
        forced = jnp.where(blk == 0, 1.0, jnp.where(blk == cur, 1.0, jnp.where(blk == cur - 1, 1.0, 0.0)))
        imp = jnp.where(blk > cur, NEG, jnp.where(forced > 0.5, FORCE, imp))
        imp_ref[0, g:g + 1, :] = jnp.where(blk < n_sel, imp, -jnp.inf)


def _nsa_dec_cmp(q3, cmp_tok, t, n_sel):
    B = q3.shape[0]
    nl = -(-n_sel // LANES) * LANES
    nc = cmp_tok.shape[1]
    blk3 = lambda b: (b, 0, 0)
    return pl.pallas_call(
        functools.partial(_nsa_dec_cmp_kernel, t=t, n_sel=n_sel),
        out_shape=(jax.ShapeDtypeStruct(q3.shape, F32), jax.ShapeDtypeStruct((B, NSA_KV_HEADS, nl), F32)),
        grid=(B,),
        in_specs=[pl.BlockSpec((1,) + q3.shape[1:], blk3), pl.BlockSpec((1, nc, cmp_tok.shape[2]), blk3)],
        out_specs=(pl.BlockSpec((1,) + q3.shape[1:], blk3), pl.BlockSpec((1, NSA_KV_HEADS, nl), blk3)),
        compiler_params=_cp("parallel"), name="nsa_dec_cmp",
    )(q3, cmp_tok)


def _topk_idx_kernel(x_ref, o_ref, *, k):
    x = x_ref[...]
    lane = _iota(x.shape, 1)
    out_lane = _iota(o_ref.shape, 1)
    out = jnp.zeros(o_ref.shape, I32)
    big = x.shape[1]
    for n in range(k):
        m = jnp.max(x, axis=-1, keepdims=True)
        idx = jnp.min(jnp.where(x == m, lane, big), axis=-1, keepdims=True)
        out = jnp.where(out_lane == n, idx, out)
        x = jnp.where(lane == idx, -jnp.inf, x)
    o_ref[...] = out


def _topk_idx(x, k):
    rows = x.shape[0]
    return pl.pallas_call(functools.partial(_topk_idx_kernel, k=k),
                          out_shape=jax.ShapeDtypeStruct((rows, LANES), I32), name="topk_idx")(x)


def _nsa_dec_attn_kernel(*refs, n_top, n_past_blocks):
    idx_ref, rb_ref = refs[0], refs[1]
    G = NSA_KV_HEADS
    R = NSA_HEADS // G
    dh = NSA_HEAD_DIM
    blocks = refs[2:2 + G * n_top]
    qr_ref, new_ref, win_ref, ocmp_ref, gl_ref, o_ref = refs[2 + G * n_top:]
    b = pl.program_id(0)
    scale = dh ** -0.5
    per_page = PAGE_SIZE // SEL_BLOCK
    nk = n_top * PAGE_SIZE
    lane = _iota((1, nk), 1)
    for g in range(G):
        qg = qr_ref[0, g * R:(g + 1) * R, :] * scale
        qb = qg.astype(BF16)
        kt = jnp.concatenate([blocks[g * n_top + n][0, :dh, :] for n in range(n_top)], axis=1).astype(BF16)
        vt = jnp.concatenate([blocks[g * n_top + n][0, dh:, :] for n in range(n_top)], axis=1).astype(BF16)
        valid = jnp.zeros((1, nk), F32)
        for n in range(n_top):
            blk = idx_ref[b, g * n_top + n]
            flag = jnp.where(blk < n_past_blocks, 1.0, 0.0)
            in_block = jnp.where((lane % PAGE_SIZE) // SEL_BLOCK == blk % per_page, flag, 0.0)
            valid = jnp.where(lane // PAGE_SIZE == n, in_block, valid)
        valid = valid > 0.5
        k_new = new_ref[0, :, (2 * G + 2 * g) * dh:(2 * G + 2 * g + 1) * dh]
        v_new = new_ref[0, :, (2 * G + 2 * g + 1) * dh:(2 * G + 2 * g + 2) * dh]
        s = jnp.where(valid, _dot(qb, kt), NEG)
        s_new = jnp.sum(qg * k_new, axis=-1, keepdims=True)
        m = jnp.maximum(jnp.max(s, axis=-1, keepdims=True), s_new)
        e = jnp.where(valid, jnp.exp(s - m), 0.0)
        e_new = jnp.exp(s_new - m)
        o_sel = (_dot_nt(e.astype(BF16), vt) + e_new * v_new) / (jnp.sum(e, axis=-1, keepdims=True) + e_new)
        kwin = win_ref[0, 2 * g * dh:(2 * g + 1) * dh, :].astype(BF16)
        vwin = win_ref[0, (2 * g + 1) * dh:(2 * g + 2) * dh, :].astype(BF16)
        k_new = new_ref[0, :, (4 * G + 2 * g) * dh:(4 * G + 2 * g + 1) * dh]
        v_new = new_ref[0, :, (4 * G + 2 * g + 1) * dh:(4 * G + 2 * g + 2) * dh]
        s = _dot(qb, kwin)
        s_new = jnp.sum(qg * k_new, axis=-1, keepdims=True)
        m = jnp.maximum(jnp.max(s, axis=-1, keepdims=True), s_new)
        e = jnp.exp(s - m)
        e_new = jnp.exp(s_new - m)
        o_win = (_dot_nt(e.astype(BF16), vwin) + e_new * v_new) / (jnp.sum(e, axis=-1, keepdims=True) + e_new)
        gate = jax.nn.sigmoid(gl_ref[0, g * R:(g + 1) * R, :])
        o_ref[0, g * R:(g + 1) * R, :] = (gate[:, 0:1] * ocmp_ref[0, g * R:(g + 1) * R, :]
                                          + gate[:, 1:2] * o_sel + gate[:, 2:3] * o_win)


def _nsa_dec_attn(idx, sel_pages, cache_t, qr3, new_kv, win_buf, win_index0, o_cmp, gl3, n_past_blocks):
    B = qr3.shape[0]
    G = NSA_KV_HEADS
    n_top = idx.shape[1] // G
    dh = NSA_HEAD_DIM
    blk3 = lambda b, idx, rb: (b, 0, 0)
    block_specs = [pl.BlockSpec((1, 2 * dh, PAGE_SIZE), functools.partial(
        lambda b, idx, rb, n, g: (rb[b, n], g, 0), n=g * n_top + n, g=g)) for g in range(G) for n in range(n_top)]
    grid_spec = pltpu.PrefetchScalarGridSpec(
        num_scalar_prefetch=2, grid=(B,),
        in_specs=block_specs + [
            pl.BlockSpec((1,) + qr3.shape[1:], blk3), pl.BlockSpec((1,) + new_kv.shape[1:], blk3),
            pl.BlockSpec((1,) + win_buf.shape[1:], lambda b, idx, rb: (win_index0 + b, 0, 0)),
            pl.BlockSpec((1,) + o_cmp.shape[1:], blk3), pl.BlockSpec((1,) + gl3.shape[1:], blk3)],
        out_specs=pl.BlockSpec((1,) + qr3.shape[1:], blk3))
    return pl.pallas_call(
        functools.partial(_nsa_dec_attn_kernel, n_top=n_top, n_past_blocks=n_past_blocks),
        out_shape=jax.ShapeDtypeStruct(qr3.shape, F32), grid_spec=grid_spec,
        compiler_params=_cp("parallel"), name="nsa_dec_attn",
    )(idx, sel_pages, *([cache_t] * (G * n_top)), qr3, new_kv, win_buf, o_cmp, gl3)


def _mla_post_kernel(dn_ref, gq_ref, gkv_ref, a_ref, b_ref, cq_ref, ckr_ref):
    x = dn_ref[...]
    cq = x[:, :Q_LORA]
    cq_ref[...] = cq * lax.rsqrt(jnp.mean(cq * cq, axis=-1, keepdims=True) + RMS_EPS) * gq_ref[...]
    ckv = x[:, Q_LORA:Q_LORA + KV_LORA]
    ckv = ckv * lax.rsqrt(jnp.mean(ckv * ckv, axis=-1, keepdims=True) + RMS_EPS) * gkv_ref[...]
    kr = x[:, Q_LORA + KV_LORA:]
    half = QK_ROPE // 2
    first = (_iota(kr.shape, 1) % QK_ROPE) < half
    partner = jnp.where(first, pltpu.roll(kr, LANES - half, 1), pltpu.roll(kr, half, 1))
    ckr_ref[...] = jnp.concatenate([ckv, kr * a_ref[...] + partner * b_ref[...]], axis=1)


def _mla_post(dn, g_q, g_kv, tabs):
    M, N = dn.shape
    tm = min(M, 512)
    nrb = tabs[0].shape[0] // tm
    row = lambda i: (i, 0)
    fix = lambda i: (0, 0)
    tab = pl.BlockSpec((tm, LANES), lambda i: (i % nrb, 0))
    return pl.pallas_call(
        _mla_post_kernel,
        out_shape=(jax.ShapeDtypeStruct((M, Q_LORA), F32), jax.ShapeDtypeStruct((M, KV_LORA + LANES), F32)),
        grid=(M // tm,),
        in_specs=[pl.BlockSpec((tm, N), row), pl.BlockSpec((1, Q_LORA), fix), pl.BlockSpec((1, KV_LORA), fix),
                  tab, tab],
        out_specs=(pl.BlockSpec((tm, Q_LORA), row), pl.BlockSpec((tm, KV_LORA + LANES), row)),
        compiler_params=_cp("parallel"), name="mla_post",
    )(dn, g_q.reshape(1, -1), g_kv.reshape(1, -1), *tabs)


def _attn_mla_kernel(q_ref, k_ref, v_ref, o_ref, k_sc, v_sc, m_sc, acc_sc, *, tq, tk):
    i = pl.program_id(2)
    t0 = i * tq
    scale = (QK_NOPE + QK_ROPE) ** -0.5

    n_vt = k_ref.shape[0] // LANES

    @pl.when(i == 0)
    def _():
        k_sc[...] = k_ref[...].astype(BF16)
        vt = _transpose_rows(v_ref[...].astype(BF16))
        for hh in range(2):
            _store_values_t(v_sc, hh * n_vt, vt[hh * V_DIM:(hh + 1) * V_DIM])

    tpos = t0 + _iota((1, tq), 1)
    hi = (t0 + tq + tk - 1) // tk
    causal = lambda kt, k0: jnp.where(k0 + _iota((tk, 1), 0) <= tpos, 0.0, NEG)
    heads = []
    for hh in range(2):
        qh = (q_ref[:, hh * MLA_SLOT:(hh + 1) * MLA_SLOT] * scale).astype(BF16)
        acc = _flash(qh, k_sc, slice(hh * MLA_SLOT, (hh + 1) * MLA_SLOT), v_sc, hh * n_vt, m_sc, acc_sc, 0, hi, tk,
                     causal, 1)
        heads.append(acc[ONES_ROWS:] / acc[0:1])
    _write_heads_t(o_ref, heads)


def _attn_mla(q_ext, kv_ext, B, T, tk=None):
    H = MLA_HEADS
    tq = min(T, 1024)
    tk = tk or min(T, 512)
    nq = T // tq
    v0 = H * MLA_SLOT // LANES
    return pl.pallas_call(
        functools.partial(_attn_mla_kernel, tq=tq, tk=tk),
        out_shape=jax.ShapeDtypeStruct((B * T, H * V_DIM), F32), grid=(B, H // 2, nq),
        in_specs=[pl.BlockSpec((tq, 2 * MLA_SLOT), lambda b, h, i: (b * nq + i, h)),
                  pl.BlockSpec((T, 2 * MLA_SLOT), lambda b, h, i: (b, h)),
                  pl.BlockSpec((T, 2 * V_DIM), lambda b, h, i: (b, v0 + h))],
        out_specs=pl.BlockSpec((tq, 2 * V_DIM), lambda b, h, i: (b * nq + i, h)),
        scratch_shapes=[pltpu.VMEM((T, 2 * MLA_SLOT), BF16),
                        pltpu.VMEM((2 * (T // LANES), ONES_ROWS + V_DIM, LANES), BF16),
                        pltpu.VMEM((1, tq), F32), pltpu.VMEM((ONES_ROWS + V_DIM, tq), F32)],
        compiler_params=_cp("parallel", "parallel", "arbitrary"), name="attn_mla",
    )(q_ext, kv_ext, kv_ext)


def _mla_decode_kernel(*refs, per_step):
    pt_ref = refs[0]
    ckv_pages = refs[1:1 + per_step]
    kr_pages = refs[1 + per_step:1 + 2 * per_step]
    ql_ref, qr_ref, cnew_ref, rnew_ref, o_ref, m_sc, l_sc, acc_sc = refs[1 + 2 * per_step:]
    s_idx = pl.program_id(1)
    scale = (QK_NOPE + QK_ROPE) ** -0.5
    ql = ql_ref[0] * scale
    qr = qr_ref[0] * scale

    @pl.when(s_idx == 0)
    def _():
        c_new = cnew_ref[0]
        s_new = (jnp.sum(ql * c_new, axis=-1, keepdims=True) + jnp.sum(qr * rnew_ref[0], axis=-1, keepdims=True))
        m_sc[...] = s_new
        l_sc[...] = jnp.ones(l_sc.shape, F32)
        acc_sc[...] = jnp.broadcast_to(c_new, acc_sc.shape)

    qlb = ql.astype(BF16)
    qrb = qr.astype(BF16)
    cs = [ref[0].astype(BF16) for ref in ckv_pages]
    s = jnp.concatenate([_dot_nt(qlb, c) + _dot(qrb, r[0].astype(BF16)) for c, r in zip(cs, kr_pages)], axis=1)
    m_old = m_sc[...]
    m_new = jnp.maximum(m_old, jnp.max(s, axis=-1, keepdims=True))
    alpha = jnp.exp(m_old - m_new)
    e = jnp.exp(s - m_new).astype(BF16)
    pv = _dot(e[:, :PAGE_SIZE], cs[0])
    for p in range(1, per_step):
        pv = pv + _dot(e[:, p * PAGE_SIZE:(p + 1) * PAGE_SIZE], cs[p])
    l_sc[...] = alpha * l_sc[...] + jnp.sum(e.astype(F32), axis=-1, keepdims=True)
    acc_sc[...] = alpha * acc_sc[...] + pv
    m_sc[...] = m_new

    @pl.when(s_idx == pl.num_programs(1) - 1)
    def _():
        o_ref[0] = acc_sc[...] / l_sc[...]


def _mla_decode(pages, cache_ckv, cache_kr, q_lat, q_rope, c_new, r_new, per_step):
    B, n_pages = pages.shape
    H = MLA_HEADS
    blk3 = lambda b, s, pt: (b, 0, 0)
    page_map = lambda p: functools.partial(lambda b, s, pt, p: (pt[b, s * per_step + p], 0, 0), p=p)
    grid_spec = pltpu.PrefetchScalarGridSpec(
        num_scalar_prefetch=1, grid=(B, n_pages // per_step),
        in_specs=([pl.BlockSpec((1, PAGE_SIZE, KV_LORA), page_map(p)) for p in range(per_step)]
                  + [pl.BlockSpec((1, QK_ROPE, PAGE_SIZE), page_map(p)) for p in range(per_step)]
                  + [pl.BlockSpec((1, H, KV_LORA), blk3), pl.BlockSpec((1, H, QK_ROPE), blk3),
                     pl.BlockSpec((1, 1, KV_LORA), blk3), pl.BlockSpec((1, 1, QK_ROPE), blk3)]),
        out_specs=pl.BlockSpec((1, H, KV_LORA), blk3),
        scratch_shapes=[pltpu.VMEM((H, 1), F32), pltpu.VMEM((H, 1), F32), pltpu.VMEM((H, KV_LORA), F32)])
    return pl.pallas_call(
        functools.partial(_mla_decode_kernel, per_step=per_step),
        out_shape=jax.ShapeDtypeStruct((B, H, KV_LORA), F32), grid_spec=grid_spec,
        compiler_params=_cp("parallel", "arbitrary"), name="mla_decode",
    )(pages, *([cache_ckv] * per_step), *([cache_kr] * per_step), q_lat, q_rope, c_new, r_new)


def _attn_moba_kernel(q_ref, kv_ref, o_ref, k2, vt_sc, mean_sc, mask_sc, m_sc, acc_sc, *, T, tq, tk, nb, n_top):
    i = pl.program_id(2)
    t0 = i * tq
    R = MOBA_HEADS // MOBA_KV_HEADS
    dh = MOBA_HEAD_DIM
    L = R * tq
    nbp = mean_sc.shape[0]
    bpt = tk // MOBA_BLOCK

    @pl.when(i == 0)
    def _():
        kv = kv_ref[...]
        k2[...] = _dup_keys(kv, dh).astype(BF16)
        _store_values_t(vt_sc, 0, _transpose_rows(kv.astype(BF16))[dh:])
        mean_sc[...] = jnp.zeros(mean_sc.shape, F32)
        for j in range(T // MOBA_BLOCK):
            blk_rows = _dup_keys(kv_ref[j * MOBA_BLOCK:(j + 1) * MOBA_BLOCK, :], dh)
            mean_sc[j:j + 1, :] = jnp.sum(blk_rows, axis=0, keepdims=True) / MOBA_BLOCK

    tpos = t0 + _iota((1, L), 1) % tq
    cur = tpos // MOBA_BLOCK
    jb = _iota((nbp, L), 0)
    qs = _stack_heads(q_ref, R, dh, dh ** -0.5)
    block_mask = jnp.where(jb == cur, 1.0, 0.0)
    if n_top > 0:
        gm = jnp.where(jb < cur, _dot_nt(mean_sc[...].astype(BF16), qs), NEG)
        rank = jnp.zeros(gm.shape, F32)
        for c in range(nb):
            row = gm[c:c + 1, :]
            rank = rank + jnp.where(row > gm, 1.0, jnp.where(row == gm, jnp.where(jb > c, 1.0, 0.0), 0.0))
        block_mask = jnp.where(jb < cur, jnp.where(rank < n_top, 1.0, 0.0), block_mask)
    mask_sc[...] = block_mask

    def bias(kt, k0):
        parts = []
        for j in range(bpt):
            picked = mask_sc[pl.ds(kt * bpt + j, 1), :]
            kpos = k0 + j * MOBA_BLOCK + _iota((MOBA_BLOCK, 1), 0)
            parts.append(jnp.where(picked > 0.5, jnp.where(kpos <= tpos, 0.0, NEG), NEG))
        return parts[0] if bpt == 1 else jnp.concatenate(parts, axis=0)

    hi = (t0 + tq + tk - 1) // tk
    acc = _flash(qs, k2, slice(None), vt_sc, 0, m_sc, acc_sc, 0, hi, tk, bias, 1)
    o = acc[ONES_ROWS:] / acc[0:1]
    _write_heads_t(o_ref, [o[:, r * tq:(r + 1) * tq] for r in range(R)])


def _attn_moba(proj, B, T, tk=None):
    KH = MOBA_KV_HEADS
    R = MOBA_HEADS // KH
    dh = MOBA_HEAD_DIM
    tq = min(T, 512)
    tk = tk or min(T, 512)
    nq = T // tq
    nb = (T - 1) // MOBA_BLOCK
    n_top = min(MOBA_TOP, nb)
    qw = MOBA_HEADS * dh // KH
    kv0 = MOBA_HEADS * dh // LANES
    nbp = max(T // MOBA_BLOCK, 8)
    return pl.pallas_call(
        functools.partial(_attn_moba_kernel, T=T, tq=tq, tk=tk, nb=nb, n_top=n_top),
        out_shape=jax.ShapeDtypeStruct((B * T, MOBA_HEADS * dh), F32), grid=(B, KH, nq),
        in_specs=[pl.BlockSpec((tq, qw), lambda b, h, i: (b * nq + i, h)),
                  pl.BlockSpec((T, 2 * dh), lambda b, h, i: (b, kv0 + h))],
        out_specs=pl.BlockSpec((tq, qw), lambda b, h, i: (b * nq + i, h)),
        scratch_shapes=[pltpu.VMEM((T, 2 * dh), BF16), pltpu.VMEM((T // LANES, ONES_ROWS + dh, LANES), BF16),
                        pltpu.VMEM((nbp, 2 * dh), F32), pltpu.VMEM((nbp, R * tq), F32),
                        pltpu.VMEM((1, R * tq), F32), pltpu.VMEM((ONES_ROWS + dh, R * tq), F32)],
        compiler_params=_cp("parallel", "parallel", "arbitrary"), name="attn_moba",
    )(proj, proj)


def _moba_means_kernel(*refs, per_step):
    pages, o_ref = refs[1:1 + per_step], refs[1 + per_step]
    KH, dh = MOBA_KV_HEADS, MOBA_HEAD_DIM
    ppb = MOBA_BLOCK // PAGE_SIZE
    s = pl.program_id(1)

    @pl.when(s == 0)
    def _():
        o_ref[...] = jnp.zeros(o_ref.shape, F32)

    lane = _iota(o_ref.shape[1:], 1)
    out = o_ref[0]
    for j in range(per_step // ppb):
        cols = []
        for kh in range(KH):
            tot = pages[j * ppb][0, kh, 0]
            for p in range(1, ppb):
                tot = tot + pages[j * ppb + p][0, kh, 0]
            cols.append(jnp.sum(tot, axis=1, keepdims=True))
        col = jnp.concatenate(cols, axis=0) / MOBA_BLOCK
        out = jnp.where(lane == s * (per_step // ppb) + j, col, out)
    o_ref[0] = out


def _moba_means(pages, cache_t, per_step):
    B, n_pages = pages.shape
    KH, dh = MOBA_KV_HEADS, MOBA_HEAD_DIM
    nk = KH * dh
    assert n_pages * PAGE_SIZE // MOBA_BLOCK <= LANES
    cache_t = cache_t.reshape(-1, KH, 2, dh, PAGE_SIZE)
    grid_spec = pltpu.PrefetchScalarGridSpec(
        num_scalar_prefetch=1, grid=(B, n_pages // per_step),
        in_specs=[pl.BlockSpec((1, KH, 1, dh, PAGE_SIZE), functools.partial(
            lambda b, s, pt, p: (pt[b, s * per_step + p], 0, 0, 0, 0), p=p)) for p in range(per_step)],
        out_specs=pl.BlockSpec((1, nk, LANES), lambda b, s, pt: (b, 0, 0)))
    return pl.pallas_call(
        functools.partial(_moba_means_kernel, per_step=per_step),
        out_shape=jax.ShapeDtypeStruct((B, nk, LANES), F32), grid_spec=grid_spec,
        compiler_params=_cp("parallel", "arbitrary"), name="moba_means",
    )(pages, *([cache_t] * per_step))


def _moba_dec_score_kernel(q_ref, mean_ref, o_ref, *, nb):
    KH = MOBA_KV_HEADS
    R = MOBA_HEADS // KH
    dh = MOBA_HEAD_DIM
    lane = _iota((R, LANES), 1)
    for kh in range(KH):
        mk = mean_ref[0, kh * dh:(kh + 1) * dh, :].astype(BF16)
        g = _dot(q_ref[0, kh * R:(kh + 1) * R, :].astype(BF16), mk)
        o_ref[0, kh * R:(kh + 1) * R, :] = jnp.where(lane < nb, g, -jnp.inf)


def _moba_dec_score(q3, means, nb):
    B, H, dh = q3.shape
    blk3 = lambda b: (b, 0, 0)
    return pl.pallas_call(
        functools.partial(_moba_dec_score_kernel, nb=nb),
        out_shape=jax.ShapeDtypeStruct((B, H, LANES), F32), grid=(B,),
        in_specs=[pl.BlockSpec((1, H, dh), blk3), pl.BlockSpec((1,) + means.shape[1:], blk3)],
        out_specs=pl.BlockSpec((1, H, LANES), blk3), compiler_params=_cp("parallel"), name="moba_dec_score",
    )(q3, means)


def _moba_dec_attn_kernel(*refs, n_blk):
    R = MOBA_HEADS // MOBA_KV_HEADS
    dh = MOBA_HEAD_DIM
    pages = refs[1:1 + R * n_blk]
    q_ref, new_ref, o_ref = refs[1 + R * n_blk:]
    k_new = new_ref[0, :, :dh]
    v_new = new_ref[0, :, dh:]
    for r in range(R):
        q = q_ref[0, 0, r:r + 1, :] * dh ** -0.5
        kt = jnp.concatenate([pages[r * n_blk + n][0, :dh, :] for n in range(n_blk)], axis=1).astype(BF16)
        vt = jnp.concatenate([pages[r * n_blk + n][0, dh:, :] for n in range(n_blk)], axis=1).astype(BF16)
        s = _dot(q.astype(BF16), kt)
        s_new = jnp.sum(q * k_new, axis=-1, keepdims=True)
        m = jnp.maximum(jnp.max(s, axis=-1, keepdims=True), s_new)
        e = jnp.exp(s - m)
        e_new = jnp.exp(s_new - m)
        o_ref[0, 0, r:r + 1, :] = ((_dot_nt(e.astype(BF16), vt) + e_new * v_new)
                                   / (jnp.sum(e, axis=-1, keepdims=True) + e_new))


def _moba_dec_attn(page_ids, cache_t, q4, new_kv):
    B, KH, R, dh = q4.shape
    n_blk = page_ids.shape[1] // (KH * R)
    page_specs = [pl.BlockSpec((1, 2 * dh, PAGE_SIZE), functools.partial(
        lambda b, h, pg, n: (pg[b, h * R * n_blk + n], h, 0), n=n)) for n in range(R * n_blk)]
    grid_spec = pltpu.PrefetchScalarGridSpec(
        num_scalar_prefetch=1, grid=(B, KH),
        in_specs=page_specs + [pl.BlockSpec((1, 1, R, dh), lambda b, h, pg: (b, h, 0, 0)),
                               pl.BlockSpec((1, 1, 2 * dh), lambda b, h, pg: (b, 0, h))],
        out_specs=pl.BlockSpec((1, 1, R, dh), lambda b, h, pg: (b, h, 0, 0)))
    return pl.pallas_call(
        functools.partial(_moba_dec_attn_kernel, n_blk=n_blk),
        out_shape=jax.ShapeDtypeStruct(q4.shape, F32), grid_spec=grid_spec,
        compiler_params=_cp("parallel", "parallel"), name="moba_dec_attn",
    )(page_ids, *([cache_t] * (R * n_blk)), q4, new_kv)


def _router_kernel(x_ref, w_ref, b_ref, o_ref):
    s = jax.nn.sigmoid(lax.dot_general(w_ref[...], x_ref[...], (((1,), (1,)), ((), ())),
                                       preferred_element_type=F32, precision=lax.Precision.HIGHEST))
    row = _iota(s.shape, 0)
    x = s + b_ref[...]
    chosen = jnp.zeros(s.shape, F32)
    for _ in range(TOP_K):
        m = jnp.max(x, axis=0, keepdims=True)
        idx = jnp.min(jnp.where(x == m, row, N_EXPERTS), axis=0, keepdims=True)
        hit = row == idx
        chosen = jnp.where(hit, 1.0, chosen)
        x = jnp.where(hit, -jnp.inf, x)
    w = chosen * s
    w = w / jnp.sum(w, axis=0, keepdims=True) * ROUTED_SCALE
    o_ref[...] = jnp.concatenate([w, jnp.zeros((LANES - N_EXPERTS, w.shape[1]), F32)], axis=0).T


def _moe_router(x, w_router_t, b_router):
    M, K = x.shape
    tm = min(M, 512)
    E = w_router_t.shape[0]
    row = lambda i: (i, 0)
    fix = lambda i: (0, 0)
    return pl.pallas_call(
        _router_kernel, out_shape=jax.ShapeDtypeStruct((M, LANES), F32), grid=(M // tm,),
        in_specs=[pl.BlockSpec((tm, K), row), pl.BlockSpec((E, K), fix), pl.BlockSpec((E, 1), fix)],
        out_specs=pl.BlockSpec((tm, LANES), row), compiler_params=_cp("parallel"), name="moe_router",
    )(x, w_router_t, b_router)


def _moe_kernel(x_ref, gate_ref, wg_ref, wu_ref, wd_ref, wsg_ref, wsu_ref, wsd_ref, g_ref, b_ref, o_ref,
                xb_sc, acc_sc):
    e = pl.program_id(1)
    nh = MOE_GROUP * D_EXPERT

    @pl.when(e == 0)
    def _():
        xb = x_ref[...].astype(BF16)
        xb_sc[...] = xb
        hs = _silu(_dot(xb, wsg_ref[0].astype(BF16))) * _dot(xb, wsu_ref[0].astype(BF16))
        acc_sc[...] = _dot(hs.astype(BF16), wsd_ref[0].astype(BF16))

    xb = xb_sc[...]
    w_gate = jnp.concatenate([wg_ref[0, k].astype(BF16) for k in range(MOE_GROUP)], axis=1)
    w_up = jnp.concatenate([wu_ref[0, k].astype(BF16) for k in range(MOE_GROUP)], axis=1)
    gate = gate_ref[...]
    lane = _iota(gate.shape, 1)
    act = _silu(_dot(xb, w_gate)) * _dot(xb, w_up)
    h = jnp.concatenate(
        [act[:, k * D_EXPERT:(k + 1) * D_EXPERT]
         * jnp.sum(jnp.where(lane == MOE_GROUP * e + k, gate, 0.0), axis=1, keepdims=True)
         for k in range(MOE_GROUP)], axis=1)
    w_down = jnp.concatenate([wd_ref[0, k].astype(BF16) for k in range(MOE_GROUP)], axis=0)
    acc_sc[...] += _dot(h.astype(BF16), w_down)

    @pl.when(e == pl.num_programs(1) - 1)
    def _():
        o_ref[...] = _layer_norm(ALPHA * x_ref[...] + acc_sc[...], g_ref[...], b_ref[...])


def _moe(x, gate, w, g, b):
    M, D = x.shape
    tm = min(M, 1024)
    layer = w['layer']
    E, _, F = w['w_gate'].shape[1:]
    n = MOE_GROUP
    row = lambda i, e: (i, 0)
    fix = lambda i, e: (0, 0)
    grp = lambda i, e: (layer, e, 0, 0)
    lay = lambda i, e: (layer, 0, 0)
    return pl.pallas_call(
        _moe_kernel, out_shape=jax.ShapeDtypeStruct((M, D), F32), grid=(M // tm, E // n),
        in_specs=[pl.BlockSpec((tm, D), row), pl.BlockSpec((tm, LANES), row),
                  pl.BlockSpec((1, n, D, F), grp), pl.BlockSpec((1, n, D, F), grp), pl.BlockSpec((1, n, F, D), grp),
                  pl.BlockSpec((1, D, D_SHARED), lay), pl.BlockSpec((1, D, D_SHARED), lay),
                  pl.BlockSpec((1, D_SHARED, D), lay),
                  pl.BlockSpec((1, D), fix), pl.BlockSpec((1, D), fix)],
        out_specs=pl.BlockSpec((tm, D), row),
        scratch_shapes=[pltpu.VMEM((tm, D), BF16), pltpu.VMEM((tm, D), F32)],
        compiler_params=_cp("parallel", "arbitrary"), name="moe",
    )(x, gate, w['w_gate'], w['w_up'], w['w_down'], w['ws_gate'], w['ws_up'], w['ws_down'],
      g.reshape(1, D), b.reshape(1, D))


def _pad_cols(w, n):
    return jnp.pad(w, ((0, 0), (0, n - w.shape[1])))


def _block_diag(blocks):
    n, a, b = blocks.shape
    eye = jnp.eye(n, dtype=blocks.dtype)
    return (eye[:, None, :, None] * blocks[:, :, None, :]).reshape(n * a, n * b)


def _nsa_weights(w_in, cmp_pe, cmp_w1, cmp_b1, cmp_w2, cmp_b2, w_o):
    H, G, dh = NSA_HEADS, NSA_KV_HEADS, NSA_HEAD_DIM
    R = H // G
    nq, nkv = H * dh, 6 * G * dh
    wq, wkv = w_in[:, :nq], w_in[:, nq:nq + nkv]
    wg = w_in[:, nq + nkv:].reshape(-1, 3, G, R).transpose(0, 2, 1, 3).reshape(-1, G, 3 * R)
    wg = jnp.pad(wg, ((0, 0), (0, 0), (0, LANES - 3 * R))).reshape(-1, G * LANES)
    w_ext = jnp.concatenate([wq, wq, wkv, wg], axis=1).astype(BF16)
    w1 = cmp_w1.reshape(2, CMP_STRIDE, 2, dh, CMP_HIDDEN)
    eye_k = jnp.eye(2, dtype=F32)
    w_lohi = (w1.transpose(1, 2, 3, 0, 4)[:, :, :, :, None, :] * eye_k[None, :, None, None, :, None])
    w_lohi = w_lohi.reshape(CMP_STRIDE // 2, 2 * 2 * dh, 2 * 2 * CMP_HIDDEN).astype(BF16)
    pe_rows = jnp.broadcast_to(cmp_pe[:, None], (CMP_BLOCK, G, 2, dh)).reshape(1, CMP_BLOCK, G * 2 * dh)
    b1 = jnp.broadcast_to(cmp_b1[None], (G, 2, CMP_HIDDEN)).reshape(1, -1)
    b2 = jnp.broadcast_to(cmp_b2[None], (G, 2, dh)).reshape(1, -1)
    w2 = _block_diag(jnp.tile(cmp_w2, (G, 1, 1))).astype(BF16)
    return dict(w_ext=w_ext, w_lohi=w_lohi, pe_rows=pe_rows, b1=b1, b2=b2, w2=w2, w_o=w_o.astype(BF16))


def _nsa_tables(pos):
    H, G = NSA_HEADS, NSA_KV_HEADS
    dh = NSA_HEAD_DIM
    kv_rot = [(1.0, dh), (0.0, dh)] * G
    layout = [(0.0, H * dh), (1.0, H * dh), (0.0, 2 * G * dh)] + kv_rot + kv_rot + [(0.0, G * LANES)]
    return _unit_tables(pos, dh // 2) + (_column_mask(layout),)


def _mla_weights(w_dn, g_q, w_uq, g_kv, w_uk, w_uv, w_o):
    H = MLA_HEADS
    pad = MLA_SLOT - QK_NOPE - QK_ROPE
    w_dn_p = _pad_cols(w_dn, Q_LORA + KV_LORA + LANES).astype(BF16)
    wq = jnp.pad(w_uq.reshape(Q_LORA, H, QK_NOPE + QK_ROPE), ((0, 0), (0, 0), (0, pad)))
    w_q = wq.reshape(Q_LORA, H * MLA_SLOT).astype(BF16)
    wk_c = jnp.pad(w_uk, ((0, 0), (0, 0), (0, MLA_SLOT - QK_NOPE))).reshape(KV_LORA, H * MLA_SLOT)
    eye = jnp.pad(jnp.eye(QK_ROPE, dtype=F32), ((0, 0), (QK_NOPE, pad)))
    wk_r = jnp.tile(eye, (1, H))
    wk = jnp.concatenate([wk_c, wk_r, jnp.zeros((LANES - QK_ROPE, H * MLA_SLOT), F32)], axis=0)
    wv = jnp.pad(w_uv.reshape(KV_LORA, H * V_DIM), ((0, LANES), (0, 0)))
    w_kv = jnp.concatenate([wk, wv], axis=1).astype(BF16)
    absorb = jnp.pad(w_uk.transpose(1, 2, 0), ((0, 0), (0, MLA_SLOT - QK_NOPE), (0, 0)))
    w_absorb = _block_diag(absorb).astype(BF16)
    w_unabsorb = _block_diag(w_uv.transpose(1, 0, 2)).astype(BF16)
    return dict(w_dn=w_dn_p, g_q=g_q, g_kv=g_kv, w_q=w_q, w_kv=w_kv, w_absorb=w_absorb,
                w_unabsorb=w_unabsorb, w_o=w_o.astype(BF16))


def _mla_q_tables(pos):
    half = QK_ROPE // 2
    ua, ub = _rope_unit(pos, half)
    T = pos.shape[0]
    pad = MLA_SLOT - QK_NOPE - QK_ROPE
    a = jnp.concatenate([jnp.ones((T, QK_NOPE), F32), ua, jnp.ones((T, pad), F32)], axis=1)
    b = jnp.concatenate([jnp.zeros((T, QK_NOPE), F32), ub, jnp.zeros((T, pad), F32)], axis=1)
    return a, b, jnp.ones((1, MLA_HEADS * MLA_SLOT), F32)


def _mla_kr_tables(pos):
    return _tables(pos, QK_ROPE // 2, [('r', 1), ('n', LANES - QK_ROPE)])


def _moba_weights(w_in, w_o):
    return dict(w_in=w_in.astype(BF16), w_o=w_o.astype(BF16))


def _moba_tables(pos):
    dh = MOBA_HEAD_DIM
    layout = [(1.0, MOBA_HEADS * dh)] + [(1.0, dh), (0.0, dh)] * MOBA_KV_HEADS
    return _unit_tables(pos, dh // 2) + (_column_mask(layout),)


def _moe_weights(w_router, b_router, w_gate, w_up, w_down, ws_gate, ws_up, ws_down, layer=None):
    stacked = (w_router, b_router, w_gate, w_up, w_down, ws_gate, ws_up, ws_down)
    if layer is None:
        stacked, layer = tuple(a[None] for a in stacked), 0
    w_router, b_router, w_gate, w_up, w_down, ws_gate, ws_up, ws_down = stacked
    E = w_gate.shape[1]
    return dict(layer=layer, w_router=w_router[layer].T, b_router=b_router[layer].reshape(E, 1),
                w_gate=w_gate, w_up=w_up, w_down=w_down, ws_gate=ws_gate, ws_up=ws_up, ws_down=ws_down)


def _feature_major(cache):
    n, pool, rows = cache.shape[:3]
    nd = cache.ndim
    return cache.transpose(0, 1, *range(3, nd), 2).reshape(n * pool, -1, rows)


def _nsa_cmp_tokens(rows_lohi, w):
    pe_lohi = _cmp_lohi(w['pe_rows'], w['w_lohi'])
    return _cmp_combine(rows_lohi, pe_lohi, w['b1'], w['w2'], w['b2'])


def _nsa_prompt(h, B, T, w, tabs):
    G, dh = NSA_KV_HEADS, NSA_HEAD_DIM
    proj = _proj(h, w['w_ext'], tabs, dh // 2)
    kv0 = 2 * NSA_HEADS * dh
    width = 2 * G * dh
    lohi = _cmp_lohi(proj.reshape(B, T, -1), w['w_lohi'], col_block=kv0 // LANES)
    cmp_tok = _nsa_cmp_tokens(lohi, w)
    y = _attn_nsa(proj, cmp_tok, B, T)
    kv = proj[:, kv0:kv0 + 3 * width].reshape(B, T, 3, G, 2, dh)
    return y, (kv[:, :, 0], kv[:, :, 1], kv[:, -min(WINDOW, T):, 2])


def _nsa_sample(h, past_len, w, tabs, cache_cmp, cache_sel, win_state, slot, page_table):
    B = h.shape[0]
    G, H, dh = NSA_KV_HEADS, NSA_HEADS, NSA_HEAD_DIM
    R = H // G
    n_pool = cache_cmp.shape[1]
    width = 2 * G * dh
    proj = _proj(h, w['w_ext'], tabs, dh // 2)
    pages = page_table + slot * n_pool
    lohi = _cmp_lohi_paged(_feature_major(cache_cmp), pages, w['w_lohi'], per_step=min(32, page_table.shape[1]))
    cmp_tok = _nsa_cmp_tokens(lohi, w)
    n_sel = -(-(past_len + 1) // SEL_BLOCK)
    n_top = min(SEL_TOP, n_sel)
    q3 = proj[:, :H * dh].reshape(B, H, dh)
    qr3 = proj[:, H * dh:2 * H * dh].reshape(B, H, dh)
    o_cmp, imp = _nsa_dec_cmp(q3, cmp_tok, past_len, n_sel)
    idx = _topk_idx(imp.reshape(B * G, -1), n_top)[:, :n_top].reshape(B, G * n_top)
    n_past_blocks = past_len // SEL_BLOCK
    per_page = PAGE_SIZE // SEL_BLOCK
    sel_pages = jnp.take_along_axis(pages, jnp.minimum(idx, n_past_blocks - 1) // per_page, axis=1)
    kv0 = 2 * H * dh
    new_kv = proj[:, kv0:kv0 + 3 * width].reshape(B, 1, 3 * width)
    gl3 = proj[:, kv0 + 3 * width:].reshape(B, G, LANES)[:, :, :3 * R].reshape(B, G, 3, R)
    gl3 = gl3.transpose(0, 1, 3, 2).reshape(B, H, 3)
    Wn = win_state.shape[2]
    o = _nsa_dec_attn(idx, sel_pages, _feature_major(cache_sel), qr3, new_kv, _feature_major(win_state), slot * B,
                      o_cmp, gl3, n_past_blocks)
    kv = new_kv.reshape(B, 1, 3, G, 2, dh)
    new_win = jnp.concatenate([win_state[slot], kv[:, :, 2]], axis=1)[:, -Wn:]
    return o.reshape(B, H * dh), (kv[:, :, 0], kv[:, :, 1], new_win)


def _mla_front(h, w, q_tabs, kr_tabs):
    dn = _proj(h, w['w_dn'])
    c_q, ckr = _mla_post(dn, w['g_q'], w['g_kv'], kr_tabs)
    q_ext = _proj(c_q, w['w_q'], q_tabs, QK_ROPE // 2)
    return q_ext, ckr


def _mla_prompt(h, B, T, w, q_tabs, kr_tabs):
    q_ext, ckr = _mla_front(h, w, q_tabs, kr_tabs)
    kv_ext = _proj(ckr, w['w_kv'])
    y = _attn_mla(q_ext, kv_ext, B, T)
    return y, (ckr[:, :KV_LORA].reshape(B, T, KV_LORA), ckr[:, KV_LORA:KV_LORA + QK_ROPE].reshape(B, T, QK_ROPE))


def _mla_sample(h, w, q_tabs, kr_tabs, cache_ckv, cache_kr, slot, page_table):
    B = h.shape[0]
    H = MLA_HEADS
    n_pool = cache_ckv.shape[1]
    q_ext, ckr = _mla_front(h, w, q_tabs, kr_tabs)
    q_lat = _proj(q_ext, w['w_absorb']).reshape(B, H, KV_LORA)
    q_rope = q_ext.reshape(B, H, MLA_SLOT)[:, :, QK_NOPE:QK_NOPE + QK_ROPE]
    c_new = ckr[:, :KV_LORA].reshape(B, 1, KV_LORA)
    r_new = ckr[:, KV_LORA:KV_LORA + QK_ROPE].reshape(B, 1, QK_ROPE)
    pages = page_table + slot * n_pool
    o_lat = _mla_decode(pages, cache_ckv.reshape(-1, PAGE_SIZE, KV_LORA), _feature_major(cache_kr),
                        q_lat, q_rope, c_new, r_new, per_step=min(32, page_table.shape[1]))
    y = _proj(o_lat.reshape(B, H * KV_LORA), w['w_unabsorb'])
    return y, (c_new, r_new)


def _moba_prompt(h, B, T, w, tabs):
    KH, dh = MOBA_KV_HEADS, MOBA_HEAD_DIM
    proj = _proj(h, w['w_in'], tabs, dh // 2)
    y = _attn_moba(proj, B, T)
    return y, proj[:, MOBA_HEADS * dh:].reshape(B, T, KH, 2, dh)


def _moba_sample(h, past_len, w, tabs, cache, slot, page_table):
    B = h.shape[0]
    H, KH, dh = MOBA_HEADS, MOBA_KV_HEADS, MOBA_HEAD_DIM
    R = H // KH
    assert past_len % MOBA_BLOCK == 0 and past_len // MOBA_BLOCK >= MOBA_TOP
    n_pool = cache.shape[1]
    width = KH * 2 * dh
    ppb = MOBA_BLOCK // PAGE_SIZE
    proj = _proj(h, w['w_in'], tabs, dh // 2)
    pages = page_table + slot * n_pool
    cache_t = _feature_major(cache)
    means = _moba_means(pages, cache_t, per_step=min(16, page_table.shape[1]))
    q3 = proj[:, :H * dh].reshape(B, H, dh)
    scores = _moba_dec_score(q3, means, past_len // MOBA_BLOCK)
    idx = _topk_idx(scores.reshape(B * H, -1), MOBA_TOP)[:, :MOBA_TOP].reshape(B, H * MOBA_TOP)
    page_ids = jnp.take_along_axis(pages, (idx[:, :, None] * ppb + jnp.arange(ppb)).reshape(B, -1), axis=1)
    new_kv = proj[:, H * dh:].reshape(B, 1, width)
    o = _moba_dec_attn(page_ids, cache_t, q3.reshape(B, KH, R, dh), new_kv)
    return o.reshape(B, H * dh), new_kv.reshape(B, 1, KH, 2, dh)


def _moe_layer(h, w, g, b):
    gate = _moe_router(h, w['w_router'], w['b_router'])
    return _moe(h, gate, w, g, b)


def kernel(x_prompt, x_sample, cache_nsa_cmp, cache_nsa_sel, state_nsa_win, cache_mla_ckv, cache_mla_krope,
           cache_moba_kv, page_table, nsa_w_in, nsa_cmp_pe, nsa_cmp_w1, nsa_cmp_b1, nsa_cmp_w2, nsa_cmp_b2,
           nsa_w_o, mla_w_dn, mla_g_q, mla_w_uq, mla_g_kv, mla_w_uk, mla_w_uv, mla_w_o, moba_w_in, moba_w_o,
           ln1_g, ln1_b, ln2_g, ln2_b, moe_w_router, moe_b_router, moe_w_gate, moe_w_up, moe_w_down,
           moe_ws_gate, moe_ws_up, moe_ws_down):
    B, T, D = x_prompt.shape
    Bs, Ts, _ = x_sample.shape
    assert Ts == 1
    past_len = page_table.shape[1] * PAGE_SIZE
    assert state_nsa_win.shape[2] == WINDOW and past_len >= WINDOW
    pos_p = jnp.arange(T, dtype=I32)
    pos_s = jnp.full((Bs,), past_len, dtype=I32)
    hp = x_prompt.reshape(B * T, D)
    hs = x_sample.reshape(Bs, D)
    outs = {k: [] for k in ('cmp_p', 'cmp_s', 'sel_p', 'sel_s', 'win_p', 'win_s',
                            'ckv_p', 'ckv_s', 'kr_p', 'kr_s', 'mb_p', 'mb_s')}
    for i in range(DEPTH):
        kind, slot = i % N_MIXERS, i // N_MIXERS
        if kind == MIX_NSA:
            w = _nsa_weights(nsa_w_in[slot], nsa_cmp_pe[slot], nsa_cmp_w1[slot], nsa_cmp_b1[slot],
                             nsa_cmp_w2[slot], nsa_cmp_b2[slot], nsa_w_o[slot])
            yp, (a_p, b_p, c_p) = _nsa_prompt(hp, B, T, w, _nsa_tables(pos_p))
            ys, (a_s, b_s, c_s) = _nsa_sample(hs, past_len, w, _nsa_tables(pos_s), cache_nsa_cmp, cache_nsa_sel,
                                              state_nsa_win, slot, page_table)
            outs['cmp_p'].append(a_p); outs['cmp_s'].append(a_s)
            outs['sel_p'].append(b_p); outs['sel_s'].append(b_s)
            outs['win_p'].append(c_p); outs['win_s'].append(c_s)
        elif kind == MIX_MLA:
            w = _mla_weights(mla_w_dn[slot], mla_g_q[slot], mla_w_uq[slot], mla_g_kv[slot], mla_w_uk[slot],
                             mla_w_uv[slot], mla_w_o[slot])
            yp, (a_p, b_p) = _mla_prompt(hp, B, T, w, _mla_q_tables(pos_p), _mla_kr_tables(pos_p))
            ys, (a_s, b_s) = _mla_sample(hs, w, _mla_q_tables(pos_s), _mla_kr_tables(pos_s), cache_mla_ckv,
                                         cache_mla_krope, slot, page_table)
            outs['ckv_p'].append(a_p); outs['ckv_s'].append(a_s)
            outs['kr_p'].append(b_p); outs['kr_s'].append(b_s)
        else:
            w = _moba_weights(moba_w_in[slot], moba_w_o[slot])
            yp, a_p = _moba_prompt(hp, B, T, w, _moba_tables(pos_p))
            ys, a_s = _moba_sample(hs, past_len, w, _moba_tables(pos_s), cache_moba_kv, slot, page_table)
            outs['mb_p'].append(a_p); outs['mb_s'].append(a_s)
        hp = _proj_ln(yp, w['w_o'], hp, ln1_g[i], ln1_b[i])
        hs = _proj_ln(ys, w['w_o'], hs, ln1_g[i], ln1_b[i])
        mw = _moe_weights(moe_w_router, moe_b_router, moe_w_gate, moe_w_up, moe_w_down,
                          moe_ws_gate, moe_ws_up, moe_ws_down, layer=i)
        hp = _moe_layer(hp, mw, ln2_g[i], ln2_b[i])
        hs = _moe_layer(hs, mw, ln2_g[i], ln2_b[i])
    st = lambda k: jnp.stack(outs[k])
    return (hp.reshape(B, T, D), hs.reshape(Bs, Ts, D),
            st('cmp_p'), st('cmp_s'), st('sel_p'), st('sel_s'), st('win_p'), st('win_s'),
            st('ckv_p'), st('ckv_s'), st('kr_p'), st('kr_s'), st('mb_p'), st('mb_s'))
```

```python
import functools

import jax
import jax.numpy as jnp
from jax import lax
from jax.experimental import pallas as pl
from jax.experimental.pallas import tpu as pltpu

D_MODEL = 1024
DEPTH = 4
PAGE_SIZE = 128
N_MIXERS = 3
MIX_NSA, MIX_MLA, MIX_MOBA = 0, 1, 2

ALPHA = (2 * DEPTH) ** 0.25
LN_EPS = 1e-5
RMS_EPS = 1e-6
ROPE_THETA = 10000.0
NEG = -1e30
FORCE = 1e9

NSA_HEADS = 16
NSA_KV_HEADS = 2
NSA_HEAD_DIM = 64
CMP_BLOCK = 32
CMP_STRIDE = 16
CMP_HIDDEN = 128
SEL_BLOCK = 64
SEL_RATIO = SEL_BLOCK // CMP_STRIDE
SEL_TOP = 16
WINDOW = 512

MLA_HEADS = 16
Q_LORA = 384
KV_LORA = 256
QK_NOPE = 64
QK_ROPE = 32
V_DIM = 64
MLA_SLOT = 128

MOBA_HEADS = 16
MOBA_KV_HEADS = 4
MOBA_HEAD_DIM = 64
MOBA_BLOCK = 256
MOBA_TOP = 3

N_EXPERTS = 64
TOP_K = 8
D_EXPERT = 128
D_SHARED = 128
ROUTED_SCALE = 2.5

LANES = 128
FLASH_CHAINS = 1
MOE_GROUP = 4
ONES_ROWS = 16
VMEM_LIMIT = 48 * 1024 * 1024

F32 = jnp.float32
BF16 = jnp.bfloat16
I32 = jnp.int32


def _cp(*sem):
    return pltpu.CompilerParams(dimension_semantics=sem, vmem_limit_bytes=VMEM_LIMIT)


def _dot(a, b):
    return jnp.dot(a, b, preferred_element_type=F32)


def _dot_nt(a, b):
    return lax.dot_general(a, b, (((1,), (1,)), ((), ())), preferred_element_type=F32)


def _dot_exact(a, b):
    return jnp.dot(a, b, preferred_element_type=F32, precision=lax.Precision.HIGHEST)


def _iota(shape, axis):
    return lax.broadcasted_iota(I32, shape, axis)


def _layer_norm(z, g, b):
    mu = jnp.mean(z, axis=-1, keepdims=True)
    d = z - mu
    var = jnp.mean(d * d, axis=-1, keepdims=True)
    return d * lax.rsqrt(var + LN_EPS) * g + b


def _silu(x):
    return x * jax.nn.sigmoid(x)


def _softmax_rows(s, valid):
    s = jnp.where(valid, s, NEG)
    m = jnp.max(s, axis=-1, keepdims=True)
    e = jnp.where(valid, jnp.exp(s - m), 0.0)
    l = jnp.sum(e, axis=-1, keepdims=True)
    return e / jnp.where(l > 0.0, l, 1.0)


def _rank_before(x, n_cols, lane):
    rank = jnp.zeros(x.shape, F32)
    for c in range(n_cols):
        col = x[:, c:c + 1]
        rank = rank + jnp.where(col > x, 1.0, jnp.where(col == x, jnp.where(lane > c, 1.0, 0.0), 0.0))
    return rank


def _proj_kernel(x_ref, w_ref, o_ref):
    o_ref[...] = _dot(x_ref[...].astype(BF16), w_ref[...])


def _proj_rope_kernel(x_ref, w_ref, a_ref, b_ref, mask_ref, o_ref, *, half, tn):
    acc = _dot(x_ref[...].astype(BF16), w_ref[...])
    lane = _iota(acc.shape, 1)
    first = (lane % (2 * half)) < half
    partner = jnp.where(first, pltpu.roll(acc, tn - half, 1), pltpu.roll(acc, half, 1))
    rotary = mask_ref[...] > 0.5
    a = jnp.where(rotary, jnp.concatenate([a_ref[...]] * (tn // LANES), axis=1), 1.0)
    b = jnp.where(rotary, jnp.concatenate([b_ref[...]] * (tn // LANES), axis=1), 0.0)
    o_ref[...] = acc * a + partner * b


def _col_tile(n):
    return next(t for t in (512, 384, 256, 128) if n % t == 0)


def _proj(x, w, tabs=None, half=0):
    M, K = x.shape
    N = w.shape[1]
    tm = min(M, 1024)
    tn = _col_tile(N)
    grid = (M // tm, N // tn)
    x_spec = pl.BlockSpec((tm, K), lambda i, j: (i, 0))
    w_spec = pl.BlockSpec((K, tn), lambda i, j: (0, j))
    o_spec = pl.BlockSpec((tm, tn), lambda i, j: (i, j))
    out_shape = jax.ShapeDtypeStruct((M, N), F32)
    if tabs is None:
        return pl.pallas_call(_proj_kernel, out_shape=out_shape, grid=grid, in_specs=[x_spec, w_spec],
                              out_specs=o_spec, compiler_params=_cp("parallel", "parallel"), name="proj")(x, w)
    a, bm, mask = tabs
    nrb = a.shape[0] // tm
    t_spec = pl.BlockSpec((tm, LANES), lambda i, j: (i % nrb, 0))
    m_spec = pl.BlockSpec((1, tn), lambda i, j: (0, j))
    return pl.pallas_call(functools.partial(_proj_rope_kernel, half=half, tn=tn), out_shape=out_shape, grid=grid,
                          in_specs=[x_spec, w_spec, t_spec, t_spec, m_spec], out_specs=o_spec,
                          compiler_params=_cp("parallel", "parallel"), name="proj_rope")(x, w, a, bm, mask)


def _proj_ln_kernel(x_ref, w_ref, r_ref, g_ref, b_ref, wr_ref, br_ref, o_ref, gate_ref):
    y = _dot(x_ref[...].astype(BF16), w_ref[...])
    h = _layer_norm(ALPHA * r_ref[...] + y, g_ref[...], b_ref[...])
    o_ref[...] = h
    gate_ref[...] = _route(h, wr_ref[...], br_ref[...])


def _proj_ln(x, w, res, g, b, w_router_t, b_router):
    M, K = x.shape
    N = w.shape[1]
    E = w_router_t.shape[0]
    tm = min(M, 512)
    row = lambda i: (i, 0)
    fix = lambda i: (0, 0)
    return pl.pallas_call(
        _proj_ln_kernel,
        out_shape=(jax.ShapeDtypeStruct((M, N), F32), jax.ShapeDtypeStruct((M, LANES), F32)), grid=(M // tm,),
        in_specs=[pl.BlockSpec((tm, K), row), pl.BlockSpec((K, N), fix), pl.BlockSpec((tm, N), row),
                  pl.BlockSpec((1, N), fix), pl.BlockSpec((1, N), fix),
                  pl.BlockSpec((E, N), fix), pl.BlockSpec((E, 1), fix)],
        out_specs=(pl.BlockSpec((tm, N), row), pl.BlockSpec((tm, LANES), row)),
        compiler_params=_cp("parallel"), name="proj_ln",
    )(x, w, res, g.reshape(1, N), b.reshape(1, N), w_router_t, b_router)


def _rope_unit(pos, half):
    inv = jnp.power(ROPE_THETA, -jnp.arange(half, dtype=F32) / half)
    ang = pos.astype(F32)[:, None] * inv
    cos, sin = jnp.cos(ang), jnp.sin(ang)
    return jnp.concatenate([cos, cos], axis=1), jnp.concatenate([-sin, sin], axis=1)


def _unit_tables(pos, half):
    ua, ub = _rope_unit(pos, half)
    reps = LANES // (2 * half)
    return jnp.tile(ua, (1, reps)), jnp.tile(ub, (1, reps))


def _column_mask(layout):
    return jnp.concatenate([jnp.full((1, n), v, F32) for v, n in layout], axis=1)


def _tables(pos, half, layout):
    ua, ub = _rope_unit(pos, half)
    T = pos.shape[0]
    a_parts, b_parts = [], []
    for kind, n in layout:
        if kind == 'r':
            a_parts.append(jnp.tile(ua, (1, n)))
            b_parts.append(jnp.tile(ub, (1, n)))
        else:
            a_parts.append(jnp.ones((T, n), F32))
            b_parts.append(jnp.zeros((T, n), F32))
    return jnp.concatenate(a_parts, axis=1), jnp.concatenate(b_parts, axis=1)


def _add_bias(s, bias, heads):
    if heads == 1:
        return s + bias
    tq = s.shape[1] // heads
    return jnp.concatenate([s[:, h * tq:(h + 1) * tq] + bias for h in range(heads)], axis=1)


def _weighted_values(vt_ref, first, e):
    acc = _dot(vt_ref[first], e[:LANES])
    for j in range(1, e.shape[0] // LANES):
        acc = acc + _dot(vt_ref[first + j], e[j * LANES:(j + 1) * LANES])
    return acc


def _flash(qs, k_ref, k_cols, vt_ref, vt_base, m_sc, acc_sc, lo, hi, tk, bias_fn, heads):
    m_sc[...] = jnp.full(m_sc.shape, NEG, F32)
    acc_sc[...] = jnp.zeros(acc_sc.shape, F32)
    sub = tk // LANES

    def body(kt, carry):
        k0 = pl.multiple_of(kt * tk, tk)
        s = _add_bias(_dot_nt(k_ref[pl.ds(k0, tk), k_cols], qs), bias_fn(kt, k0), heads)
        m_old = m_sc[...]
        m_new = jnp.maximum(m_old, jnp.max(s, axis=0, keepdims=True))
        e = jnp.exp(s - m_new).astype(BF16)
        acc_sc[...] = jnp.exp(m_old - m_new) * acc_sc[...] + _weighted_values(vt_ref, vt_base + kt * sub, e)
        m_sc[...] = m_new
        return carry

    lax.fori_loop(lo, hi, body, 0)
    return acc_sc[...]


def _transpose_rows(x):
    eye = jnp.where(_iota((LANES, LANES), 0) == _iota((LANES, LANES), 1), 1.0, 0.0).astype(BF16)
    return _dot_nt(eye, x)


def _store_values_t(vt_ref, base, vt):
    n = vt.shape[1]
    full = jnp.concatenate([jnp.ones((ONES_ROWS, n), F32), vt], axis=0).astype(BF16)
    for j in range(n // LANES):
        vt_ref[base + j] = full[:, j * LANES:(j + 1) * LANES]


def _write_heads_t(o_ref, heads):
    tq = heads[0].shape[1]
    for p in range(len(heads) // 2):
        pair = jnp.concatenate([heads[2 * p], heads[2 * p + 1]], axis=0)
        for c in range(tq // LANES):
            o_ref[c * LANES:(c + 1) * LANES, p * LANES:(p + 1) * LANES] = pair[:, c * LANES:(c + 1) * LANES].T


def _stack_heads(ref, n_heads, dh, scale):
    parts = []
    for r in range(n_heads):
        pair = ref[:, (r // 2) * 2 * dh:(r // 2 + 1) * 2 * dh] * scale
        lane = _iota(pair.shape, 1)
        own = (lane < dh) if r % 2 == 0 else (lane >= dh)
        parts.append(jnp.where(own, pair, 0.0))
    return jnp.concatenate(parts, axis=0).astype(BF16)


def _dup_keys(kv, dh):
    return jnp.where(_iota(kv.shape, 1) < dh, kv, pltpu.roll(kv, dh, 1))


def _cmp_lohi_kernel(*refs, n_pages, rows_per_page, feature_major):
    G = NSA_KV_HEADS
    if feature_major:
        refs, x_sc = refs[1:-1], refs[-1]
    n_in = n_pages if feature_major else G
    row_refs, w_ref, o_ref = refs[:n_in], refs[n_in], refs[n_in + 1]
    cpp = rows_per_page // CMP_STRIDE
    nh = w_ref.shape[2] // 2

    def chunk_rows(g, r):
        if feature_major:
            return x_sc[pl.ds(r, n_pages * cpp, stride=CMP_STRIDE), :]
        return row_refs[g][0, pl.ds(r, cpp, stride=CMP_STRIDE), :]

    for g in range(G):
        if feature_major:
            for p in range(n_pages):
                x_sc[p * rows_per_page:(p + 1) * rows_per_page, :] = row_refs[p][0, g * LANES:(g + 1) * LANES, :].T
        acc = jnp.zeros((n_pages * cpp, 2 * nh), F32)
        for r2 in range(CMP_STRIDE // 2):
            xr = jnp.concatenate([chunk_rows(g, 2 * r2), chunk_rows(g, 2 * r2 + 1)], axis=1)
            acc = acc + _dot(xr.astype(BF16), w_ref[r2])
        o_ref[0, :, g * nh:(g + 1) * nh] = acc[:, :nh]
        o_ref[0, :, (G + g) * nh:(G + g + 1) * nh] = acc[:, nh:]


def _cmp_lohi(rows, w_lohi, col_block=0):
    B, L = rows.shape[0], rows.shape[1]
    G = NSA_KV_HEADS
    nout = G * w_lohi.shape[2]
    row_specs = [pl.BlockSpec((1, L, LANES), functools.partial(lambda b, g: (b, 0, col_block + g), g=g))
                 for g in range(G)]
    return pl.pallas_call(
        functools.partial(_cmp_lohi_kernel, n_pages=1, rows_per_page=L, feature_major=False),
        out_shape=jax.ShapeDtypeStruct((B, L // CMP_STRIDE, nout), F32), grid=(B,),
        in_specs=row_specs + [pl.BlockSpec(w_lohi.shape, lambda b: (0, 0, 0))],
        out_specs=pl.BlockSpec((1, L // CMP_STRIDE, nout), lambda b: (b, 0, 0)),
        compiler_params=_cp("parallel"), name="cmp_lohi",
    )(*([rows] * G), w_lohi)


def _cmp_lohi_paged(cache_t, pages, w_lohi, per_step):
    B, n_pages = pages.shape
    G = NSA_KV_HEADS
    nout = G * w_lohi.shape[2]
    cpp = PAGE_SIZE // CMP_STRIDE
    page_specs = [pl.BlockSpec((1, G * LANES, PAGE_SIZE), functools.partial(
        lambda b, s, pt, p: (pt[b, s * per_step + p], 0, 0), p=p)) for p in range(per_step)]
    grid_spec = pltpu.PrefetchScalarGridSpec(
        num_scalar_prefetch=1, grid=(B, n_pages // per_step),
        in_specs=page_specs + [pl.BlockSpec(w_lohi.shape, lambda b, s, pt: (0, 0, 0))],
        out_specs=pl.BlockSpec((1, per_step * cpp, nout), lambda b, s, pt: (b, s, 0)),
        scratch_shapes=[pltpu.VMEM((per_step * PAGE_SIZE, LANES), F32)])
    return pl.pallas_call(
        functools.partial(_cmp_lohi_kernel, n_pages=per_step, rows_per_page=PAGE_SIZE, feature_major=True),
        out_shape=jax.ShapeDtypeStruct((B, n_pages * cpp, nout), F32), grid_spec=grid_spec,
        compiler_params=_cp("parallel", "parallel"), name="cmp_lohi_paged",
    )(pages, *([cache_t] * per_step), w_lohi)


def _cmp_combine_kernel(lohi_ref, pe_ref, b1_ref, w2_ref, b2_ref, o_ref):
    nch = lohi_ref.shape[1]
    nh = lohi_ref.shape[2] // 2
    lo = lohi_ref[0, :, :nh]
    hi_next = pltpu.roll(lohi_ref[0, :, nh:], nch - 1, 0)
    pe = pe_ref[0, 0:1, :nh] + pe_ref[0, 1:2, nh:]
    hid = jax.nn.gelu(lo + hi_next + pe + b1_ref[...])
    o_ref[0] = _dot(hid.astype(BF16), w2_ref[...]) + b2_ref[...]


def _cmp_combine(lohi, pe_lohi, b1, w2, b2):
    B, nch, n2 = lohi.shape
    nh = n2 // 2
    nout = w2.shape[1]
    fix2 = lambda b: (0, 0)
    return pl.pallas_call(
        _cmp_combine_kernel, out_shape=jax.ShapeDtypeStruct((B, nch, nout), F32), grid=(B,),
        in_specs=[pl.BlockSpec((1, nch, n2), lambda b: (b, 0, 0)), pl.BlockSpec((1, 2, n2), lambda b: (0, 0, 0)),
                  pl.BlockSpec((1, nh), fix2), pl.BlockSpec((nh, nout), fix2), pl.BlockSpec((1, nout), fix2)],
        out_specs=pl.BlockSpec((1, nch, nout), lambda b: (b, 0, 0)),
        compiler_params=_cp("parallel"), name="cmp_combine",
    )(lohi, pe_lohi, b1, w2, b2)


def _sel_importance(p_sum, n_lanes):
    nc = p_sum.shape[1]
    c = _iota((nc, n_lanes), 0)
    j = _iota((nc, n_lanes), 1)
    a = jnp.where(c >= SEL_RATIO * j - 1, jnp.where(c <= SEL_RATIO * j + SEL_RATIO - 1, 1.0, 0.0), 0.0)
    return _dot_exact(p_sum, a)


def _attn_nsa_kernel(q_ref, qr_ref, cmp_ref, sel_ref, win_ref, gl_ref, o_ref,
                     k2s, vts, k2w, vtw, bias_ref, m_sc, acc_sc, *, T, tq, tk, n_sel, n_top):
    i = pl.program_id(2)
    t0 = i * tq
    R = NSA_HEADS // NSA_KV_HEADS
    dh = NSA_HEAD_DIM
    scale = dh ** -0.5
    wk = min(T, WINDOW + tq)

    @pl.when(i == 0)
    def _():
        for src, k_sc, vt_sc in ((sel_ref, k2s, vts), (win_ref, k2w, vtw)):
            kv = src[...]
            k_sc[...] = _dup_keys(kv, dh).astype(BF16)
            _store_values_t(vt_sc, 0, _transpose_rows(kv.astype(BF16))[dh:])

    tpos = t0 + _iota((1, tq), 1)
    cmp_tok = cmp_ref[0]
    nc = cmp_tok.shape[0]
    s = _dot_nt(_dup_keys(cmp_tok, dh).astype(BF16), _stack_heads(q_ref, R, dh, scale))
    cvalid = (_iota((nc, 1), 0) * CMP_STRIDE + CMP_BLOCK - 1) <= tpos
    probs = []
    for h in range(R):
        sh = jnp.where(cvalid, s[:, h * tq:(h + 1) * tq], NEG)
        e = jnp.where(cvalid, jnp.exp(sh - jnp.max(sh, axis=0, keepdims=True)), 0.0)
        l = jnp.sum(e, axis=0, keepdims=True)
        probs.append(e / jnp.where(l > 0.0, l, 1.0))
    p_sum = probs[0]
    for h in range(1, R):
        p_sum = p_sum + probs[h]
    vct = _transpose_rows(cmp_tok.astype(BF16))[dh:].astype(BF16)
    o_cmp = _dot(vct, jnp.concatenate(probs, axis=1).astype(BF16))

    ns = bias_ref.shape[0] // SEL_BLOCK
    blk = _iota((ns, tq), 0)
    cur = tpos // SEL_BLOCK
    c_idx = _iota((ns, nc), 1)
    lo_c = SEL_RATIO * _iota((ns, nc), 0) - 1
    spread = jnp.where(c_idx >= lo_c, jnp.where(c_idx <= lo_c + SEL_RATIO, 1.0, 0.0), 0.0)
    imp = _dot_exact(spread, p_sum)
    forced = jnp.where(blk == 0, 1.0, jnp.where(blk == cur, 1.0, jnp.where(blk == cur - 1, 1.0, 0.0)))
    imp = jnp.where(blk > cur, NEG, jnp.where(forced > 0.5, FORCE, imp))
    rank = jnp.zeros(imp.shape, F32)
    for c in range(n_sel):
        row = imp[c:c + 1, :]
        rank = rank + jnp.where(row > imp, 1.0, jnp.where(row == imp, jnp.where(blk > c, 1.0, 0.0), 0.0))
    chosen = jnp.where(blk < n_sel, jnp.where(rank < n_top, 1.0, 0.0), 0.0)
    for j in range(n_sel):
        picked = jnp.broadcast_to(chosen[j:j + 1, :], (SEL_BLOCK, tq))
        kpos = j * SEL_BLOCK + _iota((SEL_BLOCK, 1), 0)
        bias_ref[j * SEL_BLOCK:(j + 1) * SEL_BLOCK, :] = jnp.where(
            picked > 0.5, jnp.where(kpos <= tpos, 0.0, NEG), NEG)

    hi = (t0 + tq + tk - 1) // tk
    qrs = _stack_heads(qr_ref, R, dh, scale)
    acc = _flash(qrs, k2s, slice(None), vts, 0, m_sc, acc_sc, 0, hi, tk,
                 lambda kt, k0: bias_ref[pl.ds(k0, tk), :], R)
    o_sel = acc[ONES_ROWS:] / acc[0:1]
    kw0 = pl.multiple_of(jnp.clip(t0 - WINDOW, 0, T - wk), LANES)
    d = tpos - (kw0 + _iota((wk, 1), 0))
    win_bias = jnp.where(d >= 0, jnp.where(d <= WINDOW, 0.0, NEG), NEG)
    s = _add_bias(_dot_nt(k2w[pl.ds(kw0, wk), :], qrs), win_bias, R)
    e = jnp.exp(s - jnp.max(s, axis=0, keepdims=True)).astype(BF16)
    acc = _weighted_values(vtw, kw0 // LANES, e)
    o_win = acc[ONES_ROWS:] / acc[0:1]
    gate = jax.nn.sigmoid(gl_ref[...].T)
    heads = []
    for h in range(R):
        hs = slice(h * tq, (h + 1) * tq)
        heads.append(gate[h:h + 1] * o_cmp[:, hs] + gate[R + h:R + h + 1] * o_sel[:, hs]
                     + gate[2 * R + h:2 * R + h + 1] * o_win[:, hs])
    _write_heads_t(o_ref, heads)


def _attn_nsa(proj, cmp_tok, B, T, tk=None):
    G = NSA_KV_HEADS
    R = NSA_HEADS // G
    tq = 2 * LANES
    tk = tk or min(T, 512)
    nq = T // tq
    n_sel = -(-T // SEL_BLOCK)
    n_top = min(SEL_TOP, n_sel)
    qw = NSA_HEADS * NSA_HEAD_DIM // G
    kv0 = 2 * NSA_HEADS * NSA_HEAD_DIM // LANES
    gl0 = kv0 + 3 * G
    nch = cmp_tok.shape[1]
    dh = NSA_HEAD_DIM
    kern = functools.partial(_attn_nsa_kernel, T=T, tq=tq, tk=tk, n_sel=n_sel, n_top=n_top)
    return pl.pallas_call(
        kern, out_shape=jax.ShapeDtypeStruct((B * T, NSA_HEADS * dh), F32), grid=(B, G, nq),
        in_specs=[pl.BlockSpec((tq, qw), lambda b, g, i: (b * nq + i, g)),
                  pl.BlockSpec((tq, qw), lambda b, g, i: (b * nq + i, G + g)),
                  pl.BlockSpec((1, nch, 2 * dh), lambda b, g, i: (b, 0, g)),
                  pl.BlockSpec((T, 2 * dh), lambda b, g, i: (b, kv0 + G + g)),
                  pl.BlockSpec((T, 2 * dh), lambda b, g, i: (b, kv0 + 2 * G + g)),
                  pl.BlockSpec((tq, LANES), lambda b, g, i: (b * nq + i, gl0 + g))],
        out_specs=pl.BlockSpec((tq, qw), lambda b, g, i: (b * nq + i, g)),
        scratch_shapes=[pltpu.VMEM((T, 2 * dh), BF16), pltpu.VMEM((T // LANES, ONES_ROWS + dh, LANES), BF16)] * 2 + [
            pltpu.VMEM((T, tq), F32), pltpu.VMEM((1, R * tq), F32), pltpu.VMEM((ONES_ROWS + dh, R * tq), F32)],
        compiler_params=_cp("parallel", "parallel", "arbitrary"), name="attn_nsa",
    )(proj, proj, cmp_tok, proj, proj, proj)


def _nsa_dec_cmp_kernel(q_ref, cmp_ref, o_ref, imp_ref, *, t, n_sel):
    G = NSA_KV_HEADS
    R = NSA_HEADS // G
    dh = NSA_HEAD_DIM
    nc = cmp_ref.shape[1]
    nl = imp_ref.shape[2]
    cvalid = (_iota((1, nc), 1) * CMP_STRIDE + CMP_BLOCK - 1) <= t
    blk = _iota((1, nl), 1)
    cur = t // SEL_BLOCK
    for g in range(G):
        kc = cmp_ref[0, :, 2 * g * dh:(2 * g + 1) * dh].astype(BF16)
        vc = cmp_ref[0, :, (2 * g + 1) * dh:(2 * g + 2) * dh].astype(BF16)
        qg = (q_ref[0, g * R:(g + 1) * R, :] * dh ** -0.5).astype(BF16)
        p = _softmax_rows(_dot_nt(qg, kc), cvalid)
        o_ref[0, g * R:(g + 1) * R, :] = _dot(p.astype(BF16), vc)
        imp = _sel_importance(jnp.sum(p, axis=0, keepdims=True), nl)
        forced = jnp.where(blk == 0, 1.0, jnp.where(blk == cur, 1.0, jnp.where(blk == cur - 1, 1.0, 0.0)))
        imp = jnp.where(blk > cur, NEG, jnp.where(forced > 0.5, FORCE, imp))
        imp_ref[0, g:g + 1, :] = jnp.where(blk < n_sel, imp, -jnp.inf)


def _nsa_dec_cmp(q3, cmp_tok, t, n_sel):
    B = q3.shape[0]
    nl = -(-n_sel // LANES) * LANES
    nc = cmp_tok.shape[1]
    blk3 = lambda b: (b, 0, 0)
    return pl.pallas_call(
        functools.partial(_nsa_dec_cmp_kernel, t=t, n_sel=n_sel),
        out_shape=(jax.ShapeDtypeStruct(q3.shape, F32), jax.ShapeDtypeStruct((B, NSA_KV_HEADS, nl), F32)),
        grid=(B,),
        in_specs=[pl.BlockSpec((1,) + q3.shape[1:], blk3), pl.BlockSpec((1, nc, cmp_tok.shape[2]), blk3)],
        out_specs=(pl.BlockSpec((1,) + q3.shape[1:], blk3), pl.BlockSpec((1, NSA_KV_HEADS, nl), blk3)),
        compiler_params=_cp("parallel"), name="nsa_dec_cmp",
    )(q3, cmp_tok)


def _topk_idx_kernel(x_ref, o_ref, *, k):
    x = x_ref[...]
    lane = _iota(x.shape, 1)
    out_lane = _iota(o_ref.shape, 1)
    out = jnp.zeros(o_ref.shape, I32)
    big = x.shape[1]
    for n in range(k):
        m = jnp.max(x, axis=-1, keepdims=True)
        idx = jnp.min(jnp.where(x == m, lane, big), axis=-1, keepdims=True)
        out = jnp.where(out_lane == n, idx, out)
        x = jnp.where(lane == idx, -jnp.inf, x)
    o_ref[...] = out


def _topk_idx(x, k):
    rows = x.shape[0]
    return pl.pallas_call(functools.partial(_topk_idx_kernel, k=k),
                          out_shape=jax.ShapeDtypeStruct((rows, LANES), I32), name="topk_idx")(x)


def _nsa_dec_attn_kernel(*refs, n_top, n_past_blocks):
    idx_ref, rb_ref = refs[0], refs[1]
    G = NSA_KV_HEADS
    R = NSA_HEADS // G
    dh = NSA_HEAD_DIM
    blocks = refs[2:2 + G * n_top]
    qr_ref, new_ref, win_ref, ocmp_ref, gl_ref, o_ref = refs[2 + G * n_top:]
    b = pl.program_id(0)
    scale = dh ** -0.5
    per_page = PAGE_SIZE // SEL_BLOCK
    nk = n_top * PAGE_SIZE
    lane = _iota((1, nk), 1)
    for g in range(G):
        qg = qr_ref[0, g * R:(g + 1) * R, :] * scale
        qb = qg.astype(BF16)
        kt = jnp.concatenate([blocks[g * n_top + n][0, :dh, :] for n in range(n_top)], axis=1).astype(BF16)
        vt = jnp.concatenate([blocks[g * n_top + n][0, dh:, :] for n in range(n_top)], axis=1).astype(BF16)
        valid = jnp.zeros((1, nk), F32)
        for n in range(n_top):
            blk = idx_ref[b, g * n_top + n]
            flag = jnp.where(blk < n_past_blocks, 1.0, 0.0)
            in_block = jnp.where((lane % PAGE_SIZE) // SEL_BLOCK == blk % per_page, flag, 0.0)
            valid = jnp.where(lane // PAGE_SIZE == n, in_block, valid)
        valid = valid > 0.5
        k_new = new_ref[0, :, (2 * G + 2 * g) * dh:(2 * G + 2 * g + 1) * dh]
        v_new = new_ref[0, :, (2 * G + 2 * g + 1) * dh:(2 * G + 2 * g + 2) * dh]
        s = jnp.where(valid, _dot(qb, kt), NEG)
        s_new = jnp.sum(qg * k_new, axis=-1, keepdims=True)
        m = jnp.maximum(jnp.max(s, axis=-1, keepdims=True), s_new)
        e = jnp.where(valid, jnp.exp(s - m), 0.0)
        e_new = jnp.exp(s_new - m)
        o_sel = (_dot_nt(e.astype(BF16), vt) + e_new * v_new) / (jnp.sum(e, axis=-1, keepdims=True) + e_new)
        kwin = win_ref[0, 2 * g * dh:(2 * g + 1) * dh, :].astype(BF16)
        vwin = win_ref[0, (2 * g + 1) * dh:(2 * g + 2) * dh, :].astype(BF16)
        k_new = new_ref[0, :, (4 * G + 2 * g) * dh:(4 * G + 2 * g + 1) * dh]
        v_new = new_ref[0, :, (4 * G + 2 * g + 1) * dh:(4 * G + 2 * g + 2) * dh]
        s = _dot(qb, kwin)
        s_new = jnp.sum(qg * k_new, axis=-1, keepdims=True)
        m = jnp.maximum(jnp.max(s, axis=-1, keepdims=True), s_new)
        e = jnp.exp(s - m)
        e_new = jnp.exp(s_new - m)
        o_win = (_dot_nt(e.astype(BF16), vwin) + e_new * v_new) / (jnp.sum(e, axis=-1, keepdims=True) + e_new)
        gate = jax.nn.sigmoid(gl_ref[0, g * R:(g + 1) * R, :])
        o_ref[0, g * R:(g + 1) * R, :] = (gate[:, 0:1] * ocmp_ref[0, g * R:(g + 1) * R, :]
                                          + gate[:, 1:2] * o_sel + gate[:, 2:3] * o_win)


def _nsa_dec_attn(idx, sel_pages, cache_t, qr3, new_kv, win_buf, win_index0, o_cmp, gl3, n_past_blocks):
    B = qr3.shape[0]
    G = NSA_KV_HEADS
    n_top = idx.shape[1] // G
    dh = NSA_HEAD_DIM
    blk3 = lambda b, idx, rb: (b, 0, 0)
    block_specs = [pl.BlockSpec((1, 2 * dh, PAGE_SIZE), functools.partial(
        lambda b, idx, rb, n, g: (rb[b, n], g, 0), n=g * n_top + n, g=g)) for g in range(G) for n in range(n_top)]
    grid_spec = pltpu.PrefetchScalarGridSpec(
        num_scalar_prefetch=2, grid=(B,),
        in_specs=block_specs + [
            pl.BlockSpec((1,) + qr3.shape[1:], blk3), pl.BlockSpec((1,) + new_kv.shape[1:], blk3),
            pl.BlockSpec((1,) + win_buf.shape[1:], lambda b, idx, rb: (win_index0 + b, 0, 0)),
            pl.BlockSpec((1,) + o_cmp.shape[1:], blk3), pl.BlockSpec((1,) + gl3.shape[1:], blk3)],
        out_specs=pl.BlockSpec((1,) + qr3.shape[1:], blk3))
    return pl.pallas_call(
        functools.partial(_nsa_dec_attn_kernel, n_top=n_top, n_past_blocks=n_past_blocks),
        out_shape=jax.ShapeDtypeStruct(qr3.shape, F32), grid_spec=grid_spec,
        compiler_params=_cp("parallel"), name="nsa_dec_attn",
    )(idx, sel_pages, *([cache_t] * (G * n_top)), qr3, new_kv, win_buf, o_cmp, gl3)


def _mla_post_kernel(dn_ref, gq_ref, gkv_ref, a_ref, b_ref, cq_ref, ckr_ref):
    x = dn_ref[...]
    cq = x[:, :Q_LORA]
    cq_ref[...] = cq * lax.rsqrt(jnp.mean(cq * cq, axis=-1, keepdims=True) + RMS_EPS) * gq_ref[...]
    ckv = x[:, Q_LORA:Q_LORA + KV_LORA]
    ckv = ckv * lax.rsqrt(jnp.mean(ckv * ckv, axis=-1, keepdims=True) + RMS_EPS) * gkv_ref[...]
    kr = x[:, Q_LORA + KV_LORA:]
    half = QK_ROPE // 2
    first = (_iota(kr.shape, 1) % QK_ROPE) < half
    partner = jnp.where(first, pltpu.roll(kr, LANES - half, 1), pltpu.roll(kr, half, 1))
    ckr_ref[...] = jnp.concatenate([ckv, kr * a_ref[...] + partner * b_ref[...]], axis=1)


def _mla_post(dn, g_q, g_kv, tabs):
    M, N = dn.shape
    tm = min(M, 512)
    nrb = tabs[0].shape[0] // tm
    row = lambda i: (i, 0)
    fix = lambda i: (0, 0)
    tab = pl.BlockSpec((tm, LANES), lambda i: (i % nrb, 0))
    return pl.pallas_call(
        _mla_post_kernel,
        out_shape=(jax.ShapeDtypeStruct((M, Q_LORA), F32), jax.ShapeDtypeStruct((M, KV_LORA + LANES), F32)),
        grid=(M // tm,),
        in_specs=[pl.BlockSpec((tm, N), row), pl.BlockSpec((1, Q_LORA), fix), pl.BlockSpec((1, KV_LORA), fix),
                  tab, tab],
        out_specs=(pl.BlockSpec((tm, Q_LORA), row), pl.BlockSpec((tm, KV_LORA + LANES), row)),
        compiler_params=_cp("parallel"), name="mla_post",
    )(dn, g_q.reshape(1, -1), g_kv.reshape(1, -1), *tabs)


def _attn_mla_kernel(q_ref, k_ref, v_ref, o_ref, k_sc, v_sc, m_sc, acc_sc, *, tq, tk):
    i = pl.program_id(2)
    t0 = i * tq
    scale = (QK_NOPE + QK_ROPE) ** -0.5

    n_vt = k_ref.shape[0] // LANES

    @pl.when(i == 0)
    def _():
        k_sc[...] = k_ref[...].astype(BF16)
        vt = _transpose_rows(v_ref[...].astype(BF16))
        for hh in range(2):
            _store_values_t(v_sc, hh * n_vt, vt[hh * V_DIM:(hh + 1) * V_DIM])

    tpos = t0 + _iota((1, tq), 1)
    hi = (t0 + tq + tk - 1) // tk
    causal = lambda kt, k0: jnp.where(k0 + _iota((tk, 1), 0) <= tpos, 0.0, NEG)
    heads = []
    for hh in range(2):
        qh = (q_ref[:, hh * MLA_SLOT:(hh + 1) * MLA_SLOT] * scale).astype(BF16)
        acc = _flash(qh, k_sc, slice(hh * MLA_SLOT, (hh + 1) * MLA_SLOT), v_sc, hh * n_vt, m_sc, acc_sc, 0, hi, tk,
                     causal, 1)
        heads.append(acc[ONES_ROWS:] / acc[0:1])
    _write_heads_t(o_ref, heads)


def _attn_mla(q_ext, kv_ext, B, T, tk=None):
    H = MLA_HEADS
    tq = min(T, 1024)
    tk = tk or min(T, 512)
    nq = T // tq
    v0 = H * MLA_SLOT // LANES
    return pl.pallas_call(
        functools.partial(_attn_mla_kernel, tq=tq, tk=tk),
        out_shape=jax.ShapeDtypeStruct((B * T, H * V_DIM), F32), grid=(B, H // 2, nq),
        in_specs=[pl.BlockSpec((tq, 2 * MLA_SLOT), lambda b, h, i: (b * nq + i, h)),
                  pl.BlockSpec((T, 2 * MLA_SLOT), lambda b, h, i: (b, h)),
                  pl.BlockSpec((T, 2 * V_DIM), lambda b, h, i: (b, v0 + h))],
        out_specs=pl.BlockSpec((tq, 2 * V_DIM), lambda b, h, i: (b * nq + i, h)),
        scratch_shapes=[pltpu.VMEM((T, 2 * MLA_SLOT), BF16),
                        pltpu.VMEM((2 * (T // LANES), ONES_ROWS + V_DIM, LANES), BF16),
                        pltpu.VMEM((1, tq), F32), pltpu.VMEM((ONES_ROWS + V_DIM, tq), F32)],
        compiler_params=_cp("parallel", "parallel", "arbitrary"), name="attn_mla",
    )(q_ext, kv_ext, kv_ext)


def _mla_decode_kernel(*refs, per_step):
    pt_ref = refs[0]
    ckv_pages = refs[1:1 + per_step]
    kr_pages = refs[1 + per_step:1 + 2 * per_step]
    ql_ref, qr_ref, cnew_ref, rnew_ref, o_ref, m_sc, l_sc, acc_sc = refs[1 + 2 * per_step:]
    s_idx = pl.program_id(1)
    scale = (QK_NOPE + QK_ROPE) ** -0.5
    ql = ql_ref[0] * scale
    qr = qr_ref[0] * scale

    @pl.when(s_idx == 0)
    def _():
        c_new = cnew_ref[0]
        s_new = (jnp.sum(ql * c_new, axis=-1, keepdims=True) + jnp.sum(qr * rnew_ref[0], axis=-1, keepdims=True))
        m_sc[...] = s_new
        l_sc[...] = jnp.ones(l_sc.shape, F32)
        acc_sc[...] = jnp.broadcast_to(c_new, acc_sc.shape)

    qlb = ql.astype(BF16)
    qrb = qr.astype(BF16)
    cs = [ref[0].astype(BF16) for ref in ckv_pages]
    s = jnp.concatenate([_dot_nt(qlb, c) + _dot(qrb, r[0].astype(BF16)) for c, r in zip(cs, kr_pages)], axis=1)
    m_old = m_sc[...]
    m_new = jnp.maximum(m_old, jnp.max(s, axis=-1, keepdims=True))
    alpha = jnp.exp(m_old - m_new)
    e = jnp.exp(s - m_new).astype(BF16)
    pv = _dot(e[:, :PAGE_SIZE], cs[0])
    for p in range(1, per_step):
        pv = pv + _dot(e[:, p * PAGE_SIZE:(p + 1) * PAGE_SIZE], cs[p])
    l_sc[...] = alpha * l_sc[...] + jnp.sum(e.astype(F32), axis=-1, keepdims=True)
    acc_sc[...] = alpha * acc_sc[...] + pv
    m_sc[...] = m_new

    @pl.when(s_idx == pl.num_programs(1) - 1)
    def _():
        o_ref[0] = acc_sc[...] / l_sc[...]


def _mla_decode(pages, cache_ckv, cache_kr, q_lat, q_rope, c_new, r_new, per_step):
    B, n_pages = pages.shape
    H = MLA_HEADS
    blk3 = lambda b, s, pt: (b, 0, 0)
    page_map = lambda p: functools.partial(lambda b, s, pt, p: (pt[b, s * per_step + p], 0, 0), p=p)
    grid_spec = pltpu.PrefetchScalarGridSpec(
        num_scalar_prefetch=1, grid=(B, n_pages // per_step),
        in_specs=([pl.BlockSpec((1, PAGE_SIZE, KV_LORA), page_map(p)) for p in range(per_step)]
                  + [pl.BlockSpec((1, QK_ROPE, PAGE_SIZE), page_map(p)) for p in range(per_step)]
                  + [pl.BlockSpec((1, H, KV_LORA), blk3), pl.BlockSpec((1, H, QK_ROPE), blk3),
                     pl.BlockSpec((1, 1, KV_LORA), blk3), pl.BlockSpec((1, 1, QK_ROPE), blk3)]),
        out_specs=pl.BlockSpec((1, H, KV_LORA), blk3),
        scratch_shapes=[pltpu.VMEM((H, 1), F32), pltpu.VMEM((H, 1), F32), pltpu.VMEM((H, KV_LORA), F32)])
    return pl.pallas_call(
        functools.partial(_mla_decode_kernel, per_step=per_step),
        out_shape=jax.ShapeDtypeStruct((B, H, KV_LORA), F32), grid_spec=grid_spec,
        compiler_params=_cp("parallel", "arbitrary"), name="mla_decode",
    )(pages, *([cache_ckv] * per_step), *([cache_kr] * per_step), q_lat, q_rope, c_new, r_new)


def _attn_moba_kernel(q_ref, kv_ref, o_ref, k2, vt_sc, mean_sc, mask_sc, m_sc, acc_sc, *, T, tq, tk, nb, n_top):
    i = pl.program_id(2)
    t0 = i * tq
    R = MOBA_HEADS // MOBA_KV_HEADS
    dh = MOBA_HEAD_DIM
    L = R * tq
    nbp = mean_sc.shape[0]
    bpt = tk // MOBA_BLOCK

    @pl.when(i == 0)
    def _():
        kv = kv_ref[...]
        k2[...] = _dup_keys(kv, dh).astype(BF16)
        _store_values_t(vt_sc, 0, _transpose_rows(kv.astype(BF16))[dh:])
        mean_sc[...] = jnp.zeros(mean_sc.shape, F32)
        for j in range(T // MOBA_BLOCK):
            blk_rows = _dup_keys(kv_ref[j * MOBA_BLOCK:(j + 1) * MOBA_BLOCK, :], dh)
            mean_sc[j:j + 1, :] = jnp.sum(blk_rows, axis=0, keepdims=True) / MOBA_BLOCK

    tpos = t0 + _iota((1, L), 1) % tq
    cur = tpos // MOBA_BLOCK
    jb = _iota((nbp, L), 0)
    qs = _stack_heads(q_ref, R, dh, dh ** -0.5)
    block_mask = jnp.where(jb == cur, 1.0, 0.0)
    if n_top > 0:
        gm = jnp.where(jb < cur, _dot_nt(mean_sc[...].astype(BF16), qs), NEG)
        rank = jnp.zeros(gm.shape, F32)
        for c in range(nb):
            row = gm[c:c + 1, :]
            rank = rank + jnp.where(row > gm, 1.0, jnp.where(row == gm, jnp.where(jb > c, 1.0, 0.0), 0.0))
        block_mask = jnp.where(jb < cur, jnp.where(rank < n_top, 1.0, 0.0), block_mask)
    mask_sc[...] = block_mask

    def bias(kt, k0):
        parts = []
        for j in range(bpt):
            picked = mask_sc[pl.ds(kt * bpt + j, 1), :]
            kpos = k0 + j * MOBA_BLOCK + _iota((MOBA_BLOCK, 1), 0)
            parts.append(jnp.where(picked > 0.5, jnp.where(kpos <= tpos, 0.0, NEG), NEG))
        return parts[0] if bpt == 1 else jnp.concatenate(parts, axis=0)

    hi = (t0 + tq + tk - 1) // tk
    acc = _flash(qs, k2, slice(None), vt_sc, 0, m_sc, acc_sc, 0, hi, tk, bias, 1)
    o = acc[ONES_ROWS:] / acc[0:1]
    _write_heads_t(o_ref, [o[:, r * tq:(r + 1) * tq] for r in range(R)])


def _attn_moba(proj, B, T, tk=None):
    KH = MOBA_KV_HEADS
    R = MOBA_HEADS // KH
    dh = MOBA_HEAD_DIM
    tq = min(T, 512)
    tk = tk or min(T, 512)
    nq = T // tq
    nb = (T - 1) // MOBA_BLOCK
    n_top = min(MOBA_TOP, nb)
    qw = MOBA_HEADS * dh // KH
    kv0 = MOBA_HEADS * dh // LANES
    nbp = max(T // MOBA_BLOCK, 8)
    return pl.pallas_call(
        functools.partial(_attn_moba_kernel, T=T, tq=tq, tk=tk, nb=nb, n_top=n_top),
        out_shape=jax.ShapeDtypeStruct((B * T, MOBA_HEADS * dh), F32), grid=(B, KH, nq),
        in_specs=[pl.BlockSpec((tq, qw), lambda b, h, i: (b * nq + i, h)),
                  pl.BlockSpec((T, 2 * dh), lambda b, h, i: (b, kv0 + h))],
        out_specs=pl.BlockSpec((tq, qw), lambda b, h, i: (b * nq + i, h)),
        scratch_shapes=[pltpu.VMEM((T, 2 * dh), BF16), pltpu.VMEM((T // LANES, ONES_ROWS + dh, LANES), BF16),
                        pltpu.VMEM((nbp, 2 * dh), F32), pltpu.VMEM((nbp, R * tq), F32),
                        pltpu.VMEM((1, R * tq), F32), pltpu.VMEM((ONES_ROWS + dh, R * tq), F32)],
        compiler_params=_cp("parallel", "parallel", "arbitrary"), name="attn_moba",
    )(proj, proj)


def _moba_means_kernel(*refs, per_step):
    pages, o_ref = refs[1:1 + per_step], refs[1 + per_step]
    KH, dh = MOBA_KV_HEADS, MOBA_HEAD_DIM
    ppb = MOBA_BLOCK // PAGE_SIZE
    s = pl.program_id(1)

    @pl.when(s == 0)
    def _():
        o_ref[...] = jnp.zeros(o_ref.shape, F32)

    lane = _iota(o_ref.shape[1:], 1)
    out = o_ref[0]
    for j in range(per_step // ppb):
        cols = []
        for kh in range(KH):
            tot = pages[j * ppb][0, kh, 0]
            for p in range(1, ppb):
                tot = tot + pages[j * ppb + p][0, kh, 0]
            cols.append(jnp.sum(tot, axis=1, keepdims=True))
        col = jnp.concatenate(cols, axis=0) / MOBA_BLOCK
        out = jnp.where(lane == s * (per_step // ppb) + j, col, out)
    o_ref[0] = out


def _moba_means(pages, cache_t, per_step):
    B, n_pages = pages.shape
    KH, dh = MOBA_KV_HEADS, MOBA_HEAD_DIM
    nk = KH * dh
    assert n_pages * PAGE_SIZE // MOBA_BLOCK <= LANES
    cache_t = cache_t.reshape(-1, KH, 2, dh, PAGE_SIZE)
    grid_spec = pltpu.PrefetchScalarGridSpec(
        num_scalar_prefetch=1, grid=(B, n_pages // per_step),
        in_specs=[pl.BlockSpec((1, KH, 1, dh, PAGE_SIZE), functools.partial(
            lambda b, s, pt, p: (pt[b, s * per_step + p], 0, 0, 0, 0), p=p)) for p in range(per_step)],
        out_specs=pl.BlockSpec((1, nk, LANES), lambda b, s, pt: (b, 0, 0)))
    return pl.pallas_call(
        functools.partial(_moba_means_kernel, per_step=per_step),
        out_shape=jax.ShapeDtypeStruct((B, nk, LANES), F32), grid_spec=grid_spec,
        compiler_params=_cp("parallel", "arbitrary"), name="moba_means",
    )(pages, *([cache_t] * per_step))


def _moba_dec_score_kernel(q_ref, mean_ref, o_ref, *, nb):
    KH = MOBA_KV_HEADS
    R = MOBA_HEADS // KH
    dh = MOBA_HEAD_DIM
    lane = _iota((R, LANES), 1)
    for kh in range(KH):
        mk = mean_ref[0, kh * dh:(kh + 1) * dh, :].astype(BF16)
        g = _dot(q_ref[0, kh * R:(kh + 1) * R, :].astype(BF16), mk)
        o_ref[0, kh * R:(kh + 1) * R, :] = jnp.where(lane < nb, g, -jnp.inf)


def _moba_dec_score(q3, means, nb):
    B, H, dh = q3.shape
    blk3 = lambda b: (b, 0, 0)
    return pl.pallas_call(
        functools.partial(_moba_dec_score_kernel, nb=nb),
        out_shape=jax.ShapeDtypeStruct((B, H, LANES), F32), grid=(B,),
        in_specs=[pl.BlockSpec((1, H, dh), blk3), pl.BlockSpec((1,) + means.shape[1:], blk3)],
        out_specs=pl.BlockSpec((1, H, LANES), blk3), compiler_params=_cp("parallel"), name="moba_dec_score",
    )(q3, means)


def _moba_dec_attn_kernel(*refs, n_blk):
    R = MOBA_HEADS // MOBA_KV_HEADS
    dh = MOBA_HEAD_DIM
    pages = refs[1:1 + R * n_blk]
    q_ref, new_ref, o_ref = refs[1 + R * n_blk:]
    k_new = new_ref[0, :, :dh]
    v_new = new_ref[0, :, dh:]
    for r in range(R):
        q = q_ref[0, 0, r:r + 1, :] * dh ** -0.5
        kt = jnp.concatenate([pages[r * n_blk + n][0, :dh, :] for n in range(n_blk)], axis=1).astype(BF16)
        vt = jnp.concatenate([pages[r * n_blk + n][0, dh:, :] for n in range(n_blk)], axis=1).astype(BF16)
        s = _dot(q.astype(BF16), kt)
        s_new = jnp.sum(q * k_new, axis=-1, keepdims=True)
        m = jnp.maximum(jnp.max(s, axis=-1, keepdims=True), s_new)
        e = jnp.exp(s - m)
        e_new = jnp.exp(s_new - m)
        o_ref[0, 0, r:r + 1, :] = ((_dot_nt(e.astype(BF16), vt) + e_new * v_new)
                                   / (jnp.sum(e, axis=-1, keepdims=True) + e_new))


def _moba_dec_attn(page_ids, cache_t, q4, new_kv):
    B, KH, R, dh = q4.shape
    n_blk = page_ids.shape[1] // (KH * R)
    page_specs = [pl.BlockSpec((1, 2 * dh, PAGE_SIZE), functools.partial(
        lambda b, h, pg, n: (pg[b, h * R * n_blk + n], h, 0), n=n)) for n in range(R * n_blk)]
    grid_spec = pltpu.PrefetchScalarGridSpec(
        num_scalar_prefetch=1, grid=(B, KH),
        in_specs=page_specs + [pl.BlockSpec((1, 1, R, dh), lambda b, h, pg: (b, h, 0, 0)),
                               pl.BlockSpec((1, 1, 2 * dh), lambda b, h, pg: (b, 0, h))],
        out_specs=pl.BlockSpec((1, 1, R, dh), lambda b, h, pg: (b, h, 0, 0)))
    return pl.pallas_call(
        functools.partial(_moba_dec_attn_kernel, n_blk=n_blk),
        out_shape=jax.ShapeDtypeStruct(q4.shape, F32), grid_spec=grid_spec,
        compiler_params=_cp("parallel", "parallel"), name="moba_dec_attn",
    )(page_ids, *([cache_t] * (R * n_blk)), q4, new_kv)


def _router_kernel(x_ref, w_ref, b_ref, o_ref):
    o_ref[...] = _route(x_ref[...], w_ref[...], b_ref[...])


def _route(tokens, w, b):
    s = jax.nn.sigmoid(lax.dot_general(w, tokens, (((1,), (1,)), ((), ())),
                                       preferred_element_type=F32, precision=lax.Precision.HIGHEST))
    row = _iota(s.shape, 0)
    x = s + b
    chosen = jnp.zeros(s.shape, F32)
    for _ in range(TOP_K):
        m = jnp.max(x, axis=0, keepdims=True)
        idx = jnp.min(jnp.where(x == m, row, N_EXPERTS), axis=0, keepdims=True)
        hit = row == idx
        chosen = jnp.where(hit, 1.0, chosen)
        x = jnp.where(hit, -jnp.inf, x)
    w = chosen * s
    w = w / jnp.sum(w, axis=0, keepdims=True) * ROUTED_SCALE
    return jnp.concatenate([w, jnp.zeros((LANES - N_EXPERTS, w.shape[1]), F32)], axis=0).T


def _moe_router(x, w_router_t, b_router):
    M, K = x.shape
    tm = min(M, 512)
    E = w_router_t.shape[0]
    row = lambda i: (i, 0)
    fix = lambda i: (0, 0)
    return pl.pallas_call(
        _router_kernel, out_shape=jax.ShapeDtypeStruct((M, LANES), F32), grid=(M // tm,),
        in_specs=[pl.BlockSpec((tm, K), row), pl.BlockSpec((E, K), fix), pl.BlockSpec((E, 1), fix)],
        out_specs=pl.BlockSpec((tm, LANES), row), compiler_params=_cp("parallel"), name="moe_router",
    )(x, w_router_t, b_router)


def _moe_kernel(x_ref, gate_ref, wg_ref, wu_ref, wd_ref, wsg_ref, wsu_ref, wsd_ref, g_ref, b_ref, o_ref,
                xb_sc, acc_sc):
    e = pl.program_id(1)
    nh = MOE_GROUP * D_EXPERT

    @pl.when(e == 0)
    def _():
        xb = x_ref[...].astype(BF16)
        xb_sc[...] = xb
        hs = _silu(_dot(xb, wsg_ref[0].astype(BF16))) * _dot(xb, wsu_ref[0].astype(BF16))
        acc_sc[...] = _dot(hs.astype(BF16), wsd_ref[0].astype(BF16))

    xb = xb_sc[...]
    w_gate = jnp.concatenate([wg_ref[0, k].astype(BF16) for k in range(MOE_GROUP)], axis=1)
    w_up = jnp.concatenate([wu_ref[0, k].astype(BF16) for k in range(MOE_GROUP)], axis=1)
    gate = gate_ref[...]
    lane = _iota(gate.shape, 1)
    act = _silu(_dot(xb, w_gate)) * _dot(xb, w_up)
    h = jnp.concatenate(
        [act[:, k * D_EXPERT:(k + 1) * D_EXPERT]
         * jnp.sum(jnp.where(lane == MOE_GROUP * e + k, gate, 0.0), axis=1, keepdims=True)
         for k in range(MOE_GROUP)], axis=1)
    w_down = jnp.concatenate([wd_ref[0, k].astype(BF16) for k in range(MOE_GROUP)], axis=0)
    acc_sc[...] += _dot(h.astype(BF16), w_down)

    @pl.when(e == pl.num_programs(1) - 1)
    def _():
        o_ref[...] = _layer_norm(ALPHA * x_ref[...] + acc_sc[...], g_ref[...], b_ref[...])


def _moe(x, gate, w, g, b):
    M, D = x.shape
    tm = min(M, 1024)
    layer = w['layer']
    E, _, F = w['w_gate'].shape[1:]
    n = MOE_GROUP
    row = lambda i, e: (i, 0)
    fix = lambda i, e: (0, 0)
    grp = lambda i, e: (layer, e, 0, 0)
    lay = lambda i, e: (layer, 0, 0)
    return pl.pallas_call(
        _moe_kernel, out_shape=jax.ShapeDtypeStruct((M, D), F32), grid=(M // tm, E // n),
        in_specs=[pl.BlockSpec((tm, D), row), pl.BlockSpec((tm, LANES), row),
                  pl.BlockSpec((1, n, D, F), grp), pl.BlockSpec((1, n, D, F), grp), pl.BlockSpec((1, n, F, D), grp),
                  pl.BlockSpec((1, D, D_SHARED), lay), pl.BlockSpec((1, D, D_SHARED), lay),
                  pl.BlockSpec((1, D_SHARED, D), lay),
                  pl.BlockSpec((1, D), fix), pl.BlockSpec((1, D), fix)],
        out_specs=pl.BlockSpec((tm, D), row),
        scratch_shapes=[pltpu.VMEM((tm, D), BF16), pltpu.VMEM((tm, D), F32)],
        compiler_params=_cp("parallel", "arbitrary"), name="moe",
    )(x, gate, w['w_gate'], w['w_up'], w['w_down'], w['ws_gate'], w['ws_up'], w['ws_down'],
      g.reshape(1, D), b.reshape(1, D))


def _pad_cols(w, n):
    return jnp.pad(w, ((0, 0), (0, n - w.shape[1])))


def _block_diag(blocks):
    n, a, b = blocks.shape
    eye = jnp.eye(n, dtype=blocks.dtype)
    return (eye[:, None, :, None] * blocks[:, :, None, :]).reshape(n * a, n * b)


def _nsa_weights(w_in, cmp_pe, cmp_w1, cmp_b1, cmp_w2, cmp_b2, w_o):
    H, G, dh = NSA_HEADS, NSA_KV_HEADS, NSA_HEAD_DIM
    R = H // G
    nq, nkv = H * dh, 6 * G * dh
    wq, wkv = w_in[:, :nq], w_in[:, nq:nq + nkv]
    wg = w_in[:, nq + nkv:].reshape(-1, 3, G, R).transpose(0, 2, 1, 3).reshape(-1, G, 3 * R)
    wg = jnp.pad(wg, ((0, 0), (0, 0), (0, LANES - 3 * R))).reshape(-1, G * LANES)
    w_ext = jnp.concatenate([wq, wq, wkv, wg], axis=1).astype(BF16)
    w1 = cmp_w1.reshape(2, CMP_STRIDE, 2, dh, CMP_HIDDEN)
    eye_k = jnp.eye(2, dtype=F32)
    w_lohi = (w1.transpose(1, 2, 3, 0, 4)[:, :, :, :, None, :] * eye_k[None, :, None, None, :, None])
    w_lohi = w_lohi.reshape(CMP_STRIDE // 2, 2 * 2 * dh, 2 * 2 * CMP_HIDDEN).astype(BF16)
    pe_rows = jnp.broadcast_to(cmp_pe[:, None], (CMP_BLOCK, G, 2, dh)).reshape(1, CMP_BLOCK, G * 2 * dh)
    b1 = jnp.broadcast_to(cmp_b1[None], (G, 2, CMP_HIDDEN)).reshape(1, -1)
    b2 = jnp.broadcast_to(cmp_b2[None], (G, 2, dh)).reshape(1, -1)
    w2 = _block_diag(jnp.tile(cmp_w2, (G, 1, 1))).astype(BF16)
    return dict(w_ext=w_ext, w_lohi=w_lohi, pe_rows=pe_rows, b1=b1, b2=b2, w2=w2, w_o=w_o.astype(BF16))


def _nsa_tables(pos):
    H, G = NSA_HEADS, NSA_KV_HEADS
    dh = NSA_HEAD_DIM
    kv_rot = [(1.0, dh), (0.0, dh)] * G
    layout = [(0.0, H * dh), (1.0, H * dh), (0.0, 2 * G * dh)] + kv_rot + kv_rot + [(0.0, G * LANES)]
    return _unit_tables(pos, dh // 2) + (_column_mask(layout),)


def _mla_weights(w_dn, g_q, w_uq, g_kv, w_uk, w_uv, w_o):
    H = MLA_HEADS
    pad = MLA_SLOT - QK_NOPE - QK_ROPE
    w_dn_p = _pad_cols(w_dn, Q_LORA + KV_LORA + LANES).astype(BF16)
    wq = jnp.pad(w_uq.reshape(Q_LORA, H, QK_NOPE + QK_ROPE), ((0, 0), (0, 0), (0, pad)))
    w_q = wq.reshape(Q_LORA, H * MLA_SLOT).astype(BF16)
    wk_c = jnp.pad(w_uk, ((0, 0), (0, 0), (0, MLA_SLOT - QK_NOPE))).reshape(KV_LORA, H * MLA_SLOT)
    eye = jnp.pad(jnp.eye(QK_ROPE, dtype=F32), ((0, 0), (QK_NOPE, pad)))
    wk_r = jnp.tile(eye, (1, H))
    wk = jnp.concatenate([wk_c, wk_r, jnp.zeros((LANES - QK_ROPE, H * MLA_SLOT), F32)], axis=0)
    wv = jnp.pad(w_uv.reshape(KV_LORA, H * V_DIM), ((0, LANES), (0, 0)))
    w_kv = jnp.concatenate([wk, wv], axis=1).astype(BF16)
    absorb = jnp.pad(w_uk.transpose(1, 2, 0), ((0, 0), (0, MLA_SLOT - QK_NOPE), (0, 0)))
    w_absorb = _block_diag(absorb).astype(BF16)
    w_unabsorb = _block_diag(w_uv.transpose(1, 0, 2)).astype(BF16)
    return dict(w_dn=w_dn_p, g_q=g_q, g_kv=g_kv, w_q=w_q, w_kv=w_kv, w_absorb=w_absorb,
                w_unabsorb=w_unabsorb, w_o=w_o.astype(BF16))


def _mla_q_tables(pos):
    half = QK_ROPE // 2
    ua, ub = _rope_unit(pos, half)
    T = pos.shape[0]
    pad = MLA_SLOT - QK_NOPE - QK_ROPE
    a = jnp.concatenate([jnp.ones((T, QK_NOPE), F32), ua, jnp.ones((T, pad), F32)], axis=1)
    b = jnp.concatenate([jnp.zeros((T, QK_NOPE), F32), ub, jnp.zeros((T, pad), F32)], axis=1)
    return a, b, jnp.ones((1, MLA_HEADS * MLA_SLOT), F32)


def _mla_kr_tables(pos):
    return _tables(pos, QK_ROPE // 2, [('r', 1), ('n', LANES - QK_ROPE)])


def _moba_weights(w_in, w_o):
    return dict(w_in=w_in.astype(BF16), w_o=w_o.astype(BF16))


def _moba_tables(pos):
    dh = MOBA_HEAD_DIM
    layout = [(1.0, MOBA_HEADS * dh)] + [(1.0, dh), (0.0, dh)] * MOBA_KV_HEADS
    return _unit_tables(pos, dh // 2) + (_column_mask(layout),)


def _moe_weights(w_router, b_router, w_gate, w_up, w_down, ws_gate, ws_up, ws_down, layer=None):
    stacked = (w_router, b_router, w_gate, w_up, w_down, ws_gate, ws_up, ws_down)
    if layer is None:
        stacked, layer = tuple(a[None] for a in stacked), 0
    w_router, b_router, w_gate, w_up, w_down, ws_gate, ws_up, ws_down = stacked
    E = w_gate.shape[1]
    return dict(layer=layer, w_router=w_router[layer].T, b_router=b_router[layer].reshape(E, 1),
                w_gate=w_gate, w_up=w_up, w_down=w_down, ws_gate=ws_gate, ws_up=ws_up, ws_down=ws_down)


def _feature_major(cache):
    n, pool, rows = cache.shape[:3]
    nd = cache.ndim
    return cache.transpose(0, 1, *range(3, nd), 2).reshape(n * pool, -1, rows)


def _nsa_cmp_tokens(rows_lohi, w):
    pe_lohi = _cmp_lohi(w['pe_rows'], w['w_lohi'])
    return _cmp_combine(rows_lohi, pe_lohi, w['b1'], w['w2'], w['b2'])


def _nsa_prompt(h, B, T, w, tabs):
    G, dh = NSA_KV_HEADS, NSA_HEAD_DIM
    proj = _proj(h, w['w_ext'], tabs, dh // 2)
    kv0 = 2 * NSA_HEADS * dh
    width = 2 * G * dh
    lohi = _cmp_lohi(proj.reshape(B, T, -1), w['w_lohi'], col_block=kv0 // LANES)
    cmp_tok = _nsa_cmp_tokens(lohi, w)
    y = _attn_nsa(proj, cmp_tok, B, T)
    kv = proj[:, kv0:kv0 + 3 * width].reshape(B, T, 3, G, 2, dh)
    return y, (kv[:, :, 0], kv[:, :, 1], kv[:, -min(WINDOW, T):, 2])


def _nsa_sample(h, past_len, w, tabs, cache_cmp, cache_sel, win_state, slot, page_table):
    B = h.shape[0]
    G, H, dh = NSA_KV_HEADS, NSA_HEADS, NSA_HEAD_DIM
    R = H // G
    n_pool = cache_cmp.shape[1]
    width = 2 * G * dh
    proj = _proj(h, w['w_ext'], tabs, dh // 2)
    pages = page_table + slot * n_pool
    lohi = _cmp_lohi_paged(_feature_major(cache_cmp), pages, w['w_lohi'], per_step=min(32, page_table.shape[1]))
    cmp_tok = _nsa_cmp_tokens(lohi, w)
    n_sel = -(-(past_len + 1) // SEL_BLOCK)
    n_top = min(SEL_TOP, n_sel)
    q3 = proj[:, :H * dh].reshape(B, H, dh)
    qr3 = proj[:, H * dh:2 * H * dh].reshape(B, H, dh)
    o_cmp, imp = _nsa_dec_cmp(q3, cmp_tok, past_len, n_sel)
    idx = _topk_idx(imp.reshape(B * G, -1), n_top)[:, :n_top].reshape(B, G * n_top)
    n_past_blocks = past_len // SEL_BLOCK
    per_page = PAGE_SIZE // SEL_BLOCK
    sel_pages = jnp.take_along_axis(pages, jnp.minimum(idx, n_past_blocks - 1) // per_page, axis=1)
    kv0 = 2 * H * dh
    new_kv = proj[:, kv0:kv0 + 3 * width].reshape(B, 1, 3 * width)
    gl3 = proj[:, kv0 + 3 * width:].reshape(B, G, LANES)[:, :, :3 * R].reshape(B, G, 3, R)
    gl3 = gl3.transpose(0, 1, 3, 2).reshape(B, H, 3)
    Wn = win_state.shape[2]
    o = _nsa_dec_attn(idx, sel_pages, _feature_major(cache_sel), qr3, new_kv, _feature_major(win_state), slot * B,
                      o_cmp, gl3, n_past_blocks)
    kv = new_kv.reshape(B, 1, 3, G, 2, dh)
    new_win = jnp.concatenate([win_state[slot], kv[:, :, 2]], axis=1)[:, -Wn:]
    return o.reshape(B, H * dh), (kv[:, :, 0], kv[:, :, 1], new_win)


def _mla_front(h, w, q_tabs, kr_tabs):
    dn = _proj(h, w['w_dn'])
    c_q, ckr = _mla_post(dn, w['g_q'], w['g_kv'], kr_tabs)
    q_ext = _proj(c_q, w['w_q'], q_tabs, QK_ROPE // 2)
    return q_ext, ckr


def _mla_prompt(h, B, T, w, q_tabs, kr_tabs):
    q_ext, ckr = _mla_front(h, w, q_tabs, kr_tabs)
    kv_ext = _proj(ckr, w['w_kv'])
    y = _attn_mla(q_ext, kv_ext, B, T)
    return y, (ckr[:, :KV_LORA].reshape(B, T, KV_LORA), ckr[:, KV_LORA:KV_LORA + QK_ROPE].reshape(B, T, QK_ROPE))


def _mla_sample(h, w, q_tabs, kr_tabs, cache_ckv, cache_kr, slot, page_table):
    B = h.shape[0]
    H = MLA_HEADS
    n_pool = cache_ckv.shape[1]
    q_ext, ckr = _mla_front(h, w, q_tabs, kr_tabs)
    q_lat = _proj(q_ext, w['w_absorb']).reshape(B, H, KV_LORA)
    q_rope = q_ext.reshape(B, H, MLA_SLOT)[:, :, QK_NOPE:QK_NOPE + QK_ROPE]
    c_new = ckr[:, :KV_LORA].reshape(B, 1, KV_LORA)
    r_new = ckr[:, KV_LORA:KV_LORA + QK_ROPE].reshape(B, 1, QK_ROPE)
    pages = page_table + slot * n_pool
    o_lat = _mla_decode(pages, cache_ckv.reshape(-1, PAGE_SIZE, KV_LORA), _feature_major(cache_kr),
                        q_lat, q_rope, c_new, r_new, per_step=min(32, page_table.shape[1]))
    y = _proj(o_lat.reshape(B, H * KV_LORA), w['w_unabsorb'])
    return y, (c_new, r_new)


def _moba_prompt(h, B, T, w, tabs):
    KH, dh = MOBA_KV_HEADS, MOBA_HEAD_DIM
    proj = _proj(h, w['w_in'], tabs, dh // 2)
    y = _attn_moba(proj, B, T)
    return y, proj[:, MOBA_HEADS * dh:].reshape(B, T, KH, 2, dh)


def _moba_sample(h, past_len, w, tabs, cache, slot, page_table):
    B = h.shape[0]
    H, KH, dh = MOBA_HEADS, MOBA_KV_HEADS, MOBA_HEAD_DIM
    R = H // KH
    assert past_len % MOBA_BLOCK == 0 and past_len // MOBA_BLOCK >= MOBA_TOP
    n_pool = cache.shape[1]
    width = KH * 2 * dh
    ppb = MOBA_BLOCK // PAGE_SIZE
    proj = _proj(h, w['w_in'], tabs, dh // 2)
    pages = page_table + slot * n_pool
    cache_t = _feature_major(cache)
    means = _moba_means(pages, cache_t, per_step=min(16, page_table.shape[1]))
    q3 = proj[:, :H * dh].reshape(B, H, dh)
    scores = _moba_dec_score(q3, means, past_len // MOBA_BLOCK)
    idx = _topk_idx(scores.reshape(B * H, -1), MOBA_TOP)[:, :MOBA_TOP].reshape(B, H * MOBA_TOP)
    page_ids = jnp.take_along_axis(pages, (idx[:, :, None] * ppb + jnp.arange(ppb)).reshape(B, -1), axis=1)
    new_kv = proj[:, H * dh:].reshape(B, 1, width)
    o = _moba_dec_attn(page_ids, cache_t, q3.reshape(B, KH, R, dh), new_kv)
    return o.reshape(B, H * dh), new_kv.reshape(B, 1, KH, 2, dh)


def _moe_layer(h, w, g, b, gate=None):
    if gate is None:
        gate = _moe_router(h, w['w_router'], w['b_router'])
    return _moe(h, gate, w, g, b)


def kernel(x_prompt, x_sample, cache_nsa_cmp, cache_nsa_sel, state_nsa_win, cache_mla_ckv, cache_mla_krope,
           cache_moba_kv, page_table, nsa_w_in, nsa_cmp_pe, nsa_cmp_w1, nsa_cmp_b1, nsa_cmp_w2, nsa_cmp_b2,
           nsa_w_o, mla_w_dn, mla_g_q, mla_w_uq, mla_g_kv, mla_w_uk, mla_w_uv, mla_w_o, moba_w_in, moba_w_o,
           ln1_g, ln1_b, ln2_g, ln2_b, moe_w_router, moe_b_router, moe_w_gate, moe_w_up, moe_w_down,
           moe_ws_gate, moe_ws_up, moe_ws_down):
    B, T, D = x_prompt.shape
    Bs, Ts, _ = x_sample.shape
    assert Ts == 1
    past_len = page_table.shape[1] * PAGE_SIZE
    assert state_nsa_win.shape[2] == WINDOW and past_len >= WINDOW
    pos_p = jnp.arange(T, dtype=I32)
    pos_s = jnp.full((Bs,), past_len, dtype=I32)
    hp = x_prompt.reshape(B * T, D)
    hs = x_sample.reshape(Bs, D)
    outs = {k: [] for k in ('cmp_p', 'cmp_s', 'sel_p', 'sel_s', 'win_p', 'win_s',
                            'ckv_p', 'ckv_s', 'kr_p', 'kr_s', 'mb_p', 'mb_s')}
    for i in range(DEPTH):
        kind, slot = i % N_MIXERS, i // N_MIXERS
        if kind == MIX_NSA:
            w = _nsa_weights(nsa_w_in[slot], nsa_cmp_pe[slot], nsa_cmp_w1[slot], nsa_cmp_b1[slot],
                             nsa_cmp_w2[slot], nsa_cmp_b2[slot], nsa_w_o[slot])
            yp, (a_p, b_p, c_p) = _nsa_prompt(hp, B, T, w, _nsa_tables(pos_p))
            ys, (a_s, b_s, c_s) = _nsa_sample(hs, past_len, w, _nsa_tables(pos_s), cache_nsa_cmp, cache_nsa_sel,
                                              state_nsa_win, slot, page_table)
            outs['cmp_p'].append(a_p); outs['cmp_s'].append(a_s)
            outs['sel_p'].append(b_p); outs['sel_s'].append(b_s)
            outs['win_p'].append(c_p); outs['win_s'].append(c_s)
        elif kind == MIX_MLA:
            w = _mla_weights(mla_w_dn[slot], mla_g_q[slot], mla_w_uq[slot], mla_g_kv[slot], mla_w_uk[slot],
                             mla_w_uv[slot], mla_w_o[slot])
            yp, (a_p, b_p) = _mla_prompt(hp, B, T, w, _mla_q_tables(pos_p), _mla_kr_tables(pos_p))
            ys, (a_s, b_s) = _mla_sample(hs, w, _mla_q_tables(pos_s), _mla_kr_tables(pos_s), cache_mla_ckv,
                                         cache_mla_krope, slot, page_table)
            outs['ckv_p'].append(a_p); outs['ckv_s'].append(a_s)
            outs['kr_p'].append(b_p); outs['kr_s'].append(b_s)
        else:
            w = _moba_weights(moba_w_in[slot], moba_w_o[slot])
            yp, a_p = _moba_prompt(hp, B, T, w, _moba_tables(pos_p))
            ys, a_s = _moba_sample(hs, past_len, w, _moba_tables(pos_s), cache_moba_kv, slot, page_table)
            outs['mb_p'].append(a_p); outs['mb_s'].append(a_s)
        mw = _moe_weights(moe_w_router, moe_b_router, moe_w_gate, moe_w_up, moe_w_down,
                          moe_ws_gate, moe_ws_up, moe_ws_down, layer=i)
        hp, gate_p = _proj_ln(yp, w['w_o'], hp, ln1_g[i], ln1_b[i], mw['w_router'], mw['b_router'])
        hs, gate_s = _proj_ln(ys, w['w_o'], hs, ln1_g[i], ln1_b[i], mw['w_router'], mw['b_router'])
        hp = _moe_layer(hp, mw, ln2_g[i], ln2_b[i], gate_p)
        hs = _moe_layer(hs, mw, ln2_g[i], ln2_b[i], gate_s)
    st = lambda k: jnp.stack(outs[k])
    return (hp.reshape(B, T, D), hs.reshape(Bs, Ts, D),
            st('cmp_p'), st('cmp_s'), st('sel_p'), st('sel_s'), st('win_p'), st('win_s'),
            st('ckv_p'), st('ckv_s'), st('kr_p'), st('kr_s'), st('mb_p'), st('mb_s'))
```
